```python
import math
import jax, jax.numpy as jnp
from jax import lax
import numpy as np

D_MODEL = 1024
BATCH = 4
SEQ = 4096
DEPTH = 2

D_MIX = D_MODEL
HEAD_DIM = 64
D_ATT = D_MIX // 2
D_LRU = D_MIX // 4
D_SG = D_MIX - D_ATT - D_LRU
N_ATT_HEADS = D_ATT // HEAD_DIM
N_LRU_BLOCKS = D_LRU // HEAD_DIM
N_SG_GROUPS = D_SG // HEAD_DIM
Q_BLOCK = 128
SG_CHUNK = 128
CONV_WIDTH = 4
LRU_C = 8.0
EPS = 1e-6
IN_SIZES = (D_ATT, D_ATT, D_ATT, N_ATT_HEADS, D_ATT, D_LRU, D_LRU, D_SG, D_SG, D_SG)
D_IN = 4 * D_ATT + N_ATT_HEADS + 2 * D_LRU + 3 * D_SG

kernel_name = "hybrid_fox_rglru_gmlp_parallel_heads"


def rms_norm(x, g):
    xf = x.astype(jnp.float32)
    y = xf * lax.rsqrt(jnp.mean(xf * xf, axis=-1, keepdims=True) + EPS)
    return (y * g.astype(jnp.float32)).astype(x.dtype)


def layer_norm(x, g, b):
    xf = x.astype(jnp.float32)
    mu = jnp.mean(xf, axis=-1, keepdims=True)
    var = jnp.mean(jnp.square(xf - mu), axis=-1, keepdims=True)
    y = (xf - mu) * lax.rsqrt(var + EPS)
    return (y * g.astype(jnp.float32) + b.astype(jnp.float32)).astype(x.dtype)


def forgetting_attention(q, k, v, f_logit):
    B, S, H, Dh = q.shape
    n_blk = S // Q_BLOCK
    scale = 1.0 / math.sqrt(Dh)
    F = jnp.cumsum(jax.nn.log_sigmoid(f_logit.astype(jnp.float32)), axis=1)
    F = F.transpose(0, 2, 1)
    kh = k.transpose(0, 2, 1, 3)
    vh = v.transpose(0, 2, 1, 3)
    qb = q.reshape(B, n_blk, Q_BLOCK, H, Dh).transpose(1, 0, 3, 2, 4)
    Fq = F.reshape(B, H, n_blk, Q_BLOCK).transpose(2, 0, 1, 3)
    key_pos = jnp.arange(S)
    neg = jnp.finfo(jnp.float32).min

    def block(args):
        q_i, Fq_i, i = args
        s = jnp.einsum('bhqd,bhkd->bhqk', q_i, kh).astype(jnp.float32) * scale
        s = s + Fq_i[..., None] - F[:, :, None, :]
        q_pos = i * Q_BLOCK + jnp.arange(Q_BLOCK)
        mask = key_pos[None, :] <= q_pos[:, None]
        p = jax.nn.softmax(jnp.where(mask, s, neg), axis=-1)
        return jnp.einsum('bhqk,bhkd->bhqd', p.astype(vh.dtype), vh)

    out = lax.map(block, (qb, Fq, jnp.arange(n_blk)))
    return out.transpose(1, 0, 3, 2, 4).reshape(B, S, H * Dh)


def causal_depthwise_conv(x, w, b):
    S = x.shape[1]
    xp = jnp.pad(x, ((0, 0), (CONV_WIDTH - 1, 0), (0, 0)))
    y = b
    for kk in range(CONV_WIDTH):
        y = y + xp[:, kk:kk + S, :] * w[kk]
    return y


def rg_lru(x, wa, ba, wx, bx, lam):
    B, S, C = x.shape
    xg = x.reshape(B, S, N_LRU_BLOCKS, C // N_LRU_BLOCKS)
    r = jax.nn.sigmoid((jnp.einsum('bsgi,gij->bsgj', xg, wa).reshape(B, S, C) + ba).astype(jnp.float32))
    i = jax.nn.sigmoid((jnp.einsum('bsgi,gij->bsgj', xg, wx).reshape(B, S, C) + bx).astype(jnp.float32))
    log_a = -LRU_C * r * jax.nn.softplus(-lam.astype(jnp.float32))
    a = jnp.exp(log_a)
    b_term = jnp.sqrt(-jnp.expm1(2.0 * log_a)) * (i * x.astype(jnp.float32))

    def combine(left, right):
        a1, b1 = left
        a2, b2 = right
        return a1 * a2, a2 * b1 + b2

    _, h = lax.associative_scan(combine, (a, b_term), axis=1)
    return h.astype(x.dtype)


def spatial_gate(u, v, ln_g, ln_b, w_s, b_s):
    B, S, C = u.shape
    n_chunk = S // SG_CHUNK
    vn = layer_norm(v, ln_g, ln_b).reshape(B, n_chunk, SG_CHUNK, N_SG_GROUPS, C // N_SG_GROUPS)
    tril = jnp.tril(jnp.ones((SG_CHUNK, SG_CHUNK), dtype=w_s.dtype))
    w = w_s * tril
    z = jnp.einsum('gts,bcsgd->bctgd', w, vn) + b_s.T[None, None, :, :, None]
    return u * z.reshape(B, S, C)


def hybrid_layer(x, c, ada_w, ada_b, pre_g, post_g, w_in, b_f, conv_w, conv_b,
                 lru_wa, lru_ba, lru_wx, lru_bx, lru_lambda,
                 sg_ln_g, sg_ln_b, sg_w, sg_b, w_out):
    B, S, _ = x.shape
    mod = jax.nn.silu(c) @ ada_w + ada_b
    shift, scale, gate = jnp.split(mod, 3, axis=-1)
    h = rms_norm(x, pre_g) * (1.0 + scale[:, None, :]) + shift[:, None, :]

    proj = h @ w_in
    cuts = np.cumsum(IN_SIZES)[:-1].tolist()
    (q, k, v, f_logit, g_att, x_lru, g_lru, sg_u, sg_v, g_sg) = jnp.split(proj, cuts, axis=-1)

    heads = lambda t: t.reshape(B, S, N_ATT_HEADS, HEAD_DIM)
    y_att = forgetting_attention(heads(q), heads(k), heads(v), f_logit + b_f)
    y_att = y_att * jax.nn.silu(g_att)

    x_c = causal_depthwise_conv(x_lru, conv_w, conv_b)
    y_lru = rg_lru(x_c, lru_wa, lru_ba, lru_wx, lru_bx, lru_lambda) * jax.nn.silu(g_lru)

    y_sg = spatial_gate(jax.nn.gelu(sg_u), jax.nn.gelu(sg_v), sg_ln_g, sg_ln_b, sg_w, sg_b)
    y_sg = y_sg * jax.nn.silu(g_sg)

    y = jnp.concatenate([y_att, y_lru, y_sg], axis=-1) @ w_out
    return x + gate[:, None, :] * rms_norm(y, post_g)


def setup_inputs(seed: int = 0) -> dict:
    key = jax.random.key(seed)
    ks = jax.random.split(key, 24)
    D, L = D_MODEL, DEPTH
    nrm = lambda k_, shape, s: jax.random.normal(k_, shape, jnp.float32) * s
    x = jax.random.normal(ks[0], (BATCH, SEQ, D), jnp.float32)
    c = jax.random.normal(ks[1], (BATCH, D), jnp.float32)
    ada_w = nrm(ks[2], (L, D, 3 * D), 0.5 * D ** -0.5)
    ada_b = nrm(ks[3], (L, 3 * D), 0.01)
    pre_g = 1.0 + nrm(ks[4], (L, D), 0.01)
    post_g = 1.0 + nrm(ks[5], (L, D), 0.01)
    w_in = nrm(ks[6], (L, D, D_IN), D ** -0.5)
    b_f = jax.random.uniform(ks[7], (L, N_ATT_HEADS), jnp.float32, 1.0, 4.0)
    conv_w = nrm(ks[8], (L, CONV_WIDTH, D_LRU), CONV_WIDTH ** -0.5)
    conv_b = nrm(ks[9], (L, D_LRU), 0.01)
    blk = D_LRU // N_LRU_BLOCKS
    lru_wa = nrm(ks[10], (L, N_LRU_BLOCKS, blk, blk), blk ** -0.5)
    lru_ba = nrm(ks[11], (L, D_LRU), 0.01)
    lru_wx = nrm(ks[12], (L, N_LRU_BLOCKS, blk, blk), blk ** -0.5)
    lru_bx = nrm(ks[13], (L, D_LRU), 0.01)
    a0 = jax.random.uniform(ks[14], (L, D_LRU), jnp.float32, 0.9, 0.999)
    a_base = a0 ** (1.0 / LRU_C)
    lru_lambda = jnp.log(a_base) - jnp.log1p(-a_base)
    sg_ln_g = 1.0 + nrm(ks[15], (L, D_SG), 0.01)
    sg_ln_b = nrm(ks[16], (L, D_SG), 0.01)
    sg_w = nrm(ks[17], (L, N_SG_GROUPS, SG_CHUNK, SG_CHUNK), SG_CHUNK ** -0.5)
    sg_b = 1.0 + nrm(ks[18], (L, N_SG_GROUPS, SG_CHUNK), 0.01)
    w_out = nrm(ks[19], (L, D_MIX, D), D_MIX ** -0.5)
    return {"x": x, "c": c, "ada_w": ada_w, "ada_b": ada_b, "pre_g": pre_g, "post_g": post_g,
            "w_in": w_in, "b_f": b_f, "conv_w": conv_w, "conv_b": conv_b,
            "lru_wa": lru_wa, "lru_ba": lru_ba, "lru_wx": lru_wx, "lru_bx": lru_bx,
            "lru_lambda": lru_lambda, "sg_ln_g": sg_ln_g, "sg_ln_b": sg_ln_b,
            "sg_w": sg_w, "sg_b": sg_b, "w_out": w_out}


def reference(x, c, ada_w, ada_b, pre_g, post_g, w_in, b_f, conv_w, conv_b,
              lru_wa, lru_ba, lru_wx, lru_bx, lru_lambda,
              sg_ln_g, sg_ln_b, sg_w, sg_b, w_out):
    for l in range(DEPTH):
        x = hybrid_layer(x, c, ada_w[l], ada_b[l], pre_g[l], post_g[l], w_in[l], b_f[l],
                         conv_w[l], conv_b[l], lru_wa[l], lru_ba[l], lru_wx[l], lru_bx[l],
                         lru_lambda[l], sg_ln_g[l], sg_ln_b[l], sg_w[l], sg_b[l], w_out[l])
    return x
```

```python
import functools
import math

import jax
import jax.numpy as jnp
from jax import lax
from jax.experimental import pallas as pl
from jax.experimental.pallas import tpu as pltpu

D_MODEL = 1024
HEAD_DIM = 64
D_ATT = 512
D_LRU = 256
D_SG = 256
N_HEADS = D_ATT // HEAD_DIM
N_LRU_BLOCKS = D_LRU // HEAD_DIM
N_SG_GROUPS = D_SG // HEAD_DIM
SG_CHUNK = 128
CONV_WIDTH = 4
LRU_C = 8.0
EPS = 1e-6
D_REST = D_ATT + 2 * D_LRU + 3 * D_SG
KF_ROWS = 528

LANES = 128
SUBLANES = 8
VMEM_LIMIT = 56 * 1024 * 1024

TM = 512
TQ = 512
TL = 512
TC = 512
TN_ADA = 512
NEG = -1e30

F32 = jnp.float32
BF16 = jnp.bfloat16


def _sigmoid(x):
    return 1.0 / (1.0 + jnp.exp(-x))


def _silu(x):
    return x * _sigmoid(x)


def _gelu_tanh(x):
    c = math.sqrt(2.0 / math.pi)
    return 0.5 * x * (1.0 + jnp.tanh(c * (x + 0.044715 * (x * x * x))))


def _log_sigmoid(x):
    return jnp.minimum(x, 0.0) - jnp.log1p(jnp.exp(-jnp.abs(x)))


def _params(sem):
    return pltpu.CompilerParams(dimension_semantics=sem, vmem_limit_bytes=VMEM_LIMIT)


def _ada_kernel(ct_ref, w_ref, b_ref, o_ref):
    ct = ct_ref[...]
    sc = _silu(ct)
    w = w_ref[0]
    bias = b_ref[0]
    for b in range(ct.shape[1]):
        col = sc[:, b:b + 1]
        o_ref[0, b:b + 1, :] = jnp.sum(col * w, axis=0, keepdims=True) + bias


def _ada(c, ada_w, ada_b):
    L, D, N = ada_w.shape
    B = c.shape[0]
    return pl.pallas_call(
        _ada_kernel,
        grid=(L, N // TN_ADA),
        in_specs=[
            pl.BlockSpec((D, B), lambda l, n: (0, 0)),
            pl.BlockSpec((1, D, TN_ADA), lambda l, n: (l, 0, n)),
            pl.BlockSpec((1, 1, TN_ADA), lambda l, n: (l, 0, n)),
        ],
        out_specs=pl.BlockSpec((1, B, TN_ADA), lambda l, n: (l, 0, n)),
        out_shape=jax.ShapeDtypeStruct((L, B, N), F32),
        compiler_params=_params(("parallel", "parallel")),
        name="ada_mod",
    )(c.T, ada_w, ada_b.reshape(L, 1, N))


def _inproj_kernel(x_ref, shift_ref, scale_ref, g_ref, wq_ref, wkf_ref, wv_ref, wr_ref, bf_ref,
                   q_ref, kt_ref, v_ref, ls_ref, rest_ref):
    x = x_ref[0]
    ms = jnp.mean(x * x, axis=-1, keepdims=True)
    y = (x * lax.rsqrt(ms + EPS)) * g_ref[...]
    h = y * (1.0 + scale_ref[0]) + shift_ref[0]
    hb = h.astype(BF16)
    q_ref[0] = jnp.dot(hb, wq_ref[...], preferred_element_type=F32).astype(BF16)
    v_ref[0] = jnp.dot(hb, wv_ref[...], preferred_element_type=F32).astype(BF16)
    rest_ref[0] = jnp.dot(hb, wr_ref[...], preferred_element_type=F32)
    kf = lax.dot_general(wkf_ref[...], hb, (((1,), (1,)), ((), ())), preferred_element_type=F32)
    kt_ref[0] = kf[:D_ATT].astype(BF16)
    ls_ref[0] = _log_sigmoid(kf[D_ATT:D_ATT + N_HEADS] + bf_ref[...])


def _inproj(x, shift, scale, pre_g, wq, wkf, wv, wr, b_f):
    B, S, D = x.shape
    const = lambda shape: pl.BlockSpec(shape, lambda b, i: (0,) * len(shape))
    return pl.pallas_call(
        _inproj_kernel,
        grid=(B, S // TM),
        in_specs=[
            pl.BlockSpec((1, TM, D), lambda b, i: (b, i, 0)),
            pl.BlockSpec((1, 1, D), lambda b, i: (b, 0, 0)),
            pl.BlockSpec((1, 1, D), lambda b, i: (b, 0, 0)),
            const((1, D)),
            const((D, D_ATT)),
            const((KF_ROWS, D)),
            const((D, D_ATT)),
            const((D, D_REST)),
            const((N_HEADS, 1)),
        ],
        out_specs=[
            pl.BlockSpec((1, TM, D_ATT), lambda b, i: (b, i, 0)),
            pl.BlockSpec((1, D_ATT, TM), lambda b, i: (b, 0, i)),
            pl.BlockSpec((1, TM, D_ATT), lambda b, i: (b, i, 0)),
            pl.BlockSpec((1, N_HEADS, TM), lambda b, i: (b, 0, i)),
            pl.BlockSpec((1, TM, D_REST), lambda b, i: (b, i, 0)),
        ],
        out_shape=[
            jax.ShapeDtypeStruct((B, S, D_ATT), BF16),
            jax.ShapeDtypeStruct((B, D_ATT, S), BF16),
            jax.ShapeDtypeStruct((B, S, D_ATT), BF16),
            jax.ShapeDtypeStruct((B, N_HEADS, S), F32),
            jax.ShapeDtypeStruct((B, S, D_REST), F32),
        ],
        compiler_params=_params(("parallel", "parallel")),
        name="inproj",
    )(x, shift, scale, pre_g, wq, wkf, wv, wr, b_f)


def _fcum_kernel(ls_ref, f_ref, ft_ref):
    rows, S = ls_ref.shape
    lane = lax.broadcasted_iota(jnp.int32, (rows, LANES), 1)
    pad = jnp.zeros((LANES - rows, LANES), F32)
    carry = jnp.zeros((rows, 1), F32)
    for j in range(S // LANES):
        blk = ls_ref[:, j * LANES:(j + 1) * LANES]
        d = 1
        while d < LANES:
            blk = blk + jnp.where(lane >= d, pltpu.roll(blk, d, axis=1), 0.0)
            d *= 2
        blk = blk + carry
        carry = blk[:, LANES - 1:LANES]
        f_ref[:, j * LANES:(j + 1) * LANES] = blk
        ft_ref[j * LANES:(j + 1) * LANES, :] = jnp.concatenate([blk, pad], axis=0).T


def _fcum(ls):
    rows, S = ls.shape
    return pl.pallas_call(
        _fcum_kernel,
        out_shape=[jax.ShapeDtypeStruct((rows, S), F32), jax.ShapeDtypeStruct((S, LANES), F32)],
        compiler_params=pltpu.CompilerParams(vmem_limit_bytes=VMEM_LIMIT),
        name="fcum",
    )(ls)


def _attn_kernel(q_ref, kt_ref, v_ref, f_ref, ft_ref, g_ref, o_ref, m_ref, l_ref, acc_ref):
    b = pl.program_id(0)
    p = pl.program_id(1)
    i = pl.program_id(2)
    q = q_ref[0]
    lane = lax.broadcasted_iota(jnp.int32, (TQ, LANES), 1)
    low = lane < HEAD_DIM
    zero = jnp.zeros_like(q)
    qh = (jnp.where(low, q, zero), jnp.where(low, zero, q))
    ft = ft_ref[...]
    fq = []
    for hh in range(2):
        idx = b * N_HEADS + 2 * p + hh
        fq.append(jnp.sum(jnp.where(lane == idx, ft, 0.0), axis=1, keepdims=True))
    m_ref[...] = jnp.full(m_ref.shape, NEG, F32)
    l_ref[...] = jnp.zeros(l_ref.shape, F32)
    acc_ref[...] = jnp.zeros(acc_ref.shape, F32)

    def tile(j, masked):
        start = pl.multiple_of(j * TQ, TQ)
        kt = kt_ref[0, :, pl.ds(start, TQ)]
        v = v_ref[0, pl.ds(start, TQ), :]
        fk = f_ref[0, 0, :, pl.ds(start, TQ)]
        pv = []
        alphas = []
        for hh in range(2):
            s = jnp.dot(qh[hh], kt, preferred_element_type=F32)
            s = s - fk[hh:hh + 1, :]
            if masked:
                r = lax.broadcasted_iota(jnp.int32, (TQ, TQ), 0)
                c = lax.broadcasted_iota(jnp.int32, (TQ, TQ), 1)
                s = jnp.where(r >= c, s, NEG)
            m_old = m_ref[hh]
            m_new = jnp.maximum(m_old, jnp.max(s, axis=1, keepdims=True) + fq[hh])
            e = jnp.exp(s - (m_new - fq[hh]))
            alpha = jnp.exp(m_old - m_new)
            l_ref[hh] = alpha * l_ref[hh] + jnp.sum(e, axis=1, keepdims=True)
            m_ref[hh] = m_new
            pv.append(jnp.dot(e.astype(BF16), v, preferred_element_type=F32))
            alphas.append(alpha)
        acc_ref[...] = acc_ref[...] * jnp.where(low, alphas[0], alphas[1]) + jnp.where(low, pv[0], pv[1])

    def body(j, carry):
        tile(j, False)
        return carry

    lax.fori_loop(0, i, body, 0)
    tile(i, True)
    inv = jnp.where(low, 1.0 / l_ref[0], 1.0 / l_ref[1])
    o_ref[0] = (acc_ref[...] * inv * _silu(g_ref[0])).astype(BF16)


def _attention(q, kt, v, f4, ft, rest):
    B, S, _ = q.shape
    n_pair = N_HEADS // 2
    return pl.pallas_call(
        _attn_kernel,
        grid=(B, n_pair, S // TQ),
        in_specs=[
            pl.BlockSpec((1, TQ, LANES), lambda b, p, i: (b, i, p)),
            pl.BlockSpec((1, LANES, S), lambda b, p, i: (b, p, 0)),
            pl.BlockSpec((1, S, LANES), lambda b, p, i: (b, 0, p)),
            pl.BlockSpec((1, 1, 2, S), lambda b, p, i: (b, p, 0, 0)),
            pl.BlockSpec((TQ, LANES), lambda b, p, i: (i, 0)),
            pl.BlockSpec((1, TQ, LANES), lambda b, p, i: (b, i, p)),
        ],
        out_specs=pl.BlockSpec((1, TQ, LANES), lambda b, p, i: (b, i, p)),
        out_shape=jax.ShapeDtypeStruct((B, S, D_ATT), BF16),
        scratch_shapes=[
            pltpu.VMEM((2, TQ, 1), F32),
            pltpu.VMEM((2, TQ, 1), F32),
            pltpu.VMEM((TQ, LANES), F32),
        ],
        compiler_params=_params(("parallel", "parallel", "parallel")),
        name="fox_attention",
    )(q, kt, v, f4, ft, rest)


def _lru_kernel(x_ref, g_ref, cw_ref, cb_ref, wa_ref, ba_ref, wx_ref, bx_ref, lam_ref, o_ref,
                xbuf_ref, h_ref):
    i = pl.program_id(1)

    @pl.when(i == 0)
    def _():
        xbuf_ref[0:SUBLANES, :] = jnp.zeros((SUBLANES, D_LRU), F32)
        h_ref[...] = jnp.zeros(h_ref.shape, F32)

    x = x_ref[0]
    xbuf_ref[SUBLANES:SUBLANES + TL, :] = x
    xc = cb_ref[...]
    for kk in range(CONV_WIDTH):
        off = SUBLANES - (CONV_WIDTH - 1) + kk
        xc = xc + xbuf_ref[off:off + TL, :] * cw_ref[kk:kk + 1, :]
    xbuf_ref[0:SUBLANES, :] = x[TL - SUBLANES:, :]

    xcb = xc.astype(BF16)
    r = _sigmoid(jnp.dot(xcb, wa_ref[...], preferred_element_type=F32) + ba_ref[...])
    ig = _sigmoid(jnp.dot(xcb, wx_ref[...], preferred_element_type=F32) + bx_ref[...])
    nlam = -lam_ref[...]
    softplus = jnp.maximum(nlam, 0.0) + jnp.log1p(jnp.exp(-jnp.abs(nlam)))
    log_a = (-LRU_C * r) * softplus
    a = jnp.exp(log_a)
    bt = jnp.sqrt(jnp.tanh(-log_a) * (1.0 + a * a)) * (ig * xc)

    row8 = lax.broadcasted_iota(jnp.int32, (TL, D_LRU), 0) & (SUBLANES - 1)
    d = 1
    while d < SUBLANES:
        valid = row8 >= d
        a_s = jnp.where(valid, pltpu.roll(a, d, axis=0), 1.0)
        b_s = jnp.where(valid, pltpu.roll(bt, d, axis=0), 0.0)
        bt = bt + a * b_s
        a = a * a_s
        d *= 2

    h_prev = h_ref[...]
    sg = _silu(g_ref[0])
    for g in range(TL // SUBLANES):
        lo = g * SUBLANES
        hg = bt[lo:lo + SUBLANES] + a[lo:lo + SUBLANES] * h_prev
        h_prev = hg[SUBLANES - 1:SUBLANES, :]
        o_ref[0, lo:lo + SUBLANES, :] = (hg * sg[lo:lo + SUBLANES]).astype(BF16)
    h_ref[...] = h_prev


def _lru(rest, conv_w, conv_b, wa_bd, ba, wx_bd, bx, lam):
    B, S, _ = rest.shape
    const = lambda shape: pl.BlockSpec(shape, lambda b, i: (0,) * len(shape))
    x_col = D_ATT // D_LRU
    return pl.pallas_call(
        _lru_kernel,
        grid=(B, S // TL),
        in_specs=[
            pl.BlockSpec((1, TL, D_LRU), lambda b, i: (b, i, x_col)),
            pl.BlockSpec((1, TL, D_LRU), lambda b, i: (b, i, x_col + 1)),
            const((CONV_WIDTH, D_LRU)),
            const((1, D_LRU)),
            const((D_LRU, D_LRU)),
            const((1, D_LRU)),
            const((D_LRU, D_LRU)),
            const((1, D_LRU)),
            const((1, D_LRU)),
        ],
        out_specs=pl.BlockSpec((1, TL, D_LRU), lambda b, i: (b, i, 0)),
        out_shape=jax.ShapeDtypeStruct((B, S, D_LRU), BF16),
        scratch_shapes=[
            pltpu.VMEM((TL + SUBLANES, D_LRU), F32),
            pltpu.VMEM((1, D_LRU), F32),
        ],
        compiler_params=_params(("parallel", "arbitrary")),
        name="rg_lru",
    )(rest, rest, conv_w, conv_b, wa_bd, ba, wx_bd, bx, lam)


def _sg_kernel(u_ref, v_ref, g_ref, lng_ref, lnb_ref, w_ref, bias_ref, o_ref):
    u = _gelu_tanh(u_ref[0])
    v = _gelu_tanh(v_ref[0])
    mu = jnp.mean(v, axis=-1, keepdims=True)
    var = jnp.mean(jnp.square(v - mu), axis=-1, keepdims=True)
    vn = ((v - mu) * lax.rsqrt(var + EPS)) * lng_ref[...] + lnb_ref[...]
    vnb = vn.astype(BF16)
    t_idx = lax.broadcasted_iota(jnp.int32, w_ref.shape, 0) & (SG_CHUNK - 1)
    s_idx = lax.broadcasted_iota(jnp.int32, w_ref.shape, 1)
    w = jnp.where(s_idx <= t_idx, w_ref[...], jnp.zeros(w_ref.shape, BF16))
    grp = lax.broadcasted_iota(jnp.int32, (SG_CHUNK, D_SG), 1) // HEAD_DIM
    sg = _silu(g_ref[0])
    for c in range(TC // SG_CHUNK):
        lo = c * SG_CHUNK
        zz = jnp.dot(w, vnb[lo:lo + SG_CHUNK], preferred_element_type=F32)
        z = zz[0:SG_CHUNK]
        for gi in range(1, N_SG_GROUPS):
            z = jnp.where(grp == gi, zz[gi * SG_CHUNK:(gi + 1) * SG_CHUNK], z)
        z = z + bias_ref[...]
        o_ref[0, lo:lo + SG_CHUNK, :] = ((u[lo:lo + SG_CHUNK] * z) * sg[lo:lo + SG_CHUNK]).astype(BF16)


def _spatial_gate(rest, ln_g, ln_b, w_all, bias):
    B, S, _ = rest.shape
    const = lambda shape: pl.BlockSpec(shape, lambda b, i: (0,) * len(shape))
    u_col = (D_ATT + 2 * D_LRU) // D_SG
    return pl.pallas_call(
        _sg_kernel,
        grid=(B, S // TC),
        in_specs=[
            pl.BlockSpec((1, TC, D_SG), lambda b, i: (b, i, u_col)),
            pl.BlockSpec((1, TC, D_SG), lambda b, i: (b, i, u_col + 1)),
            pl.BlockSpec((1, TC, D_SG), lambda b, i: (b, i, u_col + 2)),
            const((1, D_SG)),
            const((1, D_SG)),
            const((N_SG_GROUPS * SG_CHUNK, SG_CHUNK)),
            const((SG_CHUNK, D_SG)),
        ],
        out_specs=pl.BlockSpec((1, TC, D_SG), lambda b, i: (b, i, 0)),
        out_shape=jax.ShapeDtypeStruct((B, S, D_SG), BF16),
        compiler_params=_params(("parallel", "parallel")),
        name="spatial_gate",
    )(rest, rest, rest, ln_g, ln_b, w_all, bias)


def _outproj_kernel(ya_ref, yl_ref, ys_ref, x_ref, gate_ref, pg_ref, w_ref, o_ref):
    y = jnp.dot(ya_ref[0], w_ref[0:D_ATT, :], preferred_element_type=F32)
    y = y + jnp.dot(yl_ref[0], w_ref[D_ATT:D_ATT + D_LRU, :], preferred_element_type=F32)
    y = y + jnp.dot(ys_ref[0], w_ref[D_ATT + D_LRU:, :], preferred_element_type=F32)
    ms = jnp.mean(y * y, axis=-1, keepdims=True)
    yn = (y * lax.rsqrt(ms + EPS)) * pg_ref[...]
    o_ref[0] = x_ref[0] + gate_ref[0] * yn


def _outproj(ya, yl, ys, x, gate, post_g, w_out):
    B, S, D = x.shape
    const = lambda shape: pl.BlockSpec(shape, lambda b, i: (0,) * len(shape))
    return pl.pallas_call(
        _outproj_kernel,
        grid=(B, S // TM),
        in_specs=[
            pl.BlockSpec((1, TM, D_ATT), lambda b, i: (b, i, 0)),
            pl.BlockSpec((1, TM, D_LRU), lambda b, i: (b, i, 0)),
            pl.BlockSpec((1, TM, D_SG), lambda b, i: (b, i, 0)),
            pl.BlockSpec((1, TM, D), lambda b, i: (b, i, 0)),
            pl.BlockSpec((1, 1, D), lambda b, i: (b, 0, 0)),
            const((1, D)),
            const((D, D)),
        ],
        out_specs=pl.BlockSpec((1, TM, D), lambda b, i: (b, i, 0)),
        out_shape=jax.ShapeDtypeStruct((B, S, D), F32),
        compiler_params=_params(("parallel", "parallel")),
        name="outproj",
    )(ya, yl, ys, x, gate, post_g, w_out)


def _block_diag(w):
    g, n, _ = w.shape
    out = jnp.zeros((g * n, g * n), w.dtype)
    for k in range(g):
        out = out.at[k * n:(k + 1) * n, k * n:(k + 1) * n].set(w[k])
    return out


def _layer(x, mod, pre_g, post_g, w_in, b_f, conv_w, conv_b, lru_wa, lru_ba, lru_wx, lru_bx,
           lru_lambda, sg_ln_g, sg_ln_b, sg_w, sg_b, w_out):
    B, S, D = x.shape
    shift = mod[:, None, 0:D]
    scale = mod[:, None, D:2 * D]
    gate = mod[:, None, 2 * D:3 * D]

    wq = (w_in[:, 0:D_ATT] * (1.0 / math.sqrt(HEAD_DIM))).astype(BF16)
    o = D_ATT
    wk = w_in[:, o:o + D_ATT]
    o += D_ATT
    wv = w_in[:, o:o + D_ATT].astype(BF16)
    o += D_ATT
    wf = w_in[:, o:o + N_HEADS]
    o += N_HEADS
    wr = w_in[:, o:].astype(BF16)
    wkf = jnp.concatenate([wk, wf, jnp.zeros((D, KF_ROWS - D_ATT - N_HEADS), F32)], axis=1).T.astype(BF16)

    q, kt, v, ls, rest = _inproj(x, shift, scale, pre_g[None, :], wq, wkf, wv, wr, b_f[:, None])
    f, ft = _fcum(ls.reshape(B * N_HEADS, S))
    ya = _attention(q, kt, v, f.reshape(B, N_HEADS // 2, 2, S), ft, rest)

    yl = _lru(rest, conv_w, conv_b[None, :], _block_diag(lru_wa).astype(BF16), lru_ba[None, :],
              _block_diag(lru_wx).astype(BF16), lru_bx[None, :], lru_lambda[None, :])

    w_all = sg_w.reshape(N_SG_GROUPS * SG_CHUNK, SG_CHUNK).astype(BF16)
    bias = jnp.repeat(sg_b.T, HEAD_DIM, axis=1)
    ys = _spatial_gate(rest, sg_ln_g[None, :], sg_ln_b[None, :], w_all, bias)

    return _outproj(ya, yl, ys, x, gate, post_g[None, :], w_out.astype(BF16))


def kernel(x, c, ada_w, ada_b, pre_g, post_g, w_in, b_f, conv_w, conv_b, lru_wa, lru_ba, lru_wx,
           lru_bx, lru_lambda, sg_ln_g, sg_ln_b, sg_w, sg_b, w_out):
    mod = _ada(c, ada_w, ada_b)
    for l in range(ada_w.shape[0]):
        x = _layer(x, mod[l], pre_g[l], post_g[l], w_in[l], b_f[l], conv_w[l], conv_b[l],
                   lru_wa[l], lru_ba[l], lru_wx[l], lru_bx[l], lru_lambda[l],
                   sg_ln_g[l], sg_ln_b[l], sg_w[l], sg_b[l], w_out[l])
    return x
```

```python
import math

import jax
import jax.numpy as jnp
from jax import lax
from jax.experimental import pallas as pl
from jax.experimental.pallas import tpu as pltpu

D_MODEL = 1024
HEAD_DIM = 64
D_ATT = 512
D_LRU = 256
D_SG = 256
N_HEADS = D_ATT // HEAD_DIM
N_LRU_BLOCKS = D_LRU // HEAD_DIM
N_SG_GROUPS = D_SG // HEAD_DIM
SG_CHUNK = 128
CONV_WIDTH = 4
LRU_C = 8.0
EPS = 1e-6
D_REST = D_ATT + 2 * D_LRU + 3 * D_SG

LANES = 128
SUBLANES = 8
BF16_ROWS = 16
VMEM_LIMIT = 56 * 1024 * 1024

TM = 512
TQ = 512
TL = 512
TC = 512
TN_ADA = 512
NEG = -1e30

N_PARTS = 3
N_AUG = N_HEADS * N_PARTS
LOG2E = 1.4426950408889634
Q_SCALE = LOG2E / math.sqrt(HEAD_DIM)
V_ROWS = HEAD_DIM + BF16_ROWS

F32 = jnp.float32
BF16 = jnp.bfloat16


def _sigmoid(x):
    return 1.0 / (1.0 + jnp.exp(-x))


def _silu(x):
    return x * _sigmoid(x)


def _gelu_tanh(x):
    c = math.sqrt(2.0 / math.pi)
    return 0.5 * x * (1.0 + jnp.tanh(c * (x + 0.044715 * (x * x * x))))


def _log_sigmoid(x):
    return jnp.minimum(x, 0.0) - jnp.log1p(jnp.exp(-jnp.abs(x)))


def _split3(x, sel):
    hi = x.astype(BF16).astype(F32)
    r1 = x - hi
    mid = r1.astype(BF16).astype(F32)
    lo = r1 - mid
    return jnp.where(sel == 0, hi, jnp.where(sel == 1, mid, lo))


def _params(sem):
    return pltpu.CompilerParams(dimension_semantics=sem, vmem_limit_bytes=VMEM_LIMIT)


def _ada_kernel(ct_ref, w_ref, b_ref, o_ref):
    ct = ct_ref[...]
    sc = _silu(ct)
    w = w_ref[0]
    bias = b_ref[0]
    for b in range(ct.shape[1]):
        col = sc[:, b:b + 1]
        o_ref[0, b:b + 1, :] = jnp.sum(col * w, axis=0, keepdims=True) + bias


def _ada(c, ada_w, ada_b):
    L, D, N = ada_w.shape
    B = c.shape[0]
    return pl.pallas_call(
        _ada_kernel,
        grid=(L, N // TN_ADA),
        in_specs=[
            pl.BlockSpec((D, B), lambda l, n: (0, 0)),
            pl.BlockSpec((1, D, TN_ADA), lambda l, n: (l, 0, n)),
            pl.BlockSpec((1, 1, TN_ADA), lambda l, n: (l, 0, n)),
        ],
        out_specs=pl.BlockSpec((1, B, TN_ADA), lambda l, n: (l, 0, n)),
        out_shape=jax.ShapeDtypeStruct((L, B, N), F32),
        compiler_params=_params(("parallel", "parallel")),
        name="ada_mod",
    )(c.T, ada_w, ada_b.reshape(L, 1, N))


def _inproj_kernel(x_ref, shift_ref, scale_ref, g_ref, wt_ref, wk_ref, wf_ref, wr_ref, bf_ref,
                   qt_ref, vt_ref, k_ref, ka_ref, qa_ref, rest_ref, fc_ref):
    i = pl.program_id(1)

    @pl.when(i == 0)
    def _():
        fc_ref[...] = jnp.zeros(fc_ref.shape, F32)

    x = x_ref[0]
    ms = jnp.mean(x * x, axis=-1, keepdims=True)
    y = (x * lax.rsqrt(ms + EPS)) * g_ref[...]
    h = y * (1.0 + scale_ref[0]) + shift_ref[0]
    hb = h.astype(BF16)

    tr = lax.dot_general(wt_ref[...], hb, (((1,), (1,)), ((), ())), preferred_element_type=F32)
    qt_ref[0] = (tr[:D_ATT] * Q_SCALE).astype(BF16)
    vt_ref[0] = tr[D_ATT:].astype(BF16)
    k_ref[0] = jnp.dot(hb, wk_ref[...], preferred_element_type=F32).astype(BF16)
    rest_ref[0] = jnp.dot(hb, wr_ref[...], preferred_element_type=F32)

    ls = _log_sigmoid(jnp.dot(hb, wf_ref[...], preferred_element_type=F32) + bf_ref[...])
    row8 = lax.broadcasted_iota(jnp.int32, (TM, LANES), 0) & (SUBLANES - 1)
    d = 1
    while d < SUBLANES:
        ls = ls + jnp.where(row8 >= d, pltpu.roll(ls, d, axis=0), 0.0)
        d *= 2
    carry = fc_ref[...]
    groups = []
    for g in range(TM // SUBLANES):
        fg = ls[g * SUBLANES:(g + 1) * SUBLANES] + carry
        carry = fg[SUBLANES - 1:SUBLANES, :]
        groups.append(fg)
    fc_ref[...] = carry
    f2 = jnp.concatenate(groups, axis=0) * LOG2E

    lane = lax.broadcasted_iota(jnp.int32, (TM, LANES), 1)
    parts = _split3(f2, lane % N_PARTS)
    ka_ref[0] = jnp.where(lane < N_AUG, -parts, jnp.where(lane < 2 * N_AUG, 1.0, 0.0)).astype(BF16)

    ft = f2.T[:N_AUG]
    rowt = lax.broadcasted_iota(jnp.int32, (N_AUG, TM), 0)
    parts_t = _split3(ft, rowt % N_PARTS)
    qa_ref[0] = jnp.concatenate(
        [jnp.ones((N_AUG, TM), F32), parts_t, jnp.zeros((LANES - 2 * N_AUG, TM), F32)], axis=0).astype(BF16)


def _inproj(x, shift, scale, pre_g, wt, wk, wf, wr, b_f):
    B, S, D = x.shape
    const = lambda shape: pl.BlockSpec(shape, lambda b, i: (0,) * len(shape))
    return pl.pallas_call(
        _inproj_kernel,
        grid=(B, S // TM),
        in_specs=[
            pl.BlockSpec((1, TM, D), lambda b, i: (b, i, 0)),
            pl.BlockSpec((1, 1, D), lambda b, i: (b, 0, 0)),
            pl.BlockSpec((1, 1, D), lambda b, i: (b, 0, 0)),
            const((1, D)),
            const((2 * D_ATT, D)),
            const((D, D_ATT)),
            const((D, LANES)),
            const((D, D_REST)),
            const((1, LANES)),
        ],
        out_specs=[
            pl.BlockSpec((1, D_ATT, TM), lambda b, i: (b, 0, i)),
            pl.BlockSpec((1, D_ATT, TM), lambda b, i: (b, 0, i)),
            pl.BlockSpec((1, TM, D_ATT), lambda b, i: (b, i, 0)),
            pl.BlockSpec((1, TM, LANES), lambda b, i: (b, i, 0)),
            pl.BlockSpec((1, LANES, TM), lambda b, i: (b, 0, i)),
            pl.BlockSpec((1, TM, D_REST), lambda b, i: (b, i, 0)),
        ],
        out_shape=[
            jax.ShapeDtypeStruct((B, D_ATT, S), BF16),
            jax.ShapeDtypeStruct((B, D_ATT, S), BF16),
            jax.ShapeDtypeStruct((B, S, D_ATT), BF16),
            jax.ShapeDtypeStruct((B, S, LANES), BF16),
            jax.ShapeDtypeStruct((B, LANES, S), BF16),
            jax.ShapeDtypeStruct((B, S, D_REST), F32),
        ],
        scratch_shapes=[pltpu.VMEM((1, LANES), F32)],
        compiler_params=_params(("parallel", "arbitrary")),
        name="inproj",
    )(x, shift, scale, pre_g, wt, wk, wf, wr, b_f)


def _attn_kernel(qt_ref, qa_ref, k_ref, ka_ref, vt_ref, g_ref, o_ref, m_ref, acc_ref):
    p = pl.program_id(1)
    i = pl.program_id(2)
    qt = qt_ref[0]
    qa = qa_ref[0]
    row = lax.broadcasted_iota(jnp.int32, (LANES, TQ), 0)
    zero = jnp.zeros_like(qt)
    rhs = []
    for hh in range(2):
        head = 2 * p + hh
        own = (row >= hh * HEAD_DIM) & (row < (hh + 1) * HEAD_DIM)
        lo = head * N_PARTS
        aug = ((row >= lo) & (row < lo + N_PARTS)) | ((row >= N_AUG + lo) & (row < N_AUG + lo + N_PARTS))
        rhs.append(jnp.concatenate([jnp.where(own, qt, zero), jnp.where(aug, qa, zero)], axis=0))
    ones_rows = jnp.where(lax.broadcasted_iota(jnp.int32, (BF16_ROWS, TQ), 0) == 0, 1.0, 0.0).astype(BF16)
    m_ref[...] = jnp.full(m_ref.shape, NEG, F32)
    acc_ref[...] = jnp.zeros(acc_ref.shape, F32)

    def tile(j, masked):
        start = pl.multiple_of(j * TQ, TQ)
        lhs = jnp.concatenate([k_ref[0, pl.ds(start, TQ), :], ka_ref[0, pl.ds(start, TQ), :]], axis=1)
        vt = vt_ref[0, :, pl.ds(start, TQ)]
        for hh in range(2):
            st = jnp.dot(lhs, rhs[hh], preferred_element_type=F32)
            if masked:
                kk = lax.broadcasted_iota(jnp.int32, (TQ, TQ), 0)
                qq = lax.broadcasted_iota(jnp.int32, (TQ, TQ), 1)
                st = jnp.where(kk <= qq, st, NEG)
            m_old = m_ref[hh]
            m_new = jnp.maximum(m_old, jnp.max(st, axis=0, keepdims=True))
            pt = jnp.exp2(st - m_new).astype(BF16)
            alpha = jnp.exp2(m_old - m_new)
            vaug = jnp.concatenate([vt[hh * HEAD_DIM:(hh + 1) * HEAD_DIM], ones_rows], axis=0)
            acc_ref[hh] = acc_ref[hh] * alpha + jnp.dot(vaug, pt, preferred_element_type=F32)
            m_ref[hh] = m_new

    def body(j, carry):
        tile(j, False)
        return carry

    lax.fori_loop(0, i, body, 0)
    tile(i, True)
    outs = []
    for hh in range(2):
        a = acc_ref[hh]
        outs.append(a[:HEAD_DIM] / a[HEAD_DIM:HEAD_DIM + 1])
    o = jnp.concatenate(outs, axis=0).T
    o_ref[0] = (o * _silu(g_ref[0])).astype(BF16)


def _attention(qt, qa, k, ka, vt, rest):
    B, S, _ = k.shape
    n_pair = N_HEADS // 2
    return pl.pallas_call(
        _attn_kernel,
        grid=(B, n_pair, S // TQ),
        in_specs=[
            pl.BlockSpec((1, LANES, TQ), lambda b, p, i: (b, p, i)),
            pl.BlockSpec((1, LANES, TQ), lambda b, p, i: (b, 0, i)),
            pl.BlockSpec((1, S, LANES), lambda b, p, i: (b, 0, p)),
            pl.BlockSpec((1, S, LANES), lambda b, p, i: (b, 0, 0)),
            pl.BlockSpec((1, LANES, S), lambda b, p, i: (b, p, 0)),
            pl.BlockSpec((1, TQ, LANES), lambda b, p, i: (b, i, p)),
        ],
        out_specs=pl.BlockSpec((1, TQ, LANES), lambda b, p, i: (b, i, p)),
        out_shape=jax.ShapeDtypeStruct((B, S, D_ATT), BF16),
        scratch_shapes=[
            pltpu.VMEM((2, 1, TQ), F32),
            pltpu.VMEM((2, V_ROWS, TQ), F32),
        ],
        compiler_params=_params(("parallel", "parallel", "parallel")),
        name="fox_attention",
    )(qt, qa, k, ka, vt, rest)


def _lru_kernel(x_ref, g_ref, cw_ref, cb_ref, wa_ref, ba_ref, wx_ref, bx_ref, lam_ref, o_ref,
                xbuf_ref, h_ref):
    i = pl.program_id(1)

    @pl.when(i == 0)
    def _():
        xbuf_ref[0:SUBLANES, :] = jnp.zeros((SUBLANES, D_LRU), F32)
        h_ref[...] = jnp.zeros(h_ref.shape, F32)

    x = x_ref[0]
    xbuf_ref[SUBLANES:SUBLANES + TL, :] = x
    xc = cb_ref[...]
    for kk in range(CONV_WIDTH):
        off = SUBLANES - (CONV_WIDTH - 1) + kk
        xc = xc + xbuf_ref[off:off + TL, :] * cw_ref[kk:kk + 1, :]
    xbuf_ref[0:SUBLANES, :] = x[TL - SUBLANES:, :]

    xcb = xc.astype(BF16)
    r = _sigmoid(jnp.dot(xcb, wa_ref[...], preferred_element_type=F32) + ba_ref[...])
    ig = _sigmoid(jnp.dot(xcb, wx_ref[...], preferred_element_type=F32) + bx_ref[...])
    nlam = -lam_ref[...]
    softplus = jnp.maximum(nlam, 0.0) + jnp.log1p(jnp.exp(-jnp.abs(nlam)))
    log_a = (-LRU_C * r) * softplus
    a = jnp.exp(log_a)
    bt = jnp.sqrt(jnp.tanh(-log_a) * (1.0 + a * a)) * (ig * xc)

    row8 = lax.broadcasted_iota(jnp.int32, (TL, D_LRU), 0) & (SUBLANES - 1)
    d = 1
    while d < SUBLANES:
        valid = row8 >= d
        a_s = jnp.where(valid, pltpu.roll(a, d, axis=0), 1.0)
        b_s = jnp.where(valid, pltpu.roll(bt, d, axis=0), 0.0)
        bt = bt + a * b_s
        a = a * a_s
        d *= 2

    h_prev = h_ref[...]
    sg = _silu(g_ref[0])
    for g in range(TL // SUBLANES):
        lo = g * SUBLANES
        hg = bt[lo:lo + SUBLANES] + a[lo:lo + SUBLANES] * h_prev
        h_prev = hg[SUBLANES - 1:SUBLANES, :]
        o_ref[0, lo:lo + SUBLANES, :] = (hg * sg[lo:lo + SUBLANES]).astype(BF16)
    h_ref[...] = h_prev


def _lru(rest, conv_w, conv_b, wa_bd, ba, wx_bd, bx, lam):
    B, S, _ = rest.shape
    const = lambda shape: pl.BlockSpec(shape, lambda b, i: (0,) * len(shape))
    x_col = D_ATT // D_LRU
    return pl.pallas_call(
        _lru_kernel,
        grid=(B, S // TL),
        in_specs=[
            pl.BlockSpec((1, TL, D_LRU), lambda b, i: (b, i, x_col)),
            pl.BlockSpec((1, TL, D_LRU), lambda b, i: (b, i, x_col + 1)),
            const((CONV_WIDTH, D_LRU)),
            const((1, D_LRU)),
            const((D_LRU, D_LRU)),
            const((1, D_LRU)),
            const((D_LRU, D_LRU)),
            const((1, D_LRU)),
            const((1, D_LRU)),
        ],
        out_specs=pl.BlockSpec((1, TL, D_LRU), lambda b, i: (b, i, 0)),
        out_shape=jax.ShapeDtypeStruct((B, S, D_LRU), BF16),
        scratch_shapes=[
            pltpu.VMEM((TL + SUBLANES, D_LRU), F32),
            pltpu.VMEM((1, D_LRU), F32),
        ],
        compiler_params=_params(("parallel", "arbitrary")),
        name="rg_lru",
    )(rest, rest, conv_w, conv_b, wa_bd, ba, wx_bd, bx, lam)


def _sg_kernel(u_ref, v_ref, g_ref, lng_ref, lnb_ref, w_ref, bias_ref, o_ref):
    u = _gelu_tanh(u_ref[0])
    v = _gelu_tanh(v_ref[0])
    mu = jnp.mean(v, axis=-1, keepdims=True)
    var = jnp.mean(jnp.square(v - mu), axis=-1, keepdims=True)
    vn = ((v - mu) * lax.rsqrt(var + EPS)) * lng_ref[...] + lnb_ref[...]
    vnb = vn.astype(BF16)
    t_idx = lax.broadcasted_iota(jnp.int32, w_ref.shape, 0) & (SG_CHUNK - 1)
    s_idx = lax.broadcasted_iota(jnp.int32, w_ref.shape, 1)
    w = jnp.where(s_idx <= t_idx, w_ref[...], jnp.zeros(w_ref.shape, BF16))
    grp = lax.broadcasted_iota(jnp.int32, (SG_CHUNK, D_SG), 1) // HEAD_DIM
    sg = _silu(g_ref[0])
    for c in range(TC // SG_CHUNK):
        lo = c * SG_CHUNK
        zz = jnp.dot(w, vnb[lo:lo + SG_CHUNK], preferred_element_type=F32)
        z = zz[0:SG_CHUNK]
        for gi in range(1, N_SG_GROUPS):
            z = jnp.where(grp == gi, zz[gi * SG_CHUNK:(gi + 1) * SG_CHUNK], z)
        z = z + bias_ref[...]
        o_ref[0, lo:lo + SG_CHUNK, :] = ((u[lo:lo + SG_CHUNK] * z) * sg[lo:lo + SG_CHUNK]).astype(BF16)


def _spatial_gate(rest, ln_g, ln_b, w_all, bias):
    B, S, _ = rest.shape
    const = lambda shape: pl.BlockSpec(shape, lambda b, i: (0,) * len(shape))
    u_col = (D_ATT + 2 * D_LRU) // D_SG
    return pl.pallas_call(
        _sg_kernel,
        grid=(B, S // TC),
        in_specs=[
            pl.BlockSpec((1, TC, D_SG), lambda b, i: (b, i, u_col)),
            pl.BlockSpec((1, TC, D_SG), lambda b, i: (b, i, u_col + 1)),
            pl.BlockSpec((1, TC, D_SG), lambda b, i: (b, i, u_col + 2)),
            const((1, D_SG)),
            const((1, D_SG)),
            const((N_SG_GROUPS * SG_CHUNK, SG_CHUNK)),
            const((SG_CHUNK, D_SG)),
        ],
        out_specs=pl.BlockSpec((1, TC, D_SG), lambda b, i: (b, i, 0)),
        out_shape=jax.ShapeDtypeStruct((B, S, D_SG), BF16),
        compiler_params=_params(("parallel", "parallel")),
        name="spatial_gate",
    )(rest, rest, rest, ln_g, ln_b, w_all, bias)


def _outproj_kernel(ya_ref, yl_ref, ys_ref, x_ref, gate_ref, pg_ref, w_ref, o_ref):
    y = jnp.dot(ya_ref[0], w_ref[0:D_ATT, :], preferred_element_type=F32)
    y = y + jnp.dot(yl_ref[0], w_ref[D_ATT:D_ATT + D_LRU, :], preferred_element_type=F32)
    y = y + jnp.dot(ys_ref[0], w_ref[D_ATT + D_LRU:, :], preferred_element_type=F32)
    ms = jnp.mean(y * y, axis=-1, keepdims=True)
    yn = (y * lax.rsqrt(ms + EPS)) * pg_ref[...]
    o_ref[0] = x_ref[0] + gate_ref[0] * yn


def _outproj(ya, yl, ys, x, gate, post_g, w_out):
    B, S, D = x.shape
    const = lambda shape: pl.BlockSpec(shape, lambda b, i: (0,) * len(shape))
    return pl.pallas_call(
        _outproj_kernel,
        grid=(B, S // TM),
        in_specs=[
            pl.BlockSpec((1, TM, D_ATT), lambda b, i: (b, i, 0)),
            pl.BlockSpec((1, TM, D_LRU), lambda b, i: (b, i, 0)),
            pl.BlockSpec((1, TM, D_SG), lambda b, i: (b, i, 0)),
            pl.BlockSpec((1, TM, D), lambda b, i: (b, i, 0)),
            pl.BlockSpec((1, 1, D), lambda b, i: (b, 0, 0)),
            const((1, D)),
            const((D, D)),
        ],
        out_specs=pl.BlockSpec((1, TM, D), lambda b, i: (b, i, 0)),
        out_shape=jax.ShapeDtypeStruct((B, S, D), F32),
        compiler_params=_params(("parallel", "parallel")),
        name="outproj",
    )(ya, yl, ys, x, gate, post_g, w_out)


def _block_diag(w):
    g, n, _ = w.shape
    out = jnp.zeros((g * n, g * n), w.dtype)
    for k in range(g):
        out = out.at[k * n:(k + 1) * n, k * n:(k + 1) * n].set(w[k])
    return out


def _layer(x, mod, pre_g, post_g, w_in, b_f, conv_w, conv_b, lru_wa, lru_ba, lru_wx, lru_bx,
           lru_lambda, sg_ln_g, sg_ln_b, sg_w, sg_b, w_out):
    B, S, D = x.shape
    shift = mod[:, None, 0:D]
    scale = mod[:, None, D:2 * D]
    gate = mod[:, None, 2 * D:3 * D]

    o = 0
    wq = w_in[:, o:o + D_ATT]
    o += D_ATT
    wk = w_in[:, o:o + D_ATT].astype(BF16)
    o += D_ATT
    wv = w_in[:, o:o + D_ATT]
    o += D_ATT
    wf = w_in[:, o:o + N_HEADS]
    o += N_HEADS
    wr = w_in[:, o:].astype(BF16)
    wt = jnp.concatenate([wq, wv], axis=1).T.astype(BF16)
    pad = jnp.zeros((D, LANES - N_AUG), F32)
    wf3 = jnp.concatenate([jnp.repeat(wf, N_PARTS, axis=1), pad], axis=1).astype(BF16)
    bf3 = jnp.concatenate([jnp.repeat(b_f, N_PARTS), jnp.zeros((LANES - N_AUG,), F32)])[None, :]

    qt, vt, k, ka, qa, rest = _inproj(x, shift, scale, pre_g[None, :], wt, wk, wf3, wr, bf3)
    ya = _attention(qt, qa, k, ka, vt, rest)

    yl = _lru(rest, conv_w, conv_b[None, :], _block_diag(lru_wa).astype(BF16), lru_ba[None, :],
              _block_diag(lru_wx).astype(BF16), lru_bx[None, :], lru_lambda[None, :])

    w_all = sg_w.reshape(N_SG_GROUPS * SG_CHUNK, SG_CHUNK).astype(BF16)
    bias = jnp.repeat(sg_b.T, HEAD_DIM, axis=1)
    ys = _spatial_gate(rest, sg_ln_g[None, :], sg_ln_b[None, :], w_all, bias)

    return _outproj(ya, yl, ys, x, gate, post_g[None, :], w_out.astype(BF16))


def kernel(x, c, ada_w, ada_b, pre_g, post_g, w_in, b_f, conv_w, conv_b, lru_wa, lru_ba, lru_wx,
           lru_bx, lru_lambda, sg_ln_g, sg_ln_b, sg_w, sg_b, w_out):
    mod = _ada(c, ada_w, ada_b)
    for l in range(ada_w.shape[0]):
        x = _layer(x, mod[l], pre_g[l], post_g[l], w_in[l], b_f[l], conv_w[l], conv_b[l],
                   lru_wa[l], lru_ba[l], lru_wx[l], lru_bx[l], lru_lambda[l],
                   sg_ln_g[l], sg_ln_b[l], sg_w[l], sg_b[l], w_out[l])
    return x
```

```python
import math

import jax
import jax.numpy as jnp
from jax import lax
from jax.experimental import pallas as pl
from jax.experimental.pallas import tpu as pltpu

D_MODEL = 1024
HEAD_DIM = 64
D_ATT = 512
D_LRU = 256
D_SG = 256
N_HEADS = D_ATT // HEAD_DIM
N_LRU_BLOCKS = D_LRU // HEAD_DIM
N_SG_GROUPS = D_SG // HEAD_DIM
SG_CHUNK = 128
CONV_WIDTH = 4
LRU_C = 8.0
EPS = 1e-6
D_REST = D_ATT + 2 * D_LRU + 3 * D_SG

LANES = 128
SUBLANES = 8
BF16_ROWS = 16
VMEM_LIMIT = 56 * 1024 * 1024

TM = 512
TQ = 512
TKC = 256
TL = 512
TC = 512
TN_ADA = 512
NEG = -1e30

N_PARTS = 3
N_AUG = N_HEADS * N_PARTS
LOG2E = 1.4426950408889634
Q_SCALE = LOG2E / math.sqrt(HEAD_DIM)
V_ROWS = HEAD_DIM + BF16_ROWS

F32 = jnp.float32
BF16 = jnp.bfloat16


def _sigmoid(x):
    return 1.0 / (1.0 + jnp.exp(-x))


def _silu(x):
    return x * _sigmoid(x)


def _gelu_tanh(x):
    c = math.sqrt(2.0 / math.pi)
    return 0.5 * x * (1.0 + jnp.tanh(c * (x + 0.044715 * (x * x * x))))


def _log_sigmoid(x):
    return jnp.minimum(x, 0.0) - jnp.log1p(jnp.exp(-jnp.abs(x)))


def _split3(x, sel):
    hi = x.astype(BF16).astype(F32)
    r1 = x - hi
    mid = r1.astype(BF16).astype(F32)
    lo = r1 - mid
    return jnp.where(sel == 0, hi, jnp.where(sel == 1, mid, lo))


def _params(sem):
    return pltpu.CompilerParams(dimension_semantics=sem, vmem_limit_bytes=VMEM_LIMIT)


def _ada_kernel(ct_ref, w_ref, b_ref, o_ref):
    ct = ct_ref[...]
    sc = _silu(ct)
    w = w_ref[0]
    bias = b_ref[0]
    for b in range(ct.shape[1]):
        col = sc[:, b:b + 1]
        o_ref[0, b:b + 1, :] = jnp.sum(col * w, axis=0, keepdims=True) + bias


def _ada(c, ada_w, ada_b):
    L, D, N = ada_w.shape
    B = c.shape[0]
    return pl.pallas_call(
        _ada_kernel,
        grid=(L, N // TN_ADA),
        in_specs=[
            pl.BlockSpec((D, B), lambda l, n: (0, 0)),
            pl.BlockSpec((1, D, TN_ADA), lambda l, n: (l, 0, n)),
            pl.BlockSpec((1, 1, TN_ADA), lambda l, n: (l, 0, n)),
        ],
        out_specs=pl.BlockSpec((1, B, TN_ADA), lambda l, n: (l, 0, n)),
        out_shape=jax.ShapeDtypeStruct((L, B, N), F32),
        compiler_params=_params(("parallel", "parallel")),
        name="ada_mod",
    )(c.T, ada_w, ada_b.reshape(L, 1, N))


def _inproj_kernel(x_ref, shift_ref, scale_ref, g_ref, wt_ref, wk_ref, wf_ref, wr_ref, bf_ref,
                   qt_ref, vt_ref, k_ref, ka_ref, qa_ref, rest_ref, fc_ref):
    i = pl.program_id(1)

    @pl.when(i == 0)
    def _():
        fc_ref[...] = jnp.zeros(fc_ref.shape, F32)

    x = x_ref[0]
    ms = jnp.mean(x * x, axis=-1, keepdims=True)
    y = (x * lax.rsqrt(ms + EPS)) * g_ref[...]
    h = y * (1.0 + scale_ref[0]) + shift_ref[0]
    hb = h.astype(BF16)

    tr = lax.dot_general(wt_ref[...], hb, (((1,), (1,)), ((), ())), preferred_element_type=F32)
    qt_ref[0] = (tr[:D_ATT] * Q_SCALE).astype(BF16)
    vt_ref[0] = tr[D_ATT:].astype(BF16)
    k_ref[0] = jnp.dot(hb, wk_ref[...], preferred_element_type=F32).astype(BF16)
    rest_ref[0] = jnp.dot(hb, wr_ref[...], preferred_element_type=F32)

    ls = _log_sigmoid(jnp.dot(hb, wf_ref[...], preferred_element_type=F32) + bf_ref[...])
    row8 = lax.broadcasted_iota(jnp.int32, (TM, LANES), 0) & (SUBLANES - 1)
    d = 1
    while d < SUBLANES:
        ls = ls + jnp.where(row8 >= d, pltpu.roll(ls, d, axis=0), 0.0)
        d *= 2
    carry = fc_ref[...]
    groups = []
    for g in range(TM // SUBLANES):
        fg = ls[g * SUBLANES:(g + 1) * SUBLANES] + carry
        carry = fg[SUBLANES - 1:SUBLANES, :]
        groups.append(fg)
    fc_ref[...] = carry
    f2 = jnp.concatenate(groups, axis=0) * LOG2E

    lane = lax.broadcasted_iota(jnp.int32, (TM, LANES), 1)
    parts = _split3(f2, lane % N_PARTS)
    ka_ref[0] = jnp.where(lane < N_AUG, -parts, jnp.where(lane < 2 * N_AUG, 1.0, 0.0)).astype(BF16)

    ft = f2.T[:N_AUG]
    rowt = lax.broadcasted_iota(jnp.int32, (N_AUG, TM), 0)
    parts_t = _split3(ft, rowt % N_PARTS)
    qa_ref[0] = jnp.concatenate(
        [jnp.ones((N_AUG, TM), F32), parts_t, jnp.zeros((LANES - 2 * N_AUG, TM), F32)], axis=0).astype(BF16)


def _inproj(x, shift, scale, pre_g, wt, wk, wf, wr, b_f):
    B, S, D = x.shape
    const = lambda shape: pl.BlockSpec(shape, lambda b, i: (0,) * len(shape))
    return pl.pallas_call(
        _inproj_kernel,
        grid=(B, S // TM),
        in_specs=[
            pl.BlockSpec((1, TM, D), lambda b, i: (b, i, 0)),
            pl.BlockSpec((1, 1, D), lambda b, i: (b, 0, 0)),
            pl.BlockSpec((1, 1, D), lambda b, i: (b, 0, 0)),
            const((1, D)),
            const((2 * D_ATT, D)),
            const((D, D_ATT)),
            const((D, LANES)),
            const((D, D_REST)),
            const((1, LANES)),
        ],
        out_specs=[
            pl.BlockSpec((1, D_ATT, TM), lambda b, i: (b, 0, i)),
            pl.BlockSpec((1, D_ATT, TM), lambda b, i: (b, 0, i)),
            pl.BlockSpec((1, TM, D_ATT), lambda b, i: (b, i, 0)),
            pl.BlockSpec((1, TM, LANES), lambda b, i: (b, i, 0)),
            pl.BlockSpec((1, LANES, TM), lambda b, i: (b, 0, i)),
            pl.BlockSpec((1, TM, D_REST), lambda b, i: (b, i, 0)),
        ],
        out_shape=[
            jax.ShapeDtypeStruct((B, D_ATT, S), BF16),
            jax.ShapeDtypeStruct((B, D_ATT, S), BF16),
            jax.ShapeDtypeStruct((B, S, D_ATT), BF16),
            jax.ShapeDtypeStruct((B, S, LANES), BF16),
            jax.ShapeDtypeStruct((B, LANES, S), BF16),
            jax.ShapeDtypeStruct((B, S, D_REST), F32),
        ],
        scratch_shapes=[pltpu.VMEM((1, LANES), F32)],
        compiler_params=_params(("parallel", "arbitrary")),
        name="inproj",
    )(x, shift, scale, pre_g, wt, wk, wf, wr, b_f)


def _attn_kernel(qt_ref, qa_ref, k_ref, ka_ref, vt_ref, g_ref, o_ref,
                 m_ref, acc_ref, rhs_ref, s0_ref, s1_ref, p0_ref, p1_ref, mx0_ref, mx1_ref, al0_ref, al1_ref):
    p = pl.program_id(1)
    i = pl.program_id(2)
    qt = qt_ref[0]
    qa = qa_ref[0]
    row = lax.broadcasted_iota(jnp.int32, (LANES, TQ), 0)
    zero = jnp.zeros_like(qt)
    rhs = []
    for hh in range(2):
        head = 2 * p + hh
        own = (row >= hh * HEAD_DIM) & (row < (hh + 1) * HEAD_DIM)
        lo = head * N_PARTS
        aug = ((row >= lo) & (row < lo + N_PARTS)) | ((row >= N_AUG + lo) & (row < N_AUG + lo + N_PARTS))
        rhs.append(jnp.concatenate([jnp.where(own, qt, zero), jnp.where(aug, qa, zero)], axis=0))
    for hh in range(2):
        rhs_ref[hh] = rhs[hh]
    ones_rows = jnp.where(lax.broadcasted_iota(jnp.int32, (BF16_ROWS, TQ), 0) == 0, 1.0, 0.0).astype(BF16)
    m_ref[...] = jnp.full(m_ref.shape, NEG, F32)
    acc_ref[...] = jnp.zeros(acc_ref.shape, F32)
    s_bufs = (s0_ref, s1_ref)
    p_bufs = (p0_ref, p1_ref)
    mx_bufs = (mx0_ref, mx1_ref)
    al_bufs = (al0_ref, al1_ref)

    def scores(t, slot, masked):
        rows = pl.ds(pl.multiple_of(t * TQ, TQ), TQ)
        lhs = jnp.concatenate([k_ref[0, rows, :], ka_ref[0, rows, :]], axis=1)
        for hh in range(2):
            st = jnp.dot(lhs, rhs_ref[hh], preferred_element_type=F32)
            if masked:
                kk = lax.broadcasted_iota(jnp.int32, (TQ, TQ), 0)
                qq = lax.broadcasted_iota(jnp.int32, (TQ, TQ), 1)
                st = jnp.where(kk <= qq, st, NEG)
            s_bufs[slot][hh] = st
            mx_bufs[slot][hh] = jnp.max(st, axis=0, keepdims=True)

    def softmax(slot):
        for hh in range(2):
            m_old = m_ref[hh]
            m_new = jnp.maximum(m_old, mx_bufs[slot][hh])
            p_bufs[slot][hh] = jnp.exp2(s_bufs[slot][hh] - m_new).astype(BF16)
            al_bufs[slot][hh] = jnp.exp2(m_old - m_new)
            m_ref[hh] = m_new

    def values(t, slot):
        cols = pl.ds(pl.multiple_of(t * TQ, TQ), TQ)
        for hh in range(2):
            vaug = jnp.concatenate([vt_ref[0, hh * HEAD_DIM:(hh + 1) * HEAD_DIM, cols], ones_rows], axis=0)
            acc_ref[hh] = acc_ref[hh] * al_bufs[slot][hh] + jnp.dot(vaug, p_bufs[slot][hh],
                                                                     preferred_element_type=F32)

    def step(tau, parity, do_scores=False, masked=False, do_softmax=False, do_values=False):
        if do_softmax:
            softmax(1 - parity)
        if do_scores:
            scores(tau, parity, masked)
        if do_values:
            values(tau - 2, parity)

    full = dict(do_scores=True, do_softmax=True, do_values=True)

    @pl.when(i == 0)
    def _():
        step(0, 0, do_scores=True, masked=True)
        step(1, 1, do_softmax=True)
        step(2, 0, do_values=True)

    @pl.when(i == 1)
    def _():
        step(0, 0, do_scores=True)
        step(1, 1, do_scores=True, masked=True, do_softmax=True)
        step(2, 0, do_softmax=True, do_values=True)
        step(3, 1, do_values=True)

    @pl.when(i >= 2)
    def _():
        step(0, 0, do_scores=True)
        step(1, 1, do_scores=True, do_softmax=True)

        def body(u, carry):
            step(2 + 2 * u, 0, **full)
            step(3 + 2 * u, 1, **full)
            return carry

        lax.fori_loop(0, lax.shift_right_logical(i - 2, 1), body, 0)

        @pl.when((i & 1) == 0)
        def _():
            step(i, 0, masked=True, **full)
            step(i + 1, 1, do_softmax=True, do_values=True)
            step(i + 2, 0, do_values=True)

        @pl.when((i & 1) == 1)
        def _():
            step(i - 1, 0, **full)
            step(i, 1, masked=True, **full)
            step(i + 1, 0, do_softmax=True, do_values=True)
            step(i + 2, 1, do_values=True)

    outs = []
    for hh in range(2):
        a = acc_ref[hh]
        outs.append(a[:HEAD_DIM] / a[HEAD_DIM:HEAD_DIM + 1])
    o = jnp.concatenate(outs, axis=0).T
    o_ref[0] = (o * _silu(g_ref[0])).astype(BF16)


def _attention(qt, qa, k, ka, vt, rest):
    B, S, _ = k.shape
    n_pair = N_HEADS // 2
    return pl.pallas_call(
        _attn_kernel,
        grid=(B, n_pair, S // TQ),
        in_specs=[
            pl.BlockSpec((1, LANES, TQ), lambda b, p, i: (b, p, i)),
            pl.BlockSpec((1, LANES, TQ), lambda b, p, i: (b, 0, i)),
            pl.BlockSpec((1, S, LANES), lambda b, p, i: (b, 0, p)),
            pl.BlockSpec((1, S, LANES), lambda b, p, i: (b, 0, 0)),
            pl.BlockSpec((1, LANES, S), lambda b, p, i: (b, p, 0)),
            pl.BlockSpec((1, TQ, LANES), lambda b, p, i: (b, i, p)),
        ],
        out_specs=pl.BlockSpec((1, TQ, LANES), lambda b, p, i: (b, i, p)),
        out_shape=jax.ShapeDtypeStruct((B, S, D_ATT), BF16),
        scratch_shapes=[
            pltpu.VMEM((2, 1, TQ), F32),
            pltpu.VMEM((2, V_ROWS, TQ), F32),
            pltpu.VMEM((2, 2 * LANES, TQ), BF16),
            pltpu.VMEM((2, TQ, TQ), F32),
            pltpu.VMEM((2, TQ, TQ), F32),
            pltpu.VMEM((2, TQ, TQ), BF16),
            pltpu.VMEM((2, TQ, TQ), BF16),
            pltpu.VMEM((2, 1, TQ), F32),
            pltpu.VMEM((2, 1, TQ), F32),
            pltpu.VMEM((2, 1, TQ), F32),
            pltpu.VMEM((2, 1, TQ), F32),
        ],
        compiler_params=_params(("parallel", "parallel", "parallel")),
        name="fox_attention",
    )(qt, qa, k, ka, vt, rest)


def _lru_kernel(x_ref, g_ref, cw_ref, cb_ref, wa_ref, ba_ref, wx_ref, bx_ref, lam_ref, o_ref,
                xbuf_ref, h_ref):
    i = pl.program_id(1)

    @pl.when(i == 0)
    def _():
        xbuf_ref[0:SUBLANES, :] = jnp.zeros((SUBLANES, D_LRU), F32)
        h_ref[...] = jnp.zeros(h_ref.shape, F32)

    x = x_ref[0]
    xbuf_ref[SUBLANES:SUBLANES + TL, :] = x
    xc = cb_ref[...]
    for kk in range(CONV_WIDTH):
        off = SUBLANES - (CONV_WIDTH - 1) + kk
        xc = xc + xbuf_ref[off:off + TL, :] * cw_ref[kk:kk + 1, :]
    xbuf_ref[0:SUBLANES, :] = x[TL - SUBLANES:, :]

    xcb = xc.astype(BF16)
    r = _sigmoid(jnp.dot(xcb, wa_ref[...], preferred_element_type=F32) + ba_ref[...])
    ig = _sigmoid(jnp.dot(xcb, wx_ref[...], preferred_element_type=F32) + bx_ref[...])
    nlam = -lam_ref[...]
    softplus = jnp.maximum(nlam, 0.0) + jnp.log1p(jnp.exp(-jnp.abs(nlam)))
    log_a = (-LRU_C * r) * softplus
    a = jnp.exp(log_a)
    bt = jnp.sqrt(jnp.tanh(-log_a) * (1.0 + a * a)) * (ig * xc)

    row8 = lax.broadcasted_iota(jnp.int32, (TL, D_LRU), 0) & (SUBLANES - 1)
    d = 1
    while d < SUBLANES:
        valid = row8 >= d
        a_s = jnp.where(valid, pltpu.roll(a, d, axis=0), 1.0)
        b_s = jnp.where(valid, pltpu.roll(bt, d, axis=0), 0.0)
        bt = bt + a * b_s
        a = a * a_s
        d *= 2

    h_prev = h_ref[...]
    sg = _silu(g_ref[0])
    for g in range(TL // SUBLANES):
        lo = g * SUBLANES
        hg = bt[lo:lo + SUBLANES] + a[lo:lo + SUBLANES] * h_prev
        h_prev = hg[SUBLANES - 1:SUBLANES, :]
        o_ref[0, lo:lo + SUBLANES, :] = (hg * sg[lo:lo + SUBLANES]).astype(BF16)
    h_ref[...] = h_prev


def _lru(rest, conv_w, conv_b, wa_bd, ba, wx_bd, bx, lam):
    B, S, _ = rest.shape
    const = lambda shape: pl.BlockSpec(shape, lambda b, i: (0,) * len(shape))
    x_col = D_ATT // D_LRU
    return pl.pallas_call(
        _lru_kernel,
        grid=(B, S // TL),
        in_specs=[
            pl.BlockSpec((1, TL, D_LRU), lambda b, i: (b, i, x_col)),
            pl.BlockSpec((1, TL, D_LRU), lambda b, i: (b, i, x_col + 1)),
            const((CONV_WIDTH, D_LRU)),
            const((1, D_LRU)),
            const((D_LRU, D_LRU)),
            const((1, D_LRU)),
            const((D_LRU, D_LRU)),
            const((1, D_LRU)),
            const((1, D_LRU)),
        ],
        out_specs=pl.BlockSpec((1, TL, D_LRU), lambda b, i: (b, i, 0)),
        out_shape=jax.ShapeDtypeStruct((B, S, D_LRU), BF16),
        scratch_shapes=[
            pltpu.VMEM((TL + SUBLANES, D_LRU), F32),
            pltpu.VMEM((1, D_LRU), F32),
        ],
        compiler_params=_params(("parallel", "arbitrary")),
        name="rg_lru",
    )(rest, rest, conv_w, conv_b, wa_bd, ba, wx_bd, bx, lam)


def _sg_kernel(u_ref, v_ref, g_ref, lng_ref, lnb_ref, w_ref, bias_ref, o_ref):
    u = _gelu_tanh(u_ref[0])
    v = _gelu_tanh(v_ref[0])
    mu = jnp.mean(v, axis=-1, keepdims=True)
    var = jnp.mean(jnp.square(v - mu), axis=-1, keepdims=True)
    vn = ((v - mu) * lax.rsqrt(var + EPS)) * lng_ref[...] + lnb_ref[...]
    vnb = vn.astype(BF16)
    t_idx = lax.broadcasted_iota(jnp.int32, w_ref.shape, 0) & (SG_CHUNK - 1)
    s_idx = lax.broadcasted_iota(jnp.int32, w_ref.shape, 1)
    w = jnp.where(s_idx <= t_idx, w_ref[...], jnp.zeros(w_ref.shape, BF16))
    grp = lax.broadcasted_iota(jnp.int32, (SG_CHUNK, D_SG), 1) // HEAD_DIM
    sg = _silu(g_ref[0])
    for c in range(TC // SG_CHUNK):
        lo = c * SG_CHUNK
        zz = jnp.dot(w, vnb[lo:lo + SG_CHUNK], preferred_element_type=F32)
        z = zz[0:SG_CHUNK]
        for gi in range(1, N_SG_GROUPS):
            z = jnp.where(grp == gi, zz[gi * SG_CHUNK:(gi + 1) * SG_CHUNK], z)
        z = z + bias_ref[...]
        o_ref[0, lo:lo + SG_CHUNK, :] = ((u[lo:lo + SG_CHUNK] * z) * sg[lo:lo + SG_CHUNK]).astype(BF16)


def _spatial_gate(rest, ln_g, ln_b, w_all, bias):
    B, S, _ = rest.shape
    const = lambda shape: pl.BlockSpec(shape, lambda b, i: (0,) * len(shape))
    u_col = (D_ATT + 2 * D_LRU) // D_SG
    return pl.pallas_call(
        _sg_kernel,
        grid=(B, S // TC),
        in_specs=[
            pl.BlockSpec((1, TC, D_SG), lambda b, i: (b, i, u_col)),
            pl.BlockSpec((1, TC, D_SG), lambda b, i: (b, i, u_col + 1)),
            pl.BlockSpec((1, TC, D_SG), lambda b, i: (b, i, u_col + 2)),
            const((1, D_SG)),
            const((1, D_SG)),
            const((N_SG_GROUPS * SG_CHUNK, SG_CHUNK)),
            const((SG_CHUNK, D_SG)),
        ],
        out_specs=pl.BlockSpec((1, TC, D_SG), lambda b, i: (b, i, 0)),
        out_shape=jax.ShapeDtypeStruct((B, S, D_SG), BF16),
        compiler_params=_params(("parallel", "parallel")),
        name="spatial_gate",
    )(rest, rest, rest, ln_g, ln_b, w_all, bias)


def _outproj_kernel(ya_ref, yl_ref, ys_ref, x_ref, gate_ref, pg_ref, w_ref, o_ref):
    y = jnp.dot(ya_ref[0], w_ref[0:D_ATT, :], preferred_element_type=F32)
    y = y + jnp.dot(yl_ref[0], w_ref[D_ATT:D_ATT + D_LRU, :], preferred_element_type=F32)
    y = y + jnp.dot(ys_ref[0], w_ref[D_ATT + D_LRU:, :], preferred_element_type=F32)
    ms = jnp.mean(y * y, axis=-1, keepdims=True)
    yn = (y * lax.rsqrt(ms + EPS)) * pg_ref[...]
    o_ref[0] = x_ref[0] + gate_ref[0] * yn


def _outproj(ya, yl, ys, x, gate, post_g, w_out):
    B, S, D = x.shape
    const = lambda shape: pl.BlockSpec(shape, lambda b, i: (0,) * len(shape))
    return pl.pallas_call(
        _outproj_kernel,
        grid=(B, S // TM),
        in_specs=[
            pl.BlockSpec((1, TM, D_ATT), lambda b, i: (b, i, 0)),
            pl.BlockSpec((1, TM, D_LRU), lambda b, i: (b, i, 0)),
            pl.BlockSpec((1, TM, D_SG), lambda b, i: (b, i, 0)),
            pl.BlockSpec((1, TM, D), lambda b, i: (b, i, 0)),
            pl.BlockSpec((1, 1, D), lambda b, i: (b, 0, 0)),
            const((1, D)),
            const((D, D)),
        ],
        out_specs=pl.BlockSpec((1, TM, D), lambda b, i: (b, i, 0)),
        out_shape=jax.ShapeDtypeStruct((B, S, D), F32),
        compiler_params=_params(("parallel", "parallel")),
        name="outproj",
    )(ya, yl, ys, x, gate, post_g, w_out)


def _block_diag(w):
    g, n, _ = w.shape
    out = jnp.zeros((g * n, g * n), w.dtype)
    for k in range(g):
        out = out.at[k * n:(k + 1) * n, k * n:(k + 1) * n].set(w[k])
    return out


def _layer(x, mod, pre_g, post_g, w_in, b_f, conv_w, conv_b, lru_wa, lru_ba, lru_wx, lru_bx,
           lru_lambda, sg_ln_g, sg_ln_b, sg_w, sg_b, w_out):
    B, S, D = x.shape
    shift = mod[:, None, 0:D]
    scale = mod[:, None, D:2 * D]
    gate = mod[:, None, 2 * D:3 * D]

    o = 0
    wq = w_in[:, o:o + D_ATT]
    o += D_ATT
    wk = w_in[:, o:o + D_ATT].astype(BF16)
    o += D_ATT
    wv = w_in[:, o:o + D_ATT]
    o += D_ATT
    wf = w_in[:, o:o + N_HEADS]
    o += N_HEADS
    wr = w_in[:, o:].astype(BF16)
    wt = jnp.concatenate([wq, wv], axis=1).T.astype(BF16)
    pad = jnp.zeros((D, LANES - N_AUG), F32)
    wf3 = jnp.concatenate([jnp.repeat(wf, N_PARTS, axis=1), pad], axis=1).astype(BF16)
    bf3 = jnp.concatenate([jnp.repeat(b_f, N_PARTS), jnp.zeros((LANES - N_AUG,), F32)])[None, :]

    qt, vt, k, ka, qa, rest = _inproj(x, shift, scale, pre_g[None, :], wt, wk, wf3, wr, bf3)
    ya = _attention(qt, qa, k, ka, vt, rest)

    yl = _lru(rest, conv_w, conv_b[None, :], _block_diag(lru_wa).astype(BF16), lru_ba[None, :],
              _block_diag(lru_wx).astype(BF16), lru_bx[None, :], lru_lambda[None, :])

    w_all = sg_w.reshape(N_SG_GROUPS * SG_CHUNK, SG_CHUNK).astype(BF16)
    bias = jnp.repeat(sg_b.T, HEAD_DIM, axis=1)
    ys = _spatial_gate(rest, sg_ln_g[None, :], sg_ln_b[None, :], w_all, bias)

    return _outproj(ya, yl, ys, x, gate, post_g[None, :], w_out.astype(BF16))


def kernel(x, c, ada_w, ada_b, pre_g, post_g, w_in, b_f, conv_w, conv_b, lru_wa, lru_ba, lru_wx,
           lru_bx, lru_lambda, sg_ln_g, sg_ln_b, sg_w, sg_b, w_out):
    mod = _ada(c, ada_w, ada_b)
    for l in range(ada_w.shape[0]):
        x = _layer(x, mod[l], pre_g[l], post_g[l], w_in[l], b_f[l], conv_w[l], conv_b[l],
                   lru_wa[l], lru_ba[l], lru_wx[l], lru_bx[l], lru_lambda[l],
                   sg_ln_g[l], sg_ln_b[l], sg_w[l], sg_b[l], w_out[l])
    return x
```

```python
import math

import jax
import jax.numpy as jnp
from jax import lax
from jax.experimental import pallas as pl
from jax.experimental.pallas import tpu as pltpu

D_MODEL = 1024
HEAD_DIM = 64
D_ATT = 512
D_LRU = 256
D_SG = 256
N_HEADS = D_ATT // HEAD_DIM
N_LRU_BLOCKS = D_LRU // HEAD_DIM
N_SG_GROUPS = D_SG // HEAD_DIM
SG_CHUNK = 128
CONV_WIDTH = 4
LRU_C = 8.0
EPS = 1e-6
D_EARLY = D_LRU + D_SG
D_GATES = D_ATT + D_LRU + 2 * D_SG

LANES = 128
SUBLANES = 8
BF16_ROWS = 16
VMEM_LIMIT = 56 * 1024 * 1024

TM = 512
TQ = 512
TN_ADA = 512
NEG = -1e30

N_PARTS = 3
N_AUG = N_HEADS * N_PARTS
LOG2E = 1.4426950408889634
Q_SCALE = LOG2E / math.sqrt(HEAD_DIM)
V_ROWS = HEAD_DIM + BF16_ROWS

F32 = jnp.float32
BF16 = jnp.bfloat16


def _sigmoid(x):
    return 0.5 * jnp.tanh(0.5 * x) + 0.5


def _silu(x):
    return x * _sigmoid(x)


def _gelu_tanh(x):
    c = math.sqrt(2.0 / math.pi)
    return 0.5 * x * (1.0 + jnp.tanh(c * (x + 0.044715 * (x * x * x))))


def _log_sigmoid(x):
    return jnp.minimum(x, 0.0) - jnp.log1p(jnp.exp(-jnp.abs(x)))


def _split3(x, sel):
    hi = x.astype(BF16).astype(F32)
    r1 = x - hi
    mid = r1.astype(BF16).astype(F32)
    lo = r1 - mid
    return jnp.where(sel == 0, hi, jnp.where(sel == 1, mid, lo))


def _params(sem):
    return pltpu.CompilerParams(dimension_semantics=sem, vmem_limit_bytes=VMEM_LIMIT)


def _ada_kernel(ct_ref, w_ref, b_ref, o_ref):
    ct = ct_ref[...]
    sc = _silu(ct)
    w = w_ref[0]
    bias = b_ref[0]
    for b in range(ct.shape[1]):
        col = sc[:, b:b + 1]
        o_ref[0, b:b + 1, :] = jnp.sum(col * w, axis=0, keepdims=True) + bias


def _ada(c, ada_w, ada_b):
    L, D, N = ada_w.shape
    B = c.shape[0]
    return pl.pallas_call(
        _ada_kernel,
        grid=(L, N // TN_ADA),
        in_specs=[
            pl.BlockSpec((D, B), lambda l, n: (0, 0)),
            pl.BlockSpec((1, D, TN_ADA), lambda l, n: (l, 0, n)),
            pl.BlockSpec((1, 1, TN_ADA), lambda l, n: (l, 0, n)),
        ],
        out_specs=pl.BlockSpec((1, B, TN_ADA), lambda l, n: (l, 0, n)),
        out_shape=jax.ShapeDtypeStruct((L, B, N), F32),
        compiler_params=_params(("parallel", "parallel")),
        name="ada_mod",
    )(c.T, ada_w, ada_b.reshape(L, 1, N))


def _lru_mixer(x_lru, g_lru, r_pre, i_pre, cw_ref, lam_ref, h_ref, xc):
    r = _sigmoid(r_pre)
    ig = _sigmoid(i_pre)
    nlam = -lam_ref[...]
    softplus = jnp.maximum(nlam, 0.0) + jnp.log1p(jnp.exp(-jnp.abs(nlam)))
    log_a = (-LRU_C * r) * softplus
    a = jnp.exp(log_a)
    bt = jnp.sqrt(jnp.tanh(-log_a) * (1.0 + a * a)) * (ig * xc)

    row8 = lax.broadcasted_iota(jnp.int32, (TM, D_LRU), 0) & (SUBLANES - 1)
    d = 1
    while d < SUBLANES:
        valid = row8 >= d
        a_s = jnp.where(valid, pltpu.roll(a, d, axis=0), 1.0)
        b_s = jnp.where(valid, pltpu.roll(bt, d, axis=0), 0.0)
        bt = bt + a * b_s
        a = a * a_s
        d *= 2

    h_prev = h_ref[...]
    sg = _silu(g_lru)
    outs = []
    for g in range(TM // SUBLANES):
        lo = g * SUBLANES
        hg = bt[lo:lo + SUBLANES] + a[lo:lo + SUBLANES] * h_prev
        h_prev = hg[SUBLANES - 1:SUBLANES, :]
        outs.append(hg * sg[lo:lo + SUBLANES])
    h_ref[...] = h_prev
    return jnp.concatenate(outs, axis=0)


def _inproj_kernel(x_ref, shift_ref, scale_ref, g_ref, we_ref, wg_ref, wk_ref, wf_ref, wt_ref, bf_ref,
                   cw_ref, cb_ref, wa_ref, ba_ref, wx_ref, bx_ref, lam_ref,
                   lng_ref, lnb_ref, ws_ref, sb_ref,
                   qt_ref, vt_ref, k_ref, ka_ref, qa_ref, ga_ref, yl_ref, ys_ref,
                   fc_ref, xbuf_ref, h_ref):
    i = pl.program_id(1)

    @pl.when(i == 0)
    def _():
        fc_ref[...] = jnp.zeros(fc_ref.shape, F32)
        xbuf_ref[0:SUBLANES, :] = jnp.zeros((SUBLANES, D_LRU), F32)
        h_ref[...] = jnp.zeros(h_ref.shape, F32)

    x = x_ref[0]
    ms = jnp.mean(x * x, axis=-1, keepdims=True)
    y = (x * lax.rsqrt(ms + EPS)) * g_ref[...]
    h = y * (1.0 + scale_ref[0]) + shift_ref[0]
    hb = h.astype(BF16)

    early = jnp.dot(hb, we_ref[...], preferred_element_type=F32)
    x_lru = early[:, :D_LRU]
    sg_v = early[:, D_LRU:]
    gates = jnp.dot(hb, wg_ref[...], preferred_element_type=F32)
    k_ref[0] = jnp.dot(hb, wk_ref[...], preferred_element_type=F32).astype(BF16)

    xbuf_ref[SUBLANES:SUBLANES + TM, :] = x_lru
    xc = cb_ref[...]
    for kk in range(CONV_WIDTH):
        off = SUBLANES - (CONV_WIDTH - 1) + kk
        xc = xc + xbuf_ref[off:off + TM, :] * cw_ref[kk:kk + 1, :]
    xbuf_ref[0:SUBLANES, :] = x_lru[TM - SUBLANES:, :]
    xcb = xc.astype(BF16)
    r_pre = jnp.dot(xcb, wa_ref[...], preferred_element_type=F32) + ba_ref[...]
    i_pre = jnp.dot(xcb, wx_ref[...], preferred_element_type=F32) + bx_ref[...]

    v = _gelu_tanh(sg_v)
    mu = jnp.mean(v, axis=-1, keepdims=True)
    var = jnp.mean(jnp.square(v - mu), axis=-1, keepdims=True)
    vnb = (((v - mu) * lax.rsqrt(var + EPS)) * lng_ref[...] + lnb_ref[...]).astype(BF16)
    t_idx = lax.broadcasted_iota(jnp.int32, ws_ref.shape, 0) & (SG_CHUNK - 1)
    s_idx = lax.broadcasted_iota(jnp.int32, ws_ref.shape, 1)
    ws = jnp.where(s_idx <= t_idx, ws_ref[...], jnp.zeros(ws_ref.shape, BF16))
    grp = lax.broadcasted_iota(jnp.int32, (SG_CHUNK, D_SG), 1) // HEAD_DIM
    zs = []
    for c in range(TM // SG_CHUNK):
        zz = jnp.dot(ws, vnb[c * SG_CHUNK:(c + 1) * SG_CHUNK], preferred_element_type=F32)
        z = zz[0:SG_CHUNK]
        for gi in range(1, N_SG_GROUPS):
            z = jnp.where(grp == gi, zz[gi * SG_CHUNK:(gi + 1) * SG_CHUNK], z)
        zs.append(z + sb_ref[...])
    z_all = jnp.concatenate(zs, axis=0)

    fl = jnp.dot(hb, wf_ref[...], preferred_element_type=F32)
    tr = lax.dot_general(wt_ref[...], hb, (((1,), (1,)), ((), ())), preferred_element_type=F32)
    qt_ref[0] = (tr[:D_ATT] * Q_SCALE).astype(BF16)
    vt_ref[0] = tr[D_ATT:].astype(BF16)

    ga_ref[0] = _silu(gates[:, :D_ATT]).astype(BF16)
    g_lru = gates[:, D_ATT:D_ATT + D_LRU]
    sg_u = gates[:, D_ATT + D_LRU:D_ATT + D_LRU + D_SG]
    g_sg = gates[:, D_ATT + D_LRU + D_SG:]
    ys_ref[0] = ((_gelu_tanh(sg_u) * z_all) * _silu(g_sg)).astype(BF16)
    yl_ref[0] = _lru_mixer(x_lru, g_lru, r_pre, i_pre, cw_ref, lam_ref, h_ref, xc).astype(BF16)

    ls = _log_sigmoid(fl + bf_ref[...])
    row8 = lax.broadcasted_iota(jnp.int32, (TM, LANES), 0) & (SUBLANES - 1)
    d = 1
    while d < SUBLANES:
        ls = ls + jnp.where(row8 >= d, pltpu.roll(ls, d, axis=0), 0.0)
        d *= 2
    carry = fc_ref[...]
    groups = []
    for g in range(TM // SUBLANES):
        fg = ls[g * SUBLANES:(g + 1) * SUBLANES] + carry
        carry = fg[SUBLANES - 1:SUBLANES, :]
        groups.append(fg)
    fc_ref[...] = carry
    f2 = jnp.concatenate(groups, axis=0) * LOG2E

    lane = lax.broadcasted_iota(jnp.int32, (TM, LANES), 1)
    parts = _split3(f2, lane % N_PARTS)
    ka_ref[0] = jnp.where(lane < N_AUG, -parts, jnp.where(lane < 2 * N_AUG, 1.0, 0.0)).astype(BF16)

    ft = f2.T[:N_AUG]
    rowt = lax.broadcasted_iota(jnp.int32, (N_AUG, TM), 0)
    parts_t = _split3(ft, rowt % N_PARTS)
    qa_ref[0] = jnp.concatenate(
        [jnp.ones((N_AUG, TM), F32), parts_t, jnp.zeros((LANES - 2 * N_AUG, TM), F32)], axis=0).astype(BF16)


def _inproj(x, shift, scale, pre_g, weights, small):
    B, S, D = x.shape
    const = lambda a: pl.BlockSpec(a.shape, lambda b, i: (0,) * a.ndim)
    row_tile = lambda n: pl.BlockSpec((1, TM, n), lambda b, i: (b, i, 0))
    col_tile = lambda n: pl.BlockSpec((1, n, TM), lambda b, i: (b, 0, i))
    return pl.pallas_call(
        _inproj_kernel,
        grid=(B, S // TM),
        in_specs=[
            row_tile(D),
            pl.BlockSpec((1, 1, D), lambda b, i: (b, 0, 0)),
            pl.BlockSpec((1, 1, D), lambda b, i: (b, 0, 0)),
            const(pre_g),
        ] + [const(w) for w in weights] + [const(s) for s in small],
        out_specs=[
            col_tile(D_ATT), col_tile(D_ATT), row_tile(D_ATT), row_tile(LANES), col_tile(LANES),
            row_tile(D_ATT), row_tile(D_LRU), row_tile(D_SG),
        ],
        out_shape=[
            jax.ShapeDtypeStruct((B, D_ATT, S), BF16),
            jax.ShapeDtypeStruct((B, D_ATT, S), BF16),
            jax.ShapeDtypeStruct((B, S, D_ATT), BF16),
            jax.ShapeDtypeStruct((B, S, LANES), BF16),
            jax.ShapeDtypeStruct((B, LANES, S), BF16),
            jax.ShapeDtypeStruct((B, S, D_ATT), BF16),
            jax.ShapeDtypeStruct((B, S, D_LRU), BF16),
            jax.ShapeDtypeStruct((B, S, D_SG), BF16),
        ],
        scratch_shapes=[
            pltpu.VMEM((1, LANES), F32),
            pltpu.VMEM((TM + SUBLANES, D_LRU), F32),
            pltpu.VMEM((1, D_LRU), F32),
        ],
        compiler_params=_params(("parallel", "arbitrary")),
        name="inproj",
    )(x, shift, scale, pre_g, *weights, *small)


def _attn_kernel(qt_ref, qa_ref, k_ref, ka_ref, vt_ref, g_ref, o_ref,
                 m_ref, acc_ref, rhs_ref, s0_ref, s1_ref, p0_ref, p1_ref, mx0_ref, mx1_ref, al0_ref, al1_ref):
    p = pl.program_id(1)
    i = pl.program_id(2)
    qt = qt_ref[0]
    qa = qa_ref[0]
    row = lax.broadcasted_iota(jnp.int32, (LANES, TQ), 0)
    zero = jnp.zeros_like(qt)
    for hh in range(2):
        head = 2 * p + hh
        own = (row >= hh * HEAD_DIM) & (row < (hh + 1) * HEAD_DIM)
        lo = head * N_PARTS
        aug = ((row >= lo) & (row < lo + N_PARTS)) | ((row >= N_AUG + lo) & (row < N_AUG + lo + N_PARTS))
        rhs_ref[hh] = jnp.concatenate([jnp.where(own, qt, zero), jnp.where(aug, qa, zero)], axis=0)
    ones_rows = jnp.where(lax.broadcasted_iota(jnp.int32, (BF16_ROWS, TQ), 0) == 0, 1.0, 0.0).astype(BF16)
    m_ref[...] = jnp.full(m_ref.shape, NEG, F32)
    acc_ref[...] = jnp.zeros(acc_ref.shape, F32)
    s_bufs = (s0_ref, s1_ref)
    p_bufs = (p0_ref, p1_ref)
    mx_bufs = (mx0_ref, mx1_ref)
    al_bufs = (al0_ref, al1_ref)

    def scores(t, slot, masked):
        rows = pl.ds(pl.multiple_of(t * TQ, TQ), TQ)
        lhs = jnp.concatenate([k_ref[0, rows, :], ka_ref[0, rows, :]], axis=1)
        for hh in range(2):
            st = jnp.dot(lhs, rhs_ref[hh], preferred_element_type=F32)
            if masked:
                kk = lax.broadcasted_iota(jnp.int32, (TQ, TQ), 0)
                qq = lax.broadcasted_iota(jnp.int32, (TQ, TQ), 1)
                st = jnp.where(kk <= qq, st, NEG)
            s_bufs[slot][hh] = st
            mx_bufs[slot][hh] = jnp.max(st, axis=0, keepdims=True)

    def softmax(slot):
        for hh in range(2):
            m_old = m_ref[hh]
            m_new = jnp.maximum(m_old, mx_bufs[slot][hh])
            p_bufs[slot][hh] = jnp.exp2(s_bufs[slot][hh] - m_new).astype(BF16)
            al_bufs[slot][hh] = jnp.exp2(m_old - m_new)
            m_ref[hh] = m_new

    def values(t, slot):
        cols = pl.ds(pl.multiple_of(t * TQ, TQ), TQ)
        for hh in range(2):
            vaug = jnp.concatenate([vt_ref[0, hh * HEAD_DIM:(hh + 1) * HEAD_DIM, cols], ones_rows], axis=0)
            acc_ref[hh] = acc_ref[hh] * al_bufs[slot][hh] + jnp.dot(vaug, p_bufs[slot][hh],
                                                                     preferred_element_type=F32)

    def step(tau, parity, do_scores=False, masked=False, do_softmax=False, do_values=False):
        if do_softmax:
            softmax(1 - parity)
        if do_scores:
            scores(tau, parity, masked)
        if do_values:
            values(tau - 2, parity)

    full = dict(do_scores=True, do_softmax=True, do_values=True)

    @pl.when(i == 0)
    def _():
        step(0, 0, do_scores=True, masked=True)
        step(1, 1, do_softmax=True)
        step(2, 0, do_values=True)

    @pl.when(i == 1)
    def _():
        step(0, 0, do_scores=True)
        step(1, 1, do_scores=True, masked=True, do_softmax=True)
        step(2, 0, do_softmax=True, do_values=True)
        step(3, 1, do_values=True)

    @pl.when(i >= 2)
    def _():
        step(0, 0, do_scores=True)
        step(1, 1, do_scores=True, do_softmax=True)

        def body(u, carry):
            step(2 + 2 * u, 0, **full)
            step(3 + 2 * u, 1, **full)
            return carry

        lax.fori_loop(0, lax.shift_right_logical(i - 2, 1), body, 0)

        @pl.when((i & 1) == 0)
        def _():
            step(i, 0, masked=True, **full)
            step(i + 1, 1, do_softmax=True, do_values=True)
            step(i + 2, 0, do_values=True)

        @pl.when((i & 1) == 1)
        def _():
            step(i - 1, 0, **full)
            step(i, 1, masked=True, **full)
            step(i + 1, 0, do_softmax=True, do_values=True)
            step(i + 2, 1, do_values=True)

    outs = []
    for hh in range(2):
        a = acc_ref[hh]
        outs.append(a[:HEAD_DIM] / a[HEAD_DIM:HEAD_DIM + 1])
    o = jnp.concatenate(outs, axis=0).T
    o_ref[0] = (o * g_ref[0].astype(F32)).astype(BF16)


def _attention(qt, qa, k, ka, vt, gate):
    B, S, _ = k.shape
    n_pair = N_HEADS // 2
    return pl.pallas_call(
        _attn_kernel,
        grid=(B, n_pair, S // TQ),
        in_specs=[
            pl.BlockSpec((1, LANES, TQ), lambda b, p, i: (b, p, i)),
            pl.BlockSpec((1, LANES, TQ), lambda b, p, i: (b, 0, i)),
            pl.BlockSpec((1, S, LANES), lambda b, p, i: (b, 0, p)),
            pl.BlockSpec((1, S, LANES), lambda b, p, i: (b, 0, 0)),
            pl.BlockSpec((1, LANES, S), lambda b, p, i: (b, p, 0)),
            pl.BlockSpec((1, TQ, LANES), lambda b, p, i: (b, i, p)),
        ],
        out_specs=pl.BlockSpec((1, TQ, LANES), lambda b, p, i: (b, i, p)),
        out_shape=jax.ShapeDtypeStruct((B, S, D_ATT), BF16),
        scratch_shapes=[
            pltpu.VMEM((2, 1, TQ), F32),
            pltpu.VMEM((2, V_ROWS, TQ), F32),
            pltpu.VMEM((2, 2 * LANES, TQ), BF16),
            pltpu.VMEM((2, TQ, TQ), F32),
            pltpu.VMEM((2, TQ, TQ), F32),
            pltpu.VMEM((2, TQ, TQ), BF16),
            pltpu.VMEM((2, TQ, TQ), BF16),
            pltpu.VMEM((2, 1, TQ), F32),
            pltpu.VMEM((2, 1, TQ), F32),
            pltpu.VMEM((2, 1, TQ), F32),
            pltpu.VMEM((2, 1, TQ), F32),
        ],
        compiler_params=_params(("parallel", "parallel", "parallel")),
        name="fox_attention",
    )(qt, qa, k, ka, vt, gate)


def _outproj_kernel(ya_ref, yl_ref, ys_ref, x_ref, gate_ref, pg_ref, w_ref, o_ref):
    y = jnp.dot(ya_ref[0], w_ref[0:D_ATT, :], preferred_element_type=F32)
    y = y + jnp.dot(yl_ref[0], w_ref[D_ATT:D_ATT + D_LRU, :], preferred_element_type=F32)
    y = y + jnp.dot(ys_ref[0], w_ref[D_ATT + D_LRU:, :], preferred_element_type=F32)
    ms = jnp.mean(y * y, axis=-1, keepdims=True)
    yn = (y * lax.rsqrt(ms + EPS)) * pg_ref[...]
    o_ref[0] = x_ref[0] + gate_ref[0] * yn


def _outproj(ya, yl, ys, x, gate, post_g, w_out):
    B, S, D = x.shape
    const = lambda shape: pl.BlockSpec(shape, lambda b, i: (0,) * len(shape))
    return pl.pallas_call(
        _outproj_kernel,
        grid=(B, S // TM),
        in_specs=[
            pl.BlockSpec((1, TM, D_ATT), lambda b, i: (b, i, 0)),
            pl.BlockSpec((1, TM, D_LRU), lambda b, i: (b, i, 0)),
            pl.BlockSpec((1, TM, D_SG), lambda b, i: (b, i, 0)),
            pl.BlockSpec((1, TM, D), lambda b, i: (b, i, 0)),
            pl.BlockSpec((1, 1, D), lambda b, i: (b, 0, 0)),
            const((1, D)),
            const((D, D)),
        ],
        out_specs=pl.BlockSpec((1, TM, D), lambda b, i: (b, i, 0)),
        out_shape=jax.ShapeDtypeStruct((B, S, D), F32),
        compiler_params=_params(("parallel", "parallel")),
        name="outproj",
    )(ya, yl, ys, x, gate, post_g, w_out)


def _block_diag(w):
    g, n, _ = w.shape
    out = jnp.zeros((g * n, g * n), w.dtype)
    for k in range(g):
        out = out.at[k * n:(k + 1) * n, k * n:(k + 1) * n].set(w[k])
    return out


def _layer(x, mod, pre_g, post_g, w_in, b_f, conv_w, conv_b, lru_wa, lru_ba, lru_wx, lru_bx,
           lru_lambda, sg_ln_g, sg_ln_b, sg_w, sg_b, w_out):
    B, S, D = x.shape
    shift = mod[:, None, 0:D]
    scale = mod[:, None, D:2 * D]
    gate = mod[:, None, 2 * D:3 * D]

    cuts = [0]
    for n in (D_ATT, D_ATT, D_ATT, N_HEADS, D_ATT, D_LRU, D_LRU, D_SG, D_SG, D_SG):
        cuts.append(cuts[-1] + n)
    wq, wk, wv, wf, wga, wxl, wgl, wsu, wsv, wgs = (w_in[:, cuts[j]:cuts[j + 1]] for j in range(10))
    we = jnp.concatenate([wxl, wsv], axis=1).astype(BF16)
    wg = jnp.concatenate([wga, wgl, wsu, wgs], axis=1).astype(BF16)
    wt = jnp.concatenate([wq, wv], axis=1).T.astype(BF16)
    pad = jnp.zeros((D, LANES - N_AUG), F32)
    wf3 = jnp.concatenate([jnp.repeat(wf, N_PARTS, axis=1), pad], axis=1).astype(BF16)
    bf3 = jnp.concatenate([jnp.repeat(b_f, N_PARTS), jnp.zeros((LANES - N_AUG,), F32)])[None, :]
    ws_all = sg_w.reshape(N_SG_GROUPS * SG_CHUNK, SG_CHUNK).astype(BF16)
    sg_bias = jnp.repeat(sg_b.T, HEAD_DIM, axis=1)

    weights = (we, wg, wk.astype(BF16), wf3, wt, bf3)
    small = (conv_w, conv_b[None, :], _block_diag(lru_wa).astype(BF16), lru_ba[None, :],
             _block_diag(lru_wx).astype(BF16), lru_bx[None, :], lru_lambda[None, :],
             sg_ln_g[None, :], sg_ln_b[None, :], ws_all, sg_bias)
    qt, vt, k, ka, qa, ga, yl, ys = _inproj(x, shift, scale, pre_g[None, :], weights, small)
    ya = _attention(qt, qa, k, ka, vt, ga)
    return _outproj(ya, yl, ys, x, gate, post_g[None, :], w_out.astype(BF16))


def kernel(x, c, ada_w, ada_b, pre_g, post_g, w_in, b_f, conv_w, conv_b, lru_wa, lru_ba, lru_wx,
           lru_bx, lru_lambda, sg_ln_g, sg_ln_b, sg_w, sg_b, w_out):
    mod = _ada(c, ada_w, ada_b)
    for l in range(ada_w.shape[0]):
        x = _layer(x, mod[l], pre_g[l], post_g[l], w_in[l], b_f[l], conv_w[l], conv_b[l],
                   lru_wa[l], lru_ba[l], lru_wx[l], lru_bx[l], lru_lambda[l],
                   sg_ln_g[l], sg_ln_b[l], sg_w[l], sg_b[l], w_out[l])
    return x
```

```python
import math

import jax
import jax.numpy as jnp
from jax import lax
from jax.experimental import pallas as pl
from jax.experimental.pallas import tpu as pltpu

D_MODEL = 1024
HEAD_DIM = 64
D_ATT = 512
D_LRU = 256
D_SG = 256
N_HEADS = D_ATT // HEAD_DIM
N_LRU_BLOCKS = D_LRU // HEAD_DIM
N_SG_GROUPS = D_SG // HEAD_DIM
SG_CHUNK = 128
CONV_WIDTH = 4
LRU_C = 8.0
EPS = 1e-6
IN_SIZES = (D_ATT, D_ATT, D_ATT, N_HEADS, D_ATT, D_LRU, D_LRU, D_SG, D_SG, D_SG)

LANES = 128
SUBLANES = 8
BF16_ROWS = 16
VMEM_LIMIT = 56 * 1024 * 1024

TM = 512
TO = 1024
TQ = 512
TN_ADA = 512
NEG = -1e30

N_PARTS = 3
N_AUG = N_HEADS * N_PARTS
T_ROWS = 2 * D_ATT + 2 * BF16_ROWS
LOG2E = 1.4426950408889634
Q_SCALE = LOG2E / math.sqrt(HEAD_DIM)
V_ROWS = HEAD_DIM + BF16_ROWS
GELU_C0 = math.sqrt(2.0 / math.pi)
GELU_C1 = GELU_C0 * 0.044715

F32 = jnp.float32
BF16 = jnp.bfloat16


def _sigmoid(x):
    return 0.5 * jnp.tanh(0.5 * x) + 0.5


def _silu(x):
    hx = 0.5 * x
    return hx + hx * jnp.tanh(hx)


def _gelu_tanh(x):
    hx = 0.5 * x
    return hx + hx * jnp.tanh(x * (GELU_C0 + GELU_C1 * (x * x)))


def _log_sigmoid(x):
    return jnp.minimum(x, 0.0) - jnp.log1p(jnp.exp(-jnp.abs(x)))


def _split3(x, sel):
    hi = x.astype(BF16).astype(F32)
    r1 = x - hi
    mid = r1.astype(BF16).astype(F32)
    lo = r1 - mid
    return jnp.where(sel == 0, hi, jnp.where(sel == 1, mid, lo))


def _mod_row(ref):
    return jnp.concatenate([ref[0, 0, j] for j in range(ref.shape[2])], axis=1)


def _params(sem):
    return pltpu.CompilerParams(dimension_semantics=sem, vmem_limit_bytes=VMEM_LIMIT)


def _ada_kernel(ct_ref, w_ref, b_ref, o_ref):
    ct = ct_ref[...]
    sc = _silu(ct)
    w = w_ref[0]
    bias = b_ref[0]
    for b in range(ct.shape[1]):
        col = sc[:, b:b + 1]
        o_ref[0, b, 0] = jnp.sum(col * w, axis=0, keepdims=True) + bias


def _ada(c, ada_w, ada_b):
    L, D, N = ada_w.shape
    B = c.shape[0]
    return pl.pallas_call(
        _ada_kernel,
        grid=(L, N // TN_ADA),
        in_specs=[
            pl.BlockSpec((D, B), lambda l, n: (0, 0)),
            pl.BlockSpec((1, D, TN_ADA), lambda l, n: (l, 0, n)),
            pl.BlockSpec((1, 1, TN_ADA), lambda l, n: (l, 0, n)),
        ],
        out_specs=pl.BlockSpec((1, B, 1, 1, TN_ADA), lambda l, n: (l, 0, n, 0, 0)),
        out_shape=jax.ShapeDtypeStruct((L, B, N // TN_ADA, 1, TN_ADA), F32),
        compiler_params=_params(("parallel", "parallel")),
        name="ada_mod",
    )(c.T, ada_w, ada_b.reshape(L, 1, N))


def _lru_mixer(g_lru, r_pre, i_pre, lam, h_ref, xc):
    r = _sigmoid(r_pre)
    ig = _sigmoid(i_pre)
    nlam = -lam
    softplus = jnp.maximum(nlam, 0.0) + jnp.log1p(jnp.exp(-jnp.abs(nlam)))
    log_a = (-LRU_C * r) * softplus
    a = jnp.exp(log_a)
    bt = jnp.sqrt(jnp.tanh(-log_a) * (1.0 + a * a)) * (ig * xc)

    row8 = lax.broadcasted_iota(jnp.int32, (TM, D_LRU), 0) & (SUBLANES - 1)
    d = 1
    while d < SUBLANES:
        valid = row8 >= d
        a_s = jnp.where(valid, pltpu.roll(a, d, axis=0), 1.0)
        b_s = jnp.where(valid, pltpu.roll(bt, d, axis=0), 0.0)
        bt = bt + a * b_s
        a = a * a_s
        d *= 2

    h_prev = h_ref[...]
    sg = _silu(g_lru)
    outs = []
    for g in range(TM // SUBLANES):
        lo = g * SUBLANES
        hg = bt[lo:lo + SUBLANES] + a[lo:lo + SUBLANES] * h_prev
        h_prev = hg[SUBLANES - 1:SUBLANES, :]
        outs.append(hg * sg[lo:lo + SUBLANES])
    h_ref[...] = h_prev
    return jnp.concatenate(outs, axis=0)


def _inproj_kernel(x_ref, shift_ref, scale_ref, g_ref, we_ref, wg_ref, wk_ref, wt_ref, bf_ref,
                   cw_ref, cb_ref, wa_ref, ba_ref, wx_ref, bx_ref, lam_ref,
                   lng_ref, lnb_ref, ws_ref, sb_ref,
                   qt_ref, vt_ref, k_ref, ka_ref, qa_ref, ga_ref, yl_ref, ys_ref,
                   fc_ref, xbuf_ref, h_ref):
    i = pl.program_id(1)

    @pl.when(i == 0)
    def _():
        fc_ref[...] = jnp.zeros(fc_ref.shape, F32)
        xbuf_ref[0:SUBLANES, :] = jnp.zeros((SUBLANES, D_LRU), F32)
        h_ref[...] = jnp.zeros(h_ref.shape, F32)

    x = x_ref[0]
    ms = jnp.mean(x * x, axis=-1, keepdims=True)
    gs = g_ref[0] * (1.0 + _mod_row(scale_ref))
    h = (x * lax.rsqrt(ms + EPS)) * gs + _mod_row(shift_ref)
    hb = h.astype(BF16)

    early = jnp.dot(hb, we_ref[0], preferred_element_type=F32)
    x_lru = early[:, :D_LRU]
    sg_v = early[:, D_LRU:]
    tr = lax.dot_general(wt_ref[0], hb, (((1,), (1,)), ((), ())), preferred_element_type=F32)
    qt_ref[0] = (tr[:D_ATT] * Q_SCALE).astype(BF16)
    vt_ref[0] = tr[D_ATT:2 * D_ATT].astype(BF16)
    fl = tr[2 * D_ATT:2 * D_ATT + N_AUG]

    xbuf_ref[SUBLANES:SUBLANES + TM, :] = x_lru
    xc = cb_ref[0]
    for kk in range(CONV_WIDTH):
        off = SUBLANES - (CONV_WIDTH - 1) + kk
        xc = xc + xbuf_ref[off:off + TM, :] * cw_ref[0, kk:kk + 1, :]
    xbuf_ref[0:SUBLANES, :] = x_lru[TM - SUBLANES:, :]
    xcb = xc.astype(BF16)
    r_pre = jnp.dot(xcb, wa_ref[0], preferred_element_type=F32) + ba_ref[0]
    i_pre = jnp.dot(xcb, wx_ref[0], preferred_element_type=F32) + bx_ref[0]

    v = _gelu_tanh(sg_v)
    mu = jnp.mean(v, axis=-1, keepdims=True)
    var = jnp.mean(jnp.square(v - mu), axis=-1, keepdims=True)
    vnb = (((v - mu) * lax.rsqrt(var + EPS)) * lng_ref[0] + lnb_ref[0]).astype(BF16)
    w_shape = ws_ref.shape[1:]
    t_idx = lax.broadcasted_iota(jnp.int32, w_shape, 0) & (SG_CHUNK - 1)
    s_idx = lax.broadcasted_iota(jnp.int32, w_shape, 1)
    ws = jnp.where(s_idx <= t_idx, ws_ref[0], jnp.zeros(w_shape, BF16))
    grp = lax.broadcasted_iota(jnp.int32, (SG_CHUNK, D_SG), 1) // HEAD_DIM
    zs = []
    for c in range(TM // SG_CHUNK):
        zz = jnp.dot(ws, vnb[c * SG_CHUNK:(c + 1) * SG_CHUNK], preferred_element_type=F32)
        z = zz[0:SG_CHUNK]
        for gi in range(1, N_SG_GROUPS):
            z = jnp.where(grp == gi, zz[gi * SG_CHUNK:(gi + 1) * SG_CHUNK], z)
        zs.append(z + sb_ref[0])
    z_all = jnp.concatenate(zs, axis=0)

    gates = jnp.dot(hb, wg_ref[0], preferred_element_type=F32)
    k_ref[0] = jnp.dot(hb, wk_ref[0], preferred_element_type=F32).astype(BF16)

    ga_ref[0] = _silu(gates[:, :D_ATT]).astype(BF16)
    g_lru = gates[:, D_ATT:D_ATT + D_LRU]
    sg_u = gates[:, D_ATT + D_LRU:D_ATT + D_LRU + D_SG]
    g_sg = gates[:, D_ATT + D_LRU + D_SG:]
    ys_ref[0] = ((_gelu_tanh(sg_u) * z_all) * _silu(g_sg)).astype(BF16)
    yl_ref[0] = _lru_mixer(g_lru, r_pre, i_pre, lam_ref[0], h_ref, xc).astype(BF16)

    reps = TM // LANES
    ls = _log_sigmoid(fl + jnp.concatenate([bf_ref[0]] * reps, axis=1))
    lane = lax.broadcasted_iota(jnp.int32, (N_AUG, TM), 1)
    d = 1
    while d < TM:
        ls = ls + jnp.where(lane >= d, pltpu.roll(ls, d, axis=1), 0.0)
        d *= 2
    f = ls + jnp.concatenate([fc_ref[...]] * reps, axis=1)
    fc_ref[...] = jnp.broadcast_to(f[:, TM - 1:TM], (N_AUG, LANES))
    row = lax.broadcasted_iota(jnp.int32, (N_AUG, TM), 0)
    parts = _split3(f * LOG2E, row % N_PARTS)
    ones = jnp.ones((N_AUG, TM), F32)
    zeros = jnp.zeros((LANES - 2 * N_AUG, TM), F32)
    qa_ref[0] = jnp.concatenate([ones, parts, zeros], axis=0).astype(BF16)
    ka_ref[0] = jnp.concatenate([-parts, ones, zeros], axis=0).T.astype(BF16)


def _inproj(layer, x, mod, prep):
    B, S, D = x.shape
    per_layer = lambda a: pl.BlockSpec((1,) + a.shape[1:], lambda b, i: (layer,) + (0,) * (a.ndim - 1))
    mod_chunk = lambda j: pl.BlockSpec((1, 1, D // TN_ADA, 1, TN_ADA), lambda b, i: (layer, b, j, 0, 0))
    row_tile = lambda n: pl.BlockSpec((1, TM, n), lambda b, i: (b, i, 0))
    col_tile = lambda n: pl.BlockSpec((1, n, TM), lambda b, i: (b, 0, i))
    names = ("pre_g", "we", "wg", "wk", "wt", "bf", "conv_w", "conv_b", "wa", "ba", "wx", "bx", "lam",
             "ln_g", "ln_b", "ws", "sg_bias")
    operands = [prep[n] for n in names]
    return pl.pallas_call(
        _inproj_kernel,
        grid=(B, S // TM),
        in_specs=[row_tile(D), mod_chunk(0), mod_chunk(1)] + [per_layer(a) for a in operands],
        out_specs=[
            col_tile(D_ATT), col_tile(D_ATT), row_tile(D_ATT), row_tile(LANES), col_tile(LANES),
            row_tile(D_ATT), row_tile(D_LRU), row_tile(D_SG),
        ],
        out_shape=[
            jax.ShapeDtypeStruct((B, D_ATT, S), BF16),
            jax.ShapeDtypeStruct((B, D_ATT, S), BF16),
            jax.ShapeDtypeStruct((B, S, D_ATT), BF16),
            jax.ShapeDtypeStruct((B, S, LANES), BF16),
            jax.ShapeDtypeStruct((B, LANES, S), BF16),
            jax.ShapeDtypeStruct((B, S, D_ATT), BF16),
            jax.ShapeDtypeStruct((B, S, D_LRU), BF16),
            jax.ShapeDtypeStruct((B, S, D_SG), BF16),
        ],
        scratch_shapes=[
            pltpu.VMEM((N_AUG, LANES), F32),
            pltpu.VMEM((TM + SUBLANES, D_LRU), F32),
            pltpu.VMEM((1, D_LRU), F32),
        ],
        compiler_params=_params(("parallel", "arbitrary")),
        name="inproj",
    )(x, mod, mod, *operands)


def _attn_kernel(qt_ref, qa_ref, k_ref, ka_ref, vt_ref, g_ref, o_ref,
                 m_ref, acc_ref, rhs_ref, s0_ref, s1_ref, p0_ref, p1_ref, mx0_ref, mx1_ref, al0_ref, al1_ref):
    p = pl.program_id(1)
    i = pl.program_id(2)
    qt = qt_ref[0]
    qa = qa_ref[0]
    row = lax.broadcasted_iota(jnp.int32, (LANES, TQ), 0)
    zero = jnp.zeros_like(qt)
    for hh in range(2):
        head = 2 * p + hh
        own = (row >= hh * HEAD_DIM) & (row < (hh + 1) * HEAD_DIM)
        lo = head * N_PARTS
        aug = ((row >= lo) & (row < lo + N_PARTS)) | ((row >= N_AUG + lo) & (row < N_AUG + lo + N_PARTS))
        rhs_ref[hh] = jnp.concatenate([jnp.where(own, qt, zero), jnp.where(aug, qa, zero)], axis=0)
    ones_rows = jnp.where(lax.broadcasted_iota(jnp.int32, (BF16_ROWS, TQ), 0) == 0, 1.0, 0.0).astype(BF16)
    m_ref[...] = jnp.full(m_ref.shape, NEG, F32)
    acc_ref[...] = jnp.zeros(acc_ref.shape, F32)
    s_bufs = (s0_ref, s1_ref)
    p_bufs = (p0_ref, p1_ref)
    mx_bufs = (mx0_ref, mx1_ref)
    al_bufs = (al0_ref, al1_ref)

    def scores(t, slot, masked):
        rows = pl.ds(pl.multiple_of(t * TQ, TQ), TQ)
        lhs = jnp.concatenate([k_ref[0, rows, :], ka_ref[0, rows, :]], axis=1)
        for hh in range(2):
            st = jnp.dot(lhs, rhs_ref[hh], preferred_element_type=F32)
            if masked:
                kk = lax.broadcasted_iota(jnp.int32, (TQ, TQ), 0)
                qq = lax.broadcasted_iota(jnp.int32, (TQ, TQ), 1)
                st = jnp.where(kk <= qq, st, NEG)
            s_bufs[slot][hh] = st
            mx_bufs[slot][hh] = jnp.max(st, axis=0, keepdims=True)

    def softmax(slot):
        for hh in range(2):
            m_old = m_ref[hh]
            m_new = jnp.maximum(m_old, mx_bufs[slot][hh])
            p_bufs[slot][hh] = jnp.exp2(s_bufs[slot][hh] - m_new).astype(BF16)
            al_bufs[slot][hh] = jnp.exp2(m_old - m_new)
            m_ref[hh] = m_new

    def values(t, slot):
        cols = pl.ds(pl.multiple_of(t * TQ, TQ), TQ)
        for hh in range(2):
            vaug = jnp.concatenate([vt_ref[0, hh * HEAD_DIM:(hh + 1) * HEAD_DIM, cols], ones_rows], axis=0)
            acc_ref[hh] = acc_ref[hh] * al_bufs[slot][hh] + jnp.dot(vaug, p_bufs[slot][hh],
                                                                     preferred_element_type=F32)

    def step(tau, parity, do_scores=False, masked=False, do_softmax=False, do_values=False):
        if do_softmax:
            softmax(1 - parity)
        if do_scores:
            scores(tau, parity, masked)
        if do_values:
            values(tau - 2, parity)

    full = dict(do_scores=True, do_softmax=True, do_values=True)

    @pl.when(i == 0)
    def _():
        step(0, 0, do_scores=True, masked=True)
        step(1, 1, do_softmax=True)
        step(2, 0, do_values=True)

    @pl.when(i == 1)
    def _():
        step(0, 0, do_scores=True)
        step(1, 1, do_scores=True, masked=True, do_softmax=True)
        step(2, 0, do_softmax=True, do_values=True)
        step(3, 1, do_values=True)

    @pl.when(i >= 2)
    def _():
        step(0, 0, do_scores=True)
        step(1, 1, do_scores=True, do_softmax=True)

        def body(u, carry):
            step(2 + 2 * u, 0, **full)
            step(3 + 2 * u, 1, **full)
            return carry

        lax.fori_loop(0, lax.shift_right_logical(i - 2, 1), body, 0)

        @pl.when((i & 1) == 0)
        def _():
            step(i, 0, masked=True, **full)
            step(i + 1, 1, do_softmax=True, do_values=True)
            step(i + 2, 0, do_values=True)

        @pl.when((i & 1) == 1)
        def _():
            step(i - 1, 0, **full)
            step(i, 1, masked=True, **full)
            step(i + 1, 0, do_softmax=True, do_values=True)
            step(i + 2, 1, do_values=True)

    outs = []
    for hh in range(2):
        a = acc_ref[hh]
        outs.append(a[:HEAD_DIM] / a[HEAD_DIM:HEAD_DIM + 1])
    o = jnp.concatenate(outs, axis=0).T
    o_ref[0] = (o * g_ref[0].astype(F32)).astype(BF16)


def _attention(qt, qa, k, ka, vt, gate):
    B, S, _ = k.shape
    n_pair = N_HEADS // 2
    return pl.pallas_call(
        _attn_kernel,
        grid=(B, n_pair, S // TQ),
        in_specs=[
            pl.BlockSpec((1, LANES, TQ), lambda b, p, i: (b, p, i)),
            pl.BlockSpec((1, LANES, TQ), lambda b, p, i: (b, 0, i)),
            pl.BlockSpec((1, S, LANES), lambda b, p, i: (b, 0, p)),
            pl.BlockSpec((1, S, LANES), lambda b, p, i: (b, 0, 0)),
            pl.BlockSpec((1, LANES, S), lambda b, p, i: (b, p, 0)),
            pl.BlockSpec((1, TQ, LANES), lambda b, p, i: (b, i, p)),
        ],
        out_specs=pl.BlockSpec((1, TQ, LANES), lambda b, p, i: (b, i, p)),
        out_shape=jax.ShapeDtypeStruct((B, S, D_ATT), BF16),
        scratch_shapes=[
            pltpu.VMEM((2, 1, TQ), F32),
            pltpu.VMEM((2, V_ROWS, TQ), F32),
            pltpu.VMEM((2, 2 * LANES, TQ), BF16),
            pltpu.VMEM((2, TQ, TQ), F32),
            pltpu.VMEM((2, TQ, TQ), F32),
            pltpu.VMEM((2, TQ, TQ), BF16),
            pltpu.VMEM((2, TQ, TQ), BF16),
            pltpu.VMEM((2, 1, TQ), F32),
            pltpu.VMEM((2, 1, TQ), F32),
            pltpu.VMEM((2, 1, TQ), F32),
            pltpu.VMEM((2, 1, TQ), F32),
        ],
        compiler_params=_params(("parallel", "parallel", "parallel")),
        name="fox_attention",
    )(qt, qa, k, ka, vt, gate)


def _outproj_kernel(ya_ref, yl_ref, ys_ref, x_ref, gate_ref, pg_ref, w_ref, o_ref):
    y = jnp.dot(ya_ref[0], w_ref[0, 0:D_ATT, :], preferred_element_type=F32)
    y = y + jnp.dot(yl_ref[0], w_ref[0, D_ATT:D_ATT + D_LRU, :], preferred_element_type=F32)
    y = y + jnp.dot(ys_ref[0], w_ref[0, D_ATT + D_LRU:, :], preferred_element_type=F32)
    ms = jnp.mean(y * y, axis=-1, keepdims=True)
    yn = (y * lax.rsqrt(ms + EPS)) * pg_ref[0]
    o_ref[0] = x_ref[0] + _mod_row(gate_ref) * yn


def _outproj(layer, ya, yl, ys, x, mod, prep):
    B, S, D = x.shape
    per_layer = lambda a: pl.BlockSpec((1,) + a.shape[1:], lambda b, i: (layer,) + (0,) * (a.ndim - 1))
    row_tile = lambda n: pl.BlockSpec((1, TO, n), lambda b, i: (b, i, 0))
    return pl.pallas_call(
        _outproj_kernel,
        grid=(B, S // TO),
        in_specs=[
            row_tile(D_ATT), row_tile(D_LRU), row_tile(D_SG), row_tile(D),
            pl.BlockSpec((1, 1, D // TN_ADA, 1, TN_ADA), lambda b, i: (layer, b, 2, 0, 0)),
            per_layer(prep["post_g"]),
            per_layer(prep["w_out"]),
        ],
        out_specs=row_tile(D),
        out_shape=jax.ShapeDtypeStruct((B, S, D), F32),
        compiler_params=_params(("parallel", "parallel")),
        name="outproj",
    )(ya, yl, ys, x, mod, prep["post_g"], prep["w_out"])


def _block_diag(w):
    L, G, n, _ = w.shape
    eye = jnp.eye(G, dtype=w.dtype)
    return (w[:, :, :, None, :] * eye[None, :, None, :, None]).reshape(L, G * n, G * n)


def _prepare(pre_g, post_g, w_in, b_f, conv_w, conv_b, lru_wa, lru_ba, lru_wx, lru_bx, lru_lambda,
             sg_ln_g, sg_ln_b, sg_w, sg_b, w_out):
    L, D, _ = w_in.shape
    cuts = [0]
    for n in IN_SIZES:
        cuts.append(cuts[-1] + n)
    wq, wk, wv, wf, wga, wxl, wgl, wsu, wsv, wgs = (w_in[:, :, cuts[j]:cuts[j + 1]] for j in range(10))
    row = lambda a: a[:, None, :]
    wf3 = jnp.repeat(wf, N_PARTS, axis=2)
    pad = jnp.zeros((L, D, T_ROWS - 2 * D_ATT - N_AUG), F32)
    return dict(
        pre_g=row(pre_g), post_g=row(post_g),
        we=jnp.concatenate([wxl, wsv], axis=2).astype(BF16),
        wg=jnp.concatenate([wga, wgl, wsu, wgs], axis=2).astype(BF16),
        wk=wk.astype(BF16),
        wt=jnp.concatenate([wq, wv, wf3, pad], axis=2).transpose(0, 2, 1).astype(BF16),
        bf=jnp.broadcast_to(jnp.repeat(b_f, N_PARTS, axis=1)[:, :, None], (L, N_AUG, LANES)),
        conv_w=conv_w, conv_b=row(conv_b),
        wa=_block_diag(lru_wa).astype(BF16), ba=row(lru_ba),
        wx=_block_diag(lru_wx).astype(BF16), bx=row(lru_bx), lam=row(lru_lambda),
        ln_g=row(sg_ln_g), ln_b=row(sg_ln_b),
        ws=sg_w.reshape(L, N_SG_GROUPS * SG_CHUNK, SG_CHUNK).astype(BF16),
        sg_bias=jnp.repeat(sg_b.transpose(0, 2, 1), HEAD_DIM, axis=2),
        w_out=w_out.astype(BF16),
    )


def kernel(x, c, ada_w, ada_b, pre_g, post_g, w_in, b_f, conv_w, conv_b, lru_wa, lru_ba, lru_wx,
           lru_bx, lru_lambda, sg_ln_g, sg_ln_b, sg_w, sg_b, w_out):
    mod = _ada(c, ada_w, ada_b)
    prep = _prepare(pre_g, post_g, w_in, b_f, conv_w, conv_b, lru_wa, lru_ba, lru_wx, lru_bx,
                    lru_lambda, sg_ln_g, sg_ln_b, sg_w, sg_b, w_out)
    for layer in range(ada_w.shape[0]):
        qt, vt, k, ka, qa, ga, yl, ys = _inproj(layer, x, mod, prep)
        ya = _attention(qt, qa, k, ka, vt, ga)
        x = _outproj(layer, ya, yl, ys, x, mod, prep)
    return x
```

```python
import math

import jax
import jax.numpy as jnp
from jax import lax
from jax.experimental import pallas as pl
from jax.experimental.pallas import tpu as pltpu

D_MODEL = 1024
HEAD_DIM = 64
D_ATT = 512
D_LRU = 256
D_SG = 256
N_HEADS = D_ATT // HEAD_DIM
N_LRU_BLOCKS = D_LRU // HEAD_DIM
N_SG_GROUPS = D_SG // HEAD_DIM
SG_CHUNK = 128
CONV_WIDTH = 4
LRU_C = 8.0
EPS = 1e-6
IN_SIZES = (D_ATT, D_ATT, D_ATT, N_HEADS, D_ATT, D_LRU, D_LRU, D_SG, D_SG, D_SG)

LANES = 128
SUBLANES = 8
BF16_ROWS = 16
VMEM_LIMIT = 56 * 1024 * 1024

TM = 512
TO = 1024
TQ = 512
TN_ADA = 512
NEG = -1e30

N_PARTS = 3
N_AUG = N_HEADS * N_PARTS
T_ROWS = 2 * D_ATT + 2 * BF16_ROWS
LOG2E = 1.4426950408889634
Q_SCALE = LOG2E / math.sqrt(HEAD_DIM)
V_ROWS = HEAD_DIM + BF16_ROWS
GELU_C0 = math.sqrt(2.0 / math.pi)
GELU_C1 = GELU_C0 * 0.044715

F32 = jnp.float32
BF16 = jnp.bfloat16


def _sigmoid(x):
    return 0.5 * jnp.tanh(0.5 * x) + 0.5


def _silu(x):
    hx = 0.5 * x
    return hx + hx * jnp.tanh(hx)


def _gelu_tanh(x):
    hx = 0.5 * x
    return hx + hx * jnp.tanh(x * (GELU_C0 + GELU_C1 * (x * x)))


def _log_sigmoid(x):
    return jnp.minimum(x, 0.0) - jnp.log1p(jnp.exp(-jnp.abs(x)))


def _split3(x, sel):
    hi = x.astype(BF16).astype(F32)
    r1 = x - hi
    mid = r1.astype(BF16).astype(F32)
    lo = r1 - mid
    return jnp.where(sel == 0, hi, jnp.where(sel == 1, mid, lo))


def _mod_row(ref):
    return jnp.concatenate([ref[0, 0, j] for j in range(ref.shape[2])], axis=1)


def _params(sem):
    return pltpu.CompilerParams(dimension_semantics=sem, vmem_limit_bytes=VMEM_LIMIT)


def _ada_kernel(ct_ref, w_ref, b_ref, o_ref):
    ct = ct_ref[...]
    sc = _silu(ct)
    w = w_ref[0]
    bias = b_ref[0]
    for b in range(ct.shape[1]):
        col = sc[:, b:b + 1]
        o_ref[0, b, 0] = jnp.sum(col * w, axis=0, keepdims=True) + bias


def _ada(c, ada_w, ada_b):
    L, D, N = ada_w.shape
    B = c.shape[0]
    return pl.pallas_call(
        _ada_kernel,
        grid=(L, N // TN_ADA),
        in_specs=[
            pl.BlockSpec((D, B), lambda l, n: (0, 0)),
            pl.BlockSpec((1, D, TN_ADA), lambda l, n: (l, 0, n)),
            pl.BlockSpec((1, 1, TN_ADA), lambda l, n: (l, 0, n)),
        ],
        out_specs=pl.BlockSpec((1, B, 1, 1, TN_ADA), lambda l, n: (l, 0, n, 0, 0)),
        out_shape=jax.ShapeDtypeStruct((L, B, N // TN_ADA, 1, TN_ADA), F32),
        compiler_params=_params(("parallel", "parallel")),
        name="ada_mod",
    )(c.T, ada_w, ada_b.reshape(L, 1, N))


def _lru_mixer(g_lru, r_pre, i_pre, lam, h_ref, xc):
    r = _sigmoid(r_pre)
    ig = _sigmoid(i_pre)
    nlam = -lam
    softplus = jnp.maximum(nlam, 0.0) + jnp.log1p(jnp.exp(-jnp.abs(nlam)))
    log_a = (-LRU_C * r) * softplus
    a = jnp.exp(log_a)
    bt = jnp.sqrt(jnp.tanh(-log_a) * (1.0 + a * a)) * (ig * xc)

    row8 = lax.broadcasted_iota(jnp.int32, (TM, D_LRU), 0) & (SUBLANES - 1)
    d = 1
    while d < SUBLANES:
        valid = row8 >= d
        a_s = jnp.where(valid, pltpu.roll(a, d, axis=0), 1.0)
        b_s = jnp.where(valid, pltpu.roll(bt, d, axis=0), 0.0)
        bt = bt + a * b_s
        a = a * a_s
        d *= 2

    h_prev = h_ref[...]
    sg = _silu(g_lru)
    outs = []
    for g in range(TM // SUBLANES):
        lo = g * SUBLANES
        hg = bt[lo:lo + SUBLANES] + a[lo:lo + SUBLANES] * h_prev
        h_prev = hg[SUBLANES - 1:SUBLANES, :]
        outs.append(hg * sg[lo:lo + SUBLANES])
    h_ref[...] = h_prev
    return jnp.concatenate(outs, axis=0)


def _inproj_kernel(x_ref, shift_ref, scale_ref, g_ref, we_ref, wg_ref, wk_ref, wt_ref, bf_ref,
                   cw_ref, cb_ref, wa_ref, ba_ref, wx_ref, bx_ref, lam_ref,
                   lng_ref, lnb_ref, ws_ref, sb_ref,
                   qt_ref, vt_ref, k_ref, ka_ref, qa_ref, ga_ref, yl_ref, ys_ref,
                   fc_ref, xbuf_ref, h_ref):
    i = pl.program_id(1)

    @pl.when(i == 0)
    def _():
        fc_ref[...] = jnp.zeros(fc_ref.shape, F32)
        xbuf_ref[0:SUBLANES, :] = jnp.zeros((SUBLANES, D_LRU), F32)
        h_ref[...] = jnp.zeros(h_ref.shape, F32)

    x = x_ref[0]
    ms = jnp.mean(x * x, axis=-1, keepdims=True)
    gs = g_ref[0] * (1.0 + _mod_row(scale_ref))
    h = (x * lax.rsqrt(ms + EPS)) * gs + _mod_row(shift_ref)
    hb = h.astype(BF16)

    early = jnp.dot(hb, we_ref[0], preferred_element_type=F32)
    x_lru = early[:, :D_LRU]
    sg_v = early[:, D_LRU:]
    tr = lax.dot_general(wt_ref[0], hb, (((1,), (1,)), ((), ())), preferred_element_type=F32)
    qt_ref[0] = (tr[:D_ATT] * Q_SCALE).astype(BF16)
    vt_ref[0] = tr[D_ATT:2 * D_ATT].astype(BF16)
    fl = tr[2 * D_ATT:2 * D_ATT + N_AUG]
    k_ref[0] = jnp.dot(hb, wk_ref[0], preferred_element_type=F32).astype(BF16)

    v = _gelu_tanh(sg_v)
    mu = jnp.mean(v, axis=-1, keepdims=True)
    var = jnp.mean(jnp.square(v - mu), axis=-1, keepdims=True)
    vnb = (((v - mu) * lax.rsqrt(var + EPS)) * lng_ref[0] + lnb_ref[0]).astype(BF16)
    w_shape = ws_ref.shape[1:]
    t_idx = lax.broadcasted_iota(jnp.int32, w_shape, 0) & (SG_CHUNK - 1)
    s_idx = lax.broadcasted_iota(jnp.int32, w_shape, 1)
    ws = jnp.where(s_idx <= t_idx, ws_ref[0], jnp.zeros(w_shape, BF16))
    grp = lax.broadcasted_iota(jnp.int32, (SG_CHUNK, D_SG), 1) // HEAD_DIM
    zs = []
    for c in range(TM // SG_CHUNK):
        zz = jnp.dot(ws, vnb[c * SG_CHUNK:(c + 1) * SG_CHUNK], preferred_element_type=F32)
        z = zz[0:SG_CHUNK]
        for gi in range(1, N_SG_GROUPS):
            z = jnp.where(grp == gi, zz[gi * SG_CHUNK:(gi + 1) * SG_CHUNK], z)
        zs.append(z + sb_ref[0])
    z_all = jnp.concatenate(zs, axis=0)

    xbuf_ref[SUBLANES:SUBLANES + TM, :] = x_lru
    xc = cb_ref[0]
    for kk in range(CONV_WIDTH):
        off = SUBLANES - (CONV_WIDTH - 1) + kk
        xc = xc + xbuf_ref[off:off + TM, :] * cw_ref[0, kk:kk + 1, :]
    xbuf_ref[0:SUBLANES, :] = x_lru[TM - SUBLANES:, :]
    xcb = xc.astype(BF16)
    r_pre = jnp.dot(xcb, wa_ref[0], preferred_element_type=F32) + ba_ref[0]
    i_pre = jnp.dot(xcb, wx_ref[0], preferred_element_type=F32) + bx_ref[0]

    gates = jnp.dot(hb, wg_ref[0], preferred_element_type=F32)

    ga_ref[0] = gates[:, :D_ATT].astype(BF16)
    g_lru = gates[:, D_ATT:D_ATT + D_LRU]
    sg_u = gates[:, D_ATT + D_LRU:D_ATT + D_LRU + D_SG]
    g_sg = gates[:, D_ATT + D_LRU + D_SG:]
    ys_ref[0] = ((_gelu_tanh(sg_u) * z_all) * _silu(g_sg)).astype(BF16)
    yl_ref[0] = _lru_mixer(g_lru, r_pre, i_pre, lam_ref[0], h_ref, xc).astype(BF16)

    reps = TM // LANES
    ls = _log_sigmoid(fl + jnp.concatenate([bf_ref[0]] * reps, axis=1))
    lane = lax.broadcasted_iota(jnp.int32, (N_AUG, TM), 1)
    d = 1
    while d < TM:
        ls = ls + jnp.where(lane >= d, pltpu.roll(ls, d, axis=1), 0.0)
        d *= 2
    f = ls + jnp.concatenate([fc_ref[...]] * reps, axis=1)
    fc_ref[...] = jnp.broadcast_to(f[:, TM - 1:TM], (N_AUG, LANES))
    row = lax.broadcasted_iota(jnp.int32, (N_AUG, TM), 0)
    parts = _split3(f * LOG2E, row % N_PARTS)
    ones = jnp.ones((N_AUG, TM), F32)
    zeros = jnp.zeros((LANES - 2 * N_AUG, TM), F32)
    qa_ref[0] = jnp.concatenate([ones, parts, zeros], axis=0).astype(BF16)
    ka_ref[0] = jnp.concatenate([-parts, ones, zeros], axis=0).T.astype(BF16)


def _inproj(layer, x, mod, prep):
    B, S, D = x.shape
    per_layer = lambda a: pl.BlockSpec((1,) + a.shape[1:], lambda b, i: (layer,) + (0,) * (a.ndim - 1))
    mod_chunk = lambda j: pl.BlockSpec((1, 1, D // TN_ADA, 1, TN_ADA), lambda b, i: (layer, b, j, 0, 0))
    row_tile = lambda n: pl.BlockSpec((1, TM, n), lambda b, i: (b, i, 0))
    col_tile = lambda n: pl.BlockSpec((1, n, TM), lambda b, i: (b, 0, i))
    names = ("pre_g", "we", "wg", "wk", "wt", "bf", "conv_w", "conv_b", "wa", "ba", "wx", "bx", "lam",
             "ln_g", "ln_b", "ws", "sg_bias")
    operands = [prep[n] for n in names]
    return pl.pallas_call(
        _inproj_kernel,
        grid=(B, S // TM),
        in_specs=[row_tile(D), mod_chunk(0), mod_chunk(1)] + [per_layer(a) for a in operands],
        out_specs=[
            col_tile(D_ATT), col_tile(D_ATT), row_tile(D_ATT), row_tile(LANES), col_tile(LANES),
            row_tile(D_ATT), row_tile(D_LRU), row_tile(D_SG),
        ],
        out_shape=[
            jax.ShapeDtypeStruct((B, D_ATT, S), BF16),
            jax.ShapeDtypeStruct((B, D_ATT, S), BF16),
            jax.ShapeDtypeStruct((B, S, D_ATT), BF16),
            jax.ShapeDtypeStruct((B, S, LANES), BF16),
            jax.ShapeDtypeStruct((B, LANES, S), BF16),
            jax.ShapeDtypeStruct((B, S, D_ATT), BF16),
            jax.ShapeDtypeStruct((B, S, D_LRU), BF16),
            jax.ShapeDtypeStruct((B, S, D_SG), BF16),
        ],
        scratch_shapes=[
            pltpu.VMEM((N_AUG, LANES), F32),
            pltpu.VMEM((TM + SUBLANES, D_LRU), F32),
            pltpu.VMEM((1, D_LRU), F32),
        ],
        compiler_params=_params(("parallel", "arbitrary")),
        name="inproj",
    )(x, mod, mod, *operands)


def _attn_kernel(qt_ref, qa_ref, k_ref, ka_ref, vt_ref, g_ref, o_ref,
                 m_ref, acc_ref, rhs_ref, s0_ref, s1_ref, p0_ref, p1_ref, mx0_ref, mx1_ref, al0_ref, al1_ref):
    p = pl.program_id(1)
    i = pl.program_id(2)
    qt = qt_ref[0]
    qa = qa_ref[0]
    row = lax.broadcasted_iota(jnp.int32, (LANES, TQ), 0)
    zero = jnp.zeros_like(qt)
    for hh in range(2):
        head = 2 * p + hh
        own = (row >= hh * HEAD_DIM) & (row < (hh + 1) * HEAD_DIM)
        lo = head * N_PARTS
        aug = ((row >= lo) & (row < lo + N_PARTS)) | ((row >= N_AUG + lo) & (row < N_AUG + lo + N_PARTS))
        rhs_ref[hh] = jnp.concatenate([jnp.where(own, qt, zero), jnp.where(aug, qa, zero)], axis=0)
    ones_rows = jnp.where(lax.broadcasted_iota(jnp.int32, (BF16_ROWS, TQ), 0) == 0, 1.0, 0.0).astype(BF16)
    m_ref[...] = jnp.full(m_ref.shape, NEG, F32)
    acc_ref[...] = jnp.zeros(acc_ref.shape, F32)
    s_bufs = (s0_ref, s1_ref)
    p_bufs = (p0_ref, p1_ref)
    mx_bufs = (mx0_ref, mx1_ref)
    al_bufs = (al0_ref, al1_ref)

    def scores(t, slot, masked):
        rows = pl.ds(pl.multiple_of(t * TQ, TQ), TQ)
        lhs = jnp.concatenate([k_ref[0, rows, :], ka_ref[0, rows, :]], axis=1)
        for hh in range(2):
            st = jnp.dot(lhs, rhs_ref[hh], preferred_element_type=F32)
            if masked:
                kk = lax.broadcasted_iota(jnp.int32, (TQ, TQ), 0)
                qq = lax.broadcasted_iota(jnp.int32, (TQ, TQ), 1)
                st = jnp.where(kk <= qq, st, NEG)
            s_bufs[slot][hh] = st
            mx_bufs[slot][hh] = jnp.max(st, axis=0, keepdims=True)

    def softmax(slot):
        for hh in range(2):
            m_old = m_ref[hh]
            m_new = jnp.maximum(m_old, mx_bufs[slot][hh])
            p_bufs[slot][hh] = jnp.exp2(s_bufs[slot][hh] - m_new).astype(BF16)
            al_bufs[slot][hh] = jnp.exp2(m_old - m_new)
            m_ref[hh] = m_new

    def values(t, slot):
        cols = pl.ds(pl.multiple_of(t * TQ, TQ), TQ)
        for hh in range(2):
            vaug = jnp.concatenate([vt_ref[0, hh * HEAD_DIM:(hh + 1) * HEAD_DIM, cols], ones_rows], axis=0)
            acc_ref[hh] = acc_ref[hh] * al_bufs[slot][hh] + jnp.dot(vaug, p_bufs[slot][hh],
                                                                     preferred_element_type=F32)

    def step(tau, parity, do_scores=False, masked=False, do_softmax=False, do_values=False):
        if do_softmax:
            softmax(1 - parity)
        if do_scores:
            scores(tau, parity, masked)
        if do_values:
            values(tau - 2, parity)

    full = dict(do_scores=True, do_softmax=True, do_values=True)

    @pl.when(i == 0)
    def _():
        step(0, 0, do_scores=True, masked=True)
        step(1, 1, do_softmax=True)
        step(2, 0, do_values=True)

    @pl.when(i == 1)
    def _():
        step(0, 0, do_scores=True)
        step(1, 1, do_scores=True, masked=True, do_softmax=True)
        step(2, 0, do_softmax=True, do_values=True)
        step(3, 1, do_values=True)

    @pl.when(i >= 2)
    def _():
        step(0, 0, do_scores=True)
        step(1, 1, do_scores=True, do_softmax=True)

        def body(u, carry):
            step(2 + 2 * u, 0, **full)
            step(3 + 2 * u, 1, **full)
            return carry

        lax.fori_loop(0, lax.shift_right_logical(i - 2, 1), body, 0)

        @pl.when((i & 1) == 0)
        def _():
            step(i, 0, masked=True, **full)
            step(i + 1, 1, do_softmax=True, do_values=True)
            step(i + 2, 0, do_values=True)

        @pl.when((i & 1) == 1)
        def _():
            step(i - 1, 0, **full)
            step(i, 1, masked=True, **full)
            step(i + 1, 0, do_softmax=True, do_values=True)
            step(i + 2, 1, do_values=True)

    outs = []
    for hh in range(2):
        a = acc_ref[hh]
        outs.append(a[:HEAD_DIM] / a[HEAD_DIM:HEAD_DIM + 1])
    o = jnp.concatenate(outs, axis=0).T
    o_ref[0] = (o * _silu(g_ref[0].astype(F32))).astype(BF16)


def _attention(qt, qa, k, ka, vt, gate):
    B, S, _ = k.shape
    n_pair = N_HEADS // 2
    return pl.pallas_call(
        _attn_kernel,
        grid=(B, n_pair, S // TQ),
        in_specs=[
            pl.BlockSpec((1, LANES, TQ), lambda b, p, i: (b, p, i)),
            pl.BlockSpec((1, LANES, TQ), lambda b, p, i: (b, 0, i)),
            pl.BlockSpec((1, S, LANES), lambda b, p, i: (b, 0, p)),
            pl.BlockSpec((1, S, LANES), lambda b, p, i: (b, 0, 0)),
            pl.BlockSpec((1, LANES, S), lambda b, p, i: (b, p, 0)),
            pl.BlockSpec((1, TQ, LANES), lambda b, p, i: (b, i, p)),
        ],
        out_specs=pl.BlockSpec((1, TQ, LANES), lambda b, p, i: (b, i, p)),
        out_shape=jax.ShapeDtypeStruct((B, S, D_ATT), BF16),
        scratch_shapes=[
            pltpu.VMEM((2, 1, TQ), F32),
            pltpu.VMEM((2, V_ROWS, TQ), F32),
            pltpu.VMEM((2, 2 * LANES, TQ), BF16),
            pltpu.VMEM((2, TQ, TQ), F32),
            pltpu.VMEM((2, TQ, TQ), F32),
            pltpu.VMEM((2, TQ, TQ), BF16),
            pltpu.VMEM((2, TQ, TQ), BF16),
            pltpu.VMEM((2, 1, TQ), F32),
            pltpu.VMEM((2, 1, TQ), F32),
            pltpu.VMEM((2, 1, TQ), F32),
            pltpu.VMEM((2, 1, TQ), F32),
        ],
        compiler_params=_params(("parallel", "parallel", "parallel")),
        name="fox_attention",
    )(qt, qa, k, ka, vt, gate)


def _outproj_kernel(ya_ref, yl_ref, ys_ref, x_ref, gate_ref, pg_ref, w_ref, o_ref):
    y = jnp.dot(ya_ref[0], w_ref[0, 0:D_ATT, :], preferred_element_type=F32)
    y = y + jnp.dot(yl_ref[0], w_ref[0, D_ATT:D_ATT + D_LRU, :], preferred_element_type=F32)
    y = y + jnp.dot(ys_ref[0], w_ref[0, D_ATT + D_LRU:, :], preferred_element_type=F32)
    ms = jnp.mean(y * y, axis=-1, keepdims=True)
    yn = (y * lax.rsqrt(ms + EPS)) * pg_ref[0]
    o_ref[0] = x_ref[0] + _mod_row(gate_ref) * yn


def _outproj(layer, ya, yl, ys, x, mod, prep):
    B, S, D = x.shape
    per_layer = lambda a: pl.BlockSpec((1,) + a.shape[1:], lambda b, i: (layer,) + (0,) * (a.ndim - 1))
    row_tile = lambda n: pl.BlockSpec((1, TO, n), lambda b, i: (b, i, 0))
    return pl.pallas_call(
        _outproj_kernel,
        grid=(B, S // TO),
        in_specs=[
            row_tile(D_ATT), row_tile(D_LRU), row_tile(D_SG), row_tile(D),
            pl.BlockSpec((1, 1, D // TN_ADA, 1, TN_ADA), lambda b, i: (layer, b, 2, 0, 0)),
            per_layer(prep["post_g"]),
            per_layer(prep["w_out"]),
        ],
        out_specs=row_tile(D),
        out_shape=jax.ShapeDtypeStruct((B, S, D), F32),
        compiler_params=_params(("parallel", "parallel")),
        name="outproj",
    )(ya, yl, ys, x, mod, prep["post_g"], prep["w_out"])


def _block_diag(w):
    L, G, n, _ = w.shape
    eye = jnp.eye(G, dtype=w.dtype)
    return (w[:, :, :, None, :] * eye[None, :, None, :, None]).reshape(L, G * n, G * n)


def _prepare(pre_g, post_g, w_in, b_f, conv_w, conv_b, lru_wa, lru_ba, lru_wx, lru_bx, lru_lambda,
             sg_ln_g, sg_ln_b, sg_w, sg_b, w_out):
    L, D, _ = w_in.shape
    cuts = [0]
    for n in IN_SIZES:
        cuts.append(cuts[-1] + n)
    wq, wk, wv, wf, wga, wxl, wgl, wsu, wsv, wgs = (w_in[:, :, cuts[j]:cuts[j + 1]] for j in range(10))
    row = lambda a: a[:, None, :]
    wf3 = jnp.repeat(wf, N_PARTS, axis=2)
    pad = jnp.zeros((L, D, T_ROWS - 2 * D_ATT - N_AUG), F32)
    return dict(
        pre_g=row(pre_g), post_g=row(post_g),
        we=jnp.concatenate([wxl, wsv], axis=2).astype(BF16),
        wg=jnp.concatenate([wga, wgl, wsu, wgs], axis=2).astype(BF16),
        wk=wk.astype(BF16),
        wt=jnp.concatenate([wq, wv, wf3, pad], axis=2).transpose(0, 2, 1).astype(BF16),
        bf=jnp.broadcast_to(jnp.repeat(b_f, N_PARTS, axis=1)[:, :, None], (L, N_AUG, LANES)),
        conv_w=conv_w, conv_b=row(conv_b),
        wa=_block_diag(lru_wa).astype(BF16), ba=row(lru_ba),
        wx=_block_diag(lru_wx).astype(BF16), bx=row(lru_bx), lam=row(lru_lambda),
        ln_g=row(sg_ln_g), ln_b=row(sg_ln_b),
        ws=sg_w.reshape(L, N_SG_GROUPS * SG_CHUNK, SG_CHUNK).astype(BF16),
        sg_bias=jnp.repeat(sg_b.transpose(0, 2, 1), HEAD_DIM, axis=2),
        w_out=w_out.astype(BF16),
    )


def kernel(x, c, ada_w, ada_b, pre_g, post_g, w_in, b_f, conv_w, conv_b, lru_wa, lru_ba, lru_wx,
           lru_bx, lru_lambda, sg_ln_g, sg_ln_b, sg_w, sg_b, w_out):
    mod = _ada(c, ada_w, ada_b)
    prep = _prepare(pre_g, post_g, w_in, b_f, conv_w, conv_b, lru_wa, lru_ba, lru_wx, lru_bx,
                    lru_lambda, sg_ln_g, sg_ln_b, sg_w, sg_b, w_out)
    for layer in range(ada_w.shape[0]):
        qt, vt, k, ka, qa, ga, yl, ys = _inproj(layer, x, mod, prep)
        ya = _attention(qt, qa, k, ka, vt, ga)
        x = _outproj(layer, ya, yl, ys, x, mod, prep)
    return x
```

```python
import math

import jax
import jax.numpy as jnp
from jax import lax
from jax.experimental import pallas as pl
from jax.experimental.pallas import tpu as pltpu

D_MODEL = 1024
HEAD_DIM = 64
D_ATT = 512
D_LRU = 256
D_SG = 256
N_HEADS = D_ATT // HEAD_DIM
N_LRU_BLOCKS = D_LRU // HEAD_DIM
N_SG_GROUPS = D_SG // HEAD_DIM
SG_CHUNK = 128
CONV_WIDTH = 4
LRU_C = 8.0
EPS = 1e-6
IN_SIZES = (D_ATT, D_ATT, D_ATT, N_HEADS, D_ATT, D_LRU, D_LRU, D_SG, D_SG, D_SG)

LANES = 128
SUBLANES = 8
BF16_ROWS = 16
VMEM_LIMIT = 56 * 1024 * 1024

TM = 512
TO = 1024
TQ = 512
TN_ADA = 512
NEG = -1e30

N_PARTS = 3
N_AUG = N_HEADS * N_PARTS
T_ROWS = 2 * D_ATT + 2 * BF16_ROWS
LOG2E = 1.4426950408889634
Q_SCALE = LOG2E / math.sqrt(HEAD_DIM)
V_ROWS = HEAD_DIM + BF16_ROWS
GELU_C0 = math.sqrt(2.0 / math.pi)
GELU_C1 = GELU_C0 * 0.044715

F32 = jnp.float32
BF16 = jnp.bfloat16


def _sigmoid(x):
    return 0.5 * jnp.tanh(0.5 * x) + 0.5


def _silu(x):
    hx = 0.5 * x
    return hx + hx * jnp.tanh(hx)


def _gelu_tanh(x):
    hx = 0.5 * x
    return hx + hx * jnp.tanh(x * (GELU_C0 + GELU_C1 * (x * x)))


def _log_sigmoid(x):
    return jnp.minimum(x, 0.0) - jnp.log1p(jnp.exp(-jnp.abs(x)))


def _split3(x, sel):
    hi = x.astype(BF16).astype(F32)
    r1 = x - hi
    mid = r1.astype(BF16).astype(F32)
    lo = r1 - mid
    return jnp.where(sel == 0, hi, jnp.where(sel == 1, mid, lo))


def _mod_row(ref):
    return jnp.concatenate([ref[0, 0, j] for j in range(ref.shape[2])], axis=1)


def _params(sem):
    return pltpu.CompilerParams(dimension_semantics=sem, vmem_limit_bytes=VMEM_LIMIT)


def _ada_kernel(ct_ref, w_ref, b_ref, o_ref):
    ct = ct_ref[...]
    sc = _silu(ct)
    w = w_ref[0]
    bias = b_ref[0]
    for b in range(ct.shape[1]):
        col = sc[:, b:b + 1]
        o_ref[0, b, 0] = jnp.sum(col * w, axis=0, keepdims=True) + bias


def _ada(c, ada_w, ada_b):
    L, D, N = ada_w.shape
    B = c.shape[0]
    return pl.pallas_call(
        _ada_kernel,
        grid=(L, N // TN_ADA),
        in_specs=[
            pl.BlockSpec((D, B), lambda l, n: (0, 0)),
            pl.BlockSpec((1, D, TN_ADA), lambda l, n: (l, 0, n)),
            pl.BlockSpec((1, 1, TN_ADA), lambda l, n: (l, 0, n)),
        ],
        out_specs=pl.BlockSpec((1, B, 1, 1, TN_ADA), lambda l, n: (l, 0, n, 0, 0)),
        out_shape=jax.ShapeDtypeStruct((L, B, N // TN_ADA, 1, TN_ADA), F32),
        compiler_params=_params(("parallel", "parallel")),
        name="ada_mod",
    )(c.T, ada_w, ada_b.reshape(L, 1, N))


def _lru_mixer(g_lru, r_pre, i_pre, lam, h_ref, xc):
    r = _sigmoid(r_pre)
    ig = _sigmoid(i_pre)
    nlam = -lam
    softplus = jnp.maximum(nlam, 0.0) + jnp.log1p(jnp.exp(-jnp.abs(nlam)))
    log_a = (-LRU_C * r) * softplus
    a = jnp.exp(log_a)
    bt = jnp.sqrt(jnp.tanh(-log_a) * (1.0 + a * a)) * (ig * xc)

    row8 = lax.broadcasted_iota(jnp.int32, (TM, D_LRU), 0) & (SUBLANES - 1)
    d = 1
    while d < SUBLANES:
        valid = row8 >= d
        a_s = jnp.where(valid, pltpu.roll(a, d, axis=0), 1.0)
        b_s = jnp.where(valid, pltpu.roll(bt, d, axis=0), 0.0)
        bt = bt + a * b_s
        a = a * a_s
        d *= 2

    h_prev = h_ref[...]
    sg = _silu(g_lru)
    outs = []
    for g in range(TM // SUBLANES):
        lo = g * SUBLANES
        hg = bt[lo:lo + SUBLANES] + a[lo:lo + SUBLANES] * h_prev
        h_prev = hg[SUBLANES - 1:SUBLANES, :]
        outs.append(hg * sg[lo:lo + SUBLANES])
    h_ref[...] = h_prev
    return jnp.concatenate(outs, axis=0)


def _inproj_kernel(x_ref, shift_ref, scale_ref, g_ref, we_ref, wg_ref, wk_ref, wt_ref, bf_ref,
                   cw_ref, cb_ref, wa_ref, ba_ref, wx_ref, bx_ref, lam_ref,
                   lng_ref, lnb_ref, ws_ref, sb_ref,
                   qt_ref, vt_ref, k_ref, ka_ref, qa_ref, ga_ref, yl_ref, ys_ref,
                   fc_ref, xbuf_ref, h_ref):
    i = pl.program_id(1)

    @pl.when(i == 0)
    def _():
        fc_ref[...] = jnp.zeros(fc_ref.shape, F32)
        xbuf_ref[0:SUBLANES, :] = jnp.zeros((SUBLANES, D_LRU), F32)
        h_ref[...] = jnp.zeros(h_ref.shape, F32)

    x = x_ref[0]
    ms = jnp.mean(x * x, axis=-1, keepdims=True)
    gs = g_ref[0] * (1.0 + _mod_row(scale_ref))
    h = (x * lax.rsqrt(ms + EPS)) * gs + _mod_row(shift_ref)
    hb = h.astype(BF16)

    early = jnp.dot(hb, we_ref[0], preferred_element_type=F32)
    x_lru = early[:, :D_LRU]
    sg_v = early[:, D_LRU:]
    tr = lax.dot_general(wt_ref[0], hb, (((1,), (1,)), ((), ())), preferred_element_type=F32)
    qt_ref[0] = (tr[:D_ATT] * Q_SCALE).astype(BF16)
    vt_ref[0] = tr[D_ATT:2 * D_ATT].astype(BF16)
    fl = tr[2 * D_ATT:2 * D_ATT + N_AUG]
    k_ref[0] = jnp.dot(hb, wk_ref[0], preferred_element_type=F32).astype(BF16)

    v = _gelu_tanh(sg_v)
    mu = jnp.mean(v, axis=-1, keepdims=True)
    var = jnp.mean(jnp.square(v - mu), axis=-1, keepdims=True)
    vnb = (((v - mu) * lax.rsqrt(var + EPS)) * lng_ref[0] + lnb_ref[0]).astype(BF16)
    w_shape = ws_ref.shape[1:]
    t_idx = lax.broadcasted_iota(jnp.int32, w_shape, 0) & (SG_CHUNK - 1)
    s_idx = lax.broadcasted_iota(jnp.int32, w_shape, 1)
    ws = jnp.where(s_idx <= t_idx, ws_ref[0], jnp.zeros(w_shape, BF16))
    grp = lax.broadcasted_iota(jnp.int32, (SG_CHUNK, D_SG), 1) // HEAD_DIM
    zs = []
    for c in range(TM // SG_CHUNK):
        zz = jnp.dot(ws, vnb[c * SG_CHUNK:(c + 1) * SG_CHUNK], preferred_element_type=F32)
        z = zz[0:SG_CHUNK]
        for gi in range(1, N_SG_GROUPS):
            z = jnp.where(grp == gi, zz[gi * SG_CHUNK:(gi + 1) * SG_CHUNK], z)
        zs.append(z + sb_ref[0])
    z_all = jnp.concatenate(zs, axis=0)

    xbuf_ref[SUBLANES:SUBLANES + TM, :] = x_lru
    xc = cb_ref[0]
    for kk in range(CONV_WIDTH):
        off = SUBLANES - (CONV_WIDTH - 1) + kk
        xc = xc + xbuf_ref[off:off + TM, :] * cw_ref[0, kk:kk + 1, :]
    xbuf_ref[0:SUBLANES, :] = x_lru[TM - SUBLANES:, :]
    xcb = xc.astype(BF16)
    r_pre = jnp.dot(xcb, wa_ref[0], preferred_element_type=F32) + ba_ref[0]
    i_pre = jnp.dot(xcb, wx_ref[0], preferred_element_type=F32) + bx_ref[0]

    gates = jnp.dot(hb, wg_ref[0], preferred_element_type=F32)

    ga_ref[0] = gates[:, :D_ATT].astype(BF16)
    g_lru = gates[:, D_ATT:D_ATT + D_LRU]
    sg_u = gates[:, D_ATT + D_LRU:D_ATT + D_LRU + D_SG]
    g_sg = gates[:, D_ATT + D_LRU + D_SG:]
    ys_ref[0] = ((_gelu_tanh(sg_u) * z_all) * _silu(g_sg)).astype(BF16)
    yl_ref[0] = _lru_mixer(g_lru, r_pre, i_pre, lam_ref[0], h_ref, xc).astype(BF16)

    reps = TM // LANES
    ls = _log_sigmoid(fl + jnp.concatenate([bf_ref[0]] * reps, axis=1))
    lane = lax.broadcasted_iota(jnp.int32, (N_AUG, TM), 1)
    d = 1
    while d < TM:
        ls = ls + jnp.where(lane >= d, pltpu.roll(ls, d, axis=1), 0.0)
        d *= 2
    f = ls + jnp.concatenate([fc_ref[...]] * reps, axis=1)
    fc_ref[...] = jnp.broadcast_to(f[:, TM - 1:TM], (N_AUG, LANES))
    row = lax.broadcasted_iota(jnp.int32, (N_AUG, TM), 0)
    parts = _split3(f * LOG2E, row % N_PARTS)
    ones = jnp.ones((N_AUG, TM), F32)
    zeros = jnp.zeros((LANES - 2 * N_AUG, TM), F32)
    qa_ref[0] = jnp.concatenate([ones, parts, zeros], axis=0).astype(BF16)
    ka_ref[0] = jnp.concatenate([-parts, ones, zeros], axis=0).T.astype(BF16)


def _inproj(layer, x, mod, prep):
    B, S, D = x.shape
    per_layer = lambda a: pl.BlockSpec((1,) + a.shape[1:], lambda b, i: (layer,) + (0,) * (a.ndim - 1))
    mod_chunk = lambda j: pl.BlockSpec((1, 1, D // TN_ADA, 1, TN_ADA), lambda b, i: (layer, b, j, 0, 0))
    row_tile = lambda n: pl.BlockSpec((1, TM, n), lambda b, i: (b, i, 0))
    col_tile = lambda n: pl.BlockSpec((1, n, TM), lambda b, i: (b, 0, i))
    names = ("pre_g", "we", "wg", "wk", "wt", "bf", "conv_w", "conv_b", "wa", "ba", "wx", "bx", "lam",
             "ln_g", "ln_b", "ws", "sg_bias")
    operands = [prep[n] for n in names]
    return pl.pallas_call(
        _inproj_kernel,
        grid=(B, S // TM),
        in_specs=[row_tile(D), mod_chunk(0), mod_chunk(1)] + [per_layer(a) for a in operands],
        out_specs=[
            col_tile(D_ATT), col_tile(D_ATT), row_tile(D_ATT), row_tile(LANES), col_tile(LANES),
            row_tile(D_ATT), row_tile(D_LRU), row_tile(D_SG),
        ],
        out_shape=[
            jax.ShapeDtypeStruct((B, D_ATT, S), BF16),
            jax.ShapeDtypeStruct((B, D_ATT, S), BF16),
            jax.ShapeDtypeStruct((B, S, D_ATT), BF16),
            jax.ShapeDtypeStruct((B, S, LANES), BF16),
            jax.ShapeDtypeStruct((B, LANES, S), BF16),
            jax.ShapeDtypeStruct((B, S, D_ATT), BF16),
            jax.ShapeDtypeStruct((B, S, D_LRU), BF16),
            jax.ShapeDtypeStruct((B, S, D_SG), BF16),
        ],
        scratch_shapes=[
            pltpu.VMEM((N_AUG, LANES), F32),
            pltpu.VMEM((TM + SUBLANES, D_LRU), F32),
            pltpu.VMEM((1, D_LRU), F32),
        ],
        compiler_params=_params(("parallel", "arbitrary")),
        name="inproj",
    )(x, mod, mod, *operands)


def _attn_kernel(*refs):
    def body(p, carry):
        _attn_pair(p, *refs)
        return carry

    lax.fori_loop(0, N_HEADS // 2, body, 0)


def _attn_pair(p, qt_ref, qa_ref, k_ref, ka_ref, vt_ref, g_ref, o_ref,
               m_ref, acc_ref, rhs_ref, s0_ref, s1_ref, p0_ref, p1_ref, mx0_ref, mx1_ref, al0_ref, al1_ref):
    i = pl.program_id(1)
    pair_rows = pl.ds(pl.multiple_of(p * LANES, LANES), LANES)
    qt = qt_ref[0, pair_rows, :]
    qa = qa_ref[0]
    row = lax.broadcasted_iota(jnp.int32, (LANES, TQ), 0)
    zero = jnp.zeros_like(qt)
    for hh in range(2):
        head = 2 * p + hh
        own = (row >= hh * HEAD_DIM) & (row < (hh + 1) * HEAD_DIM)
        lo = head * N_PARTS
        aug = ((row >= lo) & (row < lo + N_PARTS)) | ((row >= N_AUG + lo) & (row < N_AUG + lo + N_PARTS))
        rhs_ref[hh] = jnp.concatenate([jnp.where(own, qt, zero), jnp.where(aug, qa, zero)], axis=0)
    ones_rows = jnp.where(lax.broadcasted_iota(jnp.int32, (BF16_ROWS, TQ), 0) == 0, 1.0, 0.0).astype(BF16)
    m_ref[...] = jnp.full(m_ref.shape, NEG, F32)
    acc_ref[...] = jnp.zeros(acc_ref.shape, F32)
    s_bufs = (s0_ref, s1_ref)
    p_bufs = (p0_ref, p1_ref)
    mx_bufs = (mx0_ref, mx1_ref)
    al_bufs = (al0_ref, al1_ref)

    def scores(t, slot, masked):
        rows = pl.ds(pl.multiple_of(t * TQ, TQ), TQ)
        lhs = jnp.concatenate([k_ref[0, rows, pair_rows], ka_ref[0, rows, :]], axis=1)
        for hh in range(2):
            st = jnp.dot(lhs, rhs_ref[hh], preferred_element_type=F32)
            if masked:
                kk = lax.broadcasted_iota(jnp.int32, (TQ, TQ), 0)
                qq = lax.broadcasted_iota(jnp.int32, (TQ, TQ), 1)
                st = jnp.where(kk <= qq, st, NEG)
            s_bufs[slot][hh] = st
            mx_bufs[slot][hh] = jnp.max(st, axis=0, keepdims=True)

    def softmax(slot):
        for hh in range(2):
            m_old = m_ref[hh]
            m_new = jnp.maximum(m_old, mx_bufs[slot][hh])
            p_bufs[slot][hh] = jnp.exp2(s_bufs[slot][hh] - m_new).astype(BF16)
            al_bufs[slot][hh] = jnp.exp2(m_old - m_new)
            m_ref[hh] = m_new

    def values(t, slot):
        cols = pl.ds(pl.multiple_of(t * TQ, TQ), TQ)
        for hh in range(2):
            head_rows = pl.ds(pl.multiple_of(p * LANES + hh * HEAD_DIM, HEAD_DIM), HEAD_DIM)
            vaug = jnp.concatenate([vt_ref[0, head_rows, cols], ones_rows], axis=0)
            acc_ref[hh] = acc_ref[hh] * al_bufs[slot][hh] + jnp.dot(vaug, p_bufs[slot][hh],
                                                                     preferred_element_type=F32)

    def step(tau, parity, do_scores=False, masked=False, do_softmax=False, do_values=False):
        if do_softmax:
            softmax(1 - parity)
        if do_scores:
            scores(tau, parity, masked)
        if do_values:
            values(tau - 2, parity)

    full = dict(do_scores=True, do_softmax=True, do_values=True)

    @pl.when(i == 0)
    def _():
        step(0, 0, do_scores=True, masked=True)
        step(1, 1, do_softmax=True)
        step(2, 0, do_values=True)

    @pl.when(i == 1)
    def _():
        step(0, 0, do_scores=True)
        step(1, 1, do_scores=True, masked=True, do_softmax=True)
        step(2, 0, do_softmax=True, do_values=True)
        step(3, 1, do_values=True)

    @pl.when(i >= 2)
    def _():
        step(0, 0, do_scores=True)
        step(1, 1, do_scores=True, do_softmax=True)

        def body(u, carry):
            step(2 + 2 * u, 0, **full)
            step(3 + 2 * u, 1, **full)
            return carry

        lax.fori_loop(0, lax.shift_right_logical(i - 2, 1), body, 0)

        @pl.when((i & 1) == 0)
        def _():
            step(i, 0, masked=True, **full)
            step(i + 1, 1, do_softmax=True, do_values=True)
            step(i + 2, 0, do_values=True)

        @pl.when((i & 1) == 1)
        def _():
            step(i - 1, 0, **full)
            step(i, 1, masked=True, **full)
            step(i + 1, 0, do_softmax=True, do_values=True)
            step(i + 2, 1, do_values=True)

    outs = []
    for hh in range(2):
        a = acc_ref[hh]
        outs.append(a[:HEAD_DIM] / a[HEAD_DIM:HEAD_DIM + 1])
    o = jnp.concatenate(outs, axis=0).T
    o_ref[0, :, pair_rows] = (o * _silu(g_ref[0, :, pair_rows].astype(F32))).astype(BF16)


def _attention(qt, qa, k, ka, vt, gate):
    B, S, _ = k.shape
    return pl.pallas_call(
        _attn_kernel,
        grid=(B, S // TQ),
        in_specs=[
            pl.BlockSpec((1, D_ATT, TQ), lambda b, i: (b, 0, i)),
            pl.BlockSpec((1, LANES, TQ), lambda b, i: (b, 0, i)),
            pl.BlockSpec((1, S, D_ATT), lambda b, i: (b, 0, 0)),
            pl.BlockSpec((1, S, LANES), lambda b, i: (b, 0, 0)),
            pl.BlockSpec((1, D_ATT, S), lambda b, i: (b, 0, 0)),
            pl.BlockSpec((1, TQ, D_ATT), lambda b, i: (b, i, 0)),
        ],
        out_specs=pl.BlockSpec((1, TQ, D_ATT), lambda b, i: (b, i, 0)),
        out_shape=jax.ShapeDtypeStruct((B, S, D_ATT), BF16),
        scratch_shapes=[
            pltpu.VMEM((2, 1, TQ), F32),
            pltpu.VMEM((2, V_ROWS, TQ), F32),
            pltpu.VMEM((2, 2 * LANES, TQ), BF16),
            pltpu.VMEM((2, TQ, TQ), F32),
            pltpu.VMEM((2, TQ, TQ), F32),
            pltpu.VMEM((2, TQ, TQ), BF16),
            pltpu.VMEM((2, TQ, TQ), BF16),
            pltpu.VMEM((2, 1, TQ), F32),
            pltpu.VMEM((2, 1, TQ), F32),
            pltpu.VMEM((2, 1, TQ), F32),
            pltpu.VMEM((2, 1, TQ), F32),
        ],
        compiler_params=_params(("parallel", "parallel")),
        name="fox_attention",
    )(qt, qa, k, ka, vt, gate)


def _outproj_kernel(ya_ref, yl_ref, ys_ref, x_ref, gate_ref, pg_ref, w_ref, o_ref):
    y = jnp.dot(ya_ref[0], w_ref[0, 0:D_ATT, :], preferred_element_type=F32)
    y = y + jnp.dot(yl_ref[0], w_ref[0, D_ATT:D_ATT + D_LRU, :], preferred_element_type=F32)
    y = y + jnp.dot(ys_ref[0], w_ref[0, D_ATT + D_LRU:, :], preferred_element_type=F32)
    ms = jnp.mean(y * y, axis=-1, keepdims=True)
    yn = (y * lax.rsqrt(ms + EPS)) * pg_ref[0]
    o_ref[0] = x_ref[0] + _mod_row(gate_ref) * yn


def _outproj(layer, ya, yl, ys, x, mod, prep):
    B, S, D = x.shape
    per_layer = lambda a: pl.BlockSpec((1,) + a.shape[1:], lambda b, i: (layer,) + (0,) * (a.ndim - 1))
    row_tile = lambda n: pl.BlockSpec((1, TO, n), lambda b, i: (b, i, 0))
    return pl.pallas_call(
        _outproj_kernel,
        grid=(B, S // TO),
        in_specs=[
            row_tile(D_ATT), row_tile(D_LRU), row_tile(D_SG), row_tile(D),
            pl.BlockSpec((1, 1, D // TN_ADA, 1, TN_ADA), lambda b, i: (layer, b, 2, 0, 0)),
            per_layer(prep["post_g"]),
            per_layer(prep["w_out"]),
        ],
        out_specs=row_tile(D),
        out_shape=jax.ShapeDtypeStruct((B, S, D), F32),
        compiler_params=_params(("parallel", "parallel")),
        name="outproj",
    )(ya, yl, ys, x, mod, prep["post_g"], prep["w_out"])


def _block_diag(w):
    L, G, n, _ = w.shape
    eye = jnp.eye(G, dtype=w.dtype)
    return (w[:, :, :, None, :] * eye[None, :, None, :, None]).reshape(L, G * n, G * n)


def _prepare(pre_g, post_g, w_in, b_f, conv_w, conv_b, lru_wa, lru_ba, lru_wx, lru_bx, lru_lambda,
             sg_ln_g, sg_ln_b, sg_w, sg_b, w_out):
    L, D, _ = w_in.shape
    cuts = [0]
    for n in IN_SIZES:
        cuts.append(cuts[-1] + n)
    wq, wk, wv, wf, wga, wxl, wgl, wsu, wsv, wgs = (w_in[:, :, cuts[j]:cuts[j + 1]] for j in range(10))
    row = lambda a: a[:, None, :]
    wf3 = jnp.repeat(wf, N_PARTS, axis=2)
    pad = jnp.zeros((L, D, T_ROWS - 2 * D_ATT - N_AUG), F32)
    return dict(
        pre_g=row(pre_g), post_g=row(post_g),
        we=jnp.concatenate([wxl, wsv], axis=2).astype(BF16),
        wg=jnp.concatenate([wga, wgl, wsu, wgs], axis=2).astype(BF16),
        wk=wk.astype(BF16),
        wt=jnp.concatenate([wq, wv, wf3, pad], axis=2).transpose(0, 2, 1).astype(BF16),
        bf=jnp.broadcast_to(jnp.repeat(b_f, N_PARTS, axis=1)[:, :, None], (L, N_AUG, LANES)),
        conv_w=conv_w, conv_b=row(conv_b),
        wa=_block_diag(lru_wa).astype(BF16), ba=row(lru_ba),
        wx=_block_diag(lru_wx).astype(BF16), bx=row(lru_bx), lam=row(lru_lambda),
        ln_g=row(sg_ln_g), ln_b=row(sg_ln_b),
        ws=sg_w.reshape(L, N_SG_GROUPS * SG_CHUNK, SG_CHUNK).astype(BF16),
        sg_bias=jnp.repeat(sg_b.transpose(0, 2, 1), HEAD_DIM, axis=2),
        w_out=w_out.astype(BF16),
    )


def kernel(x, c, ada_w, ada_b, pre_g, post_g, w_in, b_f, conv_w, conv_b, lru_wa, lru_ba, lru_wx,
           lru_bx, lru_lambda, sg_ln_g, sg_ln_b, sg_w, sg_b, w_out):
    mod = _ada(c, ada_w, ada_b)
    prep = _prepare(pre_g, post_g, w_in, b_f, conv_w, conv_b, lru_wa, lru_ba, lru_wx, lru_bx,
                    lru_lambda, sg_ln_g, sg_ln_b, sg_w, sg_b, w_out)
    for layer in range(ada_w.shape[0]):
        qt, vt, k, ka, qa, ga, yl, ys = _inproj(layer, x, mod, prep)
        ya = _attention(qt, qa, k, ka, vt, ga)
        x = _outproj(layer, ya, yl, ys, x, mod, prep)
    return x
```

```python
import math

import jax
import jax.numpy as jnp
from jax import lax
from jax.experimental import pallas as pl
from jax.experimental.pallas import tpu as pltpu

D_MODEL = 1024
HEAD_DIM = 64
D_ATT = 512
D_LRU = 256
D_SG = 256
N_HEADS = D_ATT // HEAD_DIM
N_LRU_BLOCKS = D_LRU // HEAD_DIM
N_SG_GROUPS = D_SG // HEAD_DIM
SG_CHUNK = 128
CONV_WIDTH = 4
LRU_C = 8.0
EPS = 1e-6
IN_SIZES = (D_ATT, D_ATT, D_ATT, N_HEADS, D_ATT, D_LRU, D_LRU, D_SG, D_SG, D_SG)

LANES = 128
SUBLANES = 8
BF16_ROWS = 16
VMEM_LIMIT = 56 * 1024 * 1024

TM = 512
TO = 1024
TQ = 512
TN_ADA = 512
NEG = -1e30

N_PARTS = 3
N_AUG = N_HEADS * N_PARTS
T_ROWS = 2 * D_ATT + 2 * BF16_ROWS
LOG2E = 1.4426950408889634
Q_SCALE = LOG2E / math.sqrt(HEAD_DIM)
V_ROWS = HEAD_DIM + BF16_ROWS
SKIP_LOG2 = 160.0
NORM_SLACK = 1.02
GELU_C0 = math.sqrt(2.0 / math.pi)
GELU_C1 = GELU_C0 * 0.044715

F32 = jnp.float32
BF16 = jnp.bfloat16


def _sigmoid(x):
    return 0.5 * jnp.tanh(0.5 * x) + 0.5


def _silu(x):
    hx = 0.5 * x
    return hx + hx * jnp.tanh(hx)


def _gelu_tanh(x):
    hx = 0.5 * x
    return hx + hx * jnp.tanh(x * (GELU_C0 + GELU_C1 * (x * x)))


def _log_sigmoid(x):
    return jnp.minimum(x, 0.0) - jnp.log1p(jnp.exp(-jnp.abs(x)))


def _split3(x, sel):
    hi = x.astype(BF16).astype(F32)
    r1 = x - hi
    mid = r1.astype(BF16).astype(F32)
    lo = r1 - mid
    return jnp.where(sel == 0, hi, jnp.where(sel == 1, mid, lo))


def _mod_row(ref):
    return jnp.concatenate([ref[0, 0, j] for j in range(ref.shape[2])], axis=1)


def _params(sem):
    return pltpu.CompilerParams(dimension_semantics=sem, vmem_limit_bytes=VMEM_LIMIT)


def _ada_kernel(ct_ref, w_ref, b_ref, o_ref):
    ct = ct_ref[...]
    sc = _silu(ct)
    w = w_ref[0]
    bias = b_ref[0]
    for b in range(ct.shape[1]):
        col = sc[:, b:b + 1]
        o_ref[0, b, 0] = jnp.sum(col * w, axis=0, keepdims=True) + bias


def _ada(c, ada_w, ada_b):
    L, D, N = ada_w.shape
    B = c.shape[0]
    return pl.pallas_call(
        _ada_kernel,
        grid=(L, N // TN_ADA),
        in_specs=[
            pl.BlockSpec((D, B), lambda l, n: (0, 0)),
            pl.BlockSpec((1, D, TN_ADA), lambda l, n: (l, 0, n)),
            pl.BlockSpec((1, 1, TN_ADA), lambda l, n: (l, 0, n)),
        ],
        out_specs=pl.BlockSpec((1, B, 1, 1, TN_ADA), lambda l, n: (l, 0, n, 0, 0)),
        out_shape=jax.ShapeDtypeStruct((L, B, N // TN_ADA, 1, TN_ADA), F32),
        compiler_params=_params(("parallel", "parallel")),
        name="ada_mod",
    )(c.T, ada_w, ada_b.reshape(L, 1, N))


def _lru_mixer(g_lru, r_pre, i_pre, lam, h_ref, xc):
    r = _sigmoid(r_pre)
    ig = _sigmoid(i_pre)
    nlam = -lam
    softplus = jnp.maximum(nlam, 0.0) + jnp.log1p(jnp.exp(-jnp.abs(nlam)))
    log_a = (-LRU_C * r) * softplus
    a = jnp.exp(log_a)
    bt = jnp.sqrt(jnp.tanh(-log_a) * (1.0 + a * a)) * (ig * xc)

    row8 = lax.broadcasted_iota(jnp.int32, (TM, D_LRU), 0) & (SUBLANES - 1)
    d = 1
    while d < SUBLANES:
        valid = row8 >= d
        a_s = jnp.where(valid, pltpu.roll(a, d, axis=0), 1.0)
        b_s = jnp.where(valid, pltpu.roll(bt, d, axis=0), 0.0)
        bt = bt + a * b_s
        a = a * a_s
        d *= 2

    h_prev = h_ref[...]
    sg = _silu(g_lru)
    outs = []
    for g in range(TM // SUBLANES):
        lo = g * SUBLANES
        hg = bt[lo:lo + SUBLANES] + a[lo:lo + SUBLANES] * h_prev
        h_prev = hg[SUBLANES - 1:SUBLANES, :]
        outs.append(hg * sg[lo:lo + SUBLANES])
    h_ref[...] = h_prev
    return jnp.concatenate(outs, axis=0)


def _inproj_kernel(x_ref, shift_ref, scale_ref, g_ref, we_ref, wg_ref, wk_ref, wt_ref, bf_ref,
                   cw_ref, cb_ref, wa_ref, ba_ref, wx_ref, bx_ref, lam_ref,
                   lng_ref, lnb_ref, ws_ref, sb_ref, he_ref,
                   qt_ref, vt_ref, k_ref, ka_ref, qa_ref, ga_ref, yl_ref, ys_ref,
                   qs_ref, ks_ref, fs_ref, fe_ref,
                   fc_ref, xbuf_ref, h_ref):
    i = pl.program_id(1)

    @pl.when(i == 0)
    def _():
        fc_ref[...] = jnp.zeros(fc_ref.shape, F32)
        xbuf_ref[0:SUBLANES, :] = jnp.zeros((SUBLANES, D_LRU), F32)
        h_ref[...] = jnp.zeros(h_ref.shape, F32)

    x = x_ref[0]
    ms = jnp.mean(x * x, axis=-1, keepdims=True)
    gs = g_ref[0] * (1.0 + _mod_row(scale_ref))
    h = (x * lax.rsqrt(ms + EPS)) * gs + _mod_row(shift_ref)
    hb = h.astype(BF16)

    early = jnp.dot(hb, we_ref[0], preferred_element_type=F32)
    x_lru = early[:, :D_LRU]
    sg_v = early[:, D_LRU:]
    tr = lax.dot_general(wt_ref[0], hb, (((1,), (1,)), ((), ())), preferred_element_type=F32)
    tq = tr[:D_ATT] * Q_SCALE
    qt_ref[0] = tq.astype(BF16)
    vt_ref[0] = tr[D_ATT:2 * D_ATT].astype(BF16)
    fl = tr[2 * D_ATT:2 * D_ATT + N_AUG]
    kf = jnp.dot(hb, wk_ref[0], preferred_element_type=F32)
    k_ref[0] = kf.astype(BF16)
    qn2 = jnp.sum((tq * tq).reshape(N_HEADS, HEAD_DIM, TM), axis=1)
    qs_ref[0, 0] = jnp.broadcast_to(jnp.sqrt(jnp.max(qn2, axis=1, keepdims=True)), (N_HEADS, LANES))
    kn2 = jnp.dot((kf * kf).astype(BF16), he_ref[...], preferred_element_type=F32)
    ks_ref[0, 0] = jnp.sqrt(jnp.max(kn2, axis=0, keepdims=True))

    v = _gelu_tanh(sg_v)
    mu = jnp.mean(v, axis=-1, keepdims=True)
    var = jnp.mean(jnp.square(v - mu), axis=-1, keepdims=True)
    vnb = (((v - mu) * lax.rsqrt(var + EPS)) * lng_ref[0] + lnb_ref[0]).astype(BF16)
    w_shape = ws_ref.shape[1:]
    t_idx = lax.broadcasted_iota(jnp.int32, w_shape, 0) & (SG_CHUNK - 1)
    s_idx = lax.broadcasted_iota(jnp.int32, w_shape, 1)
    ws = jnp.where(s_idx <= t_idx, ws_ref[0], jnp.zeros(w_shape, BF16))
    grp = lax.broadcasted_iota(jnp.int32, (SG_CHUNK, D_SG), 1) // HEAD_DIM
    zs = []
    for c in range(TM // SG_CHUNK):
        zz = jnp.dot(ws, vnb[c * SG_CHUNK:(c + 1) * SG_CHUNK], preferred_element_type=F32)
        z = zz[0:SG_CHUNK]
        for gi in range(1, N_SG_GROUPS):
            z = jnp.where(grp == gi, zz[gi * SG_CHUNK:(gi + 1) * SG_CHUNK], z)
        zs.append(z + sb_ref[0])
    z_all = jnp.concatenate(zs, axis=0)

    xbuf_ref[SUBLANES:SUBLANES + TM, :] = x_lru
    xc = cb_ref[0]
    for kk in range(CONV_WIDTH):
        off = SUBLANES - (CONV_WIDTH - 1) + kk
        xc = xc + xbuf_ref[off:off + TM, :] * cw_ref[0, kk:kk + 1, :]
    xbuf_ref[0:SUBLANES, :] = x_lru[TM - SUBLANES:, :]
    xcb = xc.astype(BF16)
    r_pre = jnp.dot(xcb, wa_ref[0], preferred_element_type=F32) + ba_ref[0]
    i_pre = jnp.dot(xcb, wx_ref[0], preferred_element_type=F32) + bx_ref[0]

    gates = jnp.dot(hb, wg_ref[0], preferred_element_type=F32)

    ga_ref[0] = gates[:, :D_ATT].astype(BF16)
    g_lru = gates[:, D_ATT:D_ATT + D_LRU]
    sg_u = gates[:, D_ATT + D_LRU:D_ATT + D_LRU + D_SG]
    g_sg = gates[:, D_ATT + D_LRU + D_SG:]
    ys_ref[0] = ((_gelu_tanh(sg_u) * z_all) * _silu(g_sg)).astype(BF16)
    yl_ref[0] = _lru_mixer(g_lru, r_pre, i_pre, lam_ref[0], h_ref, xc).astype(BF16)

    reps = TM // LANES
    ls = _log_sigmoid(fl + jnp.concatenate([bf_ref[0]] * reps, axis=1))
    lane = lax.broadcasted_iota(jnp.int32, (N_AUG, TM), 1)
    d = 1
    while d < TM:
        ls = ls + jnp.where(lane >= d, pltpu.roll(ls, d, axis=1), 0.0)
        d *= 2
    f = ls + jnp.concatenate([fc_ref[...]] * reps, axis=1)
    fc_ref[...] = jnp.broadcast_to(f[:, TM - 1:TM], (N_AUG, LANES))
    row = lax.broadcasted_iota(jnp.int32, (N_AUG, TM), 0)
    f2 = f * LOG2E
    fs_ref[0, 0] = jnp.broadcast_to(f2[:, 0:1], (N_AUG, LANES))
    fe_ref[0, 0] = jnp.broadcast_to(f2[:, TM - 1:TM], (N_AUG, LANES))
    parts = _split3(f2, row % N_PARTS)
    ones = jnp.ones((N_AUG, TM), F32)
    zeros = jnp.zeros((LANES - 2 * N_AUG, TM), F32)
    qa_ref[0] = jnp.concatenate([ones, parts, zeros], axis=0).astype(BF16)
    ka_ref[0] = jnp.concatenate([-parts, ones, zeros], axis=0).T.astype(BF16)


def _inproj(layer, x, mod, prep):
    B, S, D = x.shape
    per_layer = lambda a: pl.BlockSpec((1,) + a.shape[1:], lambda b, i: (layer,) + (0,) * (a.ndim - 1))
    mod_chunk = lambda j: pl.BlockSpec((1, 1, D // TN_ADA, 1, TN_ADA), lambda b, i: (layer, b, j, 0, 0))
    row_tile = lambda n: pl.BlockSpec((1, TM, n), lambda b, i: (b, i, 0))
    col_tile = lambda n: pl.BlockSpec((1, n, TM), lambda b, i: (b, 0, i))
    names = ("pre_g", "we", "wg", "wk", "wt", "bf", "conv_w", "conv_b", "wa", "ba", "wx", "bx", "lam",
             "ln_g", "ln_b", "ws", "sg_bias")
    operands = [prep[n] for n in names]
    head_sum = prep["head_sum"]
    stat = lambda rows: pl.BlockSpec((1, 1, rows, LANES), lambda b, i: (b, i, 0, 0))
    stat_shape = lambda rows: jax.ShapeDtypeStruct((B, S // TM, rows, LANES), F32)
    return pl.pallas_call(
        _inproj_kernel,
        grid=(B, S // TM),
        in_specs=[row_tile(D), mod_chunk(0), mod_chunk(1)] + [per_layer(a) for a in operands]
        + [pl.BlockSpec(head_sum.shape, lambda b, i: (0, 0))],
        out_specs=[
            col_tile(D_ATT), col_tile(D_ATT), row_tile(D_ATT), row_tile(LANES), col_tile(LANES),
            row_tile(D_ATT), row_tile(D_LRU), row_tile(D_SG),
            stat(N_HEADS), stat(1), stat(N_AUG), stat(N_AUG),
        ],
        out_shape=[
            jax.ShapeDtypeStruct((B, D_ATT, S), BF16),
            jax.ShapeDtypeStruct((B, D_ATT, S), BF16),
            jax.ShapeDtypeStruct((B, S, D_ATT), BF16),
            jax.ShapeDtypeStruct((B, S, LANES), BF16),
            jax.ShapeDtypeStruct((B, LANES, S), BF16),
            jax.ShapeDtypeStruct((B, S, D_ATT), BF16),
            jax.ShapeDtypeStruct((B, S, D_LRU), BF16),
            jax.ShapeDtypeStruct((B, S, D_SG), BF16),
            stat_shape(N_HEADS),
            stat_shape(1),
            stat_shape(N_AUG),
            stat_shape(N_AUG),
        ],
        scratch_shapes=[
            pltpu.VMEM((N_AUG, LANES), F32),
            pltpu.VMEM((TM + SUBLANES, D_LRU), F32),
            pltpu.VMEM((1, D_LRU), F32),
        ],
        compiler_params=_params(("parallel", "arbitrary")),
        name="inproj",
    )(x, mod, mod, *operands, head_sum)


def _attn_kernel(*refs):
    def body(p, carry):
        _attn_pair(p, *refs)
        return carry

    lax.fori_loop(0, N_HEADS // 2, body, 0)


def _attn_pair(p, first_ref, qt_ref, qa_ref, k_ref, ka_ref, vt_ref, g_ref, o_ref,
               m_ref, acc_ref, rhs_ref, s0_ref, s1_ref, p0_ref, p1_ref, mx0_ref, mx1_ref, al0_ref, al1_ref):
    first = first_ref[(pl.program_id(0) * (N_HEADS // 2) + p) * pl.num_programs(1) + pl.program_id(1)]
    i = pl.program_id(1) - first
    pair_rows = pl.ds(pl.multiple_of(p * LANES, LANES), LANES)
    qt = qt_ref[0, pair_rows, :]
    qa = qa_ref[0]
    row = lax.broadcasted_iota(jnp.int32, (LANES, TQ), 0)
    zero = jnp.zeros_like(qt)
    for hh in range(2):
        head = 2 * p + hh
        own = (row >= hh * HEAD_DIM) & (row < (hh + 1) * HEAD_DIM)
        lo = head * N_PARTS
        aug = ((row >= lo) & (row < lo + N_PARTS)) | ((row >= N_AUG + lo) & (row < N_AUG + lo + N_PARTS))
        rhs_ref[hh] = jnp.concatenate([jnp.where(own, qt, zero), jnp.where(aug, qa, zero)], axis=0)
    ones_rows = jnp.where(lax.broadcasted_iota(jnp.int32, (BF16_ROWS, TQ), 0) == 0, 1.0, 0.0).astype(BF16)
    m_ref[...] = jnp.full(m_ref.shape, NEG, F32)
    acc_ref[...] = jnp.zeros(acc_ref.shape, F32)
    s_bufs = (s0_ref, s1_ref)
    p_bufs = (p0_ref, p1_ref)
    mx_bufs = (mx0_ref, mx1_ref)
    al_bufs = (al0_ref, al1_ref)

    def scores(t, slot, masked):
        rows = pl.ds(pl.multiple_of((first + t) * TQ, TQ), TQ)
        lhs = jnp.concatenate([k_ref[0, rows, pair_rows], ka_ref[0, rows, :]], axis=1)
        for hh in range(2):
            st = jnp.dot(lhs, rhs_ref[hh], preferred_element_type=F32)
            if masked:
                kk = lax.broadcasted_iota(jnp.int32, (TQ, TQ), 0)
                qq = lax.broadcasted_iota(jnp.int32, (TQ, TQ), 1)
                st = jnp.where(kk <= qq, st, NEG)
            s_bufs[slot][hh] = st
            mx_bufs[slot][hh] = jnp.max(st, axis=0, keepdims=True)

    def softmax(slot):
        for hh in range(2):
            m_old = m_ref[hh]
            m_new = jnp.maximum(m_old, mx_bufs[slot][hh])
            p_bufs[slot][hh] = jnp.exp2(s_bufs[slot][hh] - m_new).astype(BF16)
            al_bufs[slot][hh] = jnp.exp2(m_old - m_new)
            m_ref[hh] = m_new

    def values(t, slot):
        cols = pl.ds(pl.multiple_of((first + t) * TQ, TQ), TQ)
        for hh in range(2):
            head_rows = pl.ds(pl.multiple_of(p * LANES + hh * HEAD_DIM, HEAD_DIM), HEAD_DIM)
            vaug = jnp.concatenate([vt_ref[0, head_rows, cols], ones_rows], axis=0)
            acc_ref[hh] = acc_ref[hh] * al_bufs[slot][hh] + jnp.dot(vaug, p_bufs[slot][hh],
                                                                     preferred_element_type=F32)

    def step(tau, parity, do_scores=False, masked=False, do_softmax=False, do_values=False):
        if do_softmax:
            softmax(1 - parity)
        if do_scores:
            scores(tau, parity, masked)
        if do_values:
            values(tau - 2, parity)

    full = dict(do_scores=True, do_softmax=True, do_values=True)

    @pl.when(i == 0)
    def _():
        step(0, 0, do_scores=True, masked=True)
        step(1, 1, do_softmax=True)
        step(2, 0, do_values=True)

    @pl.when(i == 1)
    def _():
        step(0, 0, do_scores=True)
        step(1, 1, do_scores=True, masked=True, do_softmax=True)
        step(2, 0, do_softmax=True, do_values=True)
        step(3, 1, do_values=True)

    @pl.when(i >= 2)
    def _():
        step(0, 0, do_scores=True)
        step(1, 1, do_scores=True, do_softmax=True)

        def body(u, carry):
            step(2 + 2 * u, 0, **full)
            step(3 + 2 * u, 1, **full)
            return carry

        lax.fori_loop(0, lax.shift_right_logical(i - 2, 1), body, 0)

        @pl.when((i & 1) == 0)
        def _():
            step(i, 0, masked=True, **full)
            step(i + 1, 1, do_softmax=True, do_values=True)
            step(i + 2, 0, do_values=True)

        @pl.when((i & 1) == 1)
        def _():
            step(i - 1, 0, **full)
            step(i, 1, masked=True, **full)
            step(i + 1, 0, do_softmax=True, do_values=True)
            step(i + 2, 1, do_values=True)

    outs = []
    for hh in range(2):
        a = acc_ref[hh]
        outs.append(a[:HEAD_DIM] / a[HEAD_DIM:HEAD_DIM + 1])
    o = jnp.concatenate(outs, axis=0).T
    o_ref[0, :, pair_rows] = (o * _silu(g_ref[0, :, pair_rows].astype(F32))).astype(BF16)


def _first_key_tile(qs, ks, fs, fe):
    B, n_tiles = qs.shape[:2]
    qmax = qs[:, :, :, 0]
    kmax = ks[:, :, 0, :N_HEADS]
    f_first = fs[:, :, ::N_PARTS, 0]
    f_last = fe[:, :, ::N_PARTS, 0]
    gap = f_first[:, :, None, :] - f_last[:, None, :, :]
    reach = NORM_SLACK * qmax[:, :, None, :] * (kmax[:, None, :, :] + kmax[:, :, None, :])
    negligible = (gap + reach) <= -SKIP_LOG2
    both = jnp.all(negligible.reshape(B, n_tiles, n_tiles, N_HEADS // 2, 2), axis=-1)
    tiles = jnp.arange(n_tiles)
    both = both & (tiles[None, :] < tiles[:, None])[None, :, :, None]
    first = jnp.sum(jnp.cumprod(both.astype(jnp.int32), axis=2), axis=2)
    return first.transpose(0, 2, 1).reshape(-1)


def _attention(first, qt, qa, k, ka, vt, gate):
    B, S, _ = k.shape
    grid_spec = pltpu.PrefetchScalarGridSpec(
        num_scalar_prefetch=1,
        grid=(B, S // TQ),
        in_specs=[
            pl.BlockSpec((1, D_ATT, TQ), lambda b, i, first: (b, 0, i)),
            pl.BlockSpec((1, LANES, TQ), lambda b, i, first: (b, 0, i)),
            pl.BlockSpec((1, S, D_ATT), lambda b, i, first: (b, 0, 0)),
            pl.BlockSpec((1, S, LANES), lambda b, i, first: (b, 0, 0)),
            pl.BlockSpec((1, D_ATT, S), lambda b, i, first: (b, 0, 0)),
            pl.BlockSpec((1, TQ, D_ATT), lambda b, i, first: (b, i, 0)),
        ],
        out_specs=pl.BlockSpec((1, TQ, D_ATT), lambda b, i, first: (b, i, 0)),
        scratch_shapes=[
            pltpu.VMEM((2, 1, TQ), F32),
            pltpu.VMEM((2, V_ROWS, TQ), F32),
            pltpu.VMEM((2, 2 * LANES, TQ), BF16),
            pltpu.VMEM((2, TQ, TQ), F32),
            pltpu.VMEM((2, TQ, TQ), F32),
            pltpu.VMEM((2, TQ, TQ), BF16),
            pltpu.VMEM((2, TQ, TQ), BF16),
            pltpu.VMEM((2, 1, TQ), F32),
            pltpu.VMEM((2, 1, TQ), F32),
            pltpu.VMEM((2, 1, TQ), F32),
            pltpu.VMEM((2, 1, TQ), F32),
        ],
    )
    return pl.pallas_call(
        _attn_kernel,
        grid_spec=grid_spec,
        out_shape=jax.ShapeDtypeStruct((B, S, D_ATT), BF16),
        compiler_params=_params(("parallel", "parallel")),
        name="fox_attention",
    )(first, qt, qa, k, ka, vt, gate)


def _outproj_kernel(ya_ref, yl_ref, ys_ref, x_ref, gate_ref, pg_ref, w_ref, o_ref):
    y = jnp.dot(ya_ref[0], w_ref[0, 0:D_ATT, :], preferred_element_type=F32)
    y = y + jnp.dot(yl_ref[0], w_ref[0, D_ATT:D_ATT + D_LRU, :], preferred_element_type=F32)
    y = y + jnp.dot(ys_ref[0], w_ref[0, D_ATT + D_LRU:, :], preferred_element_type=F32)
    ms = jnp.mean(y * y, axis=-1, keepdims=True)
    yn = (y * lax.rsqrt(ms + EPS)) * pg_ref[0]
    o_ref[0] = x_ref[0] + _mod_row(gate_ref) * yn


def _outproj(layer, ya, yl, ys, x, mod, prep):
    B, S, D = x.shape
    per_layer = lambda a: pl.BlockSpec((1,) + a.shape[1:], lambda b, i: (layer,) + (0,) * (a.ndim - 1))
    row_tile = lambda n: pl.BlockSpec((1, TO, n), lambda b, i: (b, i, 0))
    return pl.pallas_call(
        _outproj_kernel,
        grid=(B, S // TO),
        in_specs=[
            row_tile(D_ATT), row_tile(D_LRU), row_tile(D_SG), row_tile(D),
            pl.BlockSpec((1, 1, D // TN_ADA, 1, TN_ADA), lambda b, i: (layer, b, 2, 0, 0)),
            per_layer(prep["post_g"]),
            per_layer(prep["w_out"]),
        ],
        out_specs=row_tile(D),
        out_shape=jax.ShapeDtypeStruct((B, S, D), F32),
        compiler_params=_params(("parallel", "parallel")),
        name="outproj",
    )(ya, yl, ys, x, mod, prep["post_g"], prep["w_out"])


def _block_diag(w):
    L, G, n, _ = w.shape
    eye = jnp.eye(G, dtype=w.dtype)
    return (w[:, :, :, None, :] * eye[None, :, None, :, None]).reshape(L, G * n, G * n)


def _prepare(pre_g, post_g, w_in, b_f, conv_w, conv_b, lru_wa, lru_ba, lru_wx, lru_bx, lru_lambda,
             sg_ln_g, sg_ln_b, sg_w, sg_b, w_out):
    L, D, _ = w_in.shape
    cuts = [0]
    for n in IN_SIZES:
        cuts.append(cuts[-1] + n)
    wq, wk, wv, wf, wga, wxl, wgl, wsu, wsv, wgs = (w_in[:, :, cuts[j]:cuts[j + 1]] for j in range(10))
    row = lambda a: a[:, None, :]
    wf3 = jnp.repeat(wf, N_PARTS, axis=2)
    pad = jnp.zeros((L, D, T_ROWS - 2 * D_ATT - N_AUG), F32)
    return dict(
        pre_g=row(pre_g), post_g=row(post_g),
        we=jnp.concatenate([wxl, wsv], axis=2).astype(BF16),
        wg=jnp.concatenate([wga, wgl, wsu, wgs], axis=2).astype(BF16),
        wk=wk.astype(BF16),
        wt=jnp.concatenate([wq, wv, wf3, pad], axis=2).transpose(0, 2, 1).astype(BF16),
        bf=jnp.broadcast_to(jnp.repeat(b_f, N_PARTS, axis=1)[:, :, None], (L, N_AUG, LANES)),
        conv_w=conv_w, conv_b=row(conv_b),
        wa=_block_diag(lru_wa).astype(BF16), ba=row(lru_ba),
        wx=_block_diag(lru_wx).astype(BF16), bx=row(lru_bx), lam=row(lru_lambda),
        ln_g=row(sg_ln_g), ln_b=row(sg_ln_b),
        ws=sg_w.reshape(L, N_SG_GROUPS * SG_CHUNK, SG_CHUNK).astype(BF16),
        sg_bias=jnp.repeat(sg_b.transpose(0, 2, 1), HEAD_DIM, axis=2),
        w_out=w_out.astype(BF16),
        head_sum=(jnp.arange(D_ATT)[:, None] // HEAD_DIM == jnp.arange(LANES)[None, :]).astype(BF16),
    )


def kernel(x, c, ada_w, ada_b, pre_g, post_g, w_in, b_f, conv_w, conv_b, lru_wa, lru_ba, lru_wx,
           lru_bx, lru_lambda, sg_ln_g, sg_ln_b, sg_w, sg_b, w_out):
    mod = _ada(c, ada_w, ada_b)
    prep = _prepare(pre_g, post_g, w_in, b_f, conv_w, conv_b, lru_wa, lru_ba, lru_wx, lru_bx,
                    lru_lambda, sg_ln_g, sg_ln_b, sg_w, sg_b, w_out)
    for layer in range(ada_w.shape[0]):
        qt, vt, k, ka, qa, ga, yl, ys, qs, ks, fs, fe = _inproj(layer, x, mod, prep)
        ya = _attention(_first_key_tile(qs, ks, fs, fe), qt, qa, k, ka, vt, ga)
        x = _outproj(layer, ya, yl, ys, x, mod, prep)
    return x
```

```python
import math

import jax
import jax.numpy as jnp
from jax import lax
from jax.experimental import pallas as pl
from jax.experimental.pallas import tpu as pltpu

D_MODEL = 1024
HEAD_DIM = 64
D_ATT = 512
D_LRU = 256
D_SG = 256
N_HEADS = D_ATT // HEAD_DIM
N_LRU_BLOCKS = D_LRU // HEAD_DIM
N_SG_GROUPS = D_SG // HEAD_DIM
SG_CHUNK = 128
CONV_WIDTH = 4
LRU_C = 8.0
EPS = 1e-6
IN_SIZES = (D_ATT, D_ATT, D_ATT, N_HEADS, D_ATT, D_LRU, D_LRU, D_SG, D_SG, D_SG)

LANES = 128
SUBLANES = 8
BF16_ROWS = 16
VMEM_LIMIT = 56 * 1024 * 1024

TM = 512
TO = 1024
TQ = 512
TN_ADA = 512
NEG = -1e30

N_PARTS = 3
N_AUG = N_HEADS * N_PARTS
T_ROWS = 2 * D_ATT + 2 * BF16_ROWS
LOG2E = 1.4426950408889634
Q_SCALE = LOG2E / math.sqrt(HEAD_DIM)
V_ROWS = HEAD_DIM + BF16_ROWS
SKIP_LOG2 = 160.0
NORM_SLACK = 1.02
GELU_C0 = math.sqrt(2.0 / math.pi)
GELU_C1 = GELU_C0 * 0.044715

F32 = jnp.float32
BF16 = jnp.bfloat16


def _sigmoid(x):
    return 0.5 * jnp.tanh(0.5 * x) + 0.5


def _silu(x):
    hx = 0.5 * x
    return hx + hx * jnp.tanh(hx)


def _gelu_tanh(x):
    hx = 0.5 * x
    return hx + hx * jnp.tanh(x * (GELU_C0 + GELU_C1 * (x * x)))


def _log_sigmoid(x):
    return jnp.minimum(x, 0.0) - jnp.log1p(jnp.exp(-jnp.abs(x)))


def _split3(x, sel):
    hi = x.astype(BF16).astype(F32)
    r1 = x - hi
    mid = r1.astype(BF16).astype(F32)
    lo = r1 - mid
    return jnp.where(sel == 0, hi, jnp.where(sel == 1, mid, lo))


def _mod_row(ref):
    return jnp.concatenate([ref[0, 0, j] for j in range(ref.shape[2])], axis=1)


def _params(sem):
    return pltpu.CompilerParams(dimension_semantics=sem, vmem_limit_bytes=VMEM_LIMIT)


def _ada_kernel(ct_ref, w_ref, b_ref, o_ref):
    ct = ct_ref[...]
    sc = _silu(ct)
    w = w_ref[0]
    bias = b_ref[0]
    for b in range(ct.shape[1]):
        col = sc[:, b:b + 1]
        o_ref[0, b, 0] = jnp.sum(col * w, axis=0, keepdims=True) + bias


def _ada(c, ada_w, ada_b):
    L, D, N = ada_w.shape
    B = c.shape[0]
    return pl.pallas_call(
        _ada_kernel,
        grid=(L, N // TN_ADA),
        in_specs=[
            pl.BlockSpec((D, B), lambda l, n: (0, 0)),
            pl.BlockSpec((1, D, TN_ADA), lambda l, n: (l, 0, n)),
            pl.BlockSpec((1, 1, TN_ADA), lambda l, n: (l, 0, n)),
        ],
        out_specs=pl.BlockSpec((1, B, 1, 1, TN_ADA), lambda l, n: (l, 0, n, 0, 0)),
        out_shape=jax.ShapeDtypeStruct((L, B, N // TN_ADA, 1, TN_ADA), F32),
        compiler_params=_params(("parallel", "parallel")),
        name="ada_mod",
    )(c.T, ada_w, ada_b.reshape(L, 1, N))


def _lru_mixer(g_lru, r_pre, i_pre, lam, h_ref, xc):
    r = _sigmoid(r_pre)
    ig = _sigmoid(i_pre)
    nlam = -lam
    softplus = jnp.maximum(nlam, 0.0) + jnp.log1p(jnp.exp(-jnp.abs(nlam)))
    log_a = (-LRU_C * r) * softplus
    a = jnp.exp(log_a)
    bt = jnp.sqrt(jnp.tanh(-log_a) * (1.0 + a * a)) * (ig * xc)

    row8 = lax.broadcasted_iota(jnp.int32, (TM, D_LRU), 0) & (SUBLANES - 1)
    d = 1
    while d < SUBLANES:
        valid = row8 >= d
        a_s = jnp.where(valid, pltpu.roll(a, d, axis=0), 1.0)
        b_s = jnp.where(valid, pltpu.roll(bt, d, axis=0), 0.0)
        bt = bt + a * b_s
        a = a * a_s
        d *= 2

    h_prev = h_ref[...]
    sg = _silu(g_lru)
    outs = []
    for g in range(TM // SUBLANES):
        lo = g * SUBLANES
        hg = bt[lo:lo + SUBLANES] + a[lo:lo + SUBLANES] * h_prev
        h_prev = hg[SUBLANES - 1:SUBLANES, :]
        outs.append(hg * sg[lo:lo + SUBLANES])
    h_ref[...] = h_prev
    return jnp.concatenate(outs, axis=0)


def _inproj_kernel(x_ref, shift_ref, scale_ref, g_ref, we_ref, wg_ref, wk_ref, wt_ref, bf_ref,
                   cw_ref, cb_ref, wa_ref, ba_ref, wx_ref, bx_ref, lam_ref,
                   lng_ref, lnb_ref, ws_ref, sb_ref, he_ref,
                   qt_ref, vt_ref, k_ref, ka_ref, qa_ref, ga_ref, yl_ref, ys_ref,
                   qs_ref, ks_ref, fs_ref, fe_ref,
                   fc_ref, xbuf_ref, h_ref):
    i = pl.program_id(1)

    @pl.when(i == 0)
    def _():
        fc_ref[...] = jnp.zeros(fc_ref.shape, F32)
        xbuf_ref[0:SUBLANES, :] = jnp.zeros((SUBLANES, D_LRU), F32)
        h_ref[...] = jnp.zeros(h_ref.shape, F32)

    x = x_ref[0]
    ms = jnp.mean(x * x, axis=-1, keepdims=True)
    gs = g_ref[0] * (1.0 + _mod_row(scale_ref))
    h = (x * lax.rsqrt(ms + EPS)) * gs + _mod_row(shift_ref)
    hb = h.astype(BF16)

    early = jnp.dot(hb, we_ref[0], preferred_element_type=F32)
    x_lru = early[:, :D_LRU]
    sg_v = early[:, D_LRU:]
    tr = lax.dot_general(wt_ref[0], hb, (((1,), (1,)), ((), ())), preferred_element_type=F32)
    tq = tr[:D_ATT] * Q_SCALE
    qt_ref[0] = tq.astype(BF16)
    vt_ref[0] = tr[D_ATT:2 * D_ATT].astype(BF16)
    fl = tr[2 * D_ATT:2 * D_ATT + N_AUG]
    kb = jnp.dot(hb, wk_ref[0], preferred_element_type=F32).astype(BF16)
    k_ref[0] = kb
    qn2 = jnp.sum((tq * tq).reshape(N_HEADS, HEAD_DIM, TM), axis=1)
    qs_ref[0, 0] = jnp.broadcast_to(jnp.sqrt(jnp.max(qn2, axis=1, keepdims=True)), (N_HEADS, LANES))

    v = _gelu_tanh(sg_v)
    mu = jnp.mean(v, axis=-1, keepdims=True)
    var = jnp.mean(jnp.square(v - mu), axis=-1, keepdims=True)
    vnb = (((v - mu) * lax.rsqrt(var + EPS)) * lng_ref[0] + lnb_ref[0]).astype(BF16)
    w_shape = ws_ref.shape[1:]
    t_idx = lax.broadcasted_iota(jnp.int32, w_shape, 0) & (SG_CHUNK - 1)
    s_idx = lax.broadcasted_iota(jnp.int32, w_shape, 1)
    ws = jnp.where(s_idx <= t_idx, ws_ref[0], jnp.zeros(w_shape, BF16))
    grp = lax.broadcasted_iota(jnp.int32, (SG_CHUNK, D_SG), 1) // HEAD_DIM
    zs = []
    for c in range(TM // SG_CHUNK):
        zz = jnp.dot(ws, vnb[c * SG_CHUNK:(c + 1) * SG_CHUNK], preferred_element_type=F32)
        z = zz[0:SG_CHUNK]
        for gi in range(1, N_SG_GROUPS):
            z = jnp.where(grp == gi, zz[gi * SG_CHUNK:(gi + 1) * SG_CHUNK], z)
        zs.append(z + sb_ref[0])
    z_all = jnp.concatenate(zs, axis=0)

    xbuf_ref[SUBLANES:SUBLANES + TM, :] = x_lru
    xc = cb_ref[0]
    for kk in range(CONV_WIDTH):
        off = SUBLANES - (CONV_WIDTH - 1) + kk
        xc = xc + xbuf_ref[off:off + TM, :] * cw_ref[0, kk:kk + 1, :]
    xbuf_ref[0:SUBLANES, :] = x_lru[TM - SUBLANES:, :]
    xcb = xc.astype(BF16)
    r_pre = jnp.dot(xcb, wa_ref[0], preferred_element_type=F32) + ba_ref[0]
    i_pre = jnp.dot(xcb, wx_ref[0], preferred_element_type=F32) + bx_ref[0]

    gates = jnp.dot(hb, wg_ref[0], preferred_element_type=F32)
    kn2 = jnp.dot(kb * kb, he_ref[...], preferred_element_type=F32)
    ks_ref[0, 0] = jnp.sqrt(jnp.max(kn2, axis=0, keepdims=True))

    ga_ref[0] = gates[:, :D_ATT].astype(BF16)
    g_lru = gates[:, D_ATT:D_ATT + D_LRU]
    sg_u = gates[:, D_ATT + D_LRU:D_ATT + D_LRU + D_SG]
    g_sg = gates[:, D_ATT + D_LRU + D_SG:]
    ys_ref[0] = ((_gelu_tanh(sg_u) * z_all) * _silu(g_sg)).astype(BF16)
    yl_ref[0] = _lru_mixer(g_lru, r_pre, i_pre, lam_ref[0], h_ref, xc).astype(BF16)

    reps = TM // LANES
    ls = _log_sigmoid(fl + jnp.concatenate([bf_ref[0]] * reps, axis=1))
    lane = lax.broadcasted_iota(jnp.int32, (N_AUG, TM), 1)
    d = 1
    while d < TM:
        ls = ls + jnp.where(lane >= d, pltpu.roll(ls, d, axis=1), 0.0)
        d *= 2
    f = ls + jnp.concatenate([fc_ref[...]] * reps, axis=1)
    fc_ref[...] = jnp.broadcast_to(f[:, TM - 1:TM], (N_AUG, LANES))
    row = lax.broadcasted_iota(jnp.int32, (N_AUG, TM), 0)
    f2 = f * LOG2E
    fs_ref[0, 0] = jnp.broadcast_to(f2[:, 0:1], (N_AUG, LANES))
    fe_ref[0, 0] = jnp.broadcast_to(f2[:, TM - 1:TM], (N_AUG, LANES))
    parts = _split3(f2, row % N_PARTS)
    ones = jnp.ones((N_AUG, TM), F32)
    zeros = jnp.zeros((LANES - 2 * N_AUG, TM), F32)
    qa_ref[0] = jnp.concatenate([ones, parts, zeros], axis=0).astype(BF16)
    ka_ref[0] = jnp.concatenate([-parts, ones, zeros], axis=0).T.astype(BF16)


def _inproj(layer, x, mod, prep):
    B, S, D = x.shape
    per_layer = lambda a: pl.BlockSpec((1,) + a.shape[1:], lambda b, i: (layer,) + (0,) * (a.ndim - 1))
    mod_chunk = lambda j: pl.BlockSpec((1, 1, D // TN_ADA, 1, TN_ADA), lambda b, i: (layer, b, j, 0, 0))
    row_tile = lambda n: pl.BlockSpec((1, TM, n), lambda b, i: (b, i, 0))
    col_tile = lambda n: pl.BlockSpec((1, n, TM), lambda b, i: (b, 0, i))
    names = ("pre_g", "we", "wg", "wk", "wt", "bf", "conv_w", "conv_b", "wa", "ba", "wx", "bx", "lam",
             "ln_g", "ln_b", "ws", "sg_bias")
    operands = [prep[n] for n in names]
    head_sum = prep["head_sum"]
    stat = lambda rows: pl.BlockSpec((1, 1, rows, LANES), lambda b, i: (b, i, 0, 0))
    stat_shape = lambda rows: jax.ShapeDtypeStruct((B, S // TM, rows, LANES), F32)
    return pl.pallas_call(
        _inproj_kernel,
        grid=(B, S // TM),
        in_specs=[row_tile(D), mod_chunk(0), mod_chunk(1)] + [per_layer(a) for a in operands]
        + [pl.BlockSpec(head_sum.shape, lambda b, i: (0, 0))],
        out_specs=[
            col_tile(D_ATT), col_tile(D_ATT), row_tile(D_ATT), row_tile(LANES), col_tile(LANES),
            row_tile(D_ATT), row_tile(D_LRU), row_tile(D_SG),
            stat(N_HEADS), stat(1), stat(N_AUG), stat(N_AUG),
        ],
        out_shape=[
            jax.ShapeDtypeStruct((B, D_ATT, S), BF16),
            jax.ShapeDtypeStruct((B, D_ATT, S), BF16),
            jax.ShapeDtypeStruct((B, S, D_ATT), BF16),
            jax.ShapeDtypeStruct((B, S, LANES), BF16),
            jax.ShapeDtypeStruct((B, LANES, S), BF16),
            jax.ShapeDtypeStruct((B, S, D_ATT), BF16),
            jax.ShapeDtypeStruct((B, S, D_LRU), BF16),
            jax.ShapeDtypeStruct((B, S, D_SG), BF16),
            stat_shape(N_HEADS),
            stat_shape(1),
            stat_shape(N_AUG),
            stat_shape(N_AUG),
        ],
        scratch_shapes=[
            pltpu.VMEM((N_AUG, LANES), F32),
            pltpu.VMEM((TM + SUBLANES, D_LRU), F32),
            pltpu.VMEM((1, D_LRU), F32),
        ],
        compiler_params=_params(("parallel", "arbitrary")),
        name="inproj",
    )(x, mod, mod, *operands, head_sum)


def _attn_kernel(*refs):
    def body(p, carry):
        _attn_pair(p, *refs)
        return carry

    lax.fori_loop(0, N_HEADS // 2, body, 0)


def _attn_pair(p, first_ref, qt_ref, qa_ref, k_ref, ka_ref, vt_ref, g_ref, o_ref,
               m_ref, acc_ref, rhs_ref, s0_ref, s1_ref, p0_ref, p1_ref, mx0_ref, mx1_ref, al0_ref, al1_ref):
    first = first_ref[(pl.program_id(0) * (N_HEADS // 2) + p) * pl.num_programs(1) + pl.program_id(1)]
    i = pl.program_id(1) - first
    pair_rows = pl.ds(pl.multiple_of(p * LANES, LANES), LANES)
    qt = qt_ref[0, pair_rows, :]
    qa = qa_ref[0]
    row = lax.broadcasted_iota(jnp.int32, (LANES, TQ), 0)
    zero = jnp.zeros_like(qt)
    for hh in range(2):
        head = 2 * p + hh
        own = (row >= hh * HEAD_DIM) & (row < (hh + 1) * HEAD_DIM)
        lo = head * N_PARTS
        aug = ((row >= lo) & (row < lo + N_PARTS)) | ((row >= N_AUG + lo) & (row < N_AUG + lo + N_PARTS))
        rhs_ref[hh] = jnp.concatenate([jnp.where(own, qt, zero), jnp.where(aug, qa, zero)], axis=0)
    ones_rows = jnp.where(lax.broadcasted_iota(jnp.int32, (BF16_ROWS, TQ), 0) == 0, 1.0, 0.0).astype(BF16)
    m_ref[...] = jnp.full(m_ref.shape, NEG, F32)
    acc_ref[...] = jnp.zeros(acc_ref.shape, F32)
    s_bufs = (s0_ref, s1_ref)
    p_bufs = (p0_ref, p1_ref)
    mx_bufs = (mx0_ref, mx1_ref)
    al_bufs = (al0_ref, al1_ref)

    def scores(t, slot, masked):
        rows = pl.ds(pl.multiple_of((first + t) * TQ, TQ), TQ)
        lhs = jnp.concatenate([k_ref[0, rows, pair_rows], ka_ref[0, rows, :]], axis=1)
        for hh in range(2):
            st = jnp.dot(lhs, rhs_ref[hh], preferred_element_type=F32)
            if masked:
                kk = lax.broadcasted_iota(jnp.int32, (TQ, TQ), 0)
                qq = lax.broadcasted_iota(jnp.int32, (TQ, TQ), 1)
                st = jnp.where(kk <= qq, st, NEG)
            s_bufs[slot][hh] = st
            mx_bufs[slot][hh] = jnp.max(st, axis=0, keepdims=True)

    def softmax(slot):
        for hh in range(2):
            m_old = m_ref[hh]
            m_new = jnp.maximum(m_old, mx_bufs[slot][hh])
            p_bufs[slot][hh] = jnp.exp2(s_bufs[slot][hh] - m_new).astype(BF16)
            al_bufs[slot][hh] = jnp.exp2(m_old - m_new)
            m_ref[hh] = m_new

    def values(t, slot):
        cols = pl.ds(pl.multiple_of((first + t) * TQ, TQ), TQ)
        for hh in range(2):
            head_rows = pl.ds(pl.multiple_of(p * LANES + hh * HEAD_DIM, HEAD_DIM), HEAD_DIM)
            vaug = jnp.concatenate([vt_ref[0, head_rows, cols], ones_rows], axis=0)
            acc_ref[hh] = acc_ref[hh] * al_bufs[slot][hh] + jnp.dot(vaug, p_bufs[slot][hh],
                                                                     preferred_element_type=F32)

    def step(tau, parity, do_scores=False, masked=False, do_softmax=False, do_values=False):
        if do_softmax:
            softmax(1 - parity)
        if do_scores:
            scores(tau, parity, masked)
        if do_values:
            values(tau - 2, parity)

    full = dict(do_scores=True, do_softmax=True, do_values=True)

    @pl.when(i == 0)
    def _():
        step(0, 0, do_scores=True, masked=True)
        step(1, 1, do_softmax=True)
        step(2, 0, do_values=True)

    @pl.when(i == 1)
    def _():
        step(0, 0, do_scores=True)
        step(1, 1, do_scores=True, masked=True, do_softmax=True)
        step(2, 0, do_softmax=True, do_values=True)
        step(3, 1, do_values=True)

    @pl.when(i >= 2)
    def _():
        step(0, 0, do_scores=True)
        step(1, 1, do_scores=True, do_softmax=True)

        def body(u, carry):
            step(2 + 2 * u, 0, **full)
            step(3 + 2 * u, 1, **full)
            return carry

        lax.fori_loop(0, lax.shift_right_logical(i - 2, 1), body, 0)

        @pl.when((i & 1) == 0)
        def _():
            step(i, 0, masked=True, **full)
            step(i + 1, 1, do_softmax=True, do_values=True)
            step(i + 2, 0, do_values=True)

        @pl.when((i & 1) == 1)
        def _():
            step(i - 1, 0, **full)
            step(i, 1, masked=True, **full)
            step(i + 1, 0, do_softmax=True, do_values=True)
            step(i + 2, 1, do_values=True)

    outs = []
    for hh in range(2):
        a = acc_ref[hh]
        outs.append(a[:HEAD_DIM] / a[HEAD_DIM:HEAD_DIM + 1])
    o = jnp.concatenate(outs, axis=0).T
    o_ref[0, :, pair_rows] = (o * _silu(g_ref[0, :, pair_rows].astype(F32))).astype(BF16)


def _first_key_tile(qs, ks, fs, fe):
    B, n_tiles = qs.shape[:2]
    qmax = qs[:, :, :, 0]
    kmax = ks[:, :, 0, :N_HEADS]
    f_first = fs[:, :, ::N_PARTS, 0]
    f_last = fe[:, :, ::N_PARTS, 0]
    gap = f_first[:, :, None, :] - f_last[:, None, :, :]
    reach = NORM_SLACK * qmax[:, :, None, :] * (kmax[:, None, :, :] + kmax[:, :, None, :])
    negligible = (gap + reach) <= -SKIP_LOG2
    both = jnp.all(negligible.reshape(B, n_tiles, n_tiles, N_HEADS // 2, 2), axis=-1)
    tiles = jnp.arange(n_tiles)
    both = both & (tiles[None, :] < tiles[:, None])[None, :, :, None]
    first = jnp.min(jnp.where(both, n_tiles, tiles[None, None, :, None]), axis=2)
    return first.transpose(0, 2, 1).reshape(-1)


def _attention(first, qt, qa, k, ka, vt, gate):
    B, S, _ = k.shape
    grid_spec = pltpu.PrefetchScalarGridSpec(
        num_scalar_prefetch=1,
        grid=(B, S // TQ),
        in_specs=[
            pl.BlockSpec((1, D_ATT, TQ), lambda b, i, first: (b, 0, i)),
            pl.BlockSpec((1, LANES, TQ), lambda b, i, first: (b, 0, i)),
            pl.BlockSpec((1, S, D_ATT), lambda b, i, first: (b, 0, 0)),
            pl.BlockSpec((1, S, LANES), lambda b, i, first: (b, 0, 0)),
            pl.BlockSpec((1, D_ATT, S), lambda b, i, first: (b, 0, 0)),
            pl.BlockSpec((1, TQ, D_ATT), lambda b, i, first: (b, i, 0)),
        ],
        out_specs=pl.BlockSpec((1, TQ, D_ATT), lambda b, i, first: (b, i, 0)),
        scratch_shapes=[
            pltpu.VMEM((2, 1, TQ), F32),
            pltpu.VMEM((2, V_ROWS, TQ), F32),
            pltpu.VMEM((2, 2 * LANES, TQ), BF16),
            pltpu.VMEM((2, TQ, TQ), F32),
            pltpu.VMEM((2, TQ, TQ), F32),
            pltpu.VMEM((2, TQ, TQ), BF16),
            pltpu.VMEM((2, TQ, TQ), BF16),
            pltpu.VMEM((2, 1, TQ), F32),
            pltpu.VMEM((2, 1, TQ), F32),
            pltpu.VMEM((2, 1, TQ), F32),
            pltpu.VMEM((2, 1, TQ), F32),
        ],
    )
    return pl.pallas_call(
        _attn_kernel,
        grid_spec=grid_spec,
        out_shape=jax.ShapeDtypeStruct((B, S, D_ATT), BF16),
        compiler_params=_params(("parallel", "parallel")),
        name="fox_attention",
    )(first, qt, qa, k, ka, vt, gate)


def _outproj_kernel(ya_ref, yl_ref, ys_ref, x_ref, gate_ref, pg_ref, w_ref, o_ref):
    y = jnp.dot(ya_ref[0], w_ref[0, 0:D_ATT, :], preferred_element_type=F32)
    y = y + jnp.dot(yl_ref[0], w_ref[0, D_ATT:D_ATT + D_LRU, :], preferred_element_type=F32)
    y = y + jnp.dot(ys_ref[0], w_ref[0, D_ATT + D_LRU:, :], preferred_element_type=F32)
    ms = jnp.mean(y * y, axis=-1, keepdims=True)
    yn = (y * lax.rsqrt(ms + EPS)) * pg_ref[0]
    o_ref[0] = x_ref[0] + _mod_row(gate_ref) * yn


def _outproj(layer, ya, yl, ys, x, mod, prep):
    B, S, D = x.shape
    per_layer = lambda a: pl.BlockSpec((1,) + a.shape[1:], lambda b, i: (layer,) + (0,) * (a.ndim - 1))
    row_tile = lambda n: pl.BlockSpec((1, TO, n), lambda b, i: (b, i, 0))
    return pl.pallas_call(
        _outproj_kernel,
        grid=(B, S // TO),
        in_specs=[
            row_tile(D_ATT), row_tile(D_LRU), row_tile(D_SG), row_tile(D),
            pl.BlockSpec((1, 1, D // TN_ADA, 1, TN_ADA), lambda b, i: (layer, b, 2, 0, 0)),
            per_layer(prep["post_g"]),
            per_layer(prep["w_out"]),
        ],
        out_specs=row_tile(D),
        out_shape=jax.ShapeDtypeStruct((B, S, D), F32),
        compiler_params=_params(("parallel", "parallel")),
        name="outproj",
    )(ya, yl, ys, x, mod, prep["post_g"], prep["w_out"])


def _block_diag(w):
    L, G, n, _ = w.shape
    eye = jnp.eye(G, dtype=w.dtype)
    return (w[:, :, :, None, :] * eye[None, :, None, :, None]).reshape(L, G * n, G * n)


def _prepare(pre_g, post_g, w_in, b_f, conv_w, conv_b, lru_wa, lru_ba, lru_wx, lru_bx, lru_lambda,
             sg_ln_g, sg_ln_b, sg_w, sg_b, w_out):
    L, D, _ = w_in.shape
    cuts = [0]
    for n in IN_SIZES:
        cuts.append(cuts[-1] + n)
    wq, wk, wv, wf, wga, wxl, wgl, wsu, wsv, wgs = (w_in[:, :, cuts[j]:cuts[j + 1]] for j in range(10))
    row = lambda a: a[:, None, :]
    wf3 = jnp.repeat(wf, N_PARTS, axis=2)
    pad = jnp.zeros((L, D, T_ROWS - 2 * D_ATT - N_AUG), F32)
    return dict(
        pre_g=row(pre_g), post_g=row(post_g),
        we=jnp.concatenate([wxl, wsv], axis=2).astype(BF16),
        wg=jnp.concatenate([wga, wgl, wsu, wgs], axis=2).astype(BF16),
        wk=wk.astype(BF16),
        wt=jnp.concatenate([wq, wv, wf3, pad], axis=2).transpose(0, 2, 1).astype(BF16),
        bf=jnp.broadcast_to(jnp.repeat(b_f, N_PARTS, axis=1)[:, :, None], (L, N_AUG, LANES)),
        conv_w=conv_w, conv_b=row(conv_b),
        wa=_block_diag(lru_wa).astype(BF16), ba=row(lru_ba),
        wx=_block_diag(lru_wx).astype(BF16), bx=row(lru_bx), lam=row(lru_lambda),
        ln_g=row(sg_ln_g), ln_b=row(sg_ln_b),
        ws=sg_w.reshape(L, N_SG_GROUPS * SG_CHUNK, SG_CHUNK).astype(BF16),
        sg_bias=jnp.repeat(sg_b.transpose(0, 2, 1), HEAD_DIM, axis=2),
        w_out=w_out.astype(BF16),
        head_sum=(jnp.arange(D_ATT)[:, None] // HEAD_DIM == jnp.arange(LANES)[None, :]).astype(BF16),
    )


def kernel(x, c, ada_w, ada_b, pre_g, post_g, w_in, b_f, conv_w, conv_b, lru_wa, lru_ba, lru_wx,
           lru_bx, lru_lambda, sg_ln_g, sg_ln_b, sg_w, sg_b, w_out):
    mod = _ada(c, ada_w, ada_b)
    prep = _prepare(pre_g, post_g, w_in, b_f, conv_w, conv_b, lru_wa, lru_ba, lru_wx, lru_bx,
                    lru_lambda, sg_ln_g, sg_ln_b, sg_w, sg_b, w_out)
    for layer in range(ada_w.shape[0]):
        qt, vt, k, ka, qa, ga, yl, ys, qs, ks, fs, fe = _inproj(layer, x, mod, prep)
        ya = _attention(_first_key_tile(qs, ks, fs, fe), qt, qa, k, ka, vt, ga)
        x = _outproj(layer, ya, yl, ys, x, mod, prep)
    return x
```

```python
import math

import jax
import jax.numpy as jnp
from jax import lax
from jax.experimental import pallas as pl
from jax.experimental.pallas import tpu as pltpu

D_MODEL = 1024
HEAD_DIM = 64
D_ATT = 512
D_LRU = 256
D_SG = 256
N_HEADS = D_ATT // HEAD_DIM
N_PAIRS = N_HEADS // 2
N_LRU_BLOCKS = D_LRU // HEAD_DIM
N_SG_GROUPS = D_SG // HEAD_DIM
SG_CHUNK = 128
CONV_WIDTH = 4
LRU_C = 8.0
EPS = 1e-6
IN_SIZES = (D_ATT, D_ATT, D_ATT, N_HEADS, D_ATT, D_LRU, D_LRU, D_SG, D_SG, D_SG)

LANES = 128
SUBLANES = 8
BF16_ROWS = 16
VMEM_LIMIT = 56 * 1024 * 1024

TM = 512
TO = 1024
TQ = 512
TN_ADA = 512
NEG = -1e30

N_PARTS = 3
N_AUG = N_HEADS * N_PARTS
T_ROWS = 2 * D_ATT + 2 * BF16_ROWS
LOG2E = 1.4426950408889634
Q_SCALE = LOG2E / math.sqrt(HEAD_DIM)
V_ROWS = HEAD_DIM + BF16_ROWS
SKIP_LOG2 = 160.0
NORM_SLACK = 1.02
GELU_C0 = math.sqrt(2.0 / math.pi)
GELU_C1 = GELU_C0 * 0.044715

F32 = jnp.float32
BF16 = jnp.bfloat16


def _sigmoid(x):
    return 0.5 * jnp.tanh(0.5 * x) + 0.5


def _silu(x):
    hx = 0.5 * x
    return hx + hx * jnp.tanh(hx)


def _gelu_tanh(x):
    hx = 0.5 * x
    return hx + hx * jnp.tanh(x * (GELU_C0 + GELU_C1 * (x * x)))


def _log_sigmoid(x):
    return jnp.minimum(x, 0.0) - jnp.log1p(jnp.exp(-jnp.abs(x)))


def _split3(x, sel):
    hi = x.astype(BF16).astype(F32)
    r1 = x - hi
    mid = r1.astype(BF16).astype(F32)
    lo = r1 - mid
    return jnp.where(sel == 0, hi, jnp.where(sel == 1, mid, lo))


def _mod_row(ref):
    return jnp.concatenate([ref[0, 0, j] for j in range(ref.shape[2])], axis=1)


def _params(sem):
    return pltpu.CompilerParams(dimension_semantics=sem, vmem_limit_bytes=VMEM_LIMIT)


def _ada_kernel(ct_ref, w_ref, b_ref, o_ref):
    ct = ct_ref[...]
    sc = _silu(ct)
    w = w_ref[0]
    bias = b_ref[0]
    for b in range(ct.shape[1]):
        col = sc[:, b:b + 1]
        o_ref[0, b, 0] = jnp.sum(col * w, axis=0, keepdims=True) + bias


def _ada(c, ada_w, ada_b):
    L, D, N = ada_w.shape
    B = c.shape[0]
    return pl.pallas_call(
        _ada_kernel,
        grid=(L, N // TN_ADA),
        in_specs=[
            pl.BlockSpec((D, B), lambda l, n: (0, 0)),
            pl.BlockSpec((1, D, TN_ADA), lambda l, n: (l, 0, n)),
            pl.BlockSpec((1, 1, TN_ADA), lambda l, n: (l, 0, n)),
        ],
        out_specs=pl.BlockSpec((1, B, 1, 1, TN_ADA), lambda l, n: (l, 0, n, 0, 0)),
        out_shape=jax.ShapeDtypeStruct((L, B, N // TN_ADA, 1, TN_ADA), F32),
        compiler_params=_params(("parallel", "parallel")),
        name="ada_mod",
    )(c.T, ada_w, ada_b.reshape(L, 1, N))


def _lru_mixer(g_lru, r_pre, i_pre, lam, h_ref, xc):
    r = _sigmoid(r_pre)
    ig = _sigmoid(i_pre)
    nlam = -lam
    softplus = jnp.maximum(nlam, 0.0) + jnp.log1p(jnp.exp(-jnp.abs(nlam)))
    log_a = (-LRU_C * r) * softplus
    a = jnp.exp(log_a)
    bt = jnp.sqrt(jnp.tanh(-log_a) * (1.0 + a * a)) * (ig * xc)

    row8 = lax.broadcasted_iota(jnp.int32, (TM, D_LRU), 0) & (SUBLANES - 1)
    d = 1
    while d < SUBLANES:
        valid = row8 >= d
        a_s = jnp.where(valid, pltpu.roll(a, d, axis=0), 1.0)
        b_s = jnp.where(valid, pltpu.roll(bt, d, axis=0), 0.0)
        bt = bt + a * b_s
        a = a * a_s
        d *= 2

    h_prev = h_ref[...]
    sg = _silu(g_lru)
    outs = []
    for g in range(TM // SUBLANES):
        lo = g * SUBLANES
        hg = bt[lo:lo + SUBLANES] + a[lo:lo + SUBLANES] * h_prev
        h_prev = hg[SUBLANES - 1:SUBLANES, :]
        outs.append(hg * sg[lo:lo + SUBLANES])
    h_ref[...] = h_prev
    return jnp.concatenate(outs, axis=0)


def _inproj_kernel(x_ref, shift_ref, scale_ref, g_ref, we_ref, wg_ref, wk_ref, wt_ref, bf_ref,
                   cw_ref, cb_ref, wa_ref, ba_ref, wx_ref, bx_ref, lam_ref,
                   lng_ref, lnb_ref, ws_ref, sb_ref, he_ref,
                   qt_ref, vt_ref, k_ref, ka_ref, qa_ref, ga_ref, yl_ref, ys_ref,
                   qs_ref, ks_ref, fs_ref, fe_ref,
                   fc_ref, xbuf_ref, h_ref):
    i = pl.program_id(1)

    @pl.when(i == 0)
    def _():
        fc_ref[...] = jnp.zeros(fc_ref.shape, F32)
        xbuf_ref[0:SUBLANES, :] = jnp.zeros((SUBLANES, D_LRU), F32)
        h_ref[...] = jnp.zeros(h_ref.shape, F32)

    x = x_ref[0]
    ms = jnp.mean(x * x, axis=-1, keepdims=True)
    gs = g_ref[0] * (1.0 + _mod_row(scale_ref))
    h = (x * lax.rsqrt(ms + EPS)) * gs + _mod_row(shift_ref)
    hb = h.astype(BF16)

    early = jnp.dot(hb, we_ref[0], preferred_element_type=F32)
    x_lru = early[:, :D_LRU]
    sg_v = early[:, D_LRU:]
    tr = lax.dot_general(wt_ref[0], hb, (((1,), (1,)), ((), ())), preferred_element_type=F32)
    tq = tr[:D_ATT] * Q_SCALE
    qt_ref[0] = tq.astype(BF16)
    vt_ref[0] = tr[D_ATT:2 * D_ATT].astype(BF16)
    fl = tr[2 * D_ATT:2 * D_ATT + N_AUG]
    kb = jnp.dot(hb, wk_ref[0], preferred_element_type=F32).astype(BF16)
    k_ref[0] = kb
    qn2 = jnp.sum((tq * tq).reshape(N_HEADS, HEAD_DIM, TM), axis=1)
    qs_ref[0, 0] = jnp.broadcast_to(jnp.sqrt(jnp.max(qn2, axis=1, keepdims=True)), (N_HEADS, LANES))

    v = _gelu_tanh(sg_v)
    mu = jnp.mean(v, axis=-1, keepdims=True)
    var = jnp.mean(jnp.square(v - mu), axis=-1, keepdims=True)
    vnb = (((v - mu) * lax.rsqrt(var + EPS)) * lng_ref[0] + lnb_ref[0]).astype(BF16)
    w_shape = ws_ref.shape[1:]
    t_idx = lax.broadcasted_iota(jnp.int32, w_shape, 0) & (SG_CHUNK - 1)
    s_idx = lax.broadcasted_iota(jnp.int32, w_shape, 1)
    ws = jnp.where(s_idx <= t_idx, ws_ref[0], jnp.zeros(w_shape, BF16))
    grp = lax.broadcasted_iota(jnp.int32, (SG_CHUNK, D_SG), 1) // HEAD_DIM
    zs = []
    for c in range(TM // SG_CHUNK):
        zz = jnp.dot(ws, vnb[c * SG_CHUNK:(c + 1) * SG_CHUNK], preferred_element_type=F32)
        z = zz[0:SG_CHUNK]
        for gi in range(1, N_SG_GROUPS):
            z = jnp.where(grp == gi, zz[gi * SG_CHUNK:(gi + 1) * SG_CHUNK], z)
        zs.append(z + sb_ref[0])
    z_all = jnp.concatenate(zs, axis=0)

    xbuf_ref[SUBLANES:SUBLANES + TM, :] = x_lru
    xc = cb_ref[0]
    for kk in range(CONV_WIDTH):
        off = SUBLANES - (CONV_WIDTH - 1) + kk
        xc = xc + xbuf_ref[off:off + TM, :] * cw_ref[0, kk:kk + 1, :]
    xbuf_ref[0:SUBLANES, :] = x_lru[TM - SUBLANES:, :]
    xcb = xc.astype(BF16)
    r_pre = jnp.dot(xcb, wa_ref[0], preferred_element_type=F32) + ba_ref[0]
    i_pre = jnp.dot(xcb, wx_ref[0], preferred_element_type=F32) + bx_ref[0]

    gates = jnp.dot(hb, wg_ref[0], preferred_element_type=F32)
    kn2 = jnp.dot(kb * kb, he_ref[...], preferred_element_type=F32)
    ks_ref[0, 0] = jnp.sqrt(jnp.max(kn2, axis=0, keepdims=True))

    ga_ref[0] = gates[:, :D_ATT].astype(BF16)
    g_lru = gates[:, D_ATT:D_ATT + D_LRU]
    sg_u = gates[:, D_ATT + D_LRU:D_ATT + D_LRU + D_SG]
    g_sg = gates[:, D_ATT + D_LRU + D_SG:]
    ys_ref[0] = ((_gelu_tanh(sg_u) * z_all) * _silu(g_sg)).astype(BF16)
    yl_ref[0] = _lru_mixer(g_lru, r_pre, i_pre, lam_ref[0], h_ref, xc).astype(BF16)

    reps = TM // LANES
    ls = _log_sigmoid(fl + jnp.concatenate([bf_ref[0]] * reps, axis=1))
    lane = lax.broadcasted_iota(jnp.int32, (N_AUG, TM), 1)
    d = 1
    while d < TM:
        ls = ls + jnp.where(lane >= d, pltpu.roll(ls, d, axis=1), 0.0)
        d *= 2
    f = ls + jnp.concatenate([fc_ref[...]] * reps, axis=1)
    fc_ref[...] = jnp.broadcast_to(f[:, TM - 1:TM], (N_AUG, LANES))
    row = lax.broadcasted_iota(jnp.int32, (N_AUG, TM), 0)
    f2 = f * LOG2E
    fs_ref[0, 0] = jnp.broadcast_to(f2[:, 0:1], (N_AUG, LANES))
    fe_ref[0, 0] = jnp.broadcast_to(f2[:, TM - 1:TM], (N_AUG, LANES))
    parts = _split3(f2, row % N_PARTS)
    ones = jnp.ones((N_AUG, TM), F32)
    zeros = jnp.zeros((LANES - 2 * N_AUG, TM), F32)
    qa_ref[0] = jnp.concatenate([ones, parts, zeros], axis=0).astype(BF16)
    ka_ref[0] = jnp.concatenate([-parts, ones, zeros], axis=0).T.astype(BF16)


def _inproj(layer, x, mod, prep):
    B, S, D = x.shape
    per_layer = lambda a: pl.BlockSpec((1,) + a.shape[1:], lambda b, i: (layer,) + (0,) * (a.ndim - 1))
    mod_chunk = lambda j: pl.BlockSpec((1, 1, D // TN_ADA, 1, TN_ADA), lambda b, i: (layer, b, j, 0, 0))
    row_tile = lambda n: pl.BlockSpec((1, TM, n), lambda b, i: (b, i, 0))
    col_tile = lambda n: pl.BlockSpec((1, n, TM), lambda b, i: (b, 0, i))
    names = ("pre_g", "we", "wg", "wk", "wt", "bf", "conv_w", "conv_b", "wa", "ba", "wx", "bx", "lam",
             "ln_g", "ln_b", "ws", "sg_bias")
    operands = [prep[n] for n in names]
    head_sum = prep["head_sum"]
    stat = lambda rows: pl.BlockSpec((1, 1, rows, LANES), lambda b, i: (b, i, 0, 0))
    stat_shape = lambda rows: jax.ShapeDtypeStruct((B, S // TM, rows, LANES), F32)
    return pl.pallas_call(
        _inproj_kernel,
        grid=(B, S // TM),
        in_specs=[row_tile(D), mod_chunk(0), mod_chunk(1)] + [per_layer(a) for a in operands]
        + [pl.BlockSpec(head_sum.shape, lambda b, i: (0, 0))],
        out_specs=[
            col_tile(D_ATT), col_tile(D_ATT), row_tile(D_ATT), row_tile(LANES), col_tile(LANES),
            row_tile(D_ATT), row_tile(D_LRU), row_tile(D_SG),
            stat(N_HEADS), stat(1), stat(N_AUG), stat(N_AUG),
        ],
        out_shape=[
            jax.ShapeDtypeStruct((B, D_ATT, S), BF16),
            jax.ShapeDtypeStruct((B, D_ATT, S), BF16),
            jax.ShapeDtypeStruct((B, S, D_ATT), BF16),
            jax.ShapeDtypeStruct((B, S, LANES), BF16),
            jax.ShapeDtypeStruct((B, LANES, S), BF16),
            jax.ShapeDtypeStruct((B, S, D_ATT), BF16),
            jax.ShapeDtypeStruct((B, S, D_LRU), BF16),
            jax.ShapeDtypeStruct((B, S, D_SG), BF16),
            stat_shape(N_HEADS),
            stat_shape(1),
            stat_shape(N_AUG),
            stat_shape(N_AUG),
        ],
        scratch_shapes=[
            pltpu.VMEM((N_AUG, LANES), F32),
            pltpu.VMEM((TM + SUBLANES, D_LRU), F32),
            pltpu.VMEM((1, D_LRU), F32),
        ],
        compiler_params=_params(("parallel", "arbitrary")),
        name="inproj",
    )(x, mod, mod, *operands, head_sum)


def _attn_kernel(seq_ref, noff_ref, qt_ref, qa_ref, k_ref, ka_ref, vt_ref, g_ref, o_ref,
                 m_ref, acc_ref, rhs_ref, bias_ref, s0_ref, s1_ref, p0_ref, p1_ref,
                 mx0_ref, mx1_ref, al0_ref, al1_ref):
    b = pl.program_id(0)
    i = pl.program_id(1)
    n_tiles = pl.num_programs(1)

    @pl.when((b == 0) & (i == 0))
    def _():
        kk = lax.broadcasted_iota(jnp.int32, (TQ, TQ), 0)
        qq = lax.broadcasted_iota(jnp.int32, (TQ, TQ), 1)
        bias_ref[0] = jnp.zeros((TQ, TQ), F32)
        bias_ref[1] = jnp.where(kk <= qq, 0.0, NEG)

    qa = qa_ref[0]
    row = lax.broadcasted_iota(jnp.int32, (LANES, TQ), 0)
    for p in range(N_PAIRS):
        qt = qt_ref[0, p * LANES:(p + 1) * LANES, :]
        zero = jnp.zeros_like(qt)
        for hh in range(2):
            own = (row >= hh * HEAD_DIM) & (row < (hh + 1) * HEAD_DIM)
            lo = (2 * p + hh) * N_PARTS
            aug = ((row >= lo) & (row < lo + N_PARTS)) | ((row >= N_AUG + lo) & (row < N_AUG + lo + N_PARTS))
            rhs_ref[p, hh] = jnp.concatenate([jnp.where(own, qt, zero), jnp.where(aug, qa, zero)], axis=0)
    ones_rows = jnp.where(lax.broadcasted_iota(jnp.int32, (BF16_ROWS, TQ), 0) == 0, 1.0, 0.0).astype(BF16)
    m_ref[...] = jnp.full(m_ref.shape, NEG, F32)
    acc_ref[...] = jnp.zeros(acc_ref.shape, F32)
    s_bufs = (s0_ref, s1_ref)
    p_bufs = (p0_ref, p1_ref)
    mx_bufs = (mx0_ref, mx1_ref)
    al_bufs = (al0_ref, al1_ref)

    def scores(p, t, slot, off_diagonal):
        rows = pl.ds(pl.multiple_of(t * TQ, TQ), TQ)
        lanes = pl.ds(pl.multiple_of(p * LANES, LANES), LANES)
        lhs = jnp.concatenate([k_ref[0, rows, lanes], ka_ref[0, rows, :]], axis=1)
        for hh in range(2):
            st = jnp.dot(lhs, rhs_ref[p, hh], preferred_element_type=F32)
            if not off_diagonal:
                st = st + bias_ref[(t == i).astype(jnp.int32)]
            s_bufs[slot][hh] = st
            mx_bufs[slot][hh] = jnp.max(st, axis=0, keepdims=True)

    def softmax(p, slot):
        for hh in range(2):
            m_old = m_ref[p, hh]
            m_new = jnp.maximum(m_old, mx_bufs[slot][hh])
            p_bufs[slot][hh] = jnp.exp2(s_bufs[slot][hh] - m_new).astype(BF16)
            al_bufs[slot][hh] = jnp.exp2(m_old - m_new)
            m_ref[p, hh] = m_new

    def values(p, t, slot):
        cols = pl.ds(pl.multiple_of(t * TQ, TQ), TQ)
        for hh in range(2):
            head_rows = pl.ds(pl.multiple_of(p * LANES + hh * HEAD_DIM, HEAD_DIM), HEAD_DIM)
            vaug = jnp.concatenate([vt_ref[0, head_rows, cols], ones_rows], axis=0)
            acc_ref[p, hh] = acc_ref[p, hh] * al_bufs[slot][hh] + jnp.dot(vaug, p_bufs[slot][hh],
                                                                           preferred_element_type=F32)

    def tile(n):
        code = seq_ref[(b * n_tiles + i) * (N_PAIRS * n_tiles) + n]
        return lax.div(code, n_tiles), lax.rem(code, n_tiles)

    def full_step(n, parity, off_diagonal):
        softmax(tile(n - 1)[0], 1 - parity)
        scores(*tile(n), parity, off_diagonal)
        values(*tile(n - 2), parity)

    n_off = noff_ref[b * n_tiles + i]
    scores(*tile(0), 0, False)
    softmax(tile(0)[0], 0)
    scores(*tile(1), 1, False)

    trips = lax.shift_right_logical(jnp.maximum(n_off - 2, 0), 1)

    def body(u, carry):
        full_step(2 + 2 * u, 0, True)
        full_step(3 + 2 * u, 1, True)
        return carry

    lax.fori_loop(0, trips, body, 0)
    n0 = 2 + 2 * trips
    for left in range(2, 6):
        @pl.when(n_off + N_PAIRS - n0 == left)
        def _():
            for j in range(left):
                full_step(n0 + j, j % 2, False)
            softmax(tile(n0 + left - 1)[0], (left - 1) % 2)
            values(*tile(n0 + left - 2), left % 2)
            values(*tile(n0 + left - 1), (left - 1) % 2)

    for p in range(N_PAIRS):
        outs = []
        for hh in range(2):
            a = acc_ref[p, hh]
            outs.append(a[:HEAD_DIM] / a[HEAD_DIM:HEAD_DIM + 1])
        o = jnp.concatenate(outs, axis=0).T
        lanes = slice(p * LANES, (p + 1) * LANES)
        o_ref[0, :, lanes] = (o * _silu(g_ref[0, :, lanes].astype(F32))).astype(BF16)


def _tile_schedule(qs, ks, fs, fe):
    B, n_tiles = qs.shape[:2]
    qmax = qs[:, :, :, 0]
    kmax = ks[:, :, 0, :N_HEADS]
    f_first = fs[:, :, ::N_PARTS, 0]
    f_last = fe[:, :, ::N_PARTS, 0]
    gap = f_first[:, :, None, :] - f_last[:, None, :, :]
    reach = NORM_SLACK * qmax[:, :, None, :] * (kmax[:, None, :, :] + kmax[:, :, None, :])
    negligible = (gap + reach) <= -SKIP_LOG2
    both = jnp.all(negligible.reshape(B, n_tiles, n_tiles, N_PAIRS, 2), axis=-1)
    tiles = jnp.arange(n_tiles)
    both = both & (tiles[None, :] < tiles[:, None])[None, :, :, None]
    first = jnp.min(jnp.where(both, n_tiles, tiles[None, None, :, None]), axis=2)
    first = first.transpose(0, 2, 1)
    visit = (tiles[None, None, None, :] >= first[:, :, :, None]) & (tiles[None, None, None, :] < tiles[None, None, :, None])
    flat = visit.transpose(0, 2, 1, 3).reshape(B, n_tiles, N_PAIRS * n_tiles).astype(jnp.int32)
    pos = jnp.cumsum(flat, axis=2) - flat
    n_off = jnp.sum(flat, axis=2)
    codes = jnp.arange(N_PAIRS * n_tiles)
    slots = jnp.arange(N_PAIRS * n_tiles)
    hit = (flat[:, :, None, :] == 1) & (pos[:, :, None, :] == slots[None, None, :, None])
    seq = jnp.sum(jnp.where(hit, codes[None, None, None, :], 0), axis=3)
    pair = slots[None, None, :] - n_off[:, :, None]
    seq = jnp.where((pair >= 0) & (pair < N_PAIRS), pair * n_tiles + tiles[None, :, None], seq)
    return seq.reshape(-1).astype(jnp.int32), n_off.reshape(-1).astype(jnp.int32)


def _attention(schedule, qt, qa, k, ka, vt, gate):
    B, S, _ = k.shape
    grid_spec = pltpu.PrefetchScalarGridSpec(
        num_scalar_prefetch=2,
        grid=(B, S // TQ),
        in_specs=[
            pl.BlockSpec((1, D_ATT, TQ), lambda b, i, *_: (b, 0, i)),
            pl.BlockSpec((1, LANES, TQ), lambda b, i, *_: (b, 0, i)),
            pl.BlockSpec((1, S, D_ATT), lambda b, i, *_: (b, 0, 0)),
            pl.BlockSpec((1, S, LANES), lambda b, i, *_: (b, 0, 0)),
            pl.BlockSpec((1, D_ATT, S), lambda b, i, *_: (b, 0, 0)),
            pl.BlockSpec((1, TQ, D_ATT), lambda b, i, *_: (b, i, 0)),
        ],
        out_specs=pl.BlockSpec((1, TQ, D_ATT), lambda b, i, *_: (b, i, 0)),
        scratch_shapes=[
            pltpu.VMEM((N_PAIRS, 2, 1, TQ), F32),
            pltpu.VMEM((N_PAIRS, 2, V_ROWS, TQ), F32),
            pltpu.VMEM((N_PAIRS, 2, 2 * LANES, TQ), BF16),
            pltpu.VMEM((2, TQ, TQ), F32),
            pltpu.VMEM((2, TQ, TQ), F32),
            pltpu.VMEM((2, TQ, TQ), F32),
            pltpu.VMEM((2, TQ, TQ), BF16),
            pltpu.VMEM((2, TQ, TQ), BF16),
            pltpu.VMEM((2, 1, TQ), F32),
            pltpu.VMEM((2, 1, TQ), F32),
            pltpu.VMEM((2, 1, TQ), F32),
            pltpu.VMEM((2, 1, TQ), F32),
        ],
    )
    return pl.pallas_call(
        _attn_kernel,
        grid_spec=grid_spec,
        out_shape=jax.ShapeDtypeStruct((B, S, D_ATT), BF16),
        compiler_params=_params(("arbitrary", "arbitrary")),
        name="fox_attention",
    )(*schedule, qt, qa, k, ka, vt, gate)


def _outproj_kernel(ya_ref, yl_ref, ys_ref, x_ref, gate_ref, pg_ref, w_ref, o_ref):
    y = jnp.dot(ya_ref[0], w_ref[0, 0:D_ATT, :], preferred_element_type=F32)
    y = y + jnp.dot(yl_ref[0], w_ref[0, D_ATT:D_ATT + D_LRU, :], preferred_element_type=F32)
    y = y + jnp.dot(ys_ref[0], w_ref[0, D_ATT + D_LRU:, :], preferred_element_type=F32)
    ms = jnp.mean(y * y, axis=-1, keepdims=True)
    yn = (y * lax.rsqrt(ms + EPS)) * pg_ref[0]
    o_ref[0] = x_ref[0] + _mod_row(gate_ref) * yn


def _outproj(layer, ya, yl, ys, x, mod, prep):
    B, S, D = x.shape
    per_layer = lambda a: pl.BlockSpec((1,) + a.shape[1:], lambda b, i: (layer,) + (0,) * (a.ndim - 1))
    row_tile = lambda n: pl.BlockSpec((1, TO, n), lambda b, i: (b, i, 0))
    return pl.pallas_call(
        _outproj_kernel,
        grid=(B, S // TO),
        in_specs=[
            row_tile(D_ATT), row_tile(D_LRU), row_tile(D_SG), row_tile(D),
            pl.BlockSpec((1, 1, D // TN_ADA, 1, TN_ADA), lambda b, i: (layer, b, 2, 0, 0)),
            per_layer(prep["post_g"]),
            per_layer(prep["w_out"]),
        ],
        out_specs=row_tile(D),
        out_shape=jax.ShapeDtypeStruct((B, S, D), F32),
        compiler_params=_params(("parallel", "parallel")),
        name="outproj",
    )(ya, yl, ys, x, mod, prep["post_g"], prep["w_out"])


def _block_diag(w):
    L, G, n, _ = w.shape
    eye = jnp.eye(G, dtype=w.dtype)
    return (w[:, :, :, None, :] * eye[None, :, None, :, None]).reshape(L, G * n, G * n)


def _prepare(pre_g, post_g, w_in, b_f, conv_w, conv_b, lru_wa, lru_ba, lru_wx, lru_bx, lru_lambda,
             sg_ln_g, sg_ln_b, sg_w, sg_b, w_out):
    L, D, _ = w_in.shape
    cuts = [0]
    for n in IN_SIZES:
        cuts.append(cuts[-1] + n)
    wq, wk, wv, wf, wga, wxl, wgl, wsu, wsv, wgs = (w_in[:, :, cuts[j]:cuts[j + 1]] for j in range(10))
    row = lambda a: a[:, None, :]
    wf3 = jnp.repeat(wf, N_PARTS, axis=2)
    pad = jnp.zeros((L, D, T_ROWS - 2 * D_ATT - N_AUG), F32)
    return dict(
        pre_g=row(pre_g), post_g=row(post_g),
        we=jnp.concatenate([wxl, wsv], axis=2).astype(BF16),
        wg=jnp.concatenate([wga, wgl, wsu, wgs], axis=2).astype(BF16),
        wk=wk.astype(BF16),
        wt=jnp.concatenate([wq, wv, wf3, pad], axis=2).transpose(0, 2, 1).astype(BF16),
        bf=jnp.broadcast_to(jnp.repeat(b_f, N_PARTS, axis=1)[:, :, None], (L, N_AUG, LANES)),
        conv_w=conv_w, conv_b=row(conv_b),
        wa=_block_diag(lru_wa).astype(BF16), ba=row(lru_ba),
        wx=_block_diag(lru_wx).astype(BF16), bx=row(lru_bx), lam=row(lru_lambda),
        ln_g=row(sg_ln_g), ln_b=row(sg_ln_b),
        ws=sg_w.reshape(L, N_SG_GROUPS * SG_CHUNK, SG_CHUNK).astype(BF16),
        sg_bias=jnp.repeat(sg_b.transpose(0, 2, 1), HEAD_DIM, axis=2),
        w_out=w_out.astype(BF16),
        head_sum=(jnp.arange(D_ATT)[:, None] // HEAD_DIM == jnp.arange(LANES)[None, :]).astype(BF16),
    )


def kernel(x, c, ada_w, ada_b, pre_g, post_g, w_in, b_f, conv_w, conv_b, lru_wa, lru_ba, lru_wx,
           lru_bx, lru_lambda, sg_ln_g, sg_ln_b, sg_w, sg_b, w_out):
    mod = _ada(c, ada_w, ada_b)
    prep = _prepare(pre_g, post_g, w_in, b_f, conv_w, conv_b, lru_wa, lru_ba, lru_wx, lru_bx,
                    lru_lambda, sg_ln_g, sg_ln_b, sg_w, sg_b, w_out)
    for layer in range(ada_w.shape[0]):
        qt, vt, k, ka, qa, ga, yl, ys, qs, ks, fs, fe = _inproj(layer, x, mod, prep)
        ya = _attention(_tile_schedule(qs, ks, fs, fe), qt, qa, k, ka, vt, ga)
        x = _outproj(layer, ya, yl, ys, x, mod, prep)
    return x
```

```python
import math

import jax
import jax.numpy as jnp
from jax import lax
from jax.experimental import pallas as pl
from jax.experimental.pallas import tpu as pltpu

D_MODEL = 1024
HEAD_DIM = 64
D_ATT = 512
D_LRU = 256
D_SG = 256
N_HEADS = D_ATT // HEAD_DIM
N_PAIRS = N_HEADS // 2
N_LRU_BLOCKS = D_LRU // HEAD_DIM
N_SG_GROUPS = D_SG // HEAD_DIM
SG_CHUNK = 128
CONV_WIDTH = 4
LRU_C = 8.0
EPS = 1e-6
IN_SIZES = (D_ATT, D_ATT, D_ATT, N_HEADS, D_ATT, D_LRU, D_LRU, D_SG, D_SG, D_SG)

LANES = 128
SUBLANES = 8
BF16_ROWS = 16
VMEM_LIMIT = 56 * 1024 * 1024

TM = 512
TO = 1024
TQ = 512
TN_ADA = 512
NEG = -1e30

N_PARTS = 3
N_AUG = N_HEADS * N_PARTS
T_ROWS = 2 * D_ATT + 2 * BF16_ROWS
LOG2E = 1.4426950408889634
Q_SCALE = LOG2E / math.sqrt(HEAD_DIM)
V_ROWS = HEAD_DIM + BF16_ROWS
SKIP_LOG2 = 160.0
NORM_SLACK = 1.02
GELU_C0 = math.sqrt(2.0 / math.pi)
GELU_C1 = GELU_C0 * 0.044715

F32 = jnp.float32
BF16 = jnp.bfloat16


def _sigmoid(x):
    return 0.5 * jnp.tanh(0.5 * x) + 0.5


def _silu(x):
    hx = 0.5 * x
    return hx + hx * jnp.tanh(hx)


def _gelu_tanh(x):
    hx = 0.5 * x
    return hx + hx * jnp.tanh(x * (GELU_C0 + GELU_C1 * (x * x)))


def _log_sigmoid(x):
    return jnp.minimum(x, 0.0) - jnp.log1p(jnp.exp(-jnp.abs(x)))


def _split3(x, sel):
    hi = x.astype(BF16).astype(F32)
    r1 = x - hi
    mid = r1.astype(BF16).astype(F32)
    lo = r1 - mid
    return jnp.where(sel == 0, hi, jnp.where(sel == 1, mid, lo))


def _mod_row(ref):
    return jnp.concatenate([ref[0, 0, j] for j in range(ref.shape[2])], axis=1)


def _params(sem):
    return pltpu.CompilerParams(dimension_semantics=sem, vmem_limit_bytes=VMEM_LIMIT)


def _ada_kernel(ct_ref, w_ref, b_ref, o_ref):
    ct = ct_ref[...]
    sc = _silu(ct)
    w = w_ref[0]
    bias = b_ref[0]
    for b in range(ct.shape[1]):
        col = sc[:, b:b + 1]
        o_ref[0, b, 0] = jnp.sum(col * w, axis=0, keepdims=True) + bias


def _ada(c, ada_w, ada_b):
    L, D, N = ada_w.shape
    B = c.shape[0]
    return pl.pallas_call(
        _ada_kernel,
        grid=(L, N // TN_ADA),
        in_specs=[
            pl.BlockSpec((D, B), lambda l, n: (0, 0)),
            pl.BlockSpec((1, D, TN_ADA), lambda l, n: (l, 0, n)),
            pl.BlockSpec((1, 1, TN_ADA), lambda l, n: (l, 0, n)),
        ],
        out_specs=pl.BlockSpec((1, B, 1, 1, TN_ADA), lambda l, n: (l, 0, n, 0, 0)),
        out_shape=jax.ShapeDtypeStruct((L, B, N // TN_ADA, 1, TN_ADA), F32),
        compiler_params=_params(("parallel", "parallel")),
        name="ada_mod",
    )(c.T, ada_w, ada_b.reshape(L, 1, N))


def _lru_mixer(g_lru, r_pre, i_pre, lam, h_ref, xc):
    r = _sigmoid(r_pre)
    ig = _sigmoid(i_pre)
    nlam = -lam
    softplus = jnp.maximum(nlam, 0.0) + jnp.log1p(jnp.exp(-jnp.abs(nlam)))
    log_a = (-LRU_C * r) * softplus
    a = jnp.exp(log_a)
    bt = jnp.sqrt(jnp.tanh(-log_a) * (1.0 + a * a)) * (ig * xc)

    row8 = lax.broadcasted_iota(jnp.int32, (TM, D_LRU), 0) & (SUBLANES - 1)
    d = 1
    while d < SUBLANES:
        valid = row8 >= d
        a_s = jnp.where(valid, pltpu.roll(a, d, axis=0), 1.0)
        b_s = jnp.where(valid, pltpu.roll(bt, d, axis=0), 0.0)
        bt = bt + a * b_s
        a = a * a_s
        d *= 2

    h_prev = h_ref[...]
    sg = _silu(g_lru)
    outs = []
    for g in range(TM // SUBLANES):
        lo = g * SUBLANES
        hg = bt[lo:lo + SUBLANES] + a[lo:lo + SUBLANES] * h_prev
        h_prev = hg[SUBLANES - 1:SUBLANES, :]
        outs.append(hg * sg[lo:lo + SUBLANES])
    h_ref[...] = h_prev
    return jnp.concatenate(outs, axis=0)


def _inproj_kernel(x_ref, shift_ref, scale_ref, g_ref, we_ref, wg_ref, wk_ref, wt_ref, bf_ref,
                   cw_ref, cb_ref, wa_ref, ba_ref, wx_ref, bx_ref, lam_ref,
                   lng_ref, lnb_ref, ws_ref, sb_ref, he_ref,
                   qt_ref, vt_ref, k_ref, ka_ref, qa_ref, ga_ref, yl_ref, ys_ref,
                   qs_ref, ks_ref, fs_ref, fe_ref,
                   fc_ref, xbuf_ref, h_ref):
    i = pl.program_id(1)

    @pl.when(i == 0)
    def _():
        fc_ref[...] = jnp.zeros(fc_ref.shape, F32)
        xbuf_ref[0:SUBLANES, :] = jnp.zeros((SUBLANES, D_LRU), F32)
        h_ref[...] = jnp.zeros(h_ref.shape, F32)

    x = x_ref[0]
    ms = jnp.mean(x * x, axis=-1, keepdims=True)
    gs = g_ref[0] * (1.0 + _mod_row(scale_ref))
    h = (x * lax.rsqrt(ms + EPS)) * gs + _mod_row(shift_ref)
    hb = h.astype(BF16)

    early = jnp.dot(hb, we_ref[0], preferred_element_type=F32)
    x_lru = early[:, :D_LRU]
    sg_v = early[:, D_LRU:]
    tr = lax.dot_general(wt_ref[0], hb, (((1,), (1,)), ((), ())), preferred_element_type=F32)
    tq = tr[:D_ATT] * Q_SCALE
    qt_ref[0] = tq.astype(BF16)
    vt_ref[0] = tr[D_ATT:2 * D_ATT].astype(BF16)
    fl = tr[2 * D_ATT:2 * D_ATT + N_AUG]
    kb = jnp.dot(hb, wk_ref[0], preferred_element_type=F32).astype(BF16)
    k_ref[0] = kb
    qn2 = jnp.sum((tq * tq).reshape(N_HEADS, HEAD_DIM, TM), axis=1)
    qs_ref[0, 0] = jnp.broadcast_to(jnp.sqrt(jnp.max(qn2, axis=1, keepdims=True)), (N_HEADS, LANES))

    v = _gelu_tanh(sg_v)
    mu = jnp.mean(v, axis=-1, keepdims=True)
    var = jnp.mean(jnp.square(v - mu), axis=-1, keepdims=True)
    vnb = (((v - mu) * lax.rsqrt(var + EPS)) * lng_ref[0] + lnb_ref[0]).astype(BF16)
    w_shape = ws_ref.shape[1:]
    t_idx = lax.broadcasted_iota(jnp.int32, w_shape, 0) & (SG_CHUNK - 1)
    s_idx = lax.broadcasted_iota(jnp.int32, w_shape, 1)
    ws = jnp.where(s_idx <= t_idx, ws_ref[0], jnp.zeros(w_shape, BF16))
    grp = lax.broadcasted_iota(jnp.int32, (SG_CHUNK, D_SG), 1) // HEAD_DIM
    zs = []
    for c in range(TM // SG_CHUNK):
        zz = jnp.dot(ws, vnb[c * SG_CHUNK:(c + 1) * SG_CHUNK], preferred_element_type=F32)
        z = zz[0:SG_CHUNK]
        for gi in range(1, N_SG_GROUPS):
            z = jnp.where(grp == gi, zz[gi * SG_CHUNK:(gi + 1) * SG_CHUNK], z)
        zs.append(z + sb_ref[0])
    z_all = jnp.concatenate(zs, axis=0)

    xbuf_ref[SUBLANES:SUBLANES + TM, :] = x_lru
    xc = cb_ref[0]
    for kk in range(CONV_WIDTH):
        off = SUBLANES - (CONV_WIDTH - 1) + kk
        xc = xc + xbuf_ref[off:off + TM, :] * cw_ref[0, kk:kk + 1, :]
    xbuf_ref[0:SUBLANES, :] = x_lru[TM - SUBLANES:, :]
    xcb = xc.astype(BF16)
    r_pre = jnp.dot(xcb, wa_ref[0], preferred_element_type=F32) + ba_ref[0]
    i_pre = jnp.dot(xcb, wx_ref[0], preferred_element_type=F32) + bx_ref[0]

    gates = jnp.dot(hb, wg_ref[0], preferred_element_type=F32)
    kn2 = jnp.dot(kb * kb, he_ref[...], preferred_element_type=F32)
    ks_ref[0, 0] = jnp.sqrt(jnp.max(kn2, axis=0, keepdims=True))

    ga_ref[0] = gates[:, :D_ATT].astype(BF16)
    g_lru = gates[:, D_ATT:D_ATT + D_LRU]
    sg_u = gates[:, D_ATT + D_LRU:D_ATT + D_LRU + D_SG]
    g_sg = gates[:, D_ATT + D_LRU + D_SG:]
    ys_ref[0] = ((_gelu_tanh(sg_u) * z_all) * _silu(g_sg)).astype(BF16)
    yl_ref[0] = _lru_mixer(g_lru, r_pre, i_pre, lam_ref[0], h_ref, xc).astype(BF16)

    reps = TM // LANES
    ls = _log_sigmoid(fl + jnp.concatenate([bf_ref[0]] * reps, axis=1))
    lane = lax.broadcasted_iota(jnp.int32, (N_AUG, TM), 1)
    d = 1
    while d < TM:
        ls = ls + jnp.where(lane >= d, pltpu.roll(ls, d, axis=1), 0.0)
        d *= 2
    f = ls + jnp.concatenate([fc_ref[...]] * reps, axis=1)
    fc_ref[...] = jnp.broadcast_to(f[:, TM - 1:TM], (N_AUG, LANES))
    row = lax.broadcasted_iota(jnp.int32, (N_AUG, TM), 0)
    f2 = f * LOG2E
    fs_ref[0, 0] = jnp.broadcast_to(f2[:, 0:1], (N_AUG, LANES))
    fe_ref[0, 0] = jnp.broadcast_to(f2[:, TM - 1:TM], (N_AUG, LANES))
    parts = _split3(f2, row % N_PARTS)
    ones = jnp.ones((N_AUG, TM), F32)
    zeros = jnp.zeros((LANES - 2 * N_AUG, TM), F32)
    qa_ref[0] = jnp.concatenate([ones, parts, zeros], axis=0).astype(BF16)
    ka_ref[0] = jnp.concatenate([-parts, ones, zeros], axis=0).T.astype(BF16)


def _inproj(layer, x, mod, prep):
    B, S, D = x.shape
    per_layer = lambda a: pl.BlockSpec((1,) + a.shape[1:], lambda b, i: (layer,) + (0,) * (a.ndim - 1))
    mod_chunk = lambda j: pl.BlockSpec((1, 1, D // TN_ADA, 1, TN_ADA), lambda b, i: (layer, b, j, 0, 0))
    row_tile = lambda n: pl.BlockSpec((1, TM, n), lambda b, i: (b, i, 0))
    col_tile = lambda n: pl.BlockSpec((1, n, TM), lambda b, i: (b, 0, i))
    names = ("pre_g", "we", "wg", "wk", "wt", "bf", "conv_w", "conv_b", "wa", "ba", "wx", "bx", "lam",
             "ln_g", "ln_b", "ws", "sg_bias")
    operands = [prep[n] for n in names]
    head_sum = prep["head_sum"]
    stat = lambda rows: pl.BlockSpec((1, 1, rows, LANES), lambda b, i: (b, i, 0, 0))
    stat_shape = lambda rows: jax.ShapeDtypeStruct((B, S // TM, rows, LANES), F32)
    return pl.pallas_call(
        _inproj_kernel,
        grid=(B, S // TM),
        in_specs=[row_tile(D), mod_chunk(0), mod_chunk(1)] + [per_layer(a) for a in operands]
        + [pl.BlockSpec(head_sum.shape, lambda b, i: (0, 0))],
        out_specs=[
            col_tile(D_ATT), col_tile(D_ATT), row_tile(D_ATT), row_tile(LANES), col_tile(LANES),
            row_tile(D_ATT), row_tile(D_LRU), row_tile(D_SG),
            stat(N_HEADS), stat(1), stat(N_AUG), stat(N_AUG),
        ],
        out_shape=[
            jax.ShapeDtypeStruct((B, D_ATT, S), BF16),
            jax.ShapeDtypeStruct((B, D_ATT, S), BF16),
            jax.ShapeDtypeStruct((B, S, D_ATT), BF16),
            jax.ShapeDtypeStruct((B, S, LANES), BF16),
            jax.ShapeDtypeStruct((B, LANES, S), BF16),
            jax.ShapeDtypeStruct((B, S, D_ATT), BF16),
            jax.ShapeDtypeStruct((B, S, D_LRU), BF16),
            jax.ShapeDtypeStruct((B, S, D_SG), BF16),
            stat_shape(N_HEADS),
            stat_shape(1),
            stat_shape(N_AUG),
            stat_shape(N_AUG),
        ],
        scratch_shapes=[
            pltpu.VMEM((N_AUG, LANES), F32),
            pltpu.VMEM((TM + SUBLANES, D_LRU), F32),
            pltpu.VMEM((1, D_LRU), F32),
        ],
        compiler_params=_params(("parallel", "arbitrary")),
        name="inproj",
    )(x, mod, mod, *operands, head_sum)


def _attn_kernel(seq_ref, noff_ref, qt_ref, qa_ref, k_ref, ka_ref, vt_ref, g_ref, o_ref,
                 m_ref, acc_ref, rhs_ref, bias_ref, s0_ref, s1_ref, p0_ref, p1_ref,
                 mx0_ref, mx1_ref, al0_ref, al1_ref):
    b = pl.program_id(0)
    i = pl.program_id(1)
    n_tiles = pl.num_programs(1)

    @pl.when((b == 0) & (i == 0))
    def _():
        kk = lax.broadcasted_iota(jnp.int32, (TQ, TQ), 0)
        qq = lax.broadcasted_iota(jnp.int32, (TQ, TQ), 1)
        bias_ref[0] = jnp.zeros((TQ, TQ), F32)
        bias_ref[1] = jnp.where(kk <= qq, 0.0, NEG)

    qa = qa_ref[0]
    row = lax.broadcasted_iota(jnp.int32, (LANES, TQ), 0)
    for p in range(N_PAIRS):
        qt = qt_ref[0, p * LANES:(p + 1) * LANES, :]
        zero = jnp.zeros_like(qt)
        for hh in range(2):
            own = (row >= hh * HEAD_DIM) & (row < (hh + 1) * HEAD_DIM)
            lo = (2 * p + hh) * N_PARTS
            aug = ((row >= lo) & (row < lo + N_PARTS)) | ((row >= N_AUG + lo) & (row < N_AUG + lo + N_PARTS))
            rhs_ref[2 * p + hh] = jnp.concatenate([jnp.where(own, qt, zero), jnp.where(aug, qa, zero)], axis=0)
    rhs_ref[N_HEADS] = jnp.zeros(rhs_ref.shape[1:], BF16)
    ones_rows = jnp.where(lax.broadcasted_iota(jnp.int32, (BF16_ROWS, TQ), 0) == 0, 1.0, 0.0).astype(BF16)
    m_ref[...] = jnp.full(m_ref.shape, NEG, F32)
    acc_ref[...] = jnp.zeros(acc_ref.shape, F32)
    s_bufs = (s0_ref, s1_ref)
    p_bufs = (p0_ref, p1_ref)
    mx_bufs = (mx0_ref, mx1_ref)
    al_bufs = (al0_ref, al1_ref)

    def item(n, hh):
        code = seq_ref[(b * n_tiles + i) * (N_HEADS * n_tiles) + 2 * n + hh]
        return lax.div(code, n_tiles), lax.rem(code, n_tiles)

    def scores(n, slot, off_diagonal):
        for hh in range(2):
            h, t = item(n, hh)
            rows = pl.ds(pl.multiple_of(t * TQ, TQ), TQ)
            pair = lax.shift_right_logical(jnp.minimum(h, N_HEADS - 1), 1)
            lanes = pl.ds(pl.multiple_of(pair * LANES, LANES), LANES)
            lhs = jnp.concatenate([k_ref[0, rows, lanes], ka_ref[0, rows, :]], axis=1)
            st = jnp.dot(lhs, rhs_ref[h], preferred_element_type=F32)
            if not off_diagonal:
                st = st + bias_ref[(t == i).astype(jnp.int32)]
            s_bufs[slot][hh] = st
            mx_bufs[slot][hh] = jnp.max(st, axis=0, keepdims=True)

    def softmax(n, slot):
        for hh in range(2):
            h, _ = item(n, hh)
            m_old = m_ref[h]
            m_new = jnp.maximum(m_old, mx_bufs[slot][hh])
            p_bufs[slot][hh] = jnp.exp2(s_bufs[slot][hh] - m_new).astype(BF16)
            al_bufs[slot][hh] = jnp.exp2(m_old - m_new)
            m_ref[h] = m_new

    def values(n, slot):
        for hh in range(2):
            h, t = item(n, hh)
            cols = pl.ds(pl.multiple_of(t * TQ, TQ), TQ)
            head_rows = pl.ds(pl.multiple_of(jnp.minimum(h, N_HEADS - 1) * HEAD_DIM, HEAD_DIM), HEAD_DIM)
            vaug = jnp.concatenate([vt_ref[0, head_rows, cols], ones_rows], axis=0)
            acc_ref[h] = acc_ref[h] * al_bufs[slot][hh] + jnp.dot(vaug, p_bufs[slot][hh],
                                                                  preferred_element_type=F32)

    def full_step(n, parity, off_diagonal):
        softmax(n - 1, 1 - parity)
        scores(n, parity, off_diagonal)
        values(n - 2, parity)

    n_off = noff_ref[b * n_tiles + i]
    scores(0, 0, False)
    softmax(0, 0)
    scores(1, 1, False)

    trips = lax.shift_right_logical(jnp.maximum(n_off - 2, 0), 1)

    def body(u, carry):
        full_step(2 + 2 * u, 0, True)
        full_step(3 + 2 * u, 1, True)
        return carry

    lax.fori_loop(0, trips, body, 0)
    n0 = 2 + 2 * trips
    for left in range(2, 6):
        @pl.when(n_off + N_PAIRS - n0 == left)
        def _():
            for j in range(left):
                full_step(n0 + j, j % 2, False)
            softmax(n0 + left - 1, (left - 1) % 2)
            values(n0 + left - 2, left % 2)
            values(n0 + left - 1, (left - 1) % 2)

    for p in range(N_PAIRS):
        outs = []
        for hh in range(2):
            a = acc_ref[2 * p + hh]
            outs.append(a[:HEAD_DIM] / a[HEAD_DIM:HEAD_DIM + 1])
        o = jnp.concatenate(outs, axis=0).T
        lanes = slice(p * LANES, (p + 1) * LANES)
        o_ref[0, :, lanes] = (o * _silu(g_ref[0, :, lanes].astype(F32))).astype(BF16)


def _tile_schedule(qs, ks, fs, fe):
    B, n_tiles = qs.shape[:2]
    qmax = qs[:, :, :, 0]
    kmax = ks[:, :, 0, :N_HEADS]
    f_first = fs[:, :, ::N_PARTS, 0]
    f_last = fe[:, :, ::N_PARTS, 0]
    gap = f_first[:, :, None, :] - f_last[:, None, :, :]
    reach = NORM_SLACK * qmax[:, :, None, :] * (kmax[:, None, :, :] + kmax[:, :, None, :])
    negligible = (gap + reach) <= -SKIP_LOG2
    tiles = jnp.arange(n_tiles)
    skip = negligible & (tiles[None, :] < tiles[:, None])[None, :, :, None]
    first = jnp.min(jnp.where(skip, n_tiles, tiles[None, None, :, None]), axis=2)
    first = first.transpose(0, 2, 1)
    visit = (tiles[None, None, None, :] >= first[:, :, :, None]) & (tiles[None, None, None, :] < tiles[None, None, :, None])
    flat = visit.transpose(0, 2, 1, 3).reshape(B, n_tiles, N_HEADS * n_tiles).astype(jnp.int32)
    pos = jnp.cumsum(flat, axis=2) - flat
    n_items = jnp.sum(flat, axis=2)
    n_off = (n_items + 1) // 2
    codes = jnp.arange(N_HEADS * n_tiles)
    slots = jnp.arange(N_HEADS * n_tiles)
    hit = (flat[:, :, None, :] == 1) & (pos[:, :, None, :] == slots[None, None, :, None])
    seq = jnp.sum(jnp.where(hit, codes[None, None, None, :], 0), axis=3)
    seq = jnp.where(slots[None, None, :] == n_items[:, :, None], N_HEADS * n_tiles, seq)
    head = slots[None, None, :] - 2 * n_off[:, :, None]
    seq = jnp.where((head >= 0) & (head < N_HEADS), head * n_tiles + tiles[None, :, None], seq)
    return seq.reshape(-1).astype(jnp.int32), n_off.reshape(-1).astype(jnp.int32)


def _attention(schedule, qt, qa, k, ka, vt, gate):
    B, S, _ = k.shape
    grid_spec = pltpu.PrefetchScalarGridSpec(
        num_scalar_prefetch=2,
        grid=(B, S // TQ),
        in_specs=[
            pl.BlockSpec((1, D_ATT, TQ), lambda b, i, *_: (b, 0, i)),
            pl.BlockSpec((1, LANES, TQ), lambda b, i, *_: (b, 0, i)),
            pl.BlockSpec((1, S, D_ATT), lambda b, i, *_: (b, 0, 0)),
            pl.BlockSpec((1, S, LANES), lambda b, i, *_: (b, 0, 0)),
            pl.BlockSpec((1, D_ATT, S), lambda b, i, *_: (b, 0, 0)),
            pl.BlockSpec((1, TQ, D_ATT), lambda b, i, *_: (b, i, 0)),
        ],
        out_specs=pl.BlockSpec((1, TQ, D_ATT), lambda b, i, *_: (b, i, 0)),
        scratch_shapes=[
            pltpu.VMEM((N_HEADS + 1, 1, TQ), F32),
            pltpu.VMEM((N_HEADS + 1, V_ROWS, TQ), F32),
            pltpu.VMEM((N_HEADS + 1, 2 * LANES, TQ), BF16),
            pltpu.VMEM((2, TQ, TQ), F32),
            pltpu.VMEM((2, TQ, TQ), F32),
            pltpu.VMEM((2, TQ, TQ), F32),
            pltpu.VMEM((2, TQ, TQ), BF16),
            pltpu.VMEM((2, TQ, TQ), BF16),
            pltpu.VMEM((2, 1, TQ), F32),
            pltpu.VMEM((2, 1, TQ), F32),
            pltpu.VMEM((2, 1, TQ), F32),
            pltpu.VMEM((2, 1, TQ), F32),
        ],
    )
    return pl.pallas_call(
        _attn_kernel,
        grid_spec=grid_spec,
        out_shape=jax.ShapeDtypeStruct((B, S, D_ATT), BF16),
        compiler_params=_params(("arbitrary", "arbitrary")),
        name="fox_attention",
    )(*schedule, qt, qa, k, ka, vt, gate)


def _outproj_kernel(ya_ref, yl_ref, ys_ref, x_ref, gate_ref, pg_ref, w_ref, o_ref):
    y = jnp.dot(ya_ref[0], w_ref[0, 0:D_ATT, :], preferred_element_type=F32)
    y = y + jnp.dot(yl_ref[0], w_ref[0, D_ATT:D_ATT + D_LRU, :], preferred_element_type=F32)
    y = y + jnp.dot(ys_ref[0], w_ref[0, D_ATT + D_LRU:, :], preferred_element_type=F32)
    ms = jnp.mean(y * y, axis=-1, keepdims=True)
    yn = (y * lax.rsqrt(ms + EPS)) * pg_ref[0]
    o_ref[0] = x_ref[0] + _mod_row(gate_ref) * yn


def _outproj(layer, ya, yl, ys, x, mod, prep):
    B, S, D = x.shape
    per_layer = lambda a: pl.BlockSpec((1,) + a.shape[1:], lambda b, i: (layer,) + (0,) * (a.ndim - 1))
    row_tile = lambda n: pl.BlockSpec((1, TO, n), lambda b, i: (b, i, 0))
    return pl.pallas_call(
        _outproj_kernel,
        grid=(B, S // TO),
        in_specs=[
            row_tile(D_ATT), row_tile(D_LRU), row_tile(D_SG), row_tile(D),
            pl.BlockSpec((1, 1, D // TN_ADA, 1, TN_ADA), lambda b, i: (layer, b, 2, 0, 0)),
            per_layer(prep["post_g"]),
            per_layer(prep["w_out"]),
        ],
        out_specs=row_tile(D),
        out_shape=jax.ShapeDtypeStruct((B, S, D), F32),
        compiler_params=_params(("parallel", "parallel")),
        name="outproj",
    )(ya, yl, ys, x, mod, prep["post_g"], prep["w_out"])


def _block_diag(w):
    L, G, n, _ = w.shape
    eye = jnp.eye(G, dtype=w.dtype)
    return (w[:, :, :, None, :] * eye[None, :, None, :, None]).reshape(L, G * n, G * n)


def _prepare(pre_g, post_g, w_in, b_f, conv_w, conv_b, lru_wa, lru_ba, lru_wx, lru_bx, lru_lambda,
             sg_ln_g, sg_ln_b, sg_w, sg_b, w_out):
    L, D, _ = w_in.shape
    cuts = [0]
    for n in IN_SIZES:
        cuts.append(cuts[-1] + n)
    wq, wk, wv, wf, wga, wxl, wgl, wsu, wsv, wgs = (w_in[:, :, cuts[j]:cuts[j + 1]] for j in range(10))
    row = lambda a: a[:, None, :]
    wf3 = jnp.repeat(wf, N_PARTS, axis=2)
    pad = jnp.zeros((L, D, T_ROWS - 2 * D_ATT - N_AUG), F32)
    return dict(
        pre_g=row(pre_g), post_g=row(post_g),
        we=jnp.concatenate([wxl, wsv], axis=2).astype(BF16),
        wg=jnp.concatenate([wga, wgl, wsu, wgs], axis=2).astype(BF16),
        wk=wk.astype(BF16),
        wt=jnp.concatenate([wq, wv, wf3, pad], axis=2).transpose(0, 2, 1).astype(BF16),
        bf=jnp.broadcast_to(jnp.repeat(b_f, N_PARTS, axis=1)[:, :, None], (L, N_AUG, LANES)),
        conv_w=conv_w, conv_b=row(conv_b),
        wa=_block_diag(lru_wa).astype(BF16), ba=row(lru_ba),
        wx=_block_diag(lru_wx).astype(BF16), bx=row(lru_bx), lam=row(lru_lambda),
        ln_g=row(sg_ln_g), ln_b=row(sg_ln_b),
        ws=sg_w.reshape(L, N_SG_GROUPS * SG_CHUNK, SG_CHUNK).astype(BF16),
        sg_bias=jnp.repeat(sg_b.transpose(0, 2, 1), HEAD_DIM, axis=2),
        w_out=w_out.astype(BF16),
        head_sum=(jnp.arange(D_ATT)[:, None] // HEAD_DIM == jnp.arange(LANES)[None, :]).astype(BF16),
    )


def kernel(x, c, ada_w, ada_b, pre_g, post_g, w_in, b_f, conv_w, conv_b, lru_wa, lru_ba, lru_wx,
           lru_bx, lru_lambda, sg_ln_g, sg_ln_b, sg_w, sg_b, w_out):
    mod = _ada(c, ada_w, ada_b)
    prep = _prepare(pre_g, post_g, w_in, b_f, conv_w, conv_b, lru_wa, lru_ba, lru_wx, lru_bx,
                    lru_lambda, sg_ln_g, sg_ln_b, sg_w, sg_b, w_out)
    for layer in range(ada_w.shape[0]):
        qt, vt, k, ka, qa, ga, yl, ys, qs, ks, fs, fe = _inproj(layer, x, mod, prep)
        ya = _attention(_tile_schedule(qs, ks, fs, fe), qt, qa, k, ka, vt, ga)
        x = _outproj(layer, ya, yl, ys, x, mod, prep)
    return x
```

```python
import math

import jax
import jax.numpy as jnp
from jax import lax
from jax.experimental import pallas as pl
from jax.experimental.pallas import tpu as pltpu

D_MODEL = 1024
HEAD_DIM = 64
D_ATT = 512
D_LRU = 256
D_SG = 256
N_HEADS = D_ATT // HEAD_DIM
N_PAIRS = N_HEADS // 2
N_LRU_BLOCKS = D_LRU // HEAD_DIM
N_SG_GROUPS = D_SG // HEAD_DIM
SG_CHUNK = 128
CONV_WIDTH = 4
LRU_C = 8.0
EPS = 1e-6
IN_SIZES = (D_ATT, D_ATT, D_ATT, N_HEADS, D_ATT, D_LRU, D_LRU, D_SG, D_SG, D_SG)

LANES = 128
SUBLANES = 8
BF16_ROWS = 16
VMEM_LIMIT = 56 * 1024 * 1024

TM = 512
TO = 1024
TQ = 512
TN_ADA = 512
NEG = -1e30
assert TM == TQ

N_PARTS = 3
N_AUG = N_HEADS * N_PARTS
T_ROWS = 2 * D_ATT + 2 * BF16_ROWS
LOG2E = 1.4426950408889634
Q_SCALE = LOG2E / math.sqrt(HEAD_DIM)
V_ROWS = HEAD_DIM + BF16_ROWS
SKIP_LOG2 = 160.0
NORM_SLACK = 1.02
GELU_C0 = math.sqrt(2.0 / math.pi)
GELU_C1 = GELU_C0 * 0.044715

F32 = jnp.float32
BF16 = jnp.bfloat16


def _sigmoid(x):
    return 0.5 * jnp.tanh(0.5 * x) + 0.5


def _silu(x):
    hx = 0.5 * x
    return hx + hx * jnp.tanh(hx)


def _gelu_tanh(x):
    hx = 0.5 * x
    return hx + hx * jnp.tanh(x * (GELU_C0 + GELU_C1 * (x * x)))


def _log_sigmoid(x):
    return jnp.minimum(x, 0.0) - jnp.log1p(jnp.exp(-jnp.abs(x)))


def _split3(x, sel):
    hi = x.astype(BF16).astype(F32)
    r1 = x - hi
    mid = r1.astype(BF16).astype(F32)
    lo = r1 - mid
    return jnp.where(sel == 0, hi, jnp.where(sel == 1, mid, lo))


def _mod_row(ref):
    return jnp.concatenate([ref[0, 0, j] for j in range(ref.shape[2])], axis=1)


def _params(sem):
    return pltpu.CompilerParams(dimension_semantics=sem, vmem_limit_bytes=VMEM_LIMIT)


def _ada_kernel(ct_ref, w_ref, b_ref, o_ref):
    ct = ct_ref[...]
    sc = _silu(ct)
    w = w_ref[0]
    bias = b_ref[0]
    for b in range(ct.shape[1]):
        col = sc[:, b:b + 1]
        o_ref[0, b, 0] = jnp.sum(col * w, axis=0, keepdims=True) + bias


def _ada(c, ada_w, ada_b):
    L, D, N = ada_w.shape
    B = c.shape[0]
    return pl.pallas_call(
        _ada_kernel,
        grid=(L, N // TN_ADA),
        in_specs=[
            pl.BlockSpec((D, B), lambda l, n: (0, 0)),
            pl.BlockSpec((1, D, TN_ADA), lambda l, n: (l, 0, n)),
            pl.BlockSpec((1, 1, TN_ADA), lambda l, n: (l, 0, n)),
        ],
        out_specs=pl.BlockSpec((1, B, 1, 1, TN_ADA), lambda l, n: (l, 0, n, 0, 0)),
        out_shape=jax.ShapeDtypeStruct((L, B, N // TN_ADA, 1, TN_ADA), F32),
        compiler_params=_params(("parallel", "parallel")),
        name="ada_mod",
    )(c.T, ada_w, ada_b.reshape(L, 1, N))


def _lru_mixer(g_lru, r_pre, i_pre, lam, h_ref, xc):
    r = _sigmoid(r_pre)
    ig = _sigmoid(i_pre)
    nlam = -lam
    softplus = jnp.maximum(nlam, 0.0) + jnp.log1p(jnp.exp(-jnp.abs(nlam)))
    log_a = (-LRU_C * r) * softplus
    a = jnp.exp(log_a)
    bt = jnp.sqrt(jnp.tanh(-log_a) * (1.0 + a * a)) * (ig * xc)

    row8 = lax.broadcasted_iota(jnp.int32, (TM, D_LRU), 0) & (SUBLANES - 1)
    d = 1
    while d < SUBLANES:
        valid = row8 >= d
        a_s = jnp.where(valid, pltpu.roll(a, d, axis=0), 1.0)
        b_s = jnp.where(valid, pltpu.roll(bt, d, axis=0), 0.0)
        bt = bt + a * b_s
        a = a * a_s
        d *= 2

    h_prev = h_ref[...]
    sg = _silu(g_lru)
    outs = []
    for g in range(TM // SUBLANES):
        lo = g * SUBLANES
        hg = bt[lo:lo + SUBLANES] + a[lo:lo + SUBLANES] * h_prev
        h_prev = hg[SUBLANES - 1:SUBLANES, :]
        outs.append(hg * sg[lo:lo + SUBLANES])
    h_ref[...] = h_prev
    return jnp.concatenate(outs, axis=0)


def _inproj_kernel(x_ref, shift_ref, scale_ref, g_ref, we_ref, wg_ref, wk_ref, wt_ref, bf_ref,
                   cw_ref, cb_ref, wa_ref, ba_ref, wx_ref, bx_ref, lam_ref,
                   lng_ref, lnb_ref, ws_ref, sb_ref, he_ref,
                   qt_ref, vt_ref, k_ref, ka_ref, qa_ref, ga_ref, yl_ref, ys_ref,
                   qs_ref, ks_ref, fs_ref, fe_ref,
                   fc_ref, xbuf_ref, h_ref):
    i = pl.program_id(1)

    @pl.when(i == 0)
    def _():
        fc_ref[...] = jnp.zeros(fc_ref.shape, F32)
        xbuf_ref[0:SUBLANES, :] = jnp.zeros((SUBLANES, D_LRU), F32)
        h_ref[...] = jnp.zeros(h_ref.shape, F32)

    x = x_ref[0]
    ms = jnp.mean(x * x, axis=-1, keepdims=True)
    gs = g_ref[0] * (1.0 + _mod_row(scale_ref))
    h = (x * lax.rsqrt(ms + EPS)) * gs + _mod_row(shift_ref)
    hb = h.astype(BF16)

    early = jnp.dot(hb, we_ref[0], preferred_element_type=F32)
    x_lru = early[:, :D_LRU]
    sg_v = early[:, D_LRU:]
    tr = lax.dot_general(wt_ref[0], hb, (((1,), (1,)), ((), ())), preferred_element_type=F32)
    tq = tr[:D_ATT] * Q_SCALE
    qt_ref[0] = tq.astype(BF16)
    vt_ref[0] = tr[D_ATT:2 * D_ATT].astype(BF16)
    fl = tr[2 * D_ATT:2 * D_ATT + N_AUG]
    kb = jnp.dot(hb, wk_ref[0], preferred_element_type=F32).astype(BF16)
    k_ref[0] = kb
    qn2 = jnp.sum((tq * tq).reshape(N_HEADS, HEAD_DIM, TM), axis=1)
    qs_ref[0, 0] = jnp.broadcast_to(jnp.sqrt(jnp.max(qn2, axis=1, keepdims=True)), (N_HEADS, LANES))

    v = _gelu_tanh(sg_v)
    mu = jnp.mean(v, axis=-1, keepdims=True)
    var = jnp.mean(jnp.square(v - mu), axis=-1, keepdims=True)
    vnb = (((v - mu) * lax.rsqrt(var + EPS)) * lng_ref[0] + lnb_ref[0]).astype(BF16)
    w_shape = ws_ref.shape[1:]
    t_idx = lax.broadcasted_iota(jnp.int32, w_shape, 0) & (SG_CHUNK - 1)
    s_idx = lax.broadcasted_iota(jnp.int32, w_shape, 1)
    ws = jnp.where(s_idx <= t_idx, ws_ref[0], jnp.zeros(w_shape, BF16))
    grp = lax.broadcasted_iota(jnp.int32, (SG_CHUNK, D_SG), 1) // HEAD_DIM
    zs = []
    for c in range(TM // SG_CHUNK):
        zz = jnp.dot(ws, vnb[c * SG_CHUNK:(c + 1) * SG_CHUNK], preferred_element_type=F32)
        z = zz[0:SG_CHUNK]
        for gi in range(1, N_SG_GROUPS):
            z = jnp.where(grp == gi, zz[gi * SG_CHUNK:(gi + 1) * SG_CHUNK], z)
        zs.append(z + sb_ref[0])
    z_all = jnp.concatenate(zs, axis=0)

    xbuf_ref[SUBLANES:SUBLANES + TM, :] = x_lru
    xc = cb_ref[0]
    for kk in range(CONV_WIDTH):
        off = SUBLANES - (CONV_WIDTH - 1) + kk
        xc = xc + xbuf_ref[off:off + TM, :] * cw_ref[0, kk:kk + 1, :]
    xbuf_ref[0:SUBLANES, :] = x_lru[TM - SUBLANES:, :]
    xcb = xc.astype(BF16)
    r_pre = jnp.dot(xcb, wa_ref[0], preferred_element_type=F32) + ba_ref[0]
    i_pre = jnp.dot(xcb, wx_ref[0], preferred_element_type=F32) + bx_ref[0]

    gates = jnp.dot(hb, wg_ref[0], preferred_element_type=F32)
    kn2 = jnp.dot(kb * kb, he_ref[...], preferred_element_type=F32)
    ks_ref[0, 0] = jnp.sqrt(jnp.max(kn2, axis=0, keepdims=True))

    ga_ref[0] = gates[:, :D_ATT].astype(BF16)
    g_lru = gates[:, D_ATT:D_ATT + D_LRU]
    sg_u = gates[:, D_ATT + D_LRU:D_ATT + D_LRU + D_SG]
    g_sg = gates[:, D_ATT + D_LRU + D_SG:]
    ys_ref[0] = ((_gelu_tanh(sg_u) * z_all) * _silu(g_sg)).astype(BF16)
    yl_ref[0] = _lru_mixer(g_lru, r_pre, i_pre, lam_ref[0], h_ref, xc).astype(BF16)

    reps = TM // LANES
    ls = _log_sigmoid(fl + jnp.concatenate([bf_ref[0]] * reps, axis=1))
    lane = lax.broadcasted_iota(jnp.int32, (N_AUG, TM), 1)
    d = 1
    while d < TM:
        ls = ls + jnp.where(lane >= d, pltpu.roll(ls, d, axis=1), 0.0)
        d *= 2
    f = ls + jnp.concatenate([fc_ref[...]] * reps, axis=1)
    fc_ref[...] = jnp.broadcast_to(f[:, TM - 1:TM], (N_AUG, LANES))
    row = lax.broadcasted_iota(jnp.int32, (N_AUG, TM), 0)
    f2 = f * LOG2E
    fs_ref[0, 0] = jnp.broadcast_to(f2[:, 0:1], (N_AUG, LANES))
    fe_ref[0, 0] = jnp.broadcast_to(f2[:, TM - 1:TM], (N_AUG, LANES))
    parts = _split3(f2, row % N_PARTS)
    ones = jnp.ones((N_AUG, TM), F32)
    zeros = jnp.zeros((LANES - 2 * N_AUG, TM), F32)
    qa_ref[0] = jnp.concatenate([ones, parts, zeros], axis=0).astype(BF16)
    ka_ref[0] = jnp.concatenate([-parts, ones, zeros], axis=0).T.astype(BF16)


def _inproj(layer, x, mod, prep):
    B, S, D = x.shape
    per_layer = lambda a: pl.BlockSpec((1,) + a.shape[1:], lambda b, i: (layer,) + (0,) * (a.ndim - 1))
    mod_chunk = lambda j: pl.BlockSpec((1, 1, D // TN_ADA, 1, TN_ADA), lambda b, i: (layer, b, j, 0, 0))
    row_tile = lambda n: pl.BlockSpec((1, TM, n), lambda b, i: (b, i, 0))
    col_tile = lambda n: pl.BlockSpec((1, n, TM), lambda b, i: (b, 0, i))
    names = ("pre_g", "we", "wg", "wk", "wt", "bf", "conv_w", "conv_b", "wa", "ba", "wx", "bx", "lam",
             "ln_g", "ln_b", "ws", "sg_bias")
    operands = [prep[n] for n in names]
    head_sum = prep["head_sum"]
    stat = lambda rows: pl.BlockSpec((1, 1, rows, LANES), lambda b, i: (b, i, 0, 0))
    stat_shape = lambda rows: jax.ShapeDtypeStruct((B, S // TM, rows, LANES), F32)
    return pl.pallas_call(
        _inproj_kernel,
        grid=(B, S // TM),
        in_specs=[row_tile(D), mod_chunk(0), mod_chunk(1)] + [per_layer(a) for a in operands]
        + [pl.BlockSpec(head_sum.shape, lambda b, i: (0, 0))],
        out_specs=[
            col_tile(D_ATT), col_tile(D_ATT), row_tile(D_ATT), row_tile(LANES), col_tile(LANES),
            row_tile(D_ATT), row_tile(D_LRU), row_tile(D_SG),
            stat(N_HEADS), stat(1), stat(N_AUG), stat(N_AUG),
        ],
        out_shape=[
            jax.ShapeDtypeStruct((B, D_ATT, S), BF16),
            jax.ShapeDtypeStruct((B, D_ATT, S), BF16),
            jax.ShapeDtypeStruct((B, S, D_ATT), BF16),
            jax.ShapeDtypeStruct((B, S, LANES), BF16),
            jax.ShapeDtypeStruct((B, LANES, S), BF16),
            jax.ShapeDtypeStruct((B, S, D_ATT), BF16),
            jax.ShapeDtypeStruct((B, S, D_LRU), BF16),
            jax.ShapeDtypeStruct((B, S, D_SG), BF16),
            stat_shape(N_HEADS),
            stat_shape(1),
            stat_shape(N_AUG),
            stat_shape(N_AUG),
        ],
        scratch_shapes=[
            pltpu.VMEM((N_AUG, LANES), F32),
            pltpu.VMEM((TM + SUBLANES, D_LRU), F32),
            pltpu.VMEM((1, D_LRU), F32),
        ],
        compiler_params=_params(("parallel", "arbitrary")),
        name="inproj",
    )(x, mod, mod, *operands, head_sum)


def _attn_kernel(seq_ref, noff_ref, qt_ref, qa_ref, k_ref, ka_ref, vt_ref, g_ref, o_ref,
                 m_ref, acc_ref, rhs_ref, bias_ref, s0_ref, s1_ref, p0_ref, p1_ref,
                 mx0_ref, mx1_ref, al0_ref, al1_ref):
    b = pl.program_id(0)
    i = pl.program_id(1)
    n_tiles = pl.num_programs(1)

    @pl.when((b == 0) & (i == 0))
    def _():
        kk = lax.broadcasted_iota(jnp.int32, (TQ, TQ), 0)
        qq = lax.broadcasted_iota(jnp.int32, (TQ, TQ), 1)
        bias_ref[0] = jnp.zeros((TQ, TQ), F32)
        bias_ref[1] = jnp.where(kk <= qq, 0.0, NEG)

    qa = qa_ref[0]
    row = lax.broadcasted_iota(jnp.int32, (LANES, TQ), 0)
    for p in range(N_PAIRS):
        qt = qt_ref[0, p * LANES:(p + 1) * LANES, :]
        zero = jnp.zeros_like(qt)
        for hh in range(2):
            own = (row >= hh * HEAD_DIM) & (row < (hh + 1) * HEAD_DIM)
            lo = (2 * p + hh) * N_PARTS
            aug = ((row >= lo) & (row < lo + N_PARTS)) | ((row >= N_AUG + lo) & (row < N_AUG + lo + N_PARTS))
            rhs_ref[2 * p + hh] = jnp.concatenate([jnp.where(own, qt, zero), jnp.where(aug, qa, zero)], axis=0)
    rhs_ref[N_HEADS] = jnp.zeros(rhs_ref.shape[1:], BF16)
    ones_rows = jnp.where(lax.broadcasted_iota(jnp.int32, (BF16_ROWS, TQ), 0) == 0, 1.0, 0.0).astype(BF16)
    m_ref[...] = jnp.full(m_ref.shape, NEG, F32)
    acc_ref[...] = jnp.zeros(acc_ref.shape, F32)
    s_bufs = (s0_ref, s1_ref)
    p_bufs = (p0_ref, p1_ref)
    mx_bufs = (mx0_ref, mx1_ref)
    al_bufs = (al0_ref, al1_ref)

    def item(n, hh):
        code = seq_ref[(b * n_tiles + i) * (N_HEADS * n_tiles) + 2 * n + hh]
        return lax.div(code, n_tiles), lax.rem(code, n_tiles)

    HALF = TQ // 2

    def key_lanes(h):
        pair = lax.shift_right_logical(jnp.minimum(h, N_HEADS - 1), 1)
        return pl.ds(pl.multiple_of(pair * LANES, LANES), LANES)

    def scores(n, slot, kind):
        for hh in range(2):
            h, t = item(n, hh)
            if kind == "diag":
                lo_rows = pl.ds(pl.multiple_of(i * TQ, TQ), HALF)
                hi_rows = pl.ds(pl.multiple_of(i * TQ + HALF, HALF), HALF)
                lhs_lo = jnp.concatenate([k_ref[0, lo_rows, key_lanes(h)], ka_ref[0, lo_rows, :]], axis=1)
                lhs_hi = jnp.concatenate([k_ref[0, hi_rows, key_lanes(h)], ka_ref[0, hi_rows, :]], axis=1)
                st_lo = jnp.dot(lhs_lo, rhs_ref[h], preferred_element_type=F32) + bias_ref[1, :HALF, :]
                st_hi = (jnp.dot(lhs_hi, rhs_ref[h, :, HALF:], preferred_element_type=F32)
                         + bias_ref[1, HALF:, HALF:])
                s_bufs[slot][hh, :HALF, :] = st_lo
                s_bufs[slot][hh, HALF:, HALF:] = st_hi
                mx_lo = jnp.max(st_lo, axis=0, keepdims=True)
                mx_hi = jnp.max(st_hi, axis=0, keepdims=True)
                mx_bufs[slot][hh] = jnp.concatenate([mx_lo[:, :HALF], jnp.maximum(mx_lo[:, HALF:], mx_hi)], axis=1)
                continue
            rows = pl.ds(pl.multiple_of(t * TQ, TQ), TQ)
            lhs = jnp.concatenate([k_ref[0, rows, key_lanes(h)], ka_ref[0, rows, :]], axis=1)
            st = jnp.dot(lhs, rhs_ref[h], preferred_element_type=F32)
            if kind == "any":
                st = st + bias_ref[(t == i).astype(jnp.int32)]
            s_bufs[slot][hh] = st
            mx_bufs[slot][hh] = jnp.max(st, axis=0, keepdims=True)

    def softmax(n, slot, kind):
        for hh in range(2):
            h, _ = item(n, hh)
            m_old = m_ref[h]
            m_new = jnp.maximum(m_old, mx_bufs[slot][hh])
            if kind == "diag":
                p_bufs[slot][hh, :HALF, :] = jnp.exp2(s_bufs[slot][hh, :HALF, :] - m_new).astype(BF16)
                p_bufs[slot][hh, HALF:, HALF:] = jnp.exp2(s_bufs[slot][hh, HALF:, HALF:] - m_new[:, HALF:]).astype(BF16)
            else:
                p_bufs[slot][hh] = jnp.exp2(s_bufs[slot][hh] - m_new).astype(BF16)
            al_bufs[slot][hh] = jnp.exp2(m_old - m_new)
            m_ref[h] = m_new

    def values(n, slot, kind):
        for hh in range(2):
            h, t = item(n, hh)
            cols = pl.ds(pl.multiple_of(t * TQ, TQ), TQ)
            head_rows = pl.ds(pl.multiple_of(jnp.minimum(h, N_HEADS - 1) * HEAD_DIM, HEAD_DIM), HEAD_DIM)
            vaug = jnp.concatenate([vt_ref[0, head_rows, cols], ones_rows], axis=0)
            if kind == "diag":
                new = jnp.dot(vaug[:, :HALF], p_bufs[slot][hh, :HALF, :], preferred_element_type=F32)
                new_hi = jnp.dot(vaug[:, HALF:], p_bufs[slot][hh, HALF:, HALF:], preferred_element_type=F32)
                new = jnp.concatenate([new[:, :HALF], new[:, HALF:] + new_hi], axis=1)
            else:
                new = jnp.dot(vaug, p_bufs[slot][hh], preferred_element_type=F32)
            acc_ref[h] = acc_ref[h] * al_bufs[slot][hh] + new

    def full_step(n, parity, kinds):
        softmax(n - 1, 1 - parity, kinds[1])
        scores(n, parity, kinds[0])
        values(n - 2, parity, kinds[2])

    n_off = noff_ref[b * n_tiles + i]
    scores(0, 0, "any")
    softmax(0, 0, "any")
    scores(1, 1, "any")

    trips = lax.shift_right_logical(jnp.maximum(n_off - 2, 0), 1)

    def body(u, carry):
        full_step(2 + 2 * u, 0, ("off",) * 3)
        full_step(3 + 2 * u, 1, ("off",) * 3)
        return carry

    lax.fori_loop(0, trips, body, 0)
    n0 = 2 + 2 * trips
    for left in range(2, 6):
        kind = lambda j, left=left: "diag" if j >= max(left - N_PAIRS, 0) else "off"

        @pl.when(n_off + N_PAIRS - n0 == left)
        def _():
            for j in range(left):
                full_step(n0 + j, j % 2, (kind(j), kind(j - 1), kind(j - 2)))
            softmax(n0 + left - 1, (left - 1) % 2, kind(left - 1))
            values(n0 + left - 2, left % 2, kind(left - 2))
            values(n0 + left - 1, (left - 1) % 2, kind(left - 1))

    for p in range(N_PAIRS):
        outs = []
        for hh in range(2):
            a = acc_ref[2 * p + hh]
            outs.append(a[:HEAD_DIM] / a[HEAD_DIM:HEAD_DIM + 1])
        o = jnp.concatenate(outs, axis=0).T
        lanes = slice(p * LANES, (p + 1) * LANES)
        o_ref[0, :, lanes] = (o * _silu(g_ref[0, :, lanes].astype(F32))).astype(BF16)


def _tile_schedule(qs, ks, fs, fe):
    B, n_tiles = qs.shape[:2]
    qmax = qs[:, :, :, 0]
    kmax = ks[:, :, 0, :N_HEADS]
    f_first = fs[:, :, ::N_PARTS, 0]
    f_last = fe[:, :, ::N_PARTS, 0]
    gap = f_first[:, :, None, :] - f_last[:, None, :, :]
    reach = NORM_SLACK * qmax[:, :, None, :] * (kmax[:, None, :, :] + kmax[:, :, None, :])
    negligible = (gap + reach) <= -SKIP_LOG2
    tiles = jnp.arange(n_tiles)
    skip = negligible & (tiles[None, :] < tiles[:, None])[None, :, :, None]
    first = jnp.min(jnp.where(skip, n_tiles, tiles[None, None, :, None]), axis=2)
    first = first.transpose(0, 2, 1)
    visit = (tiles[None, None, None, :] >= first[:, :, :, None]) & (tiles[None, None, None, :] < tiles[None, None, :, None])
    flat = visit.transpose(0, 2, 1, 3).reshape(B, n_tiles, N_HEADS * n_tiles).astype(jnp.int32)
    pos = jnp.cumsum(flat, axis=2) - flat
    n_items = jnp.sum(flat, axis=2)
    n_off = (n_items + 1) // 2
    codes = jnp.arange(N_HEADS * n_tiles)
    slots = jnp.arange(N_HEADS * n_tiles)
    hit = (flat[:, :, None, :] == 1) & (pos[:, :, None, :] == slots[None, None, :, None])
    seq = jnp.sum(jnp.where(hit, codes[None, None, None, :], 0), axis=3)
    seq = jnp.where(slots[None, None, :] == n_items[:, :, None], N_HEADS * n_tiles, seq)
    head = slots[None, None, :] - 2 * n_off[:, :, None]
    seq = jnp.where((head >= 0) & (head < N_HEADS), head * n_tiles + tiles[None, :, None], seq)
    return seq.reshape(-1).astype(jnp.int32), n_off.reshape(-1).astype(jnp.int32)


def _attention(schedule, qt, qa, k, ka, vt, gate):
    B, S, _ = k.shape
    grid_spec = pltpu.PrefetchScalarGridSpec(
        num_scalar_prefetch=2,
        grid=(B, S // TQ),
        in_specs=[
            pl.BlockSpec((1, D_ATT, TQ), lambda b, i, *_: (b, 0, i)),
            pl.BlockSpec((1, LANES, TQ), lambda b, i, *_: (b, 0, i)),
            pl.BlockSpec((1, S, D_ATT), lambda b, i, *_: (b, 0, 0)),
            pl.BlockSpec((1, S, LANES), lambda b, i, *_: (b, 0, 0)),
            pl.BlockSpec((1, D_ATT, S), lambda b, i, *_: (b, 0, 0)),
            pl.BlockSpec((1, TQ, D_ATT), lambda b, i, *_: (b, i, 0)),
        ],
        out_specs=pl.BlockSpec((1, TQ, D_ATT), lambda b, i, *_: (b, i, 0)),
        scratch_shapes=[
            pltpu.VMEM((N_HEADS + 1, 1, TQ), F32),
            pltpu.VMEM((N_HEADS + 1, V_ROWS, TQ), F32),
            pltpu.VMEM((N_HEADS + 1, 2 * LANES, TQ), BF16),
            pltpu.VMEM((2, TQ, TQ), F32),
            pltpu.VMEM((2, TQ, TQ), F32),
            pltpu.VMEM((2, TQ, TQ), F32),
            pltpu.VMEM((2, TQ, TQ), BF16),
            pltpu.VMEM((2, TQ, TQ), BF16),
            pltpu.VMEM((2, 1, TQ), F32),
            pltpu.VMEM((2, 1, TQ), F32),
            pltpu.VMEM((2, 1, TQ), F32),
            pltpu.VMEM((2, 1, TQ), F32),
        ],
    )
    return pl.pallas_call(
        _attn_kernel,
        grid_spec=grid_spec,
        out_shape=jax.ShapeDtypeStruct((B, S, D_ATT), BF16),
        compiler_params=_params(("arbitrary", "arbitrary")),
        name="fox_attention",
    )(*schedule, qt, qa, k, ka, vt, gate)


def _outproj_kernel(ya_ref, yl_ref, ys_ref, x_ref, gate_ref, pg_ref, w_ref, o_ref):
    y = jnp.dot(ya_ref[0], w_ref[0, 0:D_ATT, :], preferred_element_type=F32)
    y = y + jnp.dot(yl_ref[0], w_ref[0, D_ATT:D_ATT + D_LRU, :], preferred_element_type=F32)
    y = y + jnp.dot(ys_ref[0], w_ref[0, D_ATT + D_LRU:, :], preferred_element_type=F32)
    ms = jnp.mean(y * y, axis=-1, keepdims=True)
    yn = (y * lax.rsqrt(ms + EPS)) * pg_ref[0]
    o_ref[0] = x_ref[0] + _mod_row(gate_ref) * yn


def _outproj(layer, ya, yl, ys, x, mod, prep):
    B, S, D = x.shape
    per_layer = lambda a: pl.BlockSpec((1,) + a.shape[1:], lambda b, i: (layer,) + (0,) * (a.ndim - 1))
    row_tile = lambda n: pl.BlockSpec((1, TO, n), lambda b, i: (b, i, 0))
    return pl.pallas_call(
        _outproj_kernel,
        grid=(B, S // TO),
        in_specs=[
            row_tile(D_ATT), row_tile(D_LRU), row_tile(D_SG), row_tile(D),
            pl.BlockSpec((1, 1, D // TN_ADA, 1, TN_ADA), lambda b, i: (layer, b, 2, 0, 0)),
            per_layer(prep["post_g"]),
            per_layer(prep["w_out"]),
        ],
        out_specs=row_tile(D),
        out_shape=jax.ShapeDtypeStruct((B, S, D), F32),
        compiler_params=_params(("parallel", "parallel")),
        name="outproj",
    )(ya, yl, ys, x, mod, prep["post_g"], prep["w_out"])


def _block_diag(w):
    L, G, n, _ = w.shape
    eye = jnp.eye(G, dtype=w.dtype)
    return (w[:, :, :, None, :] * eye[None, :, None, :, None]).reshape(L, G * n, G * n)


def _prepare(pre_g, post_g, w_in, b_f, conv_w, conv_b, lru_wa, lru_ba, lru_wx, lru_bx, lru_lambda,
             sg_ln_g, sg_ln_b, sg_w, sg_b, w_out):
    L, D, _ = w_in.shape
    cuts = [0]
    for n in IN_SIZES:
        cuts.append(cuts[-1] + n)
    wq, wk, wv, wf, wga, wxl, wgl, wsu, wsv, wgs = (w_in[:, :, cuts[j]:cuts[j + 1]] for j in range(10))
    row = lambda a: a[:, None, :]
    wf3 = jnp.repeat(wf, N_PARTS, axis=2)
    pad = jnp.zeros((L, D, T_ROWS - 2 * D_ATT - N_AUG), F32)
    return dict(
        pre_g=row(pre_g), post_g=row(post_g),
        we=jnp.concatenate([wxl, wsv], axis=2).astype(BF16),
        wg=jnp.concatenate([wga, wgl, wsu, wgs], axis=2).astype(BF16),
        wk=wk.astype(BF16),
        wt=jnp.concatenate([wq, wv, wf3, pad], axis=2).transpose(0, 2, 1).astype(BF16),
        bf=jnp.broadcast_to(jnp.repeat(b_f, N_PARTS, axis=1)[:, :, None], (L, N_AUG, LANES)),
        conv_w=conv_w, conv_b=row(conv_b),
        wa=_block_diag(lru_wa).astype(BF16), ba=row(lru_ba),
        wx=_block_diag(lru_wx).astype(BF16), bx=row(lru_bx), lam=row(lru_lambda),
        ln_g=row(sg_ln_g), ln_b=row(sg_ln_b),
        ws=sg_w.reshape(L, N_SG_GROUPS * SG_CHUNK, SG_CHUNK).astype(BF16),
        sg_bias=jnp.repeat(sg_b.transpose(0, 2, 1), HEAD_DIM, axis=2),
        w_out=w_out.astype(BF16),
        head_sum=(jnp.arange(D_ATT)[:, None] // HEAD_DIM == jnp.arange(LANES)[None, :]).astype(BF16),
    )


def kernel(x, c, ada_w, ada_b, pre_g, post_g, w_in, b_f, conv_w, conv_b, lru_wa, lru_ba, lru_wx,
           lru_bx, lru_lambda, sg_ln_g, sg_ln_b, sg_w, sg_b, w_out):
    mod = _ada(c, ada_w, ada_b)
    prep = _prepare(pre_g, post_g, w_in, b_f, conv_w, conv_b, lru_wa, lru_ba, lru_wx, lru_bx,
                    lru_lambda, sg_ln_g, sg_ln_b, sg_w, sg_b, w_out)
    for layer in range(ada_w.shape[0]):
        qt, vt, k, ka, qa, ga, yl, ys, qs, ks, fs, fe = _inproj(layer, x, mod, prep)
        ya = _attention(_tile_schedule(qs, ks, fs, fe), qt, qa, k, ka, vt, ga)
        x = _outproj(layer, ya, yl, ys, x, mod, prep)
    return x
```

```python
import math

import jax
import jax.numpy as jnp
from jax import lax
from jax.experimental import pallas as pl
from jax.experimental.pallas import tpu as pltpu

D_MODEL = 1024
HEAD_DIM = 64
D_ATT = 512
D_LRU = 256
D_SG = 256
N_HEADS = D_ATT // HEAD_DIM
N_PAIRS = N_HEADS // 2
N_LRU_BLOCKS = D_LRU // HEAD_DIM
N_SG_GROUPS = D_SG // HEAD_DIM
SG_CHUNK = 128
CONV_WIDTH = 4
LRU_C = 8.0
EPS = 1e-6
IN_SIZES = (D_ATT, D_ATT, D_ATT, N_HEADS, D_ATT, D_LRU, D_LRU, D_SG, D_SG, D_SG)

LANES = 128
SUBLANES = 8
BF16_ROWS = 16
VMEM_LIMIT = 56 * 1024 * 1024

TM = 512
TO = 2048
TQ = 512
TN_ADA = 512
NEG = -1e30
assert TM == TQ

N_PARTS = 3
N_AUG = N_HEADS * N_PARTS
T_ROWS = 2 * D_ATT + 2 * BF16_ROWS
LOG2E = 1.4426950408889634
Q_SCALE = LOG2E / math.sqrt(HEAD_DIM)
V_ROWS = HEAD_DIM + BF16_ROWS
SKIP_LOG2 = 160.0
NORM_SLACK = 1.02
GELU_C0 = math.sqrt(2.0 / math.pi)
GELU_C1 = GELU_C0 * 0.044715

F32 = jnp.float32
BF16 = jnp.bfloat16


def _sigmoid(x):
    return 0.5 * jnp.tanh(0.5 * x) + 0.5


def _silu(x):
    hx = 0.5 * x
    return hx + hx * jnp.tanh(hx)


def _gelu_tanh(x):
    hx = 0.5 * x
    return hx + hx * jnp.tanh(x * (GELU_C0 + GELU_C1 * (x * x)))


def _log_sigmoid(x):
    return jnp.minimum(x, 0.0) - jnp.log1p(jnp.exp(-jnp.abs(x)))


def _split3(x, sel):
    hi = x.astype(BF16).astype(F32)
    r1 = x - hi
    mid = r1.astype(BF16).astype(F32)
    lo = r1 - mid
    return jnp.where(sel == 0, hi, jnp.where(sel == 1, mid, lo))


def _mod_row(ref):
    return jnp.concatenate([ref[0, 0, j] for j in range(ref.shape[2])], axis=1)


def _params(sem):
    return pltpu.CompilerParams(dimension_semantics=sem, vmem_limit_bytes=VMEM_LIMIT)


def _ada_kernel(ct_ref, w_ref, b_ref, o_ref):
    ct = ct_ref[...]
    sc = _silu(ct)
    w = w_ref[0]
    bias = b_ref[0]
    for b in range(ct.shape[1]):
        col = sc[:, b:b + 1]
        o_ref[0, b, 0] = jnp.sum(col * w, axis=0, keepdims=True) + bias


def _ada(c, ada_w, ada_b):
    L, D, N = ada_w.shape
    B = c.shape[0]
    return pl.pallas_call(
        _ada_kernel,
        grid=(L, N // TN_ADA),
        in_specs=[
            pl.BlockSpec((D, B), lambda l, n: (0, 0)),
            pl.BlockSpec((1, D, TN_ADA), lambda l, n: (l, 0, n)),
            pl.BlockSpec((1, 1, TN_ADA), lambda l, n: (l, 0, n)),
        ],
        out_specs=pl.BlockSpec((1, B, 1, 1, TN_ADA), lambda l, n: (l, 0, n, 0, 0)),
        out_shape=jax.ShapeDtypeStruct((L, B, N // TN_ADA, 1, TN_ADA), F32),
        compiler_params=_params(("parallel", "parallel")),
        name="ada_mod",
    )(c.T, ada_w, ada_b.reshape(L, 1, N))


def _lru_mixer(g_lru, r_pre, i_pre, lam, h_ref, xc):
    r = _sigmoid(r_pre)
    ig = _sigmoid(i_pre)
    nlam = -lam
    softplus = jnp.maximum(nlam, 0.0) + jnp.log1p(jnp.exp(-jnp.abs(nlam)))
    log_a = (-LRU_C * r) * softplus
    a = jnp.exp(log_a)
    bt = jnp.sqrt(jnp.tanh(-log_a) * (1.0 + a * a)) * (ig * xc)

    row8 = lax.broadcasted_iota(jnp.int32, (TM, D_LRU), 0) & (SUBLANES - 1)
    d = 1
    while d < SUBLANES:
        valid = row8 >= d
        a_s = jnp.where(valid, pltpu.roll(a, d, axis=0), 1.0)
        b_s = jnp.where(valid, pltpu.roll(bt, d, axis=0), 0.0)
        bt = bt + a * b_s
        a = a * a_s
        d *= 2

    h_prev = h_ref[...]
    sg = _silu(g_lru)
    outs = []
    for g in range(TM // SUBLANES):
        lo = g * SUBLANES
        hg = bt[lo:lo + SUBLANES] + a[lo:lo + SUBLANES] * h_prev
        h_prev = hg[SUBLANES - 1:SUBLANES, :]
        outs.append(hg * sg[lo:lo + SUBLANES])
    h_ref[...] = h_prev
    return jnp.concatenate(outs, axis=0)


def _inproj_kernel(x_ref, shift_ref, scale_ref, g_ref, we_ref, wg_ref, wk_ref, wt_ref, bf_ref,
                   cw_ref, cb_ref, wa_ref, ba_ref, wx_ref, bx_ref, lam_ref,
                   lng_ref, lnb_ref, ws_ref, sb_ref, he_ref,
                   qt_ref, vt_ref, k_ref, ka_ref, qa_ref, ga_ref, yl_ref, ys_ref,
                   qs_ref, ks_ref, fs_ref, fe_ref,
                   fc_ref, xbuf_ref, h_ref):
    i = pl.program_id(1)

    @pl.when(i == 0)
    def _():
        fc_ref[...] = jnp.zeros(fc_ref.shape, F32)
        xbuf_ref[0:SUBLANES, :] = jnp.zeros((SUBLANES, D_LRU), F32)
        h_ref[...] = jnp.zeros(h_ref.shape, F32)

    x = x_ref[0]
    ms = jnp.mean(x * x, axis=-1, keepdims=True)
    gs = g_ref[0] * (1.0 + _mod_row(scale_ref))
    h = (x * lax.rsqrt(ms + EPS)) * gs + _mod_row(shift_ref)
    hb = h.astype(BF16)

    early = jnp.dot(hb, we_ref[0], preferred_element_type=F32)
    x_lru = early[:, :D_LRU]
    sg_v = early[:, D_LRU:]
    tr = lax.dot_general(wt_ref[0], hb, (((1,), (1,)), ((), ())), preferred_element_type=F32)
    tq = tr[:D_ATT] * Q_SCALE
    qt_ref[0] = tq.astype(BF16)
    vt_ref[0] = tr[D_ATT:2 * D_ATT].astype(BF16)
    fl = tr[2 * D_ATT:2 * D_ATT + N_AUG]
    kb = jnp.dot(hb, wk_ref[0], preferred_element_type=F32).astype(BF16)
    k_ref[0] = kb
    qn2 = jnp.sum((tq * tq).reshape(N_HEADS, HEAD_DIM, TM), axis=1)
    qs_ref[0, 0] = jnp.broadcast_to(jnp.sqrt(jnp.max(qn2, axis=1, keepdims=True)), (N_HEADS, LANES))

    v = _gelu_tanh(sg_v)
    mu = jnp.mean(v, axis=-1, keepdims=True)
    var = jnp.mean(jnp.square(v - mu), axis=-1, keepdims=True)
    vnb = (((v - mu) * lax.rsqrt(var + EPS)) * lng_ref[0] + lnb_ref[0]).astype(BF16)
    w_shape = ws_ref.shape[1:]
    t_idx = lax.broadcasted_iota(jnp.int32, w_shape, 0) & (SG_CHUNK - 1)
    s_idx = lax.broadcasted_iota(jnp.int32, w_shape, 1)
    ws = jnp.where(s_idx <= t_idx, ws_ref[0], jnp.zeros(w_shape, BF16))
    grp = lax.broadcasted_iota(jnp.int32, (SG_CHUNK, D_SG), 1) // HEAD_DIM
    zs = []
    for c in range(TM // SG_CHUNK):
        zz = jnp.dot(ws, vnb[c * SG_CHUNK:(c + 1) * SG_CHUNK], preferred_element_type=F32)
        z = zz[0:SG_CHUNK]
        for gi in range(1, N_SG_GROUPS):
            z = jnp.where(grp == gi, zz[gi * SG_CHUNK:(gi + 1) * SG_CHUNK], z)
        zs.append(z + sb_ref[0])
    z_all = jnp.concatenate(zs, axis=0)

    xbuf_ref[SUBLANES:SUBLANES + TM, :] = x_lru
    xc = cb_ref[0]
    for kk in range(CONV_WIDTH):
        off = SUBLANES - (CONV_WIDTH - 1) + kk
        xc = xc + xbuf_ref[off:off + TM, :] * cw_ref[0, kk:kk + 1, :]
    xbuf_ref[0:SUBLANES, :] = x_lru[TM - SUBLANES:, :]
    xcb = xc.astype(BF16)
    r_pre = jnp.dot(xcb, wa_ref[0], preferred_element_type=F32) + ba_ref[0]
    i_pre = jnp.dot(xcb, wx_ref[0], preferred_element_type=F32) + bx_ref[0]

    gates = jnp.dot(hb, wg_ref[0], preferred_element_type=F32)
    kn2 = jnp.dot(kb * kb, he_ref[...], preferred_element_type=F32)
    ks_ref[0, 0] = jnp.sqrt(jnp.max(kn2, axis=0, keepdims=True))

    ga_ref[0] = gates[:, :D_ATT].astype(BF16)
    g_lru = gates[:, D_ATT:D_ATT + D_LRU]
    sg_u = gates[:, D_ATT + D_LRU:D_ATT + D_LRU + D_SG]
    g_sg = gates[:, D_ATT + D_LRU + D_SG:]
    ys_ref[0] = ((_gelu_tanh(sg_u) * z_all) * _silu(g_sg)).astype(BF16)
    yl_ref[0] = _lru_mixer(g_lru, r_pre, i_pre, lam_ref[0], h_ref, xc).astype(BF16)

    reps = TM // LANES
    ls = _log_sigmoid(fl + jnp.concatenate([bf_ref[0]] * reps, axis=1))
    lane = lax.broadcasted_iota(jnp.int32, (N_AUG, TM), 1)
    d = 1
    while d < TM:
        ls = ls + jnp.where(lane >= d, pltpu.roll(ls, d, axis=1), 0.0)
        d *= 2
    f = ls + jnp.concatenate([fc_ref[...]] * reps, axis=1)
    fc_ref[...] = jnp.broadcast_to(f[:, TM - 1:TM], (N_AUG, LANES))
    row = lax.broadcasted_iota(jnp.int32, (N_AUG, TM), 0)
    f2 = f * LOG2E
    fs_ref[0, 0] = jnp.broadcast_to(f2[:, 0:1], (N_AUG, LANES))
    fe_ref[0, 0] = jnp.broadcast_to(f2[:, TM - 1:TM], (N_AUG, LANES))
    parts = _split3(f2, row % N_PARTS)
    ones = jnp.ones((N_AUG, TM), F32)
    zeros = jnp.zeros((LANES - 2 * N_AUG, TM), F32)
    qa_ref[0] = jnp.concatenate([ones, parts, zeros], axis=0).astype(BF16)
    ka_ref[0] = jnp.concatenate([-parts, ones, zeros], axis=0).T.astype(BF16)


def _inproj(layer, x, mod, prep):
    B, S, D = x.shape
    per_layer = lambda a: pl.BlockSpec((1,) + a.shape[1:], lambda b, i: (layer,) + (0,) * (a.ndim - 1))
    mod_chunk = lambda j: pl.BlockSpec((1, 1, D // TN_ADA, 1, TN_ADA), lambda b, i: (layer, b, j, 0, 0))
    row_tile = lambda n: pl.BlockSpec((1, TM, n), lambda b, i: (b, i, 0))
    col_tile = lambda n: pl.BlockSpec((1, n, TM), lambda b, i: (b, 0, i))
    names = ("pre_g", "we", "wg", "wk", "wt", "bf", "conv_w", "conv_b", "wa", "ba", "wx", "bx", "lam",
             "ln_g", "ln_b", "ws", "sg_bias")
    operands = [prep[n] for n in names]
    head_sum = prep["head_sum"]
    stat = lambda rows: pl.BlockSpec((1, 1, rows, LANES), lambda b, i: (b, i, 0, 0))
    stat_shape = lambda rows: jax.ShapeDtypeStruct((B, S // TM, rows, LANES), F32)
    return pl.pallas_call(
        _inproj_kernel,
        grid=(B, S // TM),
        in_specs=[row_tile(D), mod_chunk(0), mod_chunk(1)] + [per_layer(a) for a in operands]
        + [pl.BlockSpec(head_sum.shape, lambda b, i: (0, 0))],
        out_specs=[
            col_tile(D_ATT), col_tile(D_ATT), row_tile(D_ATT), row_tile(LANES), col_tile(LANES),
            row_tile(D_ATT), row_tile(D_LRU), row_tile(D_SG),
            stat(N_HEADS), stat(1), stat(N_AUG), stat(N_AUG),
        ],
        out_shape=[
            jax.ShapeDtypeStruct((B, D_ATT, S), BF16),
            jax.ShapeDtypeStruct((B, D_ATT, S), BF16),
            jax.ShapeDtypeStruct((B, S, D_ATT), BF16),
            jax.ShapeDtypeStruct((B, S, LANES), BF16),
            jax.ShapeDtypeStruct((B, LANES, S), BF16),
            jax.ShapeDtypeStruct((B, S, D_ATT), BF16),
            jax.ShapeDtypeStruct((B, S, D_LRU), BF16),
            jax.ShapeDtypeStruct((B, S, D_SG), BF16),
            stat_shape(N_HEADS),
            stat_shape(1),
            stat_shape(N_AUG),
            stat_shape(N_AUG),
        ],
        scratch_shapes=[
            pltpu.VMEM((N_AUG, LANES), F32),
            pltpu.VMEM((TM + SUBLANES, D_LRU), F32),
            pltpu.VMEM((1, D_LRU), F32),
        ],
        compiler_params=_params(("parallel", "arbitrary")),
        name="inproj",
    )(x, mod, mod, *operands, head_sum)


def _attn_kernel(seq_ref, noff_ref, qt_ref, qa_ref, k_ref, ka_ref, vt_ref, g_ref, o_ref,
                 m_ref, acc_ref, rhs_ref, bias_ref, s0_ref, s1_ref, p0_ref, p1_ref,
                 mx0_ref, mx1_ref, al0_ref, al1_ref):
    b = pl.program_id(0)
    i = pl.program_id(1)
    n_tiles = pl.num_programs(1)

    @pl.when((b == 0) & (i == 0))
    def _():
        kk = lax.broadcasted_iota(jnp.int32, (TQ, TQ), 0)
        qq = lax.broadcasted_iota(jnp.int32, (TQ, TQ), 1)
        bias_ref[0] = jnp.zeros((TQ, TQ), F32)
        bias_ref[1] = jnp.where(kk <= qq, 0.0, NEG)
        rhs_ref[N_HEADS] = jnp.zeros(rhs_ref.shape[1:], BF16)
        acc_ref[...] = jnp.zeros(acc_ref.shape, F32)

    qa = qa_ref[0]
    row = lax.broadcasted_iota(jnp.int32, (LANES, TQ), 0)
    for p in range(N_PAIRS):
        qt = qt_ref[0, p * LANES:(p + 1) * LANES, :]
        zero = jnp.zeros_like(qt)
        for hh in range(2):
            own = (row >= hh * HEAD_DIM) & (row < (hh + 1) * HEAD_DIM)
            lo = (2 * p + hh) * N_PARTS
            aug = ((row >= lo) & (row < lo + N_PARTS)) | ((row >= N_AUG + lo) & (row < N_AUG + lo + N_PARTS))
            rhs_ref[2 * p + hh] = jnp.concatenate([jnp.where(own, qt, zero), jnp.where(aug, qa, zero)], axis=0)
    ones_rows = jnp.where(lax.broadcasted_iota(jnp.int32, (BF16_ROWS, TQ), 0) == 0, 1.0, 0.0).astype(BF16)
    m_ref[...] = jnp.full(m_ref.shape, NEG, F32)
    s_bufs = (s0_ref, s1_ref)
    p_bufs = (p0_ref, p1_ref)
    mx_bufs = (mx0_ref, mx1_ref)
    al_bufs = (al0_ref, al1_ref)

    def item(n, hh):
        code = seq_ref[(b * n_tiles + i) * (N_HEADS * n_tiles) + 2 * n + hh]
        return lax.div(code, n_tiles), lax.rem(code, n_tiles)

    HALF = TQ // 2

    def key_lanes(h):
        pair = lax.shift_right_logical(jnp.minimum(h, N_HEADS - 1), 1)
        return pl.ds(pl.multiple_of(pair * LANES, LANES), LANES)

    def scores(n, slot, kind):
        for hh in range(2):
            h, t = item(n, hh)
            if kind == "diag":
                lo_rows = pl.ds(pl.multiple_of(i * TQ, TQ), HALF)
                hi_rows = pl.ds(pl.multiple_of(i * TQ + HALF, HALF), HALF)
                lhs_lo = jnp.concatenate([k_ref[0, lo_rows, key_lanes(h)], ka_ref[0, lo_rows, :]], axis=1)
                lhs_hi = jnp.concatenate([k_ref[0, hi_rows, key_lanes(h)], ka_ref[0, hi_rows, :]], axis=1)
                st_lo = jnp.dot(lhs_lo, rhs_ref[h], preferred_element_type=F32) + bias_ref[1, :HALF, :]
                st_hi = (jnp.dot(lhs_hi, rhs_ref[h, :, HALF:], preferred_element_type=F32)
                         + bias_ref[1, HALF:, HALF:])
                s_bufs[slot][hh, :HALF, :] = st_lo
                s_bufs[slot][hh, HALF:, HALF:] = st_hi
                mx_lo = jnp.max(st_lo, axis=0, keepdims=True)
                mx_hi = jnp.max(st_hi, axis=0, keepdims=True)
                mx_bufs[slot][hh] = jnp.concatenate([mx_lo[:, :HALF], jnp.maximum(mx_lo[:, HALF:], mx_hi)], axis=1)
                continue
            rows = pl.ds(pl.multiple_of(t * TQ, TQ), TQ)
            lhs = jnp.concatenate([k_ref[0, rows, key_lanes(h)], ka_ref[0, rows, :]], axis=1)
            st = jnp.dot(lhs, rhs_ref[h], preferred_element_type=F32)
            if kind == "any":
                st = st + bias_ref[(t == i).astype(jnp.int32)]
            s_bufs[slot][hh] = st
            mx_bufs[slot][hh] = jnp.max(st, axis=0, keepdims=True)

    def softmax(n, slot, kind):
        for hh in range(2):
            h, _ = item(n, hh)
            m_old = m_ref[h]
            m_new = jnp.maximum(m_old, mx_bufs[slot][hh])
            if kind == "diag":
                p_bufs[slot][hh, :HALF, :] = jnp.exp2(s_bufs[slot][hh, :HALF, :] - m_new).astype(BF16)
                p_bufs[slot][hh, HALF:, HALF:] = jnp.exp2(s_bufs[slot][hh, HALF:, HALF:] - m_new[:, HALF:]).astype(BF16)
            else:
                p_bufs[slot][hh] = jnp.exp2(s_bufs[slot][hh] - m_new).astype(BF16)
            al_bufs[slot][hh] = jnp.exp2(m_old - m_new)
            m_ref[h] = m_new

    def values(n, slot, kind):
        for hh in range(2):
            h, t = item(n, hh)
            cols = pl.ds(pl.multiple_of(t * TQ, TQ), TQ)
            head_rows = pl.ds(pl.multiple_of(jnp.minimum(h, N_HEADS - 1) * HEAD_DIM, HEAD_DIM), HEAD_DIM)
            vaug = jnp.concatenate([vt_ref[0, head_rows, cols], ones_rows], axis=0)
            if kind == "diag":
                new = jnp.dot(vaug[:, :HALF], p_bufs[slot][hh, :HALF, :], preferred_element_type=F32)
                new_hi = jnp.dot(vaug[:, HALF:], p_bufs[slot][hh, HALF:, HALF:], preferred_element_type=F32)
                new = jnp.concatenate([new[:, :HALF], new[:, HALF:] + new_hi], axis=1)
            else:
                new = jnp.dot(vaug, p_bufs[slot][hh], preferred_element_type=F32)
            acc_ref[h] = acc_ref[h] * al_bufs[slot][hh] + new

    def full_step(n, parity, kinds):
        softmax(n - 1, 1 - parity, kinds[1])
        scores(n, parity, kinds[0])
        values(n - 2, parity, kinds[2])

    n_off = noff_ref[b * n_tiles + i]
    for kind, wanted in (("off", True), ("any", False)):
        @pl.when((n_off >= 2) == wanted)
        def _():
            scores(0, 0, kind)
            softmax(0, 0, kind)
            scores(1, 1, kind)

    trips = lax.shift_right_logical(jnp.maximum(n_off - 2, 0), 1)

    def body(u, carry):
        full_step(2 + 2 * u, 0, ("off",) * 3)
        full_step(3 + 2 * u, 1, ("off",) * 3)
        return carry

    lax.fori_loop(0, trips, body, 0)
    n0 = 2 + 2 * trips
    for left in range(2, 6):
        kind = lambda j, left=left: "diag" if j >= max(left - N_PAIRS, 0) else "off"

        @pl.when(n_off + N_PAIRS - n0 == left)
        def _():
            for j in range(left):
                full_step(n0 + j, j % 2, (kind(j), kind(j - 1), kind(j - 2)))
            softmax(n0 + left - 1, (left - 1) % 2, kind(left - 1))
            values(n0 + left - 2, left % 2, kind(left - 2))
            values(n0 + left - 1, (left - 1) % 2, kind(left - 1))

    for p in range(N_PAIRS):
        outs = []
        for hh in range(2):
            a = acc_ref[2 * p + hh]
            outs.append(a[:HEAD_DIM] / a[HEAD_DIM:HEAD_DIM + 1])
        o = jnp.concatenate(outs, axis=0).T
        lanes = slice(p * LANES, (p + 1) * LANES)
        o_ref[0, :, lanes] = (o * _silu(g_ref[0, :, lanes].astype(F32))).astype(BF16)


def _tile_schedule(qs, ks, fs, fe):
    B, n_tiles = qs.shape[:2]
    qmax = qs[:, :, :, 0]
    kmax = ks[:, :, 0, :N_HEADS]
    f_first = fs[:, :, ::N_PARTS, 0]
    f_last = fe[:, :, ::N_PARTS, 0]
    gap = f_first[:, :, None, :] - f_last[:, None, :, :]
    reach = NORM_SLACK * qmax[:, :, None, :] * (kmax[:, None, :, :] + kmax[:, :, None, :])
    negligible = (gap + reach) <= -SKIP_LOG2
    tiles = jnp.arange(n_tiles)
    skip = negligible & (tiles[None, :] < tiles[:, None])[None, :, :, None]
    first = jnp.min(jnp.where(skip, n_tiles, tiles[None, None, :, None]), axis=2)
    first = first.transpose(0, 2, 1)
    visit = (tiles[None, None, None, :] >= first[:, :, :, None]) & (tiles[None, None, None, :] < tiles[None, None, :, None])
    flat = visit.transpose(0, 2, 1, 3).reshape(B, n_tiles, N_HEADS * n_tiles).astype(jnp.int32)
    pos = jnp.cumsum(flat, axis=2) - flat
    n_items = jnp.sum(flat, axis=2)
    n_off = (n_items + 1) // 2
    codes = jnp.arange(N_HEADS * n_tiles)
    slots = jnp.arange(N_HEADS * n_tiles)
    hit = (flat[:, :, None, :] == 1) & (pos[:, :, None, :] == slots[None, None, :, None])
    seq = jnp.sum(jnp.where(hit, codes[None, None, None, :], 0), axis=3)
    seq = jnp.where(slots[None, None, :] == n_items[:, :, None], N_HEADS * n_tiles, seq)
    head = slots[None, None, :] - 2 * n_off[:, :, None]
    seq = jnp.where((head >= 0) & (head < N_HEADS), head * n_tiles + tiles[None, :, None], seq)
    return seq.reshape(-1).astype(jnp.int32), n_off.reshape(-1).astype(jnp.int32)


def _attention(schedule, qt, qa, k, ka, vt, gate):
    B, S, _ = k.shape
    grid_spec = pltpu.PrefetchScalarGridSpec(
        num_scalar_prefetch=2,
        grid=(B, S // TQ),
        in_specs=[
            pl.BlockSpec((1, D_ATT, TQ), lambda b, i, *_: (b, 0, i)),
            pl.BlockSpec((1, LANES, TQ), lambda b, i, *_: (b, 0, i)),
            pl.BlockSpec((1, S, D_ATT), lambda b, i, *_: (b, 0, 0)),
            pl.BlockSpec((1, S, LANES), lambda b, i, *_: (b, 0, 0)),
            pl.BlockSpec((1, D_ATT, S), lambda b, i, *_: (b, 0, 0)),
            pl.BlockSpec((1, TQ, D_ATT), lambda b, i, *_: (b, i, 0)),
        ],
        out_specs=pl.BlockSpec((1, TQ, D_ATT), lambda b, i, *_: (b, i, 0)),
        scratch_shapes=[
            pltpu.VMEM((N_HEADS + 1, 1, TQ), F32),
            pltpu.VMEM((N_HEADS + 1, V_ROWS, TQ), F32),
            pltpu.VMEM((N_HEADS + 1, 2 * LANES, TQ), BF16),
            pltpu.VMEM((2, TQ, TQ), F32),
            pltpu.VMEM((2, TQ, TQ), F32),
            pltpu.VMEM((2, TQ, TQ), F32),
            pltpu.VMEM((2, TQ, TQ), BF16),
            pltpu.VMEM((2, TQ, TQ), BF16),
            pltpu.VMEM((2, 1, TQ), F32),
            pltpu.VMEM((2, 1, TQ), F32),
            pltpu.VMEM((2, 1, TQ), F32),
            pltpu.VMEM((2, 1, TQ), F32),
        ],
    )
    return pl.pallas_call(
        _attn_kernel,
        grid_spec=grid_spec,
        out_shape=jax.ShapeDtypeStruct((B, S, D_ATT), BF16),
        compiler_params=_params(("arbitrary", "arbitrary")),
        name="fox_attention",
    )(*schedule, qt, qa, k, ka, vt, gate)


def _outproj_kernel(ya_ref, yl_ref, ys_ref, x_ref, gate_ref, pg_ref, w_ref, o_ref):
    y = jnp.dot(ya_ref[0], w_ref[0, 0:D_ATT, :], preferred_element_type=F32)
    y = y + jnp.dot(yl_ref[0], w_ref[0, D_ATT:D_ATT + D_LRU, :], preferred_element_type=F32)
    y = y + jnp.dot(ys_ref[0], w_ref[0, D_ATT + D_LRU:, :], preferred_element_type=F32)
    ms = jnp.mean(y * y, axis=-1, keepdims=True)
    yn = (y * lax.rsqrt(ms + EPS)) * pg_ref[0]
    o_ref[0] = x_ref[0] + _mod_row(gate_ref) * yn


def _outproj(layer, ya, yl, ys, x, mod, prep):
    B, S, D = x.shape
    per_layer = lambda a: pl.BlockSpec((1,) + a.shape[1:], lambda b, i: (layer,) + (0,) * (a.ndim - 1))
    row_tile = lambda n: pl.BlockSpec((1, TO, n), lambda b, i: (b, i, 0))
    return pl.pallas_call(
        _outproj_kernel,
        grid=(B, S // TO),
        in_specs=[
            row_tile(D_ATT), row_tile(D_LRU), row_tile(D_SG), row_tile(D),
            pl.BlockSpec((1, 1, D // TN_ADA, 1, TN_ADA), lambda b, i: (layer, b, 2, 0, 0)),
            per_layer(prep["post_g"]),
            per_layer(prep["w_out"]),
        ],
        out_specs=row_tile(D),
        out_shape=jax.ShapeDtypeStruct((B, S, D), F32),
        compiler_params=_params(("parallel", "parallel")),
        name="outproj",
    )(ya, yl, ys, x, mod, prep["post_g"], prep["w_out"])


def _block_diag(w):
    L, G, n, _ = w.shape
    eye = jnp.eye(G, dtype=w.dtype)
    return (w[:, :, :, None, :] * eye[None, :, None, :, None]).reshape(L, G * n, G * n)


def _prepare(pre_g, post_g, w_in, b_f, conv_w, conv_b, lru_wa, lru_ba, lru_wx, lru_bx, lru_lambda,
             sg_ln_g, sg_ln_b, sg_w, sg_b, w_out):
    L, D, _ = w_in.shape
    cuts = [0]
    for n in IN_SIZES:
        cuts.append(cuts[-1] + n)
    wq, wk, wv, wf, wga, wxl, wgl, wsu, wsv, wgs = (w_in[:, :, cuts[j]:cuts[j + 1]] for j in range(10))
    row = lambda a: a[:, None, :]
    wf3 = jnp.repeat(wf, N_PARTS, axis=2)
    pad = jnp.zeros((L, D, T_ROWS - 2 * D_ATT - N_AUG), F32)
    return dict(
        pre_g=row(pre_g), post_g=row(post_g),
        we=jnp.concatenate([wxl, wsv], axis=2).astype(BF16),
        wg=jnp.concatenate([wga, wgl, wsu, wgs], axis=2).astype(BF16),
        wk=wk.astype(BF16),
        wt=jnp.concatenate([wq, wv, wf3, pad], axis=2).transpose(0, 2, 1).astype(BF16),
        bf=jnp.broadcast_to(jnp.repeat(b_f, N_PARTS, axis=1)[:, :, None], (L, N_AUG, LANES)),
        conv_w=conv_w, conv_b=row(conv_b),
        wa=_block_diag(lru_wa).astype(BF16), ba=row(lru_ba),
        wx=_block_diag(lru_wx).astype(BF16), bx=row(lru_bx), lam=row(lru_lambda),
        ln_g=row(sg_ln_g), ln_b=row(sg_ln_b),
        ws=sg_w.reshape(L, N_SG_GROUPS * SG_CHUNK, SG_CHUNK).astype(BF16),
        sg_bias=jnp.repeat(sg_b.transpose(0, 2, 1), HEAD_DIM, axis=2),
        w_out=w_out.astype(BF16),
        head_sum=(jnp.arange(D_ATT)[:, None] // HEAD_DIM == jnp.arange(LANES)[None, :]).astype(BF16),
    )


def kernel(x, c, ada_w, ada_b, pre_g, post_g, w_in, b_f, conv_w, conv_b, lru_wa, lru_ba, lru_wx,
           lru_bx, lru_lambda, sg_ln_g, sg_ln_b, sg_w, sg_b, w_out):
    mod = _ada(c, ada_w, ada_b)
    prep = _prepare(pre_g, post_g, w_in, b_f, conv_w, conv_b, lru_wa, lru_ba, lru_wx, lru_bx,
                    lru_lambda, sg_ln_g, sg_ln_b, sg_w, sg_b, w_out)
    for layer in range(ada_w.shape[0]):
        qt, vt, k, ka, qa, ga, yl, ys, qs, ks, fs, fe = _inproj(layer, x, mod, prep)
        ya = _attention(_tile_schedule(qs, ks, fs, fe), qt, qa, k, ka, vt, ga)
        x = _outproj(layer, ya, yl, ys, x, mod, prep)
    return x
```

```python
import math

import jax
import jax.numpy as jnp
from jax import lax
from jax.experimental import pallas as pl
from jax.experimental.pallas import tpu as pltpu

D_MODEL = 1024
HEAD_DIM = 64
D_ATT = 512
D_LRU = 256
D_SG = 256
N_HEADS = D_ATT // HEAD_DIM
N_PAIRS = N_HEADS // 2
N_LRU_BLOCKS = D_LRU // HEAD_DIM
N_SG_GROUPS = D_SG // HEAD_DIM
SG_CHUNK = 128
CONV_WIDTH = 4
LRU_C = 8.0
EPS = 1e-6
IN_SIZES = (D_ATT, D_ATT, D_ATT, N_HEADS, D_ATT, D_LRU, D_LRU, D_SG, D_SG, D_SG)

LANES = 128
SUBLANES = 8
BF16_ROWS = 16
VMEM_LIMIT = 56 * 1024 * 1024

TM = 512
TQ = 512
TN_ADA = 512
NEG = -1e30
assert TM == TQ

N_PARTS = 3
N_AUG = N_HEADS * N_PARTS
T_ROWS = 2 * D_ATT + 2 * BF16_ROWS
LOG2E = 1.4426950408889634
Q_SCALE = LOG2E / math.sqrt(HEAD_DIM)
V_ROWS = HEAD_DIM + BF16_ROWS
SKIP_LOG2 = 160.0
NORM_SLACK = 1.02
GELU_C0 = math.sqrt(2.0 / math.pi)
GELU_C1 = GELU_C0 * 0.044715

F32 = jnp.float32
BF16 = jnp.bfloat16


def _sigmoid(x):
    return 0.5 * jnp.tanh(0.5 * x) + 0.5


def _silu(x):
    hx = 0.5 * x
    return hx + hx * jnp.tanh(hx)


def _gelu_tanh(x):
    hx = 0.5 * x
    return hx + hx * jnp.tanh(x * (GELU_C0 + GELU_C1 * (x * x)))


def _log_sigmoid(x):
    return jnp.minimum(x, 0.0) - jnp.log1p(jnp.exp(-jnp.abs(x)))


def _split3(x, sel):
    hi = x.astype(BF16).astype(F32)
    r1 = x - hi
    mid = r1.astype(BF16).astype(F32)
    lo = r1 - mid
    return jnp.where(sel == 0, hi, jnp.where(sel == 1, mid, lo))


def _mod_row(ref):
    return jnp.concatenate([ref[0, 0, j] for j in range(ref.shape[2])], axis=1)


def _params(sem):
    return pltpu.CompilerParams(dimension_semantics=sem, vmem_limit_bytes=VMEM_LIMIT)


def _ada_kernel(ct_ref, w_ref, b_ref, o_ref):
    ct = ct_ref[...]
    sc = _silu(ct)
    w = w_ref[0]
    bias = b_ref[0]
    for b in range(ct.shape[1]):
        col = sc[:, b:b + 1]
        o_ref[0, b, 0] = jnp.sum(col * w, axis=0, keepdims=True) + bias


def _ada(c, ada_w, ada_b):
    L, D, N = ada_w.shape
    B = c.shape[0]
    return pl.pallas_call(
        _ada_kernel,
        grid=(L, N // TN_ADA),
        in_specs=[
            pl.BlockSpec((D, B), lambda l, n: (0, 0)),
            pl.BlockSpec((1, D, TN_ADA), lambda l, n: (l, 0, n)),
            pl.BlockSpec((1, 1, TN_ADA), lambda l, n: (l, 0, n)),
        ],
        out_specs=pl.BlockSpec((1, B, 1, 1, TN_ADA), lambda l, n: (l, 0, n, 0, 0)),
        out_shape=jax.ShapeDtypeStruct((L, B, N // TN_ADA, 1, TN_ADA), F32),
        compiler_params=_params(("parallel", "parallel")),
        name="ada_mod",
    )(c.T, ada_w, ada_b.reshape(L, 1, N))


def _lru_mixer(g_lru, r_pre, i_pre, lam, h_ref, xc):
    r = _sigmoid(r_pre)
    ig = _sigmoid(i_pre)
    nlam = -lam
    softplus = jnp.maximum(nlam, 0.0) + jnp.log1p(jnp.exp(-jnp.abs(nlam)))
    log_a = (-LRU_C * r) * softplus
    a = jnp.exp(log_a)
    bt = jnp.sqrt(jnp.tanh(-log_a) * (1.0 + a * a)) * (ig * xc)

    row8 = lax.broadcasted_iota(jnp.int32, (TM, D_LRU), 0) & (SUBLANES - 1)
    d = 1
    while d < SUBLANES:
        valid = row8 >= d
        a_s = jnp.where(valid, pltpu.roll(a, d, axis=0), 1.0)
        b_s = jnp.where(valid, pltpu.roll(bt, d, axis=0), 0.0)
        bt = bt + a * b_s
        a = a * a_s
        d *= 2

    h_prev = h_ref[...]
    sg = _silu(g_lru)
    outs = []
    for g in range(TM // SUBLANES):
        lo = g * SUBLANES
        hg = bt[lo:lo + SUBLANES] + a[lo:lo + SUBLANES] * h_prev
        h_prev = hg[SUBLANES - 1:SUBLANES, :]
        outs.append(hg * sg[lo:lo + SUBLANES])
    h_ref[...] = h_prev
    return jnp.concatenate(outs, axis=0)


def _inproj_kernel(x_ref, shift_ref, scale_ref, g_ref, we_ref, wg_ref, wk_ref, wt_ref, bf_ref,
                   cw_ref, cb_ref, wa_ref, ba_ref, wx_ref, bx_ref, lam_ref,
                   lng_ref, lnb_ref, ws_ref, sb_ref, he_ref,
                   qt_ref, vt_ref, k_ref, ka_ref, qa_ref, ga_ref, yl_ref, ys_ref,
                   qs_ref, ks_ref, fs_ref, fe_ref,
                   fc_ref, xbuf_ref, h_ref):
    i = pl.program_id(1)

    @pl.when(i == 0)
    def _():
        fc_ref[...] = jnp.zeros(fc_ref.shape, F32)
        xbuf_ref[0:SUBLANES, :] = jnp.zeros((SUBLANES, D_LRU), F32)
        h_ref[...] = jnp.zeros(h_ref.shape, F32)

    x = x_ref[0]
    ms = jnp.mean(x * x, axis=-1, keepdims=True)
    gs = g_ref[0] * (1.0 + _mod_row(scale_ref))
    h = (x * lax.rsqrt(ms + EPS)) * gs + _mod_row(shift_ref)
    hb = h.astype(BF16)

    early = jnp.dot(hb, we_ref[0], preferred_element_type=F32)
    x_lru = early[:, :D_LRU]
    sg_v = early[:, D_LRU:]
    tr = lax.dot_general(wt_ref[0], hb, (((1,), (1,)), ((), ())), preferred_element_type=F32)
    tq = tr[:D_ATT] * Q_SCALE
    qt_ref[0] = tq.astype(BF16)
    vt_ref[0] = tr[D_ATT:2 * D_ATT].astype(BF16)
    fl = tr[2 * D_ATT:2 * D_ATT + N_AUG]
    kb = jnp.dot(hb, wk_ref[0], preferred_element_type=F32).astype(BF16)
    k_ref[0] = kb
    qn2 = jnp.sum((tq * tq).reshape(N_HEADS, HEAD_DIM, TM), axis=1)
    qs_ref[0, 0] = jnp.broadcast_to(jnp.sqrt(jnp.max(qn2, axis=1, keepdims=True)), (N_HEADS, LANES))

    v = _gelu_tanh(sg_v)
    mu = jnp.mean(v, axis=-1, keepdims=True)
    var = jnp.mean(jnp.square(v - mu), axis=-1, keepdims=True)
    vnb = (((v - mu) * lax.rsqrt(var + EPS)) * lng_ref[0] + lnb_ref[0]).astype(BF16)
    w_shape = ws_ref.shape[1:]
    t_idx = lax.broadcasted_iota(jnp.int32, w_shape, 0) & (SG_CHUNK - 1)
    s_idx = lax.broadcasted_iota(jnp.int32, w_shape, 1)
    ws = jnp.where(s_idx <= t_idx, ws_ref[0], jnp.zeros(w_shape, BF16))
    grp = lax.broadcasted_iota(jnp.int32, (SG_CHUNK, D_SG), 1) // HEAD_DIM
    zs = []
    for c in range(TM // SG_CHUNK):
        zz = jnp.dot(ws, vnb[c * SG_CHUNK:(c + 1) * SG_CHUNK], preferred_element_type=F32)
        z = zz[0:SG_CHUNK]
        for gi in range(1, N_SG_GROUPS):
            z = jnp.where(grp == gi, zz[gi * SG_CHUNK:(gi + 1) * SG_CHUNK], z)
        zs.append(z + sb_ref[0])
    z_all = jnp.concatenate(zs, axis=0)

    xbuf_ref[SUBLANES:SUBLANES + TM, :] = x_lru
    xc = cb_ref[0]
    for kk in range(CONV_WIDTH):
        off = SUBLANES - (CONV_WIDTH - 1) + kk
        xc = xc + xbuf_ref[off:off + TM, :] * cw_ref[0, kk:kk + 1, :]
    xbuf_ref[0:SUBLANES, :] = x_lru[TM - SUBLANES:, :]
    xcb = xc.astype(BF16)
    r_pre = jnp.dot(xcb, wa_ref[0], preferred_element_type=F32) + ba_ref[0]
    i_pre = jnp.dot(xcb, wx_ref[0], preferred_element_type=F32) + bx_ref[0]

    gates = jnp.dot(hb, wg_ref[0], preferred_element_type=F32)
    kn2 = jnp.dot(kb * kb, he_ref[...], preferred_element_type=F32)
    ks_ref[0, 0] = jnp.sqrt(jnp.max(kn2, axis=0, keepdims=True))

    ga_ref[0] = gates[:, :D_ATT].astype(BF16)
    g_lru = gates[:, D_ATT:D_ATT + D_LRU]
    sg_u = gates[:, D_ATT + D_LRU:D_ATT + D_LRU + D_SG]
    g_sg = gates[:, D_ATT + D_LRU + D_SG:]
    ys_ref[0] = ((_gelu_tanh(sg_u) * z_all) * _silu(g_sg)).astype(BF16)
    yl_ref[0] = _lru_mixer(g_lru, r_pre, i_pre, lam_ref[0], h_ref, xc).astype(BF16)

    reps = TM // LANES
    ls = _log_sigmoid(fl + jnp.concatenate([bf_ref[0]] * reps, axis=1))
    lane = lax.broadcasted_iota(jnp.int32, (N_AUG, TM), 1)
    d = 1
    while d < TM:
        ls = ls + jnp.where(lane >= d, pltpu.roll(ls, d, axis=1), 0.0)
        d *= 2
    f = ls + jnp.concatenate([fc_ref[...]] * reps, axis=1)
    fc_ref[...] = jnp.broadcast_to(f[:, TM - 1:TM], (N_AUG, LANES))
    row = lax.broadcasted_iota(jnp.int32, (N_AUG, TM), 0)
    f2 = f * LOG2E
    fs_ref[0, 0] = jnp.broadcast_to(f2[:, 0:1], (N_AUG, LANES))
    fe_ref[0, 0] = jnp.broadcast_to(f2[:, TM - 1:TM], (N_AUG, LANES))
    parts = _split3(f2, row % N_PARTS)
    ones = jnp.ones((N_AUG, TM), F32)
    zeros = jnp.zeros((LANES - 2 * N_AUG, TM), F32)
    qa_ref[0] = jnp.concatenate([ones, parts, zeros], axis=0).astype(BF16)
    ka_ref[0] = jnp.concatenate([-parts, ones, zeros], axis=0).T.astype(BF16)


def _inproj(layer, x, mod, prep):
    B, S, D = x.shape
    per_layer = lambda a: pl.BlockSpec((1,) + a.shape[1:], lambda b, i: (layer,) + (0,) * (a.ndim - 1))
    mod_chunk = lambda j: pl.BlockSpec((1, 1, D // TN_ADA, 1, TN_ADA), lambda b, i: (layer, b, j, 0, 0))
    row_tile = lambda n: pl.BlockSpec((1, TM, n), lambda b, i: (b, i, 0))
    col_tile = lambda n: pl.BlockSpec((1, n, TM), lambda b, i: (b, 0, i))
    names = ("pre_g", "we", "wg", "wk", "wt", "bf", "conv_w", "conv_b", "wa", "ba", "wx", "bx", "lam",
             "ln_g", "ln_b", "ws", "sg_bias")
    operands = [prep[n] for n in names]
    head_sum = prep["head_sum"]
    stat = lambda rows: pl.BlockSpec((1, 1, rows, LANES), lambda b, i: (b, i, 0, 0))
    stat_shape = lambda rows: jax.ShapeDtypeStruct((B, S // TM, rows, LANES), F32)
    return pl.pallas_call(
        _inproj_kernel,
        grid=(B, S // TM),
        in_specs=[row_tile(D), mod_chunk(0), mod_chunk(1)] + [per_layer(a) for a in operands]
        + [pl.BlockSpec(head_sum.shape, lambda b, i: (0, 0))],
        out_specs=[
            col_tile(D_ATT), col_tile(D_ATT), row_tile(D_ATT), row_tile(LANES), col_tile(LANES),
            row_tile(D_ATT), row_tile(D_LRU), row_tile(D_SG),
            stat(N_HEADS), stat(1), stat(N_AUG), stat(N_AUG),
        ],
        out_shape=[
            jax.ShapeDtypeStruct((B, D_ATT, S), BF16),
            jax.ShapeDtypeStruct((B, D_ATT, S), BF16),
            jax.ShapeDtypeStruct((B, S, D_ATT), BF16),
            jax.ShapeDtypeStruct((B, S, LANES), BF16),
            jax.ShapeDtypeStruct((B, LANES, S), BF16),
            jax.ShapeDtypeStruct((B, S, D_ATT), BF16),
            jax.ShapeDtypeStruct((B, S, D_LRU), BF16),
            jax.ShapeDtypeStruct((B, S, D_SG), BF16),
            stat_shape(N_HEADS),
            stat_shape(1),
            stat_shape(N_AUG),
            stat_shape(N_AUG),
        ],
        scratch_shapes=[
            pltpu.VMEM((N_AUG, LANES), F32),
            pltpu.VMEM((TM + SUBLANES, D_LRU), F32),
            pltpu.VMEM((1, D_LRU), F32),
        ],
        compiler_params=_params(("parallel", "arbitrary")),
        name="inproj",
    )(x, mod, mod, *operands, head_sum)


def _attn_kernel(seq_ref, noff_ref, qt_ref, qa_ref, k_ref, ka_ref, vt_ref, g_ref,
                 yl_ref, ys_ref, x_ref, gate_ref, pg_ref, w_ref, o_ref,
                 m_ref, acc_ref, rhs_ref, bias_ref, s0_ref, s1_ref, p0_ref, p1_ref,
                 mx0_ref, mx1_ref, al0_ref, al1_ref):
    b = pl.program_id(0)
    i = pl.program_id(1)
    n_tiles = pl.num_programs(1)

    @pl.when((b == 0) & (i == 0))
    def _():
        kk = lax.broadcasted_iota(jnp.int32, (TQ, TQ), 0)
        qq = lax.broadcasted_iota(jnp.int32, (TQ, TQ), 1)
        bias_ref[0] = jnp.zeros((TQ, TQ), F32)
        bias_ref[1] = jnp.where(kk <= qq, 0.0, NEG)
        rhs_ref[N_HEADS] = jnp.zeros(rhs_ref.shape[1:], BF16)
        acc_ref[...] = jnp.zeros(acc_ref.shape, F32)

    qa = qa_ref[0]
    row = lax.broadcasted_iota(jnp.int32, (LANES, TQ), 0)
    for p in range(N_PAIRS):
        qt = qt_ref[0, p * LANES:(p + 1) * LANES, :]
        zero = jnp.zeros_like(qt)
        for hh in range(2):
            own = (row >= hh * HEAD_DIM) & (row < (hh + 1) * HEAD_DIM)
            lo = (2 * p + hh) * N_PARTS
            aug = ((row >= lo) & (row < lo + N_PARTS)) | ((row >= N_AUG + lo) & (row < N_AUG + lo + N_PARTS))
            rhs_ref[2 * p + hh] = jnp.concatenate([jnp.where(own, qt, zero), jnp.where(aug, qa, zero)], axis=0)
    ones_rows = jnp.where(lax.broadcasted_iota(jnp.int32, (BF16_ROWS, TQ), 0) == 0, 1.0, 0.0).astype(BF16)
    m_ref[...] = jnp.full(m_ref.shape, NEG, F32)
    s_bufs = (s0_ref, s1_ref)
    p_bufs = (p0_ref, p1_ref)
    mx_bufs = (mx0_ref, mx1_ref)
    al_bufs = (al0_ref, al1_ref)

    def item(n, hh):
        code = seq_ref[(b * n_tiles + i) * (N_HEADS * n_tiles) + 2 * n + hh]
        return lax.div(code, n_tiles), lax.rem(code, n_tiles)

    HALF = TQ // 2

    def key_lanes(h):
        pair = lax.shift_right_logical(jnp.minimum(h, N_HEADS - 1), 1)
        return pl.ds(pl.multiple_of(pair * LANES, LANES), LANES)

    def scores(n, slot, kind):
        for hh in range(2):
            h, t = item(n, hh)
            if kind == "diag":
                lo_rows = pl.ds(pl.multiple_of(i * TQ, TQ), HALF)
                hi_rows = pl.ds(pl.multiple_of(i * TQ + HALF, HALF), HALF)
                lhs_lo = jnp.concatenate([k_ref[0, lo_rows, key_lanes(h)], ka_ref[0, lo_rows, :]], axis=1)
                lhs_hi = jnp.concatenate([k_ref[0, hi_rows, key_lanes(h)], ka_ref[0, hi_rows, :]], axis=1)
                st_lo = jnp.dot(lhs_lo, rhs_ref[h], preferred_element_type=F32) + bias_ref[1, :HALF, :]
                st_hi = (jnp.dot(lhs_hi, rhs_ref[h, :, HALF:], preferred_element_type=F32)
                         + bias_ref[1, HALF:, HALF:])
                s_bufs[slot][hh, :HALF, :] = st_lo
                s_bufs[slot][hh, HALF:, HALF:] = st_hi
                mx_lo = jnp.max(st_lo, axis=0, keepdims=True)
                mx_hi = jnp.max(st_hi, axis=0, keepdims=True)
                mx_bufs[slot][hh] = jnp.concatenate([mx_lo[:, :HALF], jnp.maximum(mx_lo[:, HALF:], mx_hi)], axis=1)
                continue
            rows = pl.ds(pl.multiple_of(t * TQ, TQ), TQ)
            lhs = jnp.concatenate([k_ref[0, rows, key_lanes(h)], ka_ref[0, rows, :]], axis=1)
            st = jnp.dot(lhs, rhs_ref[h], preferred_element_type=F32)
            if kind == "any":
                st = st + bias_ref[(t == i).astype(jnp.int32)]
            s_bufs[slot][hh] = st
            mx_bufs[slot][hh] = jnp.max(st, axis=0, keepdims=True)

    def softmax(n, slot, kind):
        for hh in range(2):
            h, _ = item(n, hh)
            m_old = m_ref[h]
            m_new = jnp.maximum(m_old, mx_bufs[slot][hh])
            if kind == "diag":
                p_bufs[slot][hh, :HALF, :] = jnp.exp2(s_bufs[slot][hh, :HALF, :] - m_new).astype(BF16)
                p_bufs[slot][hh, HALF:, HALF:] = jnp.exp2(s_bufs[slot][hh, HALF:, HALF:] - m_new[:, HALF:]).astype(BF16)
            else:
                p_bufs[slot][hh] = jnp.exp2(s_bufs[slot][hh] - m_new).astype(BF16)
            al_bufs[slot][hh] = jnp.exp2(m_old - m_new)
            m_ref[h] = m_new

    def values(n, slot, kind):
        for hh in range(2):
            h, t = item(n, hh)
            cols = pl.ds(pl.multiple_of(t * TQ, TQ), TQ)
            head_rows = pl.ds(pl.multiple_of(jnp.minimum(h, N_HEADS - 1) * HEAD_DIM, HEAD_DIM), HEAD_DIM)
            vaug = jnp.concatenate([vt_ref[0, head_rows, cols], ones_rows], axis=0)
            if kind == "diag":
                new = jnp.dot(vaug[:, :HALF], p_bufs[slot][hh, :HALF, :], preferred_element_type=F32)
                new_hi = jnp.dot(vaug[:, HALF:], p_bufs[slot][hh, HALF:, HALF:], preferred_element_type=F32)
                new = jnp.concatenate([new[:, :HALF], new[:, HALF:] + new_hi], axis=1)
            else:
                new = jnp.dot(vaug, p_bufs[slot][hh], preferred_element_type=F32)
            acc_ref[h] = acc_ref[h] * al_bufs[slot][hh] + new

    def full_step(n, parity, kinds):
        softmax(n - 1, 1 - parity, kinds[1])
        scores(n, parity, kinds[0])
        values(n - 2, parity, kinds[2])

    n_off = noff_ref[b * n_tiles + i]
    for kind, wanted in (("off", True), ("any", False)):
        @pl.when((n_off >= 2) == wanted)
        def _():
            scores(0, 0, kind)
            softmax(0, 0, kind)
            scores(1, 1, kind)

    trips = lax.shift_right_logical(jnp.maximum(n_off - 2, 0), 1)

    def body(u, carry):
        full_step(2 + 2 * u, 0, ("off",) * 3)
        full_step(3 + 2 * u, 1, ("off",) * 3)
        return carry

    lax.fori_loop(0, trips, body, 0)
    n0 = 2 + 2 * trips
    for left in range(2, 6):
        kind = lambda j, left=left: "diag" if j >= max(left - N_PAIRS, 0) else "off"

        @pl.when(n_off + N_PAIRS - n0 == left)
        def _():
            for j in range(left):
                full_step(n0 + j, j % 2, (kind(j), kind(j - 1), kind(j - 2)))
            softmax(n0 + left - 1, (left - 1) % 2, kind(left - 1))
            values(n0 + left - 2, left % 2, kind(left - 2))
            values(n0 + left - 1, (left - 1) % 2, kind(left - 1))

    ya = []
    for p in range(N_PAIRS):
        outs = []
        for hh in range(2):
            a = acc_ref[2 * p + hh]
            outs.append(a[:HEAD_DIM] / a[HEAD_DIM:HEAD_DIM + 1])
        o = jnp.concatenate(outs, axis=0).T
        lanes = slice(p * LANES, (p + 1) * LANES)
        ya.append((o * _silu(g_ref[0, :, lanes].astype(F32))).astype(BF16))

    y = jnp.dot(jnp.concatenate(ya, axis=1), w_ref[0, 0:D_ATT, :], preferred_element_type=F32)
    y = y + jnp.dot(yl_ref[0], w_ref[0, D_ATT:D_ATT + D_LRU, :], preferred_element_type=F32)
    y = y + jnp.dot(ys_ref[0], w_ref[0, D_ATT + D_LRU:, :], preferred_element_type=F32)
    ms = jnp.mean(y * y, axis=-1, keepdims=True)
    yn = (y * lax.rsqrt(ms + EPS)) * pg_ref[0]
    o_ref[0] = x_ref[0] + _mod_row(gate_ref) * yn


def _tile_schedule(qs, ks, fs, fe):
    B, n_tiles = qs.shape[:2]
    qmax = qs[:, :, :, 0]
    kmax = ks[:, :, 0, :N_HEADS]
    f_first = fs[:, :, ::N_PARTS, 0]
    f_last = fe[:, :, ::N_PARTS, 0]
    gap = f_first[:, :, None, :] - f_last[:, None, :, :]
    reach = NORM_SLACK * qmax[:, :, None, :] * (kmax[:, None, :, :] + kmax[:, :, None, :])
    negligible = (gap + reach) <= -SKIP_LOG2
    tiles = jnp.arange(n_tiles)
    skip = negligible & (tiles[None, :] < tiles[:, None])[None, :, :, None]
    first = jnp.min(jnp.where(skip, n_tiles, tiles[None, None, :, None]), axis=2)
    first = first.transpose(0, 2, 1)
    visit = (tiles[None, None, None, :] >= first[:, :, :, None]) & (tiles[None, None, None, :] < tiles[None, None, :, None])
    flat = visit.transpose(0, 2, 1, 3).reshape(B, n_tiles, N_HEADS * n_tiles).astype(jnp.int32)
    pos = jnp.cumsum(flat, axis=2) - flat
    n_items = jnp.sum(flat, axis=2)
    n_off = (n_items + 1) // 2
    codes = jnp.arange(N_HEADS * n_tiles)
    slots = jnp.arange(N_HEADS * n_tiles)
    hit = (flat[:, :, None, :] == 1) & (pos[:, :, None, :] == slots[None, None, :, None])
    seq = jnp.sum(jnp.where(hit, codes[None, None, None, :], 0), axis=3)
    seq = jnp.where(slots[None, None, :] == n_items[:, :, None], N_HEADS * n_tiles, seq)
    head = slots[None, None, :] - 2 * n_off[:, :, None]
    seq = jnp.where((head >= 0) & (head < N_HEADS), head * n_tiles + tiles[None, :, None], seq)
    return seq.reshape(-1).astype(jnp.int32), n_off.reshape(-1).astype(jnp.int32)


def _attention_outproj(layer, schedule, qt, qa, k, ka, vt, g_att, yl, ys, x, mod, prep):
    B, S, D = x.shape
    per_layer = lambda a: pl.BlockSpec((1,) + a.shape[1:], lambda b, i, *_: (layer,) + (0,) * (a.ndim - 1))
    row_tile = lambda n: pl.BlockSpec((1, TQ, n), lambda b, i, *_: (b, i, 0))
    grid_spec = pltpu.PrefetchScalarGridSpec(
        num_scalar_prefetch=2,
        grid=(B, S // TQ),
        in_specs=[
            pl.BlockSpec((1, D_ATT, TQ), lambda b, i, *_: (b, 0, i)),
            pl.BlockSpec((1, LANES, TQ), lambda b, i, *_: (b, 0, i)),
            pl.BlockSpec((1, S, D_ATT), lambda b, i, *_: (b, 0, 0)),
            pl.BlockSpec((1, S, LANES), lambda b, i, *_: (b, 0, 0)),
            pl.BlockSpec((1, D_ATT, S), lambda b, i, *_: (b, 0, 0)),
            row_tile(D_ATT), row_tile(D_LRU), row_tile(D_SG), row_tile(D),
            pl.BlockSpec((1, 1, D // TN_ADA, 1, TN_ADA), lambda b, i, *_: (layer, b, 2, 0, 0)),
            per_layer(prep["post_g"]),
            per_layer(prep["w_out"]),
        ],
        out_specs=row_tile(D),
        scratch_shapes=[
            pltpu.VMEM((N_HEADS + 1, 1, TQ), F32),
            pltpu.VMEM((N_HEADS + 1, V_ROWS, TQ), F32),
            pltpu.VMEM((N_HEADS + 1, 2 * LANES, TQ), BF16),
            pltpu.VMEM((2, TQ, TQ), F32),
            pltpu.VMEM((2, TQ, TQ), F32),
            pltpu.VMEM((2, TQ, TQ), F32),
            pltpu.VMEM((2, TQ, TQ), BF16),
            pltpu.VMEM((2, TQ, TQ), BF16),
            pltpu.VMEM((2, 1, TQ), F32),
            pltpu.VMEM((2, 1, TQ), F32),
            pltpu.VMEM((2, 1, TQ), F32),
            pltpu.VMEM((2, 1, TQ), F32),
        ],
    )
    return pl.pallas_call(
        _attn_kernel,
        grid_spec=grid_spec,
        out_shape=jax.ShapeDtypeStruct((B, S, D), F32),
        compiler_params=_params(("arbitrary", "arbitrary")),
        name="fox_attention_outproj",
    )(*schedule, qt, qa, k, ka, vt, g_att, yl, ys, x, mod, prep["post_g"], prep["w_out"])


def _block_diag(w):
    L, G, n, _ = w.shape
    eye = jnp.eye(G, dtype=w.dtype)
    return (w[:, :, :, None, :] * eye[None, :, None, :, None]).reshape(L, G * n, G * n)


def _prepare(pre_g, post_g, w_in, b_f, conv_w, conv_b, lru_wa, lru_ba, lru_wx, lru_bx, lru_lambda,
             sg_ln_g, sg_ln_b, sg_w, sg_b, w_out):
    L, D, _ = w_in.shape
    cuts = [0]
    for n in IN_SIZES:
        cuts.append(cuts[-1] + n)
    wq, wk, wv, wf, wga, wxl, wgl, wsu, wsv, wgs = (w_in[:, :, cuts[j]:cuts[j + 1]] for j in range(10))
    row = lambda a: a[:, None, :]
    wf3 = jnp.repeat(wf, N_PARTS, axis=2)
    pad = jnp.zeros((L, D, T_ROWS - 2 * D_ATT - N_AUG), F32)
    return dict(
        pre_g=row(pre_g), post_g=row(post_g),
        we=jnp.concatenate([wxl, wsv], axis=2).astype(BF16),
        wg=jnp.concatenate([wga, wgl, wsu, wgs], axis=2).astype(BF16),
        wk=wk.astype(BF16),
        wt=jnp.concatenate([wq, wv, wf3, pad], axis=2).transpose(0, 2, 1).astype(BF16),
        bf=jnp.broadcast_to(jnp.repeat(b_f, N_PARTS, axis=1)[:, :, None], (L, N_AUG, LANES)),
        conv_w=conv_w, conv_b=row(conv_b),
        wa=_block_diag(lru_wa).astype(BF16), ba=row(lru_ba),
        wx=_block_diag(lru_wx).astype(BF16), bx=row(lru_bx), lam=row(lru_lambda),
        ln_g=row(sg_ln_g), ln_b=row(sg_ln_b),
        ws=sg_w.reshape(L, N_SG_GROUPS * SG_CHUNK, SG_CHUNK).astype(BF16),
        sg_bias=jnp.repeat(sg_b.transpose(0, 2, 1), HEAD_DIM, axis=2),
        w_out=w_out.astype(BF16),
        head_sum=(jnp.arange(D_ATT)[:, None] // HEAD_DIM == jnp.arange(LANES)[None, :]).astype(BF16),
    )


def kernel(x, c, ada_w, ada_b, pre_g, post_g, w_in, b_f, conv_w, conv_b, lru_wa, lru_ba, lru_wx,
           lru_bx, lru_lambda, sg_ln_g, sg_ln_b, sg_w, sg_b, w_out):
    mod = _ada(c, ada_w, ada_b)
    prep = _prepare(pre_g, post_g, w_in, b_f, conv_w, conv_b, lru_wa, lru_ba, lru_wx, lru_bx,
                    lru_lambda, sg_ln_g, sg_ln_b, sg_w, sg_b, w_out)
    for layer in range(ada_w.shape[0]):
        qt, vt, k, ka, qa, ga, yl, ys, qs, ks, fs, fe = _inproj(layer, x, mod, prep)
        x = _attention_outproj(layer, _tile_schedule(qs, ks, fs, fe), qt, qa, k, ka, vt, ga, yl, ys, x, mod, prep)
    return x
```

```python
import math

import jax
import jax.numpy as jnp
from jax import lax
from jax.experimental import pallas as pl
from jax.experimental.pallas import tpu as pltpu

D_MODEL = 1024
HEAD_DIM = 64
D_ATT = 512
D_LRU = 256
D_SG = 256
N_HEADS = D_ATT // HEAD_DIM
N_PAIRS = N_HEADS // 2
N_LRU_BLOCKS = D_LRU // HEAD_DIM
N_SG_GROUPS = D_SG // HEAD_DIM
SG_CHUNK = 128
CONV_WIDTH = 4
LRU_C = 8.0
EPS = 1e-6
IN_SIZES = (D_ATT, D_ATT, D_ATT, N_HEADS, D_ATT, D_LRU, D_LRU, D_SG, D_SG, D_SG)

LANES = 128
SUBLANES = 8
BF16_ROWS = 16
VMEM_LIMIT = 56 * 1024 * 1024

TM = 512
TQ = 512
TN_ADA = 1024
NEG = -1e30
assert TM == TQ

N_PARTS = 3
N_AUG = N_HEADS * N_PARTS
T_ROWS = 2 * D_ATT + 2 * BF16_ROWS
LOG2E = 1.4426950408889634
Q_SCALE = LOG2E / math.sqrt(HEAD_DIM)
V_ROWS = HEAD_DIM + BF16_ROWS
SKIP_LOG2 = 160.0
NORM_SLACK = 1.02
GELU_C0 = math.sqrt(2.0 / math.pi)
GELU_C1 = GELU_C0 * 0.044715

F32 = jnp.float32
BF16 = jnp.bfloat16


def _sigmoid(x):
    return 0.5 * jnp.tanh(0.5 * x) + 0.5


def _silu(x):
    hx = 0.5 * x
    return hx + hx * jnp.tanh(hx)


def _gelu_tanh(x):
    hx = 0.5 * x
    return hx + hx * jnp.tanh(x * (GELU_C0 + GELU_C1 * (x * x)))


def _log_sigmoid(x):
    return jnp.minimum(x, 0.0) - jnp.log1p(jnp.exp(-jnp.abs(x)))


def _split3(x, sel):
    hi = x.astype(BF16).astype(F32)
    r1 = x - hi
    mid = r1.astype(BF16).astype(F32)
    lo = r1 - mid
    return jnp.where(sel == 0, hi, jnp.where(sel == 1, mid, lo))


def _mod_row(ref):
    return jnp.concatenate([ref[0, 0, j] for j in range(ref.shape[2])], axis=1)


def _params(sem):
    return pltpu.CompilerParams(dimension_semantics=sem, vmem_limit_bytes=VMEM_LIMIT)


def _ada_kernel(ct_ref, w_ref, b_ref, o_ref):
    ct = ct_ref[...]
    sc = _silu(ct)
    w = w_ref[0]
    bias = b_ref[0]
    for b in range(ct.shape[1]):
        col = sc[:, b:b + 1]
        o_ref[0, b, 0] = jnp.sum(col * w, axis=0, keepdims=True) + bias


def _ada(c, ada_w, ada_b):
    L, D, N = ada_w.shape
    B = c.shape[0]
    return pl.pallas_call(
        _ada_kernel,
        grid=(L, N // TN_ADA),
        in_specs=[
            pl.BlockSpec((D, B), lambda l, n: (0, 0)),
            pl.BlockSpec((1, D, TN_ADA), lambda l, n: (l, 0, n)),
            pl.BlockSpec((1, 1, TN_ADA), lambda l, n: (l, 0, n)),
        ],
        out_specs=pl.BlockSpec((1, B, 1, 1, TN_ADA), lambda l, n: (l, 0, n, 0, 0)),
        out_shape=jax.ShapeDtypeStruct((L, B, N // TN_ADA, 1, TN_ADA), F32),
        compiler_params=_params(("parallel", "parallel")),
        name="ada_mod",
    )(c.T, ada_w, ada_b.reshape(L, 1, N))


def _lru_mixer(g_lru, r_pre, i_pre, lam, h_ref, xc):
    r = _sigmoid(r_pre)
    ig = _sigmoid(i_pre)
    nlam = -lam
    softplus = jnp.maximum(nlam, 0.0) + jnp.log1p(jnp.exp(-jnp.abs(nlam)))
    log_a = (-LRU_C * r) * softplus
    a = jnp.exp(log_a)
    bt = jnp.sqrt(jnp.tanh(-log_a) * (1.0 + a * a)) * (ig * xc)

    row8 = lax.broadcasted_iota(jnp.int32, (TM, D_LRU), 0) & (SUBLANES - 1)
    d = 1
    while d < SUBLANES:
        valid = row8 >= d
        a_s = jnp.where(valid, pltpu.roll(a, d, axis=0), 1.0)
        b_s = jnp.where(valid, pltpu.roll(bt, d, axis=0), 0.0)
        bt = bt + a * b_s
        a = a * a_s
        d *= 2

    h_prev = h_ref[...]
    sg = _silu(g_lru)
    outs = []
    for g in range(TM // SUBLANES):
        lo = g * SUBLANES
        hg = bt[lo:lo + SUBLANES] + a[lo:lo + SUBLANES] * h_prev
        h_prev = hg[SUBLANES - 1:SUBLANES, :]
        outs.append(hg * sg[lo:lo + SUBLANES])
    h_ref[...] = h_prev
    return jnp.concatenate(outs, axis=0)


def _inproj_kernel(x_ref, shift_ref, scale_ref, g_ref, we_ref, wg_ref, wk_ref, wt_ref, bf_ref,
                   cw_ref, cb_ref, wa_ref, ba_ref, wx_ref, bx_ref, lam_ref,
                   lng_ref, lnb_ref, ws_ref, sb_ref, he_ref,
                   qt_ref, vt_ref, k_ref, ka_ref, qa_ref, ga_ref, yl_ref, ys_ref,
                   qs_ref, ks_ref, fs_ref, fe_ref,
                   fc_ref, xbuf_ref, h_ref):
    i = pl.program_id(1)

    @pl.when(i == 0)
    def _():
        fc_ref[...] = jnp.zeros(fc_ref.shape, F32)
        xbuf_ref[0:SUBLANES, :] = jnp.zeros((SUBLANES, D_LRU), F32)
        h_ref[...] = jnp.zeros(h_ref.shape, F32)

    x = x_ref[0]
    ms = jnp.mean(x * x, axis=-1, keepdims=True)
    gs = g_ref[0] * (1.0 + _mod_row(scale_ref))
    h = (x * lax.rsqrt(ms + EPS)) * gs + _mod_row(shift_ref)
    hb = h.astype(BF16)

    early = jnp.dot(hb, we_ref[0], preferred_element_type=F32)
    x_lru = early[:, :D_LRU]
    sg_v = early[:, D_LRU:]
    tr = lax.dot_general(wt_ref[0], hb, (((1,), (1,)), ((), ())), preferred_element_type=F32)
    tq = tr[:D_ATT] * Q_SCALE
    qt_ref[0] = tq.astype(BF16)
    vt_ref[0] = tr[D_ATT:2 * D_ATT].astype(BF16)
    fl = tr[2 * D_ATT:2 * D_ATT + N_AUG]
    kb = jnp.dot(hb, wk_ref[0], preferred_element_type=F32).astype(BF16)
    k_ref[0] = kb
    qn2 = jnp.sum((tq * tq).reshape(N_HEADS, HEAD_DIM, TM), axis=1)
    qs_ref[0, 0] = jnp.broadcast_to(jnp.sqrt(jnp.max(qn2, axis=1, keepdims=True)), (N_HEADS, LANES))

    v = _gelu_tanh(sg_v)
    mu = jnp.mean(v, axis=-1, keepdims=True)
    var = jnp.mean(jnp.square(v - mu), axis=-1, keepdims=True)
    vnb = (((v - mu) * lax.rsqrt(var + EPS)) * lng_ref[0] + lnb_ref[0]).astype(BF16)
    w_shape = ws_ref.shape[1:]
    t_idx = lax.broadcasted_iota(jnp.int32, w_shape, 0) & (SG_CHUNK - 1)
    s_idx = lax.broadcasted_iota(jnp.int32, w_shape, 1)
    ws = jnp.where(s_idx <= t_idx, ws_ref[0], jnp.zeros(w_shape, BF16))
    grp = lax.broadcasted_iota(jnp.int32, (SG_CHUNK, D_SG), 1) // HEAD_DIM
    zs = []
    for c in range(TM // SG_CHUNK):
        zz = jnp.dot(ws, vnb[c * SG_CHUNK:(c + 1) * SG_CHUNK], preferred_element_type=F32)
        z = zz[0:SG_CHUNK]
        for gi in range(1, N_SG_GROUPS):
            z = jnp.where(grp == gi, zz[gi * SG_CHUNK:(gi + 1) * SG_CHUNK], z)
        zs.append(z + sb_ref[0])
    z_all = jnp.concatenate(zs, axis=0)

    xbuf_ref[SUBLANES:SUBLANES + TM, :] = x_lru
    xc = cb_ref[0]
    for kk in range(CONV_WIDTH):
        off = SUBLANES - (CONV_WIDTH - 1) + kk
        xc = xc + xbuf_ref[off:off + TM, :] * cw_ref[0, kk:kk + 1, :]
    xbuf_ref[0:SUBLANES, :] = x_lru[TM - SUBLANES:, :]
    xcb = xc.astype(BF16)
    r_pre = jnp.dot(xcb, wa_ref[0], preferred_element_type=F32) + ba_ref[0]
    i_pre = jnp.dot(xcb, wx_ref[0], preferred_element_type=F32) + bx_ref[0]

    gates = jnp.dot(hb, wg_ref[0], preferred_element_type=F32)
    kn2 = jnp.dot(kb * kb, he_ref[...], preferred_element_type=F32)
    ks_ref[0, 0] = jnp.sqrt(jnp.max(kn2, axis=0, keepdims=True))

    ga_ref[0] = gates[:, :D_ATT].astype(BF16)
    g_lru = gates[:, D_ATT:D_ATT + D_LRU]
    sg_u = gates[:, D_ATT + D_LRU:D_ATT + D_LRU + D_SG]
    g_sg = gates[:, D_ATT + D_LRU + D_SG:]
    ys_ref[0] = ((_gelu_tanh(sg_u) * z_all) * _silu(g_sg)).astype(BF16)
    yl_ref[0] = _lru_mixer(g_lru, r_pre, i_pre, lam_ref[0], h_ref, xc).astype(BF16)

    reps = TM // LANES
    ls = _log_sigmoid(fl + jnp.concatenate([bf_ref[0]] * reps, axis=1))
    lane = lax.broadcasted_iota(jnp.int32, (N_AUG, TM), 1)
    d = 1
    while d < TM:
        ls = ls + jnp.where(lane >= d, pltpu.roll(ls, d, axis=1), 0.0)
        d *= 2
    f = ls + jnp.concatenate([fc_ref[...]] * reps, axis=1)
    fc_ref[...] = jnp.broadcast_to(f[:, TM - 1:TM], (N_AUG, LANES))
    row = lax.broadcasted_iota(jnp.int32, (N_AUG, TM), 0)
    f2 = f * LOG2E
    fs_ref[0, 0] = jnp.broadcast_to(f2[:, 0:1], (N_AUG, LANES))
    fe_ref[0, 0] = jnp.broadcast_to(f2[:, TM - 1:TM], (N_AUG, LANES))
    parts = _split3(f2, row % N_PARTS)
    ones = jnp.ones((N_AUG, TM), F32)
    zeros = jnp.zeros((LANES - 2 * N_AUG, TM), F32)
    qa_ref[0] = jnp.concatenate([ones, parts, zeros], axis=0).astype(BF16)
    ka_ref[0] = jnp.concatenate([-parts, ones, zeros], axis=0).T.astype(BF16)


def _inproj(layer, x, mod, prep):
    B, S, D = x.shape
    per_layer = lambda a: pl.BlockSpec((1,) + a.shape[1:], lambda b, i: (layer,) + (0,) * (a.ndim - 1))
    mod_chunk = lambda j: pl.BlockSpec((1, 1, D // TN_ADA, 1, TN_ADA), lambda b, i: (layer, b, j, 0, 0))
    row_tile = lambda n: pl.BlockSpec((1, TM, n), lambda b, i: (b, i, 0))
    col_tile = lambda n: pl.BlockSpec((1, n, TM), lambda b, i: (b, 0, i))
    names = ("pre_g", "we", "wg", "wk", "wt", "bf", "conv_w", "conv_b", "wa", "ba", "wx", "bx", "lam",
             "ln_g", "ln_b", "ws", "sg_bias")
    operands = [prep[n] for n in names]
    head_sum = prep["head_sum"]
    stat = lambda rows: pl.BlockSpec((1, 1, rows, LANES), lambda b, i: (b, i, 0, 0))
    stat_shape = lambda rows: jax.ShapeDtypeStruct((B, S // TM, rows, LANES), F32)
    return pl.pallas_call(
        _inproj_kernel,
        grid=(B, S // TM),
        in_specs=[row_tile(D), mod_chunk(0), mod_chunk(1)] + [per_layer(a) for a in operands]
        + [pl.BlockSpec(head_sum.shape, lambda b, i: (0, 0))],
        out_specs=[
            col_tile(D_ATT), col_tile(D_ATT), row_tile(D_ATT), row_tile(LANES), col_tile(LANES),
            row_tile(D_ATT), row_tile(D_LRU), row_tile(D_SG),
            stat(N_HEADS), stat(1), stat(N_AUG), stat(N_AUG),
        ],
        out_shape=[
            jax.ShapeDtypeStruct((B, D_ATT, S), BF16),
            jax.ShapeDtypeStruct((B, D_ATT, S), BF16),
            jax.ShapeDtypeStruct((B, S, D_ATT), BF16),
            jax.ShapeDtypeStruct((B, S, LANES), BF16),
            jax.ShapeDtypeStruct((B, LANES, S), BF16),
            jax.ShapeDtypeStruct((B, S, D_ATT), BF16),
            jax.ShapeDtypeStruct((B, S, D_LRU), BF16),
            jax.ShapeDtypeStruct((B, S, D_SG), BF16),
            stat_shape(N_HEADS),
            stat_shape(1),
            stat_shape(N_AUG),
            stat_shape(N_AUG),
        ],
        scratch_shapes=[
            pltpu.VMEM((N_AUG, LANES), F32),
            pltpu.VMEM((TM + SUBLANES, D_LRU), F32),
            pltpu.VMEM((1, D_LRU), F32),
        ],
        compiler_params=_params(("parallel", "arbitrary")),
        name="inproj",
    )(x, mod, mod, *operands, head_sum)


def _attn_kernel(seq_ref, noff_ref, qt_ref, qa_ref, k_ref, ka_ref, vt_ref, g_ref,
                 yl_ref, ys_ref, x_ref, gate_ref, pg_ref, w_ref, o_ref,
                 m_ref, acc_ref, rhs_ref, bias_ref, s0_ref, s1_ref, p0_ref, p1_ref,
                 mx0_ref, mx1_ref, al0_ref, al1_ref):
    b = pl.program_id(0)
    i = pl.program_id(1)
    n_tiles = pl.num_programs(1)

    @pl.when((b == 0) & (i == 0))
    def _():
        kk = lax.broadcasted_iota(jnp.int32, (TQ, TQ), 0)
        qq = lax.broadcasted_iota(jnp.int32, (TQ, TQ), 1)
        bias_ref[0] = jnp.zeros((TQ, TQ), F32)
        bias_ref[1] = jnp.where(kk <= qq, 0.0, NEG)
        rhs_ref[N_HEADS] = jnp.zeros(rhs_ref.shape[1:], BF16)
        acc_ref[...] = jnp.zeros(acc_ref.shape, F32)

    qa = qa_ref[0]
    row = lax.broadcasted_iota(jnp.int32, (LANES, TQ), 0)
    for p in range(N_PAIRS):
        qt = qt_ref[0, p * LANES:(p + 1) * LANES, :]
        zero = jnp.zeros_like(qt)
        for hh in range(2):
            own = (row >= hh * HEAD_DIM) & (row < (hh + 1) * HEAD_DIM)
            lo = (2 * p + hh) * N_PARTS
            aug = ((row >= lo) & (row < lo + N_PARTS)) | ((row >= N_AUG + lo) & (row < N_AUG + lo + N_PARTS))
            rhs_ref[2 * p + hh] = jnp.concatenate([jnp.where(own, qt, zero), jnp.where(aug, qa, zero)], axis=0)
    ones_rows = jnp.where(lax.broadcasted_iota(jnp.int32, (BF16_ROWS, TQ), 0) == 0, 1.0, 0.0).astype(BF16)
    m_ref[...] = jnp.full(m_ref.shape, NEG, F32)
    s_bufs = (s0_ref, s1_ref)
    p_bufs = (p0_ref, p1_ref)
    mx_bufs = (mx0_ref, mx1_ref)
    al_bufs = (al0_ref, al1_ref)

    def item(n, hh):
        code = seq_ref[(b * n_tiles + i) * (N_HEADS * n_tiles) + 2 * n + hh]
        return lax.div(code, n_tiles), lax.rem(code, n_tiles)

    HALF = TQ // 2

    def key_lanes(h):
        pair = lax.shift_right_logical(jnp.minimum(h, N_HEADS - 1), 1)
        return pl.ds(pl.multiple_of(pair * LANES, LANES), LANES)

    def scores(n, slot, kind):
        for hh in range(2):
            h, t = item(n, hh)
            if kind == "diag":
                lo_rows = pl.ds(pl.multiple_of(i * TQ, TQ), HALF)
                hi_rows = pl.ds(pl.multiple_of(i * TQ + HALF, HALF), HALF)
                lhs_lo = jnp.concatenate([k_ref[0, lo_rows, key_lanes(h)], ka_ref[0, lo_rows, :]], axis=1)
                lhs_hi = jnp.concatenate([k_ref[0, hi_rows, key_lanes(h)], ka_ref[0, hi_rows, :]], axis=1)
                st_lo = jnp.dot(lhs_lo, rhs_ref[h], preferred_element_type=F32) + bias_ref[1, :HALF, :]
                st_hi = (jnp.dot(lhs_hi, rhs_ref[h, :, HALF:], preferred_element_type=F32)
                         + bias_ref[1, HALF:, HALF:])
                s_bufs[slot][hh, :HALF, :] = st_lo
                s_bufs[slot][hh, HALF:, HALF:] = st_hi
                mx_lo = jnp.max(st_lo, axis=0, keepdims=True)
                mx_hi = jnp.max(st_hi, axis=0, keepdims=True)
                mx_bufs[slot][hh] = jnp.concatenate([mx_lo[:, :HALF], jnp.maximum(mx_lo[:, HALF:], mx_hi)], axis=1)
                continue
            rows = pl.ds(pl.multiple_of(t * TQ, TQ), TQ)
            lhs = jnp.concatenate([k_ref[0, rows, key_lanes(h)], ka_ref[0, rows, :]], axis=1)
            st = jnp.dot(lhs, rhs_ref[h], preferred_element_type=F32)
            if kind == "any":
                st = st + bias_ref[(t == i).astype(jnp.int32)]
            s_bufs[slot][hh] = st
            mx_bufs[slot][hh] = jnp.max(st, axis=0, keepdims=True)

    def softmax(n, slot, kind):
        for hh in range(2):
            h, _ = item(n, hh)
            m_old = m_ref[h]
            m_new = jnp.maximum(m_old, mx_bufs[slot][hh])
            if kind == "diag":
                p_bufs[slot][hh, :HALF, :] = jnp.exp2(s_bufs[slot][hh, :HALF, :] - m_new).astype(BF16)
                p_bufs[slot][hh, HALF:, HALF:] = jnp.exp2(s_bufs[slot][hh, HALF:, HALF:] - m_new[:, HALF:]).astype(BF16)
            else:
                p_bufs[slot][hh] = jnp.exp2(s_bufs[slot][hh] - m_new).astype(BF16)
            al_bufs[slot][hh] = jnp.exp2(m_old - m_new)
            m_ref[h] = m_new

    def values(n, slot, kind):
        for hh in range(2):
            h, t = item(n, hh)
            cols = pl.ds(pl.multiple_of(t * TQ, TQ), TQ)
            head_rows = pl.ds(pl.multiple_of(jnp.minimum(h, N_HEADS - 1) * HEAD_DIM, HEAD_DIM), HEAD_DIM)
            vaug = jnp.concatenate([vt_ref[0, head_rows, cols], ones_rows], axis=0)
            if kind == "diag":
                new = jnp.dot(vaug[:, :HALF], p_bufs[slot][hh, :HALF, :], preferred_element_type=F32)
                new_hi = jnp.dot(vaug[:, HALF:], p_bufs[slot][hh, HALF:, HALF:], preferred_element_type=F32)
                new = jnp.concatenate([new[:, :HALF], new[:, HALF:] + new_hi], axis=1)
            else:
                new = jnp.dot(vaug, p_bufs[slot][hh], preferred_element_type=F32)
            acc_ref[h] = acc_ref[h] * al_bufs[slot][hh] + new

    def full_step(n, parity, kinds):
        softmax(n - 1, 1 - parity, kinds[1])
        scores(n, parity, kinds[0])
        values(n - 2, parity, kinds[2])

    n_off = noff_ref[b * n_tiles + i]
    for kind, wanted in (("off", True), ("any", False)):
        @pl.when((n_off >= 2) == wanted)
        def _():
            scores(0, 0, kind)
            softmax(0, 0, kind)
            scores(1, 1, kind)

    trips = lax.shift_right_logical(jnp.maximum(n_off - 2, 0), 1)

    def body(u, carry):
        full_step(2 + 2 * u, 0, ("off",) * 3)
        full_step(3 + 2 * u, 1, ("off",) * 3)
        return carry

    lax.fori_loop(0, trips, body, 0)
    n0 = 2 + 2 * trips
    for left in range(2, 6):
        kind = lambda j, left=left: "diag" if j >= max(left - N_PAIRS, 0) else "off"

        @pl.when(n_off + N_PAIRS - n0 == left)
        def _():
            for j in range(left):
                full_step(n0 + j, j % 2, (kind(j), kind(j - 1), kind(j - 2)))
            softmax(n0 + left - 1, (left - 1) % 2, kind(left - 1))
            values(n0 + left - 2, left % 2, kind(left - 2))
            values(n0 + left - 1, (left - 1) % 2, kind(left - 1))

    ya = []
    for p in range(N_PAIRS):
        outs = []
        for hh in range(2):
            a = acc_ref[2 * p + hh]
            outs.append(a[:HEAD_DIM] / a[HEAD_DIM:HEAD_DIM + 1])
        o = jnp.concatenate(outs, axis=0).T
        lanes = slice(p * LANES, (p + 1) * LANES)
        ya.append((o * _silu(g_ref[0, :, lanes].astype(F32))).astype(BF16))

    y = jnp.dot(jnp.concatenate(ya, axis=1), w_ref[0, 0:D_ATT, :], preferred_element_type=F32)
    y = y + jnp.dot(yl_ref[0], w_ref[0, D_ATT:D_ATT + D_LRU, :], preferred_element_type=F32)
    y = y + jnp.dot(ys_ref[0], w_ref[0, D_ATT + D_LRU:, :], preferred_element_type=F32)
    ms = jnp.mean(y * y, axis=-1, keepdims=True)
    yn = (y * lax.rsqrt(ms + EPS)) * pg_ref[0]
    o_ref[0] = x_ref[0] + _mod_row(gate_ref) * yn


def _tile_schedule(qs, ks, fs, fe):
    B, n_tiles = qs.shape[:2]
    qmax = qs[:, :, :, 0]
    kmax = ks[:, :, 0, :N_HEADS]
    f_first = fs[:, :, ::N_PARTS, 0]
    f_last = fe[:, :, ::N_PARTS, 0]
    gap = f_first[:, :, None, :] - f_last[:, None, :, :]
    reach = NORM_SLACK * qmax[:, :, None, :] * (kmax[:, None, :, :] + kmax[:, :, None, :])
    negligible = (gap + reach) <= -SKIP_LOG2
    tiles = jnp.arange(n_tiles)
    skip = negligible & (tiles[None, :] < tiles[:, None])[None, :, :, None]
    first = jnp.min(jnp.where(skip, n_tiles, tiles[None, None, :, None]), axis=2)
    first = first.transpose(0, 2, 1)
    visit = (tiles[None, None, None, :] >= first[:, :, :, None]) & (tiles[None, None, None, :] < tiles[None, None, :, None])
    flat = visit.transpose(0, 2, 1, 3).reshape(B, n_tiles, N_HEADS * n_tiles).astype(jnp.int32)
    pos = jnp.cumsum(flat, axis=2) - flat
    n_items = jnp.sum(flat, axis=2)
    n_off = (n_items + 1) // 2
    codes = jnp.arange(N_HEADS * n_tiles)
    slots = jnp.arange(N_HEADS * n_tiles)
    hit = (flat[:, :, None, :] == 1) & (pos[:, :, None, :] == slots[None, None, :, None])
    seq = jnp.sum(jnp.where(hit, codes[None, None, None, :], 0), axis=3)
    seq = jnp.where(slots[None, None, :] == n_items[:, :, None], N_HEADS * n_tiles, seq)
    head = slots[None, None, :] - 2 * n_off[:, :, None]
    seq = jnp.where((head >= 0) & (head < N_HEADS), head * n_tiles + tiles[None, :, None], seq)
    return seq.reshape(-1).astype(jnp.int32), n_off.reshape(-1).astype(jnp.int32)


def _attention_outproj(layer, schedule, qt, qa, k, ka, vt, g_att, yl, ys, x, mod, prep):
    B, S, D = x.shape
    per_layer = lambda a: pl.BlockSpec((1,) + a.shape[1:], lambda b, i, *_: (layer,) + (0,) * (a.ndim - 1))
    row_tile = lambda n: pl.BlockSpec((1, TQ, n), lambda b, i, *_: (b, i, 0))
    grid_spec = pltpu.PrefetchScalarGridSpec(
        num_scalar_prefetch=2,
        grid=(B, S // TQ),
        in_specs=[
            pl.BlockSpec((1, D_ATT, TQ), lambda b, i, *_: (b, 0, i)),
            pl.BlockSpec((1, LANES, TQ), lambda b, i, *_: (b, 0, i)),
            pl.BlockSpec((1, S, D_ATT), lambda b, i, *_: (b, 0, 0)),
            pl.BlockSpec((1, S, LANES), lambda b, i, *_: (b, 0, 0)),
            pl.BlockSpec((1, D_ATT, S), lambda b, i, *_: (b, 0, 0)),
            row_tile(D_ATT), row_tile(D_LRU), row_tile(D_SG), row_tile(D),
            pl.BlockSpec((1, 1, D // TN_ADA, 1, TN_ADA), lambda b, i, *_: (layer, b, 2, 0, 0)),
            per_layer(prep["post_g"]),
            per_layer(prep["w_out"]),
        ],
        out_specs=row_tile(D),
        scratch_shapes=[
            pltpu.VMEM((N_HEADS + 1, 1, TQ), F32),
            pltpu.VMEM((N_HEADS + 1, V_ROWS, TQ), F32),
            pltpu.VMEM((N_HEADS + 1, 2 * LANES, TQ), BF16),
            pltpu.VMEM((2, TQ, TQ), F32),
            pltpu.VMEM((2, TQ, TQ), F32),
            pltpu.VMEM((2, TQ, TQ), F32),
            pltpu.VMEM((2, TQ, TQ), BF16),
            pltpu.VMEM((2, TQ, TQ), BF16),
            pltpu.VMEM((2, 1, TQ), F32),
            pltpu.VMEM((2, 1, TQ), F32),
            pltpu.VMEM((2, 1, TQ), F32),
            pltpu.VMEM((2, 1, TQ), F32),
        ],
    )
    return pl.pallas_call(
        _attn_kernel,
        grid_spec=grid_spec,
        out_shape=jax.ShapeDtypeStruct((B, S, D), F32),
        compiler_params=_params(("arbitrary", "arbitrary")),
        name="fox_attention_outproj",
    )(*schedule, qt, qa, k, ka, vt, g_att, yl, ys, x, mod, prep["post_g"], prep["w_out"])


def _block_diag(w):
    L, G, n, _ = w.shape
    eye = jnp.eye(G, dtype=w.dtype)
    return (w[:, :, :, None, :] * eye[None, :, None, :, None]).reshape(L, G * n, G * n)


def _prepare(pre_g, post_g, w_in, b_f, conv_w, conv_b, lru_wa, lru_ba, lru_wx, lru_bx, lru_lambda,
             sg_ln_g, sg_ln_b, sg_w, sg_b, w_out):
    L, D, _ = w_in.shape
    cuts = [0]
    for n in IN_SIZES:
        cuts.append(cuts[-1] + n)
    w_bf = w_in.astype(BF16)
    wq, wk, wv, wf, wga, wxl, wgl, wsu, wsv, wgs = (w_bf[:, :, cuts[j]:cuts[j + 1]] for j in range(10))
    row = lambda a: a[:, None, :]
    wf3 = jnp.repeat(wf, N_PARTS, axis=2)
    pad = jnp.zeros((L, D, T_ROWS - 2 * D_ATT - N_AUG), BF16)
    return dict(
        pre_g=row(pre_g), post_g=row(post_g),
        we=jnp.concatenate([wxl, wsv], axis=2),
        wg=jnp.concatenate([wga, wgl, wsu, wgs], axis=2),
        wk=wk,
        wt=jnp.concatenate([wq, wv, wf3, pad], axis=2).transpose(0, 2, 1),
        bf=jnp.broadcast_to(jnp.repeat(b_f, N_PARTS, axis=1)[:, :, None], (L, N_AUG, LANES)),
        conv_w=conv_w, conv_b=row(conv_b),
        wa=_block_diag(lru_wa).astype(BF16), ba=row(lru_ba),
        wx=_block_diag(lru_wx).astype(BF16), bx=row(lru_bx), lam=row(lru_lambda),
        ln_g=row(sg_ln_g), ln_b=row(sg_ln_b),
        ws=sg_w.reshape(L, N_SG_GROUPS * SG_CHUNK, SG_CHUNK).astype(BF16),
        sg_bias=jnp.repeat(sg_b.transpose(0, 2, 1), HEAD_DIM, axis=2),
        w_out=w_out.astype(BF16),
        head_sum=(jnp.arange(D_ATT)[:, None] // HEAD_DIM == jnp.arange(LANES)[None, :]).astype(BF16),
    )


def kernel(x, c, ada_w, ada_b, pre_g, post_g, w_in, b_f, conv_w, conv_b, lru_wa, lru_ba, lru_wx,
           lru_bx, lru_lambda, sg_ln_g, sg_ln_b, sg_w, sg_b, w_out):
    mod = _ada(c, ada_w, ada_b)
    prep = _prepare(pre_g, post_g, w_in, b_f, conv_w, conv_b, lru_wa, lru_ba, lru_wx, lru_bx,
                    lru_lambda, sg_ln_g, sg_ln_b, sg_w, sg_b, w_out)
    for layer in range(ada_w.shape[0]):
        qt, vt, k, ka, qa, ga, yl, ys, qs, ks, fs, fe = _inproj(layer, x, mod, prep)
        x = _attention_outproj(layer, _tile_schedule(qs, ks, fs, fe), qt, qa, k, ka, vt, ga, yl, ys, x, mod, prep)
    return x
```

```python
import math

import jax
import jax.numpy as jnp
from jax import lax
from jax.experimental import pallas as pl
from jax.experimental.pallas import tpu as pltpu

HEAD_DIM = 64
D_ATT = 512
D_LRU = 256
D_SG = 256
N_HEADS = D_ATT // HEAD_DIM
N_PAIRS = N_HEADS // 2
N_SG_GROUPS = D_SG // HEAD_DIM
SG_CHUNK = 128
CONV_WIDTH = 4
LRU_C = 8.0
EPS = 1e-6
IN_SIZES = (D_ATT, D_ATT, D_ATT, N_HEADS, D_ATT, D_LRU, D_LRU, D_SG, D_SG, D_SG)

LANES = 128
SUBLANES = 8
BF16_ROWS = 16
VMEM_LIMIT = 56 * 1024 * 1024

TM = 512
TQ = 512
TN_ADA = 1024
NEG = -1e30
assert TM == TQ

N_PARTS = 3
N_AUG = N_HEADS * N_PARTS
T_ROWS = 2 * D_ATT + 2 * BF16_ROWS
LOG2E = 1.4426950408889634
Q_SCALE = LOG2E / math.sqrt(HEAD_DIM)
V_ROWS = HEAD_DIM + BF16_ROWS
SKIP_LOG2 = 150.0
NORM_SLACK = 1.02
GELU_C0 = math.sqrt(2.0 / math.pi)
GELU_C1 = GELU_C0 * 0.044715

F32 = jnp.float32
BF16 = jnp.bfloat16


def _sigmoid(x):
    return 0.5 * jnp.tanh(0.5 * x) + 0.5


def _silu(x):
    hx = 0.5 * x
    return hx + hx * jnp.tanh(hx)


def _gelu_tanh(x):
    hx = 0.5 * x
    return hx + hx * jnp.tanh(x * (GELU_C0 + GELU_C1 * (x * x)))


def _log_sigmoid(x):
    return jnp.minimum(x, 0.0) - jnp.log1p(jnp.exp(-jnp.abs(x)))


def _split3(x, sel):
    hi = x.astype(BF16).astype(F32)
    r1 = x - hi
    mid = r1.astype(BF16).astype(F32)
    lo = r1 - mid
    return jnp.where(sel == 0, hi, jnp.where(sel == 1, mid, lo))


def _mod_row(ref):
    return jnp.concatenate([ref[0, 0, j] for j in range(ref.shape[2])], axis=1)


def _params(sem):
    return pltpu.CompilerParams(dimension_semantics=sem, vmem_limit_bytes=VMEM_LIMIT)


def _ada_kernel(ct_ref, w_ref, b_ref, o_ref):
    ct = ct_ref[...]
    sc = _silu(ct)
    w = w_ref[0]
    bias = b_ref[0]
    for b in range(ct.shape[1]):
        col = sc[:, b:b + 1]
        o_ref[0, b, 0] = jnp.sum(col * w, axis=0, keepdims=True) + bias


def _ada(c, ada_w, ada_b):
    L, D, N = ada_w.shape
    B = c.shape[0]
    return pl.pallas_call(
        _ada_kernel,
        grid=(L, N // TN_ADA),
        in_specs=[
            pl.BlockSpec((D, B), lambda l, n: (0, 0)),
            pl.BlockSpec((1, D, TN_ADA), lambda l, n: (l, 0, n)),
            pl.BlockSpec((1, 1, TN_ADA), lambda l, n: (l, 0, n)),
        ],
        out_specs=pl.BlockSpec((1, B, 1, 1, TN_ADA), lambda l, n: (l, 0, n, 0, 0)),
        out_shape=jax.ShapeDtypeStruct((L, B, N // TN_ADA, 1, TN_ADA), F32),
        compiler_params=_params(("parallel", "parallel")),
        name="ada_mod",
    )(c.T, ada_w, ada_b.reshape(L, 1, N))


def _lru_mixer(g_lru, r_pre, i_pre, lam, h_ref, xc):
    r = _sigmoid(r_pre)
    ig = _sigmoid(i_pre)
    nlam = -lam
    softplus = jnp.maximum(nlam, 0.0) + jnp.log1p(jnp.exp(-jnp.abs(nlam)))
    log_a = (-LRU_C * r) * softplus
    a = jnp.exp(log_a)
    bt = jnp.sqrt(jnp.tanh(-log_a) * (1.0 + a * a)) * (ig * xc)

    row8 = lax.broadcasted_iota(jnp.int32, (TM, D_LRU), 0) & (SUBLANES - 1)
    d = 1
    while d < SUBLANES:
        valid = row8 >= d
        a_s = jnp.where(valid, pltpu.roll(a, d, axis=0), 1.0)
        b_s = jnp.where(valid, pltpu.roll(bt, d, axis=0), 0.0)
        bt = bt + a * b_s
        a = a * a_s
        d *= 2

    h_prev = h_ref[...]
    sg = _silu(g_lru)
    outs = []
    for g in range(TM // SUBLANES):
        lo = g * SUBLANES
        hg = bt[lo:lo + SUBLANES] + a[lo:lo + SUBLANES] * h_prev
        h_prev = hg[SUBLANES - 1:SUBLANES, :]
        outs.append(hg * sg[lo:lo + SUBLANES])
    h_ref[...] = h_prev
    return jnp.concatenate(outs, axis=0)


def _inproj_kernel(x_ref, shift_ref, scale_ref, g_ref, we_ref, wg_ref, wk_ref, wt_ref, bf_ref,
                   cw_ref, cb_ref, wa_ref, ba_ref, wx_ref, bx_ref, lam_ref,
                   lng_ref, lnb_ref, ws_ref, sb_ref, he_ref,
                   qt_ref, vt_ref, k_ref, ka_ref, qa_ref, ga_ref, yl_ref, ys_ref,
                   qs_ref, ks_ref, fs_ref, fe_ref,
                   fc_ref, xbuf_ref, h_ref):
    i = pl.program_id(1)

    @pl.when(i == 0)
    def _():
        fc_ref[...] = jnp.zeros(fc_ref.shape, F32)
        xbuf_ref[0:SUBLANES, :] = jnp.zeros((SUBLANES, D_LRU), F32)
        h_ref[...] = jnp.zeros(h_ref.shape, F32)

    x = x_ref[0]
    ms = jnp.mean(x * x, axis=-1, keepdims=True)
    gs = g_ref[0] * (1.0 + _mod_row(scale_ref))
    h = (x * lax.rsqrt(ms + EPS)) * gs + _mod_row(shift_ref)
    hb = h.astype(BF16)

    early = jnp.dot(hb, we_ref[0], preferred_element_type=F32)
    x_lru = early[:, :D_LRU]
    sg_v = early[:, D_LRU:]
    tr = lax.dot_general(wt_ref[0], hb, (((1,), (1,)), ((), ())), preferred_element_type=F32)
    tq = tr[:D_ATT] * Q_SCALE
    qt_ref[0] = tq.astype(BF16)
    vt_ref[0] = tr[D_ATT:2 * D_ATT].astype(BF16)
    fl = tr[2 * D_ATT:2 * D_ATT + N_AUG]
    kb = jnp.dot(hb, wk_ref[0], preferred_element_type=F32).astype(BF16)
    k_ref[0] = kb
    qn2 = jnp.sum((tq * tq).reshape(N_HEADS, HEAD_DIM, TM), axis=1)
    qs_ref[0, 0] = jnp.broadcast_to(jnp.sqrt(jnp.max(qn2, axis=1, keepdims=True)), (N_HEADS, LANES))

    v = _gelu_tanh(sg_v)
    mu = jnp.mean(v, axis=-1, keepdims=True)
    var = jnp.mean(jnp.square(v - mu), axis=-1, keepdims=True)
    vnb = (((v - mu) * lax.rsqrt(var + EPS)) * lng_ref[0] + lnb_ref[0]).astype(BF16)
    w_shape = ws_ref.shape[1:]
    t_idx = lax.broadcasted_iota(jnp.int32, w_shape, 0) & (SG_CHUNK - 1)
    s_idx = lax.broadcasted_iota(jnp.int32, w_shape, 1)
    ws = jnp.where(s_idx <= t_idx, ws_ref[0], jnp.zeros(w_shape, BF16))
    grp = lax.broadcasted_iota(jnp.int32, (SG_CHUNK, D_SG), 1) // HEAD_DIM
    zs = []
    for c in range(TM // SG_CHUNK):
        zz = jnp.dot(ws, vnb[c * SG_CHUNK:(c + 1) * SG_CHUNK], preferred_element_type=F32)
        z = zz[0:SG_CHUNK]
        for gi in range(1, N_SG_GROUPS):
            z = jnp.where(grp == gi, zz[gi * SG_CHUNK:(gi + 1) * SG_CHUNK], z)
        zs.append(z + sb_ref[0])
    z_all = jnp.concatenate(zs, axis=0)

    xbuf_ref[SUBLANES:SUBLANES + TM, :] = x_lru
    xc = cb_ref[0]
    for kk in range(CONV_WIDTH):
        off = SUBLANES - (CONV_WIDTH - 1) + kk
        xc = xc + xbuf_ref[off:off + TM, :] * cw_ref[0, kk:kk + 1, :]
    xbuf_ref[0:SUBLANES, :] = x_lru[TM - SUBLANES:, :]
    xcb = xc.astype(BF16)
    r_pre = jnp.dot(xcb, wa_ref[0], preferred_element_type=F32) + ba_ref[0]
    i_pre = jnp.dot(xcb, wx_ref[0], preferred_element_type=F32) + bx_ref[0]

    gates = jnp.dot(hb, wg_ref[0], preferred_element_type=F32)
    kn2 = jnp.dot(kb * kb, he_ref[...], preferred_element_type=F32)
    ks_ref[0, 0] = jnp.sqrt(jnp.max(kn2, axis=0, keepdims=True))

    ga_ref[0] = gates[:, :D_ATT].astype(BF16)
    g_lru = gates[:, D_ATT:D_ATT + D_LRU]
    sg_u = gates[:, D_ATT + D_LRU:D_ATT + D_LRU + D_SG]
    g_sg = gates[:, D_ATT + D_LRU + D_SG:]
    ys_ref[0] = ((_gelu_tanh(sg_u) * z_all) * _silu(g_sg)).astype(BF16)
    yl_ref[0] = _lru_mixer(g_lru, r_pre, i_pre, lam_ref[0], h_ref, xc).astype(BF16)

    reps = TM // LANES
    ls = _log_sigmoid(fl + jnp.concatenate([bf_ref[0]] * reps, axis=1))
    lane = lax.broadcasted_iota(jnp.int32, (N_AUG, TM), 1)
    d = 1
    while d < TM:
        ls = ls + jnp.where(lane >= d, pltpu.roll(ls, d, axis=1), 0.0)
        d *= 2
    f = ls + jnp.concatenate([fc_ref[...]] * reps, axis=1)
    fc_ref[...] = jnp.broadcast_to(f[:, TM - 1:TM], (N_AUG, LANES))
    row = lax.broadcasted_iota(jnp.int32, (N_AUG, TM), 0)
    f2 = f * LOG2E
    fs_ref[0, 0] = jnp.broadcast_to(f2[:, 0:1], (N_AUG, LANES))
    fe_ref[0, 0] = jnp.broadcast_to(f2[:, TM - 1:TM], (N_AUG, LANES))
    parts = _split3(f2, row % N_PARTS)
    ones = jnp.ones((N_AUG, TM), F32)
    zeros = jnp.zeros((LANES - 2 * N_AUG, TM), F32)
    qa_ref[0] = jnp.concatenate([ones, parts, zeros], axis=0).astype(BF16)
    ka_ref[0] = jnp.concatenate([-parts, ones, zeros], axis=0).T.astype(BF16)


def _inproj(layer, x, mod, prep):
    B, S, D = x.shape
    per_layer = lambda a: pl.BlockSpec((1,) + a.shape[1:], lambda b, i: (layer,) + (0,) * (a.ndim - 1))
    mod_chunk = lambda j: pl.BlockSpec((1, 1, D // TN_ADA, 1, TN_ADA), lambda b, i: (layer, b, j, 0, 0))
    row_tile = lambda n: pl.BlockSpec((1, TM, n), lambda b, i: (b, i, 0))
    col_tile = lambda n: pl.BlockSpec((1, n, TM), lambda b, i: (b, 0, i))
    names = ("pre_g", "we", "wg", "wk", "wt", "bf", "conv_w", "conv_b", "wa", "ba", "wx", "bx", "lam",
             "ln_g", "ln_b", "ws", "sg_bias")
    operands = [prep[n] for n in names]
    head_sum = prep["head_sum"]
    stat = lambda rows: pl.BlockSpec((1, 1, rows, LANES), lambda b, i: (b, i, 0, 0))
    stat_shape = lambda rows: jax.ShapeDtypeStruct((B, S // TM, rows, LANES), F32)
    return pl.pallas_call(
        _inproj_kernel,
        grid=(B, S // TM),
        in_specs=[row_tile(D), mod_chunk(0), mod_chunk(1)] + [per_layer(a) for a in operands]
        + [pl.BlockSpec(head_sum.shape, lambda b, i: (0, 0))],
        out_specs=[
            col_tile(D_ATT), col_tile(D_ATT), row_tile(D_ATT), row_tile(LANES), col_tile(LANES),
            row_tile(D_ATT), row_tile(D_LRU), row_tile(D_SG),
            stat(N_HEADS), stat(1), stat(N_AUG), stat(N_AUG),
        ],
        out_shape=[
            jax.ShapeDtypeStruct((B, D_ATT, S), BF16),
            jax.ShapeDtypeStruct((B, D_ATT, S), BF16),
            jax.ShapeDtypeStruct((B, S, D_ATT), BF16),
            jax.ShapeDtypeStruct((B, S, LANES), BF16),
            jax.ShapeDtypeStruct((B, LANES, S), BF16),
            jax.ShapeDtypeStruct((B, S, D_ATT), BF16),
            jax.ShapeDtypeStruct((B, S, D_LRU), BF16),
            jax.ShapeDtypeStruct((B, S, D_SG), BF16),
            stat_shape(N_HEADS),
            stat_shape(1),
            stat_shape(N_AUG),
            stat_shape(N_AUG),
        ],
        scratch_shapes=[
            pltpu.VMEM((N_AUG, LANES), F32),
            pltpu.VMEM((TM + SUBLANES, D_LRU), F32),
            pltpu.VMEM((1, D_LRU), F32),
        ],
        compiler_params=_params(("parallel", "arbitrary")),
        name="inproj",
    )(x, mod, mod, *operands, head_sum)


def _attn_kernel(seq_ref, noff_ref, qt_ref, qa_ref, k_ref, ka_ref, vt_ref, g_ref,
                 yl_ref, ys_ref, x_ref, gate_ref, pg_ref, w_ref, o_ref,
                 m_ref, acc_ref, rhs_ref, bias_ref, s0_ref, s1_ref, p0_ref, p1_ref,
                 mx0_ref, mx1_ref, al0_ref, al1_ref):
    b = pl.program_id(0)
    i = pl.program_id(1)
    n_tiles = pl.num_programs(1)

    @pl.when((b == 0) & (i == 0))
    def _():
        kk = lax.broadcasted_iota(jnp.int32, (TQ, TQ), 0)
        qq = lax.broadcasted_iota(jnp.int32, (TQ, TQ), 1)
        bias_ref[0] = jnp.zeros((TQ, TQ), F32)
        bias_ref[1] = jnp.where(kk <= qq, 0.0, NEG)
        rhs_ref[N_HEADS] = jnp.zeros(rhs_ref.shape[1:], BF16)
        acc_ref[...] = jnp.zeros(acc_ref.shape, F32)

    qa = qa_ref[0]
    row = lax.broadcasted_iota(jnp.int32, (LANES, TQ), 0)
    for p in range(N_PAIRS):
        qt = qt_ref[0, p * LANES:(p + 1) * LANES, :]
        zero = jnp.zeros_like(qt)
        for hh in range(2):
            own = (row >= hh * HEAD_DIM) & (row < (hh + 1) * HEAD_DIM)
            lo = (2 * p + hh) * N_PARTS
            aug = ((row >= lo) & (row < lo + N_PARTS)) | ((row >= N_AUG + lo) & (row < N_AUG + lo + N_PARTS))
            rhs_ref[2 * p + hh] = jnp.concatenate([jnp.where(own, qt, zero), jnp.where(aug, qa, zero)], axis=0)
    ones_rows = jnp.where(lax.broadcasted_iota(jnp.int32, (BF16_ROWS, TQ), 0) == 0, 1.0, 0.0).astype(BF16)
    m_ref[...] = jnp.full(m_ref.shape, NEG, F32)
    s_bufs = (s0_ref, s1_ref)
    p_bufs = (p0_ref, p1_ref)
    mx_bufs = (mx0_ref, mx1_ref)
    al_bufs = (al0_ref, al1_ref)

    def item(n, hh):
        code = seq_ref[(b * n_tiles + i) * (N_HEADS * n_tiles) + 2 * n + hh]
        return lax.div(code, n_tiles), lax.rem(code, n_tiles)

    HALF = TQ // 2

    def key_lanes(h):
        pair = lax.shift_right_logical(jnp.minimum(h, N_HEADS - 1), 1)
        return pl.ds(pl.multiple_of(pair * LANES, LANES), LANES)

    def scores(n, slot, kind):
        for hh in range(2):
            h, t = item(n, hh)
            if kind == "diag":
                lo_rows = pl.ds(pl.multiple_of(i * TQ, TQ), HALF)
                hi_rows = pl.ds(pl.multiple_of(i * TQ + HALF, HALF), HALF)
                lhs_lo = jnp.concatenate([k_ref[0, lo_rows, key_lanes(h)], ka_ref[0, lo_rows, :]], axis=1)
                lhs_hi = jnp.concatenate([k_ref[0, hi_rows, key_lanes(h)], ka_ref[0, hi_rows, :]], axis=1)
                st_lo = jnp.dot(lhs_lo, rhs_ref[h], preferred_element_type=F32) + bias_ref[1, :HALF, :]
                st_hi = (jnp.dot(lhs_hi, rhs_ref[h, :, HALF:], preferred_element_type=F32)
                         + bias_ref[1, HALF:, HALF:])
                s_bufs[slot][hh, :HALF, :] = st_lo
                s_bufs[slot][hh, HALF:, HALF:] = st_hi
                mx_lo = jnp.max(st_lo, axis=0, keepdims=True)
                mx_hi = jnp.max(st_hi, axis=0, keepdims=True)
                mx_bufs[slot][hh] = jnp.concatenate([mx_lo[:, :HALF], jnp.maximum(mx_lo[:, HALF:], mx_hi)], axis=1)
                continue
            rows = pl.ds(pl.multiple_of(t * TQ, TQ), TQ)
            lhs = jnp.concatenate([k_ref[0, rows, key_lanes(h)], ka_ref[0, rows, :]], axis=1)
            st = jnp.dot(lhs, rhs_ref[h], preferred_element_type=F32)
            if kind == "any":
                st = st + bias_ref[(t == i).astype(jnp.int32)]
            s_bufs[slot][hh] = st
            mx_bufs[slot][hh] = jnp.max(st, axis=0, keepdims=True)

    def softmax(n, slot, kind):
        for hh in range(2):
            h, _ = item(n, hh)
            m_old = m_ref[h]
            m_new = jnp.maximum(m_old, mx_bufs[slot][hh])
            if kind == "diag":
                p_bufs[slot][hh, :HALF, :] = jnp.exp2(s_bufs[slot][hh, :HALF, :] - m_new).astype(BF16)
                p_bufs[slot][hh, HALF:, HALF:] = jnp.exp2(s_bufs[slot][hh, HALF:, HALF:] - m_new[:, HALF:]).astype(BF16)
            else:
                p_bufs[slot][hh] = jnp.exp2(s_bufs[slot][hh] - m_new).astype(BF16)
            al_bufs[slot][hh] = jnp.exp2(m_old - m_new)
            m_ref[h] = m_new

    def values(n, slot, kind):
        for hh in range(2):
            h, t = item(n, hh)
            cols = pl.ds(pl.multiple_of(t * TQ, TQ), TQ)
            head_rows = pl.ds(pl.multiple_of(jnp.minimum(h, N_HEADS - 1) * HEAD_DIM, HEAD_DIM), HEAD_DIM)
            vaug = jnp.concatenate([vt_ref[0, head_rows, cols], ones_rows], axis=0)
            if kind == "diag":
                new = jnp.dot(vaug[:, :HALF], p_bufs[slot][hh, :HALF, :], preferred_element_type=F32)
                new_hi = jnp.dot(vaug[:, HALF:], p_bufs[slot][hh, HALF:, HALF:], preferred_element_type=F32)
                new = jnp.concatenate([new[:, :HALF], new[:, HALF:] + new_hi], axis=1)
            else:
                new = jnp.dot(vaug, p_bufs[slot][hh], preferred_element_type=F32)
            acc_ref[h] = acc_ref[h] * al_bufs[slot][hh] + new

    def full_step(n, parity, kinds):
        softmax(n - 1, 1 - parity, kinds[1])
        scores(n, parity, kinds[0])
        values(n - 2, parity, kinds[2])

    n_off = noff_ref[b * n_tiles + i]
    for kind, wanted in (("off", True), ("any", False)):
        @pl.when((n_off >= 2) == wanted)
        def _():
            scores(0, 0, kind)
            softmax(0, 0, kind)
            scores(1, 1, kind)

    trips = lax.shift_right_logical(jnp.maximum(n_off - 2, 0), 1)

    def body(u, carry):
        full_step(2 + 2 * u, 0, ("off",) * 3)
        full_step(3 + 2 * u, 1, ("off",) * 3)
        return carry

    lax.fori_loop(0, trips, body, 0)
    n0 = 2 + 2 * trips
    for left in range(2, 6):
        kind = lambda j, left=left: "diag" if j >= max(left - N_PAIRS, 0) else "off"

        @pl.when(n_off + N_PAIRS - n0 == left)
        def _():
            for j in range(left):
                full_step(n0 + j, j % 2, (kind(j), kind(j - 1), kind(j - 2)))
            softmax(n0 + left - 1, (left - 1) % 2, kind(left - 1))
            values(n0 + left - 2, left % 2, kind(left - 2))
            values(n0 + left - 1, (left - 1) % 2, kind(left - 1))

    ya = []
    for p in range(N_PAIRS):
        outs = []
        for hh in range(2):
            a = acc_ref[2 * p + hh]
            outs.append(a[:HEAD_DIM] / a[HEAD_DIM:HEAD_DIM + 1])
        o = jnp.concatenate(outs, axis=0).T
        lanes = slice(p * LANES, (p + 1) * LANES)
        ya.append((o * _silu(g_ref[0, :, lanes].astype(F32))).astype(BF16))

    y = jnp.dot(jnp.concatenate(ya, axis=1), w_ref[0, 0:D_ATT, :], preferred_element_type=F32)
    y = y + jnp.dot(yl_ref[0], w_ref[0, D_ATT:D_ATT + D_LRU, :], preferred_element_type=F32)
    y = y + jnp.dot(ys_ref[0], w_ref[0, D_ATT + D_LRU:, :], preferred_element_type=F32)
    ms = jnp.mean(y * y, axis=-1, keepdims=True)
    yn = (y * lax.rsqrt(ms + EPS)) * pg_ref[0]
    o_ref[0] = x_ref[0] + _mod_row(gate_ref) * yn


def _tile_schedule(qs, ks, fs, fe):
    B, n_tiles = qs.shape[:2]
    qmax = qs[:, :, :, 0]
    kmax = ks[:, :, 0, :N_HEADS]
    f_first = fs[:, :, ::N_PARTS, 0]
    f_last = fe[:, :, ::N_PARTS, 0]
    gap = f_first[:, :, None, :] - f_last[:, None, :, :]
    reach = NORM_SLACK * qmax[:, :, None, :] * (kmax[:, None, :, :] + kmax[:, :, None, :])
    negligible = (gap + reach) <= -SKIP_LOG2
    tiles = jnp.arange(n_tiles)
    skip = negligible & (tiles[None, :] < tiles[:, None])[None, :, :, None]
    first = jnp.min(jnp.where(skip, n_tiles, tiles[None, None, :, None]), axis=2)
    first = first.transpose(0, 2, 1)
    visit = (tiles[None, None, None, :] >= first[:, :, :, None]) & (tiles[None, None, None, :] < tiles[None, None, :, None])
    flat = visit.transpose(0, 2, 1, 3).reshape(B, n_tiles, N_HEADS * n_tiles).astype(jnp.int32)
    pos = jnp.cumsum(flat, axis=2) - flat
    n_items = jnp.sum(flat, axis=2)
    n_off = (n_items + 1) // 2
    codes = jnp.arange(N_HEADS * n_tiles)
    slots = jnp.arange(N_HEADS * n_tiles)
    hit = (flat[:, :, None, :] == 1) & (pos[:, :, None, :] == slots[None, None, :, None])
    seq = jnp.sum(jnp.where(hit, codes[None, None, None, :], 0), axis=3)
    seq = jnp.where(slots[None, None, :] == n_items[:, :, None], N_HEADS * n_tiles, seq)
    head = slots[None, None, :] - 2 * n_off[:, :, None]
    seq = jnp.where((head >= 0) & (head < N_HEADS), head * n_tiles + tiles[None, :, None], seq)
    return seq.reshape(-1).astype(jnp.int32), n_off.reshape(-1).astype(jnp.int32)


def _attention_outproj(layer, schedule, qt, qa, k, ka, vt, g_att, yl, ys, x, mod, prep):
    B, S, D = x.shape
    per_layer = lambda a: pl.BlockSpec((1,) + a.shape[1:], lambda b, i, *_: (layer,) + (0,) * (a.ndim - 1))
    row_tile = lambda n: pl.BlockSpec((1, TQ, n), lambda b, i, *_: (b, i, 0))
    grid_spec = pltpu.PrefetchScalarGridSpec(
        num_scalar_prefetch=2,
        grid=(B, S // TQ),
        in_specs=[
            pl.BlockSpec((1, D_ATT, TQ), lambda b, i, *_: (b, 0, i)),
            pl.BlockSpec((1, LANES, TQ), lambda b, i, *_: (b, 0, i)),
            pl.BlockSpec((1, S, D_ATT), lambda b, i, *_: (b, 0, 0)),
            pl.BlockSpec((1, S, LANES), lambda b, i, *_: (b, 0, 0)),
            pl.BlockSpec((1, D_ATT, S), lambda b, i, *_: (b, 0, 0)),
            row_tile(D_ATT), row_tile(D_LRU), row_tile(D_SG), row_tile(D),
            pl.BlockSpec((1, 1, D // TN_ADA, 1, TN_ADA), lambda b, i, *_: (layer, b, 2, 0, 0)),
            per_layer(prep["post_g"]),
            per_layer(prep["w_out"]),
        ],
        out_specs=row_tile(D),
        scratch_shapes=[
            pltpu.VMEM((N_HEADS + 1, 1, TQ), F32),
            pltpu.VMEM((N_HEADS + 1, V_ROWS, TQ), F32),
            pltpu.VMEM((N_HEADS + 1, 2 * LANES, TQ), BF16),
            pltpu.VMEM((2, TQ, TQ), F32),
            pltpu.VMEM((2, TQ, TQ), F32),
            pltpu.VMEM((2, TQ, TQ), F32),
            pltpu.VMEM((2, TQ, TQ), BF16),
            pltpu.VMEM((2, TQ, TQ), BF16),
            pltpu.VMEM((2, 1, TQ), F32),
            pltpu.VMEM((2, 1, TQ), F32),
            pltpu.VMEM((2, 1, TQ), F32),
            pltpu.VMEM((2, 1, TQ), F32),
        ],
    )
    return pl.pallas_call(
        _attn_kernel,
        grid_spec=grid_spec,
        out_shape=jax.ShapeDtypeStruct((B, S, D), F32),
        compiler_params=_params(("arbitrary", "arbitrary")),
        name="fox_attention_outproj",
    )(*schedule, qt, qa, k, ka, vt, g_att, yl, ys, x, mod, prep["post_g"], prep["w_out"])


def _block_diag(w):
    L, G, n, _ = w.shape
    eye = jnp.eye(G, dtype=w.dtype)
    return (w[:, :, :, None, :] * eye[None, :, None, :, None]).reshape(L, G * n, G * n)


def _prepare(pre_g, post_g, w_in, b_f, conv_w, conv_b, lru_wa, lru_ba, lru_wx, lru_bx, lru_lambda,
             sg_ln_g, sg_ln_b, sg_w, sg_b, w_out):
    L, D, _ = w_in.shape
    cuts = [0]
    for n in IN_SIZES:
        cuts.append(cuts[-1] + n)
    w_bf = w_in.astype(BF16)
    wq, wk, wv, wf, wga, wxl, wgl, wsu, wsv, wgs = (w_bf[:, :, cuts[j]:cuts[j + 1]] for j in range(10))
    row = lambda a: a[:, None, :]
    wf3 = jnp.repeat(wf, N_PARTS, axis=2)
    pad = jnp.zeros((L, D, T_ROWS - 2 * D_ATT - N_AUG), BF16)
    return dict(
        pre_g=row(pre_g), post_g=row(post_g),
        we=jnp.concatenate([wxl, wsv], axis=2),
        wg=jnp.concatenate([wga, wgl, wsu, wgs], axis=2),
        wk=wk,
        wt=jnp.concatenate([wq, wv, wf3, pad], axis=2).transpose(0, 2, 1),
        bf=jnp.broadcast_to(jnp.repeat(b_f, N_PARTS, axis=1)[:, :, None], (L, N_AUG, LANES)),
        conv_w=conv_w, conv_b=row(conv_b),
        wa=_block_diag(lru_wa).astype(BF16), ba=row(lru_ba),
        wx=_block_diag(lru_wx).astype(BF16), bx=row(lru_bx), lam=row(lru_lambda),
        ln_g=row(sg_ln_g), ln_b=row(sg_ln_b),
        ws=sg_w.reshape(L, N_SG_GROUPS * SG_CHUNK, SG_CHUNK).astype(BF16),
        sg_bias=jnp.repeat(sg_b.transpose(0, 2, 1), HEAD_DIM, axis=2),
        w_out=w_out.astype(BF16),
        head_sum=(jnp.arange(D_ATT)[:, None] // HEAD_DIM == jnp.arange(LANES)[None, :]).astype(BF16),
    )


def kernel(x, c, ada_w, ada_b, pre_g, post_g, w_in, b_f, conv_w, conv_b, lru_wa, lru_ba, lru_wx,
           lru_bx, lru_lambda, sg_ln_g, sg_ln_b, sg_w, sg_b, w_out):
    mod = _ada(c, ada_w, ada_b)
    prep = _prepare(pre_g, post_g, w_in, b_f, conv_w, conv_b, lru_wa, lru_ba, lru_wx, lru_bx,
                    lru_lambda, sg_ln_g, sg_ln_b, sg_w, sg_b, w_out)
    for layer in range(ada_w.shape[0]):
        qt, vt, k, ka, qa, ga, yl, ys, qs, ks, fs, fe = _inproj(layer, x, mod, prep)
        x = _attention_outproj(layer, _tile_schedule(qs, ks, fs, fe), qt, qa, k, ka, vt, ga, yl, ys, x, mod, prep)
    return x
```

```python
import math

import jax
import jax.numpy as jnp
from jax import lax
from jax.experimental import pallas as pl
from jax.experimental.pallas import tpu as pltpu

HEAD_DIM = 64
D_ATT = 512
D_LRU = 256
D_SG = 256
N_HEADS = D_ATT // HEAD_DIM
N_PAIRS = N_HEADS // 2
N_SG_GROUPS = D_SG // HEAD_DIM
SG_CHUNK = 128
CONV_WIDTH = 4
LRU_C = 8.0
EPS = 1e-6
IN_SIZES = (D_ATT, D_ATT, D_ATT, N_HEADS, D_ATT, D_LRU, D_LRU, D_SG, D_SG, D_SG)

LANES = 128
SUBLANES = 8
BF16_ROWS = 16
VMEM_LIMIT = 56 * 1024 * 1024

TM = 512
TQ = 512
LOOP_STEPS = 2
TN_ADA = 1024
NEG = -1e30
assert TM == TQ

N_PARTS = 3
N_AUG = N_HEADS * N_PARTS
T_ROWS = 2 * D_ATT + 2 * BF16_ROWS
LOG2E = 1.4426950408889634
Q_SCALE = LOG2E / math.sqrt(HEAD_DIM)
V_ROWS = HEAD_DIM + BF16_ROWS
SKIP_LOG2 = 150.0
NORM_SLACK = 1.02
GELU_C0 = math.sqrt(2.0 / math.pi)
GELU_C1 = GELU_C0 * 0.044715

F32 = jnp.float32
BF16 = jnp.bfloat16


def _sigmoid(x):
    return 0.5 * jnp.tanh(0.5 * x) + 0.5


def _silu(x):
    hx = 0.5 * x
    return hx + hx * jnp.tanh(hx)


def _gelu_tanh(x):
    hx = 0.5 * x
    return hx + hx * jnp.tanh(x * (GELU_C0 + GELU_C1 * (x * x)))


def _log_sigmoid(x):
    return jnp.minimum(x, 0.0) - jnp.log1p(jnp.exp(-jnp.abs(x)))


def _split3(x, sel):
    hi = x.astype(BF16).astype(F32)
    r1 = x - hi
    mid = r1.astype(BF16).astype(F32)
    lo = r1 - mid
    return jnp.where(sel == 0, hi, jnp.where(sel == 1, mid, lo))


def _mod_row(ref):
    return jnp.concatenate([ref[0, 0, j] for j in range(ref.shape[2])], axis=1)


def _params(sem):
    return pltpu.CompilerParams(dimension_semantics=sem, vmem_limit_bytes=VMEM_LIMIT)


def _ada_kernel(ct_ref, w_ref, b_ref, o_ref):
    ct = ct_ref[...]
    sc = _silu(ct)
    w = w_ref[0]
    bias = b_ref[0]
    for b in range(ct.shape[1]):
        col = sc[:, b:b + 1]
        o_ref[0, b, 0] = jnp.sum(col * w, axis=0, keepdims=True) + bias


def _ada(c, ada_w, ada_b):
    L, D, N = ada_w.shape
    B = c.shape[0]
    return pl.pallas_call(
        _ada_kernel,
        grid=(L, N // TN_ADA),
        in_specs=[
            pl.BlockSpec((D, B), lambda l, n: (0, 0)),
            pl.BlockSpec((1, D, TN_ADA), lambda l, n: (l, 0, n)),
            pl.BlockSpec((1, 1, TN_ADA), lambda l, n: (l, 0, n)),
        ],
        out_specs=pl.BlockSpec((1, B, 1, 1, TN_ADA), lambda l, n: (l, 0, n, 0, 0)),
        out_shape=jax.ShapeDtypeStruct((L, B, N // TN_ADA, 1, TN_ADA), F32),
        compiler_params=_params(("parallel", "parallel")),
        name="ada_mod",
    )(c.T, ada_w, ada_b.reshape(L, 1, N))


def _lru_mixer(g_lru, r_pre, i_pre, lam, h_ref, xc):
    r = _sigmoid(r_pre)
    ig = _sigmoid(i_pre)
    nlam = -lam
    softplus = jnp.maximum(nlam, 0.0) + jnp.log1p(jnp.exp(-jnp.abs(nlam)))
    log_a = (-LRU_C * r) * softplus
    a = jnp.exp(log_a)
    bt = jnp.sqrt(jnp.tanh(-log_a) * (1.0 + a * a)) * (ig * xc)

    row8 = lax.broadcasted_iota(jnp.int32, (TM, D_LRU), 0) & (SUBLANES - 1)
    d = 1
    while d < SUBLANES:
        valid = row8 >= d
        a_s = jnp.where(valid, pltpu.roll(a, d, axis=0), 1.0)
        b_s = jnp.where(valid, pltpu.roll(bt, d, axis=0), 0.0)
        bt = bt + a * b_s
        a = a * a_s
        d *= 2

    h_prev = h_ref[...]
    sg = _silu(g_lru)
    outs = []
    for g in range(TM // SUBLANES):
        lo = g * SUBLANES
        hg = bt[lo:lo + SUBLANES] + a[lo:lo + SUBLANES] * h_prev
        h_prev = hg[SUBLANES - 1:SUBLANES, :]
        outs.append(hg * sg[lo:lo + SUBLANES])
    h_ref[...] = h_prev
    return jnp.concatenate(outs, axis=0)


def _inproj_kernel(x_ref, shift_ref, scale_ref, g_ref, we_ref, wg_ref, wk_ref, wt_ref, bf_ref,
                   cw_ref, cb_ref, wa_ref, ba_ref, wx_ref, bx_ref, lam_ref,
                   lng_ref, lnb_ref, ws_ref, sb_ref, he_ref,
                   qt_ref, vt_ref, k_ref, ka_ref, qa_ref, ga_ref, yl_ref, ys_ref,
                   qs_ref, ks_ref, fs_ref, fe_ref,
                   fc_ref, xbuf_ref, h_ref):
    i = pl.program_id(1)

    @pl.when(i == 0)
    def _():
        fc_ref[...] = jnp.zeros(fc_ref.shape, F32)
        xbuf_ref[0:SUBLANES, :] = jnp.zeros((SUBLANES, D_LRU), F32)
        h_ref[...] = jnp.zeros(h_ref.shape, F32)

    x = x_ref[0]
    ms = jnp.mean(x * x, axis=-1, keepdims=True)
    gs = g_ref[0] * (1.0 + _mod_row(scale_ref))
    h = (x * lax.rsqrt(ms + EPS)) * gs + _mod_row(shift_ref)
    hb = h.astype(BF16)

    early = jnp.dot(hb, we_ref[0], preferred_element_type=F32)
    x_lru = early[:, :D_LRU]
    sg_v = early[:, D_LRU:]
    tr = lax.dot_general(wt_ref[0], hb, (((1,), (1,)), ((), ())), preferred_element_type=F32)
    tq = tr[:D_ATT] * Q_SCALE
    qt_ref[0] = tq.astype(BF16)
    vt_ref[0] = tr[D_ATT:2 * D_ATT].astype(BF16)
    fl = tr[2 * D_ATT:2 * D_ATT + N_AUG]
    kb = jnp.dot(hb, wk_ref[0], preferred_element_type=F32).astype(BF16)
    k_ref[0] = kb
    qn2 = jnp.sum((tq * tq).reshape(N_HEADS, HEAD_DIM, TM), axis=1)
    qs_ref[0, 0] = jnp.broadcast_to(jnp.sqrt(jnp.max(qn2, axis=1, keepdims=True)), (N_HEADS, LANES))

    v = _gelu_tanh(sg_v)
    mu = jnp.mean(v, axis=-1, keepdims=True)
    var = jnp.mean(jnp.square(v - mu), axis=-1, keepdims=True)
    vnb = (((v - mu) * lax.rsqrt(var + EPS)) * lng_ref[0] + lnb_ref[0]).astype(BF16)
    w_shape = ws_ref.shape[1:]
    t_idx = lax.broadcasted_iota(jnp.int32, w_shape, 0) & (SG_CHUNK - 1)
    s_idx = lax.broadcasted_iota(jnp.int32, w_shape, 1)
    ws = jnp.where(s_idx <= t_idx, ws_ref[0], jnp.zeros(w_shape, BF16))
    grp = lax.broadcasted_iota(jnp.int32, (SG_CHUNK, D_SG), 1) // HEAD_DIM
    zs = []
    for c in range(TM // SG_CHUNK):
        zz = jnp.dot(ws, vnb[c * SG_CHUNK:(c + 1) * SG_CHUNK], preferred_element_type=F32)
        z = zz[0:SG_CHUNK]
        for gi in range(1, N_SG_GROUPS):
            z = jnp.where(grp == gi, zz[gi * SG_CHUNK:(gi + 1) * SG_CHUNK], z)
        zs.append(z + sb_ref[0])
    z_all = jnp.concatenate(zs, axis=0)

    xbuf_ref[SUBLANES:SUBLANES + TM, :] = x_lru
    xc = cb_ref[0]
    for kk in range(CONV_WIDTH):
        off = SUBLANES - (CONV_WIDTH - 1) + kk
        xc = xc + xbuf_ref[off:off + TM, :] * cw_ref[0, kk:kk + 1, :]
    xbuf_ref[0:SUBLANES, :] = x_lru[TM - SUBLANES:, :]
    xcb = xc.astype(BF16)
    r_pre = jnp.dot(xcb, wa_ref[0], preferred_element_type=F32) + ba_ref[0]
    i_pre = jnp.dot(xcb, wx_ref[0], preferred_element_type=F32) + bx_ref[0]

    gates = jnp.dot(hb, wg_ref[0], preferred_element_type=F32)
    kn2 = jnp.dot(kb * kb, he_ref[...], preferred_element_type=F32)
    ks_ref[0, 0] = jnp.sqrt(jnp.max(kn2, axis=0, keepdims=True))

    ga_ref[0] = gates[:, :D_ATT].astype(BF16)
    g_lru = gates[:, D_ATT:D_ATT + D_LRU]
    sg_u = gates[:, D_ATT + D_LRU:D_ATT + D_LRU + D_SG]
    g_sg = gates[:, D_ATT + D_LRU + D_SG:]
    ys_ref[0] = ((_gelu_tanh(sg_u) * z_all) * _silu(g_sg)).astype(BF16)
    yl_ref[0] = _lru_mixer(g_lru, r_pre, i_pre, lam_ref[0], h_ref, xc).astype(BF16)

    reps = TM // LANES
    ls = _log_sigmoid(fl + jnp.concatenate([bf_ref[0]] * reps, axis=1))
    lane = lax.broadcasted_iota(jnp.int32, (N_AUG, TM), 1)
    d = 1
    while d < TM:
        ls = ls + jnp.where(lane >= d, pltpu.roll(ls, d, axis=1), 0.0)
        d *= 2
    f = ls + jnp.concatenate([fc_ref[...]] * reps, axis=1)
    fc_ref[...] = jnp.broadcast_to(f[:, TM - 1:TM], (N_AUG, LANES))
    row = lax.broadcasted_iota(jnp.int32, (N_AUG, TM), 0)
    f2 = f * LOG2E
    fs_ref[0, 0] = jnp.broadcast_to(f2[:, 0:1], (N_AUG, LANES))
    fe_ref[0, 0] = jnp.broadcast_to(f2[:, TM - 1:TM], (N_AUG, LANES))
    parts = _split3(f2, row % N_PARTS)
    ones = jnp.ones((N_AUG, TM), F32)
    zeros = jnp.zeros((LANES - 2 * N_AUG, TM), F32)
    qa_ref[0] = jnp.concatenate([ones, parts, zeros], axis=0).astype(BF16)
    ka_ref[0] = jnp.concatenate([-parts, ones, zeros], axis=0).T.astype(BF16)


def _inproj(layer, x, mod, prep):
    B, S, D = x.shape
    per_layer = lambda a: pl.BlockSpec((1,) + a.shape[1:], lambda b, i: (layer,) + (0,) * (a.ndim - 1))
    mod_chunk = lambda j: pl.BlockSpec((1, 1, D // TN_ADA, 1, TN_ADA), lambda b, i: (layer, b, j, 0, 0))
    row_tile = lambda n: pl.BlockSpec((1, TM, n), lambda b, i: (b, i, 0))
    col_tile = lambda n: pl.BlockSpec((1, n, TM), lambda b, i: (b, 0, i))
    names = ("pre_g", "we", "wg", "wk", "wt", "bf", "conv_w", "conv_b", "wa", "ba", "wx", "bx", "lam",
             "ln_g", "ln_b", "ws", "sg_bias")
    operands = [prep[n] for n in names]
    head_sum = prep["head_sum"]
    stat = lambda rows: pl.BlockSpec((1, 1, rows, LANES), lambda b, i: (b, i, 0, 0))
    stat_shape = lambda rows: jax.ShapeDtypeStruct((B, S // TM, rows, LANES), F32)
    return pl.pallas_call(
        _inproj_kernel,
        grid=(B, S // TM),
        in_specs=[row_tile(D), mod_chunk(0), mod_chunk(1)] + [per_layer(a) for a in operands]
        + [pl.BlockSpec(head_sum.shape, lambda b, i: (0, 0))],
        out_specs=[
            col_tile(D_ATT), col_tile(D_ATT), row_tile(D_ATT), row_tile(LANES), col_tile(LANES),
            row_tile(D_ATT), row_tile(D_LRU), row_tile(D_SG),
            stat(N_HEADS), stat(1), stat(N_AUG), stat(N_AUG),
        ],
        out_shape=[
            jax.ShapeDtypeStruct((B, D_ATT, S), BF16),
            jax.ShapeDtypeStruct((B, D_ATT, S), BF16),
            jax.ShapeDtypeStruct((B, S, D_ATT), BF16),
            jax.ShapeDtypeStruct((B, S, LANES), BF16),
            jax.ShapeDtypeStruct((B, LANES, S), BF16),
            jax.ShapeDtypeStruct((B, S, D_ATT), BF16),
            jax.ShapeDtypeStruct((B, S, D_LRU), BF16),
            jax.ShapeDtypeStruct((B, S, D_SG), BF16),
            stat_shape(N_HEADS),
            stat_shape(1),
            stat_shape(N_AUG),
            stat_shape(N_AUG),
        ],
        scratch_shapes=[
            pltpu.VMEM((N_AUG, LANES), F32),
            pltpu.VMEM((TM + SUBLANES, D_LRU), F32),
            pltpu.VMEM((1, D_LRU), F32),
        ],
        compiler_params=_params(("parallel", "arbitrary")),
        name="inproj",
    )(x, mod, mod, *operands, head_sum)


def _attn_kernel(seq_ref, noff_ref, qt_ref, qa_ref, k_ref, ka_ref, vt_ref, g_ref,
                 yl_ref, ys_ref, x_ref, gate_ref, pg_ref, w_ref, o_ref,
                 m_ref, acc_ref, rhs_ref, bias_ref, s0_ref, s1_ref, p0_ref, p1_ref,
                 mx0_ref, mx1_ref, al0_ref, al1_ref):
    b = pl.program_id(0)
    i = pl.program_id(1)
    n_tiles = pl.num_programs(1)

    @pl.when((b == 0) & (i == 0))
    def _():
        kk = lax.broadcasted_iota(jnp.int32, (TQ, TQ), 0)
        qq = lax.broadcasted_iota(jnp.int32, (TQ, TQ), 1)
        bias_ref[0] = jnp.zeros((TQ, TQ), F32)
        bias_ref[1] = jnp.where(kk <= qq, 0.0, NEG)
        rhs_ref[...] = jnp.zeros(rhs_ref.shape, BF16)
        acc_ref[...] = jnp.zeros(acc_ref.shape, F32)

    qa = qa_ref[0, :2 * N_AUG, :]
    row = lax.broadcasted_iota(jnp.int32, (2 * N_AUG, TQ), 0)
    for h in range(N_HEADS):
        own = (h % 2) * HEAD_DIM
        rhs_ref[h, own:own + HEAD_DIM, :] = qt_ref[0, h * HEAD_DIM:(h + 1) * HEAD_DIM, :]
        lo = h * N_PARTS
        aug = ((row >= lo) & (row < lo + N_PARTS)) | ((row >= N_AUG + lo) & (row < N_AUG + lo + N_PARTS))
        rhs_ref[h, LANES:LANES + 2 * N_AUG, :] = jnp.where(aug, qa, jnp.zeros_like(qa))
    ones_rows = jnp.where(lax.broadcasted_iota(jnp.int32, (BF16_ROWS, TQ), 0) == 0, 1.0, 0.0).astype(BF16)
    m_ref[...] = jnp.full(m_ref.shape, NEG, F32)
    s_bufs = (s0_ref, s1_ref)
    p_bufs = (p0_ref, p1_ref)
    mx_bufs = (mx0_ref, mx1_ref)
    al_bufs = (al0_ref, al1_ref)

    def item(n, hh):
        code = seq_ref[(b * n_tiles + i) * (N_HEADS * n_tiles) + 2 * n + hh]
        return lax.div(code, n_tiles), lax.rem(code, n_tiles)

    HALF = TQ // 2

    def key_lanes(h):
        pair = lax.shift_right_logical(jnp.minimum(h, N_HEADS - 1), 1)
        return pl.ds(pl.multiple_of(pair * LANES, LANES), LANES)

    def scores(n, slot, kind):
        for hh in range(2):
            h, t = item(n, hh)
            if kind == "diag":
                lo_rows = pl.ds(pl.multiple_of(i * TQ, TQ), HALF)
                hi_rows = pl.ds(pl.multiple_of(i * TQ + HALF, HALF), HALF)
                lhs_lo = jnp.concatenate([k_ref[0, lo_rows, key_lanes(h)], ka_ref[0, lo_rows, :]], axis=1)
                lhs_hi = jnp.concatenate([k_ref[0, hi_rows, key_lanes(h)], ka_ref[0, hi_rows, :]], axis=1)
                st_lo = jnp.dot(lhs_lo, rhs_ref[h], preferred_element_type=F32) + bias_ref[1, :HALF, :]
                st_hi = (jnp.dot(lhs_hi, rhs_ref[h, :, HALF:], preferred_element_type=F32)
                         + bias_ref[1, HALF:, HALF:])
                s_bufs[slot][hh, :HALF, :] = st_lo
                s_bufs[slot][hh, HALF:, HALF:] = st_hi
                mx_lo = jnp.max(st_lo, axis=0, keepdims=True)
                mx_hi = jnp.max(st_hi, axis=0, keepdims=True)
                mx_bufs[slot][hh] = jnp.concatenate([mx_lo[:, :HALF], jnp.maximum(mx_lo[:, HALF:], mx_hi)], axis=1)
                continue
            rows = pl.ds(pl.multiple_of(t * TQ, TQ), TQ)
            lhs = jnp.concatenate([k_ref[0, rows, key_lanes(h)], ka_ref[0, rows, :]], axis=1)
            st = jnp.dot(lhs, rhs_ref[h], preferred_element_type=F32)
            if kind == "any":
                st = st + bias_ref[(t == i).astype(jnp.int32)]
            s_bufs[slot][hh] = st
            mx_bufs[slot][hh] = jnp.max(st, axis=0, keepdims=True)

    def softmax(n, slot, kind):
        for hh in range(2):
            h, _ = item(n, hh)
            m_old = m_ref[h]
            m_new = jnp.maximum(m_old, mx_bufs[slot][hh])
            if kind == "diag":
                p_bufs[slot][hh, :HALF, :] = jnp.exp2(s_bufs[slot][hh, :HALF, :] - m_new).astype(BF16)
                p_bufs[slot][hh, HALF:, HALF:] = jnp.exp2(s_bufs[slot][hh, HALF:, HALF:] - m_new[:, HALF:]).astype(BF16)
            else:
                p_bufs[slot][hh] = jnp.exp2(s_bufs[slot][hh] - m_new).astype(BF16)
            al_bufs[slot][hh] = jnp.exp2(m_old - m_new)
            m_ref[h] = m_new

    def values(n, slot, kind):
        for hh in range(2):
            h, t = item(n, hh)
            cols = pl.ds(pl.multiple_of(t * TQ, TQ), TQ)
            head_rows = pl.ds(pl.multiple_of(jnp.minimum(h, N_HEADS - 1) * HEAD_DIM, HEAD_DIM), HEAD_DIM)
            vaug = jnp.concatenate([vt_ref[0, head_rows, cols], ones_rows], axis=0)
            if kind == "diag":
                new = jnp.dot(vaug[:, :HALF], p_bufs[slot][hh, :HALF, :], preferred_element_type=F32)
                new_hi = jnp.dot(vaug[:, HALF:], p_bufs[slot][hh, HALF:, HALF:], preferred_element_type=F32)
                new = jnp.concatenate([new[:, :HALF], new[:, HALF:] + new_hi], axis=1)
            else:
                new = jnp.dot(vaug, p_bufs[slot][hh], preferred_element_type=F32)
            acc_ref[h] = acc_ref[h] * al_bufs[slot][hh] + new

    def full_step(n, parity, kinds):
        softmax(n - 1, 1 - parity, kinds[1])
        scores(n, parity, kinds[0])
        values(n - 2, parity, kinds[2])

    n_off = noff_ref[b * n_tiles + i]
    for kind, wanted in (("off", True), ("any", False)):
        @pl.when((n_off >= 2) == wanted)
        def _():
            scores(0, 0, kind)
            softmax(0, 0, kind)
            scores(1, 1, kind)

    trips = lax.div(jnp.maximum(n_off - 2, 0), LOOP_STEPS)

    def body(u, carry):
        for j in range(LOOP_STEPS):
            full_step(2 + LOOP_STEPS * u + j, j % 2, ("off",) * 3)
        return carry

    lax.fori_loop(0, trips, body, 0)
    n0 = 2 + LOOP_STEPS * trips
    for left in range(2, LOOP_STEPS + N_PAIRS):
        kind = lambda j, left=left: "diag" if j >= max(left - N_PAIRS, 0) else "off"

        @pl.when(n_off + N_PAIRS - n0 == left)
        def _():
            for j in range(left):
                full_step(n0 + j, j % 2, (kind(j), kind(j - 1), kind(j - 2)))
            softmax(n0 + left - 1, (left - 1) % 2, kind(left - 1))
            values(n0 + left - 2, left % 2, kind(left - 2))
            values(n0 + left - 1, (left - 1) % 2, kind(left - 1))

    ya = []
    for p in range(N_PAIRS):
        outs = []
        for hh in range(2):
            a = acc_ref[2 * p + hh]
            outs.append(a[:HEAD_DIM] / a[HEAD_DIM:HEAD_DIM + 1])
        o = jnp.concatenate(outs, axis=0).T
        lanes = slice(p * LANES, (p + 1) * LANES)
        ya.append((o * _silu(g_ref[0, :, lanes].astype(F32))).astype(BF16))

    y = jnp.dot(jnp.concatenate(ya, axis=1), w_ref[0, 0:D_ATT, :], preferred_element_type=F32)
    y = y + jnp.dot(yl_ref[0], w_ref[0, D_ATT:D_ATT + D_LRU, :], preferred_element_type=F32)
    y = y + jnp.dot(ys_ref[0], w_ref[0, D_ATT + D_LRU:, :], preferred_element_type=F32)
    ms = jnp.mean(y * y, axis=-1, keepdims=True)
    yn = (y * lax.rsqrt(ms + EPS)) * pg_ref[0]
    o_ref[0] = x_ref[0] + _mod_row(gate_ref) * yn


def _tile_schedule(qs, ks, fs, fe):
    B, n_tiles = qs.shape[:2]
    qmax = qs[:, :, :, 0]
    kmax = ks[:, :, 0, :N_HEADS]
    f_first = fs[:, :, ::N_PARTS, 0]
    f_last = fe[:, :, ::N_PARTS, 0]
    gap = f_first[:, :, None, :] - f_last[:, None, :, :]
    reach = NORM_SLACK * qmax[:, :, None, :] * (kmax[:, None, :, :] + kmax[:, :, None, :])
    negligible = (gap + reach) <= -SKIP_LOG2
    tiles = jnp.arange(n_tiles)
    skip = negligible & (tiles[None, :] < tiles[:, None])[None, :, :, None]
    first = jnp.min(jnp.where(skip, n_tiles, tiles[None, None, :, None]), axis=2)
    first = first.transpose(0, 2, 1)
    visit = (tiles[None, None, None, :] >= first[:, :, :, None]) & (tiles[None, None, None, :] < tiles[None, None, :, None])
    flat = visit.transpose(0, 2, 1, 3).reshape(B, n_tiles, N_HEADS * n_tiles).astype(jnp.int32)
    pos = jnp.cumsum(flat, axis=2) - flat
    n_items = jnp.sum(flat, axis=2)
    n_off = (n_items + 1) // 2
    codes = jnp.arange(N_HEADS * n_tiles)
    slots = jnp.arange(N_HEADS * n_tiles)
    hit = (flat[:, :, None, :] == 1) & (pos[:, :, None, :] == slots[None, None, :, None])
    seq = jnp.sum(jnp.where(hit, codes[None, None, None, :], 0), axis=3)
    seq = jnp.where(slots[None, None, :] == n_items[:, :, None], N_HEADS * n_tiles, seq)
    head = slots[None, None, :] - 2 * n_off[:, :, None]
    seq = jnp.where((head >= 0) & (head < N_HEADS), head * n_tiles + tiles[None, :, None], seq)
    return seq.reshape(-1).astype(jnp.int32), n_off.reshape(-1).astype(jnp.int32)


def _attention_outproj(layer, schedule, qt, qa, k, ka, vt, g_att, yl, ys, x, mod, prep):
    B, S, D = x.shape
    per_layer = lambda a: pl.BlockSpec((1,) + a.shape[1:], lambda b, i, *_: (layer,) + (0,) * (a.ndim - 1))
    row_tile = lambda n: pl.BlockSpec((1, TQ, n), lambda b, i, *_: (b, i, 0))
    grid_spec = pltpu.PrefetchScalarGridSpec(
        num_scalar_prefetch=2,
        grid=(B, S // TQ),
        in_specs=[
            pl.BlockSpec((1, D_ATT, TQ), lambda b, i, *_: (b, 0, i)),
            pl.BlockSpec((1, LANES, TQ), lambda b, i, *_: (b, 0, i)),
            pl.BlockSpec((1, S, D_ATT), lambda b, i, *_: (b, 0, 0)),
            pl.BlockSpec((1, S, LANES), lambda b, i, *_: (b, 0, 0)),
            pl.BlockSpec((1, D_ATT, S), lambda b, i, *_: (b, 0, 0)),
            row_tile(D_ATT), row_tile(D_LRU), row_tile(D_SG), row_tile(D),
            pl.BlockSpec((1, 1, D // TN_ADA, 1, TN_ADA), lambda b, i, *_: (layer, b, 2, 0, 0)),
            per_layer(prep["post_g"]),
            per_layer(prep["w_out"]),
        ],
        out_specs=row_tile(D),
        scratch_shapes=[
            pltpu.VMEM((N_HEADS + 1, 1, TQ), F32),
            pltpu.VMEM((N_HEADS + 1, V_ROWS, TQ), F32),
            pltpu.VMEM((N_HEADS + 1, 2 * LANES, TQ), BF16),
            pltpu.VMEM((2, TQ, TQ), F32),
            pltpu.VMEM((2, TQ, TQ), F32),
            pltpu.VMEM((2, TQ, TQ), F32),
            pltpu.VMEM((2, TQ, TQ), BF16),
            pltpu.VMEM((2, TQ, TQ), BF16),
            pltpu.VMEM((2, 1, TQ), F32),
            pltpu.VMEM((2, 1, TQ), F32),
            pltpu.VMEM((2, 1, TQ), F32),
            pltpu.VMEM((2, 1, TQ), F32),
        ],
    )
    return pl.pallas_call(
        _attn_kernel,
        grid_spec=grid_spec,
        out_shape=jax.ShapeDtypeStruct((B, S, D), F32),
        compiler_params=_params(("arbitrary", "arbitrary")),
        name="fox_attention_outproj",
    )(*schedule, qt, qa, k, ka, vt, g_att, yl, ys, x, mod, prep["post_g"], prep["w_out"])


def _block_diag(w):
    L, G, n, _ = w.shape
    eye = jnp.eye(G, dtype=w.dtype)
    return (w[:, :, :, None, :] * eye[None, :, None, :, None]).reshape(L, G * n, G * n)


def _prepare(pre_g, post_g, w_in, b_f, conv_w, conv_b, lru_wa, lru_ba, lru_wx, lru_bx, lru_lambda,
             sg_ln_g, sg_ln_b, sg_w, sg_b, w_out):
    L, D, _ = w_in.shape
    cuts = [0]
    for n in IN_SIZES:
        cuts.append(cuts[-1] + n)
    w_bf = w_in.astype(BF16)
    wq, wk, wv, wf, wga, wxl, wgl, wsu, wsv, wgs = (w_bf[:, :, cuts[j]:cuts[j + 1]] for j in range(10))
    row = lambda a: a[:, None, :]
    wf3 = jnp.repeat(wf, N_PARTS, axis=2)
    pad = jnp.zeros((L, D, T_ROWS - 2 * D_ATT - N_AUG), BF16)
    return dict(
        pre_g=row(pre_g), post_g=row(post_g),
        we=jnp.concatenate([wxl, wsv], axis=2),
        wg=jnp.concatenate([wga, wgl, wsu, wgs], axis=2),
        wk=wk,
        wt=jnp.concatenate([wq, wv, wf3, pad], axis=2).transpose(0, 2, 1),
        bf=jnp.broadcast_to(jnp.repeat(b_f, N_PARTS, axis=1)[:, :, None], (L, N_AUG, LANES)),
        conv_w=conv_w, conv_b=row(conv_b),
        wa=_block_diag(lru_wa).astype(BF16), ba=row(lru_ba),
        wx=_block_diag(lru_wx).astype(BF16), bx=row(lru_bx), lam=row(lru_lambda),
        ln_g=row(sg_ln_g), ln_b=row(sg_ln_b),
        ws=sg_w.reshape(L, N_SG_GROUPS * SG_CHUNK, SG_CHUNK).astype(BF16),
        sg_bias=jnp.repeat(sg_b.transpose(0, 2, 1), HEAD_DIM, axis=2),
        w_out=w_out.astype(BF16),
        head_sum=(jnp.arange(D_ATT)[:, None] // HEAD_DIM == jnp.arange(LANES)[None, :]).astype(BF16),
    )


def kernel(x, c, ada_w, ada_b, pre_g, post_g, w_in, b_f, conv_w, conv_b, lru_wa, lru_ba, lru_wx,
           lru_bx, lru_lambda, sg_ln_g, sg_ln_b, sg_w, sg_b, w_out):
    mod = _ada(c, ada_w, ada_b)
    prep = _prepare(pre_g, post_g, w_in, b_f, conv_w, conv_b, lru_wa, lru_ba, lru_wx, lru_bx,
                    lru_lambda, sg_ln_g, sg_ln_b, sg_w, sg_b, w_out)
    for layer in range(ada_w.shape[0]):
        qt, vt, k, ka, qa, ga, yl, ys, qs, ks, fs, fe = _inproj(layer, x, mod, prep)
        x = _attention_outproj(layer, _tile_schedule(qs, ks, fs, fe), qt, qa, k, ka, vt, ga, yl, ys, x, mod, prep)
    return x
```

```python
import math

import jax
import jax.numpy as jnp
from jax import lax
from jax.experimental import pallas as pl
from jax.experimental.pallas import tpu as pltpu

HEAD_DIM = 64
D_ATT = 512
D_LRU = 256
D_SG = 256
N_HEADS = D_ATT // HEAD_DIM
N_PAIRS = N_HEADS // 2
N_SG_GROUPS = D_SG // HEAD_DIM
SG_CHUNK = 128
CONV_WIDTH = 4
LRU_C = 8.0
EPS = 1e-6
IN_SIZES = (D_ATT, D_ATT, D_ATT, N_HEADS, D_ATT, D_LRU, D_LRU, D_SG, D_SG, D_SG)

LANES = 128
SUBLANES = 8
BF16_ROWS = 16
VMEM_LIMIT = 56 * 1024 * 1024

TM = 512
TQ = 512
LOOP_STEPS = 2
TN_ADA = 1024
NEG = -1e30
assert TM == TQ

N_PARTS = 3
N_AUG = N_HEADS * N_PARTS
T_ROWS = 2 * D_ATT + 2 * BF16_ROWS
LOG2E = 1.4426950408889634
Q_SCALE = LOG2E / math.sqrt(HEAD_DIM)
V_ROWS = HEAD_DIM + BF16_ROWS
SKIP_LOG2 = 150.0
NORM_SLACK = 1.02
GELU_C0 = math.sqrt(2.0 / math.pi)
GELU_C1 = GELU_C0 * 0.044715

F32 = jnp.float32
BF16 = jnp.bfloat16


def _sigmoid(x):
    return 0.5 * jnp.tanh(0.5 * x) + 0.5


def _silu(x):
    hx = 0.5 * x
    return hx + hx * jnp.tanh(hx)


def _gelu_tanh(x):
    hx = 0.5 * x
    return hx + hx * jnp.tanh(x * (GELU_C0 + GELU_C1 * (x * x)))


def _log_sigmoid(x):
    return jnp.minimum(x, 0.0) - jnp.log1p(jnp.exp(-jnp.abs(x)))


def _split3(x, sel):
    hi = x.astype(BF16).astype(F32)
    r1 = x - hi
    mid = r1.astype(BF16).astype(F32)
    lo = r1 - mid
    return jnp.where(sel == 0, hi, jnp.where(sel == 1, mid, lo))


def _mod_row(ref):
    return jnp.concatenate([ref[0, 0, j] for j in range(ref.shape[2])], axis=1)


def _params(sem):
    return pltpu.CompilerParams(dimension_semantics=sem, vmem_limit_bytes=VMEM_LIMIT)


def _ada_kernel(ct_ref, w_ref, b_ref, o_ref):
    ct = ct_ref[...]
    sc = _silu(ct)
    w = w_ref[0]
    bias = b_ref[0]
    for b in range(ct.shape[1]):
        col = sc[:, b:b + 1]
        o_ref[0, b, 0] = jnp.sum(col * w, axis=0, keepdims=True) + bias


def _ada(c, ada_w, ada_b):
    L, D, N = ada_w.shape
    B = c.shape[0]
    return pl.pallas_call(
        _ada_kernel,
        grid=(L, N // TN_ADA),
        in_specs=[
            pl.BlockSpec((D, B), lambda l, n: (0, 0)),
            pl.BlockSpec((1, D, TN_ADA), lambda l, n: (l, 0, n)),
            pl.BlockSpec((1, 1, TN_ADA), lambda l, n: (l, 0, n)),
        ],
        out_specs=pl.BlockSpec((1, B, 1, 1, TN_ADA), lambda l, n: (l, 0, n, 0, 0)),
        out_shape=jax.ShapeDtypeStruct((L, B, N // TN_ADA, 1, TN_ADA), F32),
        compiler_params=_params(("parallel", "parallel")),
        name="ada_mod",
    )(c.T, ada_w, ada_b.reshape(L, 1, N))


def _lru_mixer(g_lru, r_pre, i_pre, lam, h_ref, xc):
    r = _sigmoid(r_pre)
    ig = _sigmoid(i_pre)
    nlam = -lam
    softplus = jnp.maximum(nlam, 0.0) + jnp.log1p(jnp.exp(-jnp.abs(nlam)))
    log_a = (-LRU_C * r) * softplus
    a = jnp.exp(log_a)
    bt = jnp.sqrt(jnp.tanh(-log_a) * (1.0 + a * a)) * (ig * xc)

    row8 = lax.broadcasted_iota(jnp.int32, (TM, D_LRU), 0) & (SUBLANES - 1)
    d = 1
    while d < SUBLANES:
        valid = row8 >= d
        a_s = jnp.where(valid, pltpu.roll(a, d, axis=0), 1.0)
        b_s = jnp.where(valid, pltpu.roll(bt, d, axis=0), 0.0)
        bt = bt + a * b_s
        a = a * a_s
        d *= 2

    h_prev = h_ref[...]
    sg = _silu(g_lru)
    outs = []
    for g in range(TM // SUBLANES):
        lo = g * SUBLANES
        hg = bt[lo:lo + SUBLANES] + a[lo:lo + SUBLANES] * h_prev
        h_prev = hg[SUBLANES - 1:SUBLANES, :]
        outs.append(hg * sg[lo:lo + SUBLANES])
    h_ref[...] = h_prev
    return jnp.concatenate(outs, axis=0)


def _inproj_kernel(x_ref, shift_ref, scale_ref, g_ref, we_ref, wg_ref, wk_ref, wt_ref, bf_ref,
                   cw_ref, cb_ref, wa_ref, ba_ref, wx_ref, bx_ref, lam_ref,
                   lng_ref, lnb_ref, ws_ref, sb_ref, he_ref,
                   qt_ref, vt_ref, k_ref, ka_ref, qa_ref, ga_ref, yl_ref, ys_ref,
                   qs_ref, ks_ref, fs_ref, fe_ref,
                   fc_ref, xbuf_ref, h_ref):
    i = pl.program_id(1)

    @pl.when(i == 0)
    def _():
        fc_ref[...] = jnp.zeros(fc_ref.shape, F32)
        xbuf_ref[0:SUBLANES, :] = jnp.zeros((SUBLANES, D_LRU), F32)
        h_ref[...] = jnp.zeros(h_ref.shape, F32)

    x = x_ref[0]
    ms = jnp.mean(x * x, axis=-1, keepdims=True)
    gs = g_ref[0] * (1.0 + _mod_row(scale_ref))
    h = (x * lax.rsqrt(ms + EPS)) * gs + _mod_row(shift_ref)
    hb = h.astype(BF16)

    early = jnp.dot(hb, we_ref[0], preferred_element_type=F32)
    x_lru = early[:, :D_LRU]
    sg_v = early[:, D_LRU:]
    tr = lax.dot_general(wt_ref[0], hb, (((1,), (1,)), ((), ())), preferred_element_type=F32)
    tq = tr[:D_ATT] * Q_SCALE
    qt_ref[0] = tq.astype(BF16)
    vt_ref[0] = tr[D_ATT:2 * D_ATT].astype(BF16)
    fl = tr[2 * D_ATT:2 * D_ATT + N_AUG]
    kb = jnp.dot(hb, wk_ref[0], preferred_element_type=F32).astype(BF16)
    k_ref[0] = kb
    qn2 = jnp.sum((tq * tq).reshape(N_HEADS, HEAD_DIM, TM), axis=1)
    qs_ref[0, 0] = jnp.broadcast_to(jnp.sqrt(jnp.max(qn2, axis=1, keepdims=True)), (N_HEADS, LANES))

    v = _gelu_tanh(sg_v)
    mu = jnp.mean(v, axis=-1, keepdims=True)
    var = jnp.mean(jnp.square(v - mu), axis=-1, keepdims=True)
    vnb = (((v - mu) * lax.rsqrt(var + EPS)) * lng_ref[0] + lnb_ref[0]).astype(BF16)
    w_shape = ws_ref.shape[1:]
    t_idx = lax.broadcasted_iota(jnp.int32, w_shape, 0) & (SG_CHUNK - 1)
    s_idx = lax.broadcasted_iota(jnp.int32, w_shape, 1)
    ws = jnp.where(s_idx <= t_idx, ws_ref[0], jnp.zeros(w_shape, BF16))
    grp = lax.broadcasted_iota(jnp.int32, (SG_CHUNK, D_SG), 1) // HEAD_DIM
    zs = []
    for c in range(TM // SG_CHUNK):
        zz = jnp.dot(ws, vnb[c * SG_CHUNK:(c + 1) * SG_CHUNK], preferred_element_type=F32)
        z = zz[0:SG_CHUNK]
        for gi in range(1, N_SG_GROUPS):
            z = jnp.where(grp == gi, zz[gi * SG_CHUNK:(gi + 1) * SG_CHUNK], z)
        zs.append(z + sb_ref[0])
    z_all = jnp.concatenate(zs, axis=0)

    xbuf_ref[SUBLANES:SUBLANES + TM, :] = x_lru
    xc = cb_ref[0]
    for kk in range(CONV_WIDTH):
        off = SUBLANES - (CONV_WIDTH - 1) + kk
        xc = xc + xbuf_ref[off:off + TM, :] * cw_ref[0, kk:kk + 1, :]
    xbuf_ref[0:SUBLANES, :] = x_lru[TM - SUBLANES:, :]
    xcb = xc.astype(BF16)
    r_pre = jnp.dot(xcb, wa_ref[0], preferred_element_type=F32) + ba_ref[0]
    i_pre = jnp.dot(xcb, wx_ref[0], preferred_element_type=F32) + bx_ref[0]

    gates = jnp.dot(hb, wg_ref[0], preferred_element_type=F32)
    kn2 = jnp.dot(kb * kb, he_ref[...], preferred_element_type=F32)
    ks_ref[0, 0] = jnp.sqrt(jnp.max(kn2, axis=0, keepdims=True))

    ga_ref[0] = gates[:, :D_ATT].astype(BF16)
    g_lru = gates[:, D_ATT:D_ATT + D_LRU]
    sg_u = gates[:, D_ATT + D_LRU:D_ATT + D_LRU + D_SG]
    g_sg = gates[:, D_ATT + D_LRU + D_SG:]
    ys_ref[0] = ((_gelu_tanh(sg_u) * z_all) * _silu(g_sg)).astype(BF16)
    yl_ref[0] = _lru_mixer(g_lru, r_pre, i_pre, lam_ref[0], h_ref, xc).astype(BF16)

    reps = TM // LANES
    ls = _log_sigmoid(fl + jnp.concatenate([bf_ref[0]] * reps, axis=1))
    lane = lax.broadcasted_iota(jnp.int32, (N_AUG, TM), 1)
    d = 1
    while d < TM:
        ls = ls + jnp.where(lane >= d, pltpu.roll(ls, d, axis=1), 0.0)
        d *= 2
    f = ls + jnp.concatenate([fc_ref[...]] * reps, axis=1)
    fc_ref[...] = jnp.broadcast_to(f[:, TM - 1:TM], (N_AUG, LANES))
    row = lax.broadcasted_iota(jnp.int32, (N_AUG, TM), 0)
    f2 = f * LOG2E
    fs_ref[0, 0] = jnp.broadcast_to(f2[:, 0:1], (N_AUG, LANES))
    fe_ref[0, 0] = jnp.broadcast_to(f2[:, TM - 1:TM], (N_AUG, LANES))
    parts = _split3(f2, row % N_PARTS)
    ones = jnp.ones((N_AUG, TM), F32)
    zeros = jnp.zeros((LANES - 2 * N_AUG, TM), F32)
    qa_ref[0] = jnp.concatenate([ones, parts, zeros], axis=0).astype(BF16)
    ka_ref[0] = jnp.concatenate([-parts, ones, zeros], axis=0).T.astype(BF16)


def _inproj(layer, x, mod, prep):
    B, S, D = x.shape
    per_layer = lambda a: pl.BlockSpec((1,) + a.shape[1:], lambda b, i: (layer,) + (0,) * (a.ndim - 1))
    mod_chunk = lambda j: pl.BlockSpec((1, 1, D // TN_ADA, 1, TN_ADA), lambda b, i: (layer, b, j, 0, 0))
    row_tile = lambda n: pl.BlockSpec((1, TM, n), lambda b, i: (b, i, 0))
    col_tile = lambda n: pl.BlockSpec((1, n, TM), lambda b, i: (b, 0, i))
    names = ("pre_g", "we", "wg", "wk", "wt", "bf", "conv_w", "conv_b", "wa", "ba", "wx", "bx", "lam",
             "ln_g", "ln_b", "ws", "sg_bias")
    operands = [prep[n] for n in names]
    head_sum = prep["head_sum"]
    stat = lambda rows: pl.BlockSpec((1, 1, rows, LANES), lambda b, i: (b, i, 0, 0))
    stat_shape = lambda rows: jax.ShapeDtypeStruct((B, S // TM, rows, LANES), F32)
    return pl.pallas_call(
        _inproj_kernel,
        grid=(B, S // TM),
        in_specs=[row_tile(D), mod_chunk(0), mod_chunk(1)] + [per_layer(a) for a in operands]
        + [pl.BlockSpec(head_sum.shape, lambda b, i: (0, 0))],
        out_specs=[
            col_tile(D_ATT), col_tile(D_ATT), row_tile(D_ATT), row_tile(LANES), col_tile(LANES),
            row_tile(D_ATT), row_tile(D_LRU), row_tile(D_SG),
            stat(N_HEADS), stat(1), stat(N_AUG), stat(N_AUG),
        ],
        out_shape=[
            jax.ShapeDtypeStruct((B, D_ATT, S), BF16),
            jax.ShapeDtypeStruct((B, D_ATT, S), BF16),
            jax.ShapeDtypeStruct((B, S, D_ATT), BF16),
            jax.ShapeDtypeStruct((B, S, LANES), BF16),
            jax.ShapeDtypeStruct((B, LANES, S), BF16),
            jax.ShapeDtypeStruct((B, S, D_ATT), BF16),
            jax.ShapeDtypeStruct((B, S, D_LRU), BF16),
            jax.ShapeDtypeStruct((B, S, D_SG), BF16),
            stat_shape(N_HEADS),
            stat_shape(1),
            stat_shape(N_AUG),
            stat_shape(N_AUG),
        ],
        scratch_shapes=[
            pltpu.VMEM((N_AUG, LANES), F32),
            pltpu.VMEM((TM + SUBLANES, D_LRU), F32),
            pltpu.VMEM((1, D_LRU), F32),
        ],
        compiler_params=_params(("parallel", "arbitrary")),
        name="inproj",
    )(x, mod, mod, *operands, head_sum)


def _attn_kernel(seq_ref, noff_ref, qt_ref, qa_ref, k_ref, ka_ref, vt_ref, g_ref,
                 yl_ref, ys_ref, x_ref, gate_ref, pg_ref, w_ref, o_ref,
                 m_ref, acc_ref, rhs_ref, bias_ref, s0_ref, s1_ref, p0_ref, p1_ref,
                 mx0_ref, mx1_ref, al0_ref, al1_ref):
    b = pl.program_id(0)
    i = pl.program_id(1)
    n_tiles = pl.num_programs(1)

    @pl.when((b == 0) & (i == 0))
    def _():
        kk = lax.broadcasted_iota(jnp.int32, (TQ, TQ), 0)
        qq = lax.broadcasted_iota(jnp.int32, (TQ, TQ), 1)
        bias_ref[0] = jnp.zeros((TQ, TQ), F32)
        bias_ref[1] = jnp.where(kk <= qq, 0.0, NEG)
        rhs_ref[...] = jnp.zeros(rhs_ref.shape, BF16)
        acc_ref[...] = jnp.zeros(acc_ref.shape, F32)

    qa = qa_ref[0, :2 * N_AUG, :]
    row = lax.broadcasted_iota(jnp.int32, (2 * N_AUG, TQ), 0)
    for h in range(N_HEADS):
        own = (h % 2) * HEAD_DIM
        rhs_ref[h, own:own + HEAD_DIM, :] = qt_ref[0, h * HEAD_DIM:(h + 1) * HEAD_DIM, :]
        lo = h * N_PARTS
        aug = ((row >= lo) & (row < lo + N_PARTS)) | ((row >= N_AUG + lo) & (row < N_AUG + lo + N_PARTS))
        rhs_ref[h, LANES:LANES + 2 * N_AUG, :] = jnp.where(aug, qa, jnp.zeros_like(qa))
    ones_rows = jnp.where(lax.broadcasted_iota(jnp.int32, (BF16_ROWS, TQ), 0) == 0, 1.0, 0.0).astype(BF16)
    m_ref[...] = jnp.full(m_ref.shape, NEG, F32)
    s_bufs = (s0_ref, s1_ref)
    p_bufs = (p0_ref, p1_ref)
    mx_bufs = (mx0_ref, mx1_ref)
    al_bufs = (al0_ref, al1_ref)

    def item(n, hh):
        code = seq_ref[(b * n_tiles + i) * (N_HEADS * n_tiles) + 2 * n + hh]
        return lax.div(code, n_tiles), lax.rem(code, n_tiles)

    HALF = TQ // 2

    def key_lanes(h):
        pair = lax.shift_right_logical(jnp.minimum(h, N_HEADS - 1), 1)
        return pl.ds(pl.multiple_of(pair * LANES, LANES), LANES)

    def scores(n, slot, kind):
        for hh in range(2):
            h, t = item(n, hh)
            if kind == "diag":
                lo_rows = pl.ds(pl.multiple_of(i * TQ, TQ), HALF)
                hi_rows = pl.ds(pl.multiple_of(i * TQ + HALF, HALF), HALF)
                lhs_lo = jnp.concatenate([k_ref[0, lo_rows, key_lanes(h)], ka_ref[0, lo_rows, :]], axis=1)
                lhs_hi = jnp.concatenate([k_ref[0, hi_rows, key_lanes(h)], ka_ref[0, hi_rows, :]], axis=1)
                st_lo = jnp.dot(lhs_lo, rhs_ref[h], preferred_element_type=F32) + bias_ref[1, :HALF, :]
                st_hi = (jnp.dot(lhs_hi, rhs_ref[h, :, HALF:], preferred_element_type=F32)
                         + bias_ref[1, HALF:, HALF:])
                s_bufs[slot][hh, :HALF, :] = st_lo
                s_bufs[slot][hh, HALF:, HALF:] = st_hi
                mx_lo = jnp.max(st_lo, axis=0, keepdims=True)
                mx_hi = jnp.max(st_hi, axis=0, keepdims=True)
                mx_bufs[slot][hh] = jnp.concatenate([mx_lo[:, :HALF], jnp.maximum(mx_lo[:, HALF:], mx_hi)], axis=1)
                continue
            rows = pl.ds(pl.multiple_of(t * TQ, TQ), TQ)
            lhs = jnp.concatenate([k_ref[0, rows, key_lanes(h)], ka_ref[0, rows, :]], axis=1)
            st = jnp.dot(lhs, rhs_ref[h], preferred_element_type=F32)
            if kind == "any":
                st = st + bias_ref[(t == i).astype(jnp.int32)]
            s_bufs[slot][hh] = st
            mx_bufs[slot][hh] = jnp.max(st, axis=0, keepdims=True)

    def softmax(n, slot, kind):
        for hh in range(2):
            h, _ = item(n, hh)
            m_old = m_ref[h]
            m_new = jnp.maximum(m_old, mx_bufs[slot][hh])
            if kind == "diag":
                p_bufs[slot][hh, :HALF, :] = jnp.exp2(s_bufs[slot][hh, :HALF, :] - m_new).astype(BF16)
                p_bufs[slot][hh, HALF:, HALF:] = jnp.exp2(s_bufs[slot][hh, HALF:, HALF:] - m_new[:, HALF:]).astype(BF16)
            else:
                p_bufs[slot][hh] = jnp.exp2(s_bufs[slot][hh] - m_new).astype(BF16)
            al_bufs[slot][hh] = jnp.exp2(m_old - m_new)
            m_ref[h] = m_new

    def values(n, slot, kind):
        for hh in range(2):
            h, t = item(n, hh)
            cols = pl.ds(pl.multiple_of(t * TQ, TQ), TQ)
            head_rows = pl.ds(pl.multiple_of(jnp.minimum(h, N_HEADS - 1) * HEAD_DIM, HEAD_DIM), HEAD_DIM)
            vaug = jnp.concatenate([vt_ref[0, head_rows, cols], ones_rows], axis=0)
            if kind == "diag":
                new = jnp.dot(vaug[:, :HALF], p_bufs[slot][hh, :HALF, :], preferred_element_type=F32)
                new_hi = jnp.dot(vaug[:, HALF:], p_bufs[slot][hh, HALF:, HALF:], preferred_element_type=F32)
                new = jnp.concatenate([new[:, :HALF], new[:, HALF:] + new_hi], axis=1)
            else:
                new = jnp.dot(vaug, p_bufs[slot][hh], preferred_element_type=F32)
            acc_ref[h] = acc_ref[h] * al_bufs[slot][hh] + new

    def full_step(n, parity, kinds):
        softmax(n - 1, 1 - parity, kinds[1])
        scores(n, parity, kinds[0])
        values(n - 2, parity, kinds[2])

    n_off = noff_ref[b * n_tiles + i]
    long_run = n_off >= 2 + LOOP_STEPS
    for kind, peel, wanted in (("off", LOOP_STEPS, long_run), ("off", 0, (n_off >= 2) & ~long_run),
                               ("any", 0, n_off < 2)):
        @pl.when(wanted)
        def _():
            scores(0, 0, kind)
            softmax(0, 0, kind)
            scores(1, 1, kind)
            for j in range(peel):
                full_step(2 + j, j % 2, ("off",) * 3)

    start = jnp.where(long_run, 2 + LOOP_STEPS, 2)
    trips = lax.div(jnp.maximum(n_off - start, 0), LOOP_STEPS)

    def body(u, carry):
        for j in range(LOOP_STEPS):
            full_step(start + LOOP_STEPS * u + j, j % 2, ("off",) * 3)
        return carry

    lax.fori_loop(0, trips, body, 0)
    n0 = start + LOOP_STEPS * trips
    for left in range(2, LOOP_STEPS + N_PAIRS):
        kind = lambda j, left=left: "diag" if j >= max(left - N_PAIRS, 0) else "off"

        @pl.when(n_off + N_PAIRS - n0 == left)
        def _():
            for j in range(left):
                full_step(n0 + j, j % 2, (kind(j), kind(j - 1), kind(j - 2)))
            softmax(n0 + left - 1, (left - 1) % 2, kind(left - 1))
            values(n0 + left - 2, left % 2, kind(left - 2))
            values(n0 + left - 1, (left - 1) % 2, kind(left - 1))

    ya = []
    for p in range(N_PAIRS):
        outs = []
        for hh in range(2):
            a = acc_ref[2 * p + hh]
            outs.append(a[:HEAD_DIM] / a[HEAD_DIM:HEAD_DIM + 1])
        o = jnp.concatenate(outs, axis=0).T
        lanes = slice(p * LANES, (p + 1) * LANES)
        ya.append((o * _silu(g_ref[0, :, lanes].astype(F32))).astype(BF16))

    y = jnp.dot(jnp.concatenate(ya, axis=1), w_ref[0, 0:D_ATT, :], preferred_element_type=F32)
    y = y + jnp.dot(yl_ref[0], w_ref[0, D_ATT:D_ATT + D_LRU, :], preferred_element_type=F32)
    y = y + jnp.dot(ys_ref[0], w_ref[0, D_ATT + D_LRU:, :], preferred_element_type=F32)
    ms = jnp.mean(y * y, axis=-1, keepdims=True)
    yn = (y * lax.rsqrt(ms + EPS)) * pg_ref[0]
    o_ref[0] = x_ref[0] + _mod_row(gate_ref) * yn


def _tile_schedule(qs, ks, fs, fe):
    B, n_tiles = qs.shape[:2]
    qmax = qs[:, :, :, 0]
    kmax = ks[:, :, 0, :N_HEADS]
    f_first = fs[:, :, ::N_PARTS, 0]
    f_last = fe[:, :, ::N_PARTS, 0]
    gap = f_first[:, :, None, :] - f_last[:, None, :, :]
    reach = NORM_SLACK * qmax[:, :, None, :] * (kmax[:, None, :, :] + kmax[:, :, None, :])
    negligible = (gap + reach) <= -SKIP_LOG2
    tiles = jnp.arange(n_tiles)
    skip = negligible & (tiles[None, :] < tiles[:, None])[None, :, :, None]
    first = jnp.min(jnp.where(skip, n_tiles, tiles[None, None, :, None]), axis=2)
    first = first.transpose(0, 2, 1)
    visit = (tiles[None, None, None, :] >= first[:, :, :, None]) & (tiles[None, None, None, :] < tiles[None, None, :, None])
    flat = visit.transpose(0, 2, 1, 3).reshape(B, n_tiles, N_HEADS * n_tiles).astype(jnp.int32)
    pos = jnp.cumsum(flat, axis=2) - flat
    n_items = jnp.sum(flat, axis=2)
    n_off = (n_items + 1) // 2
    codes = jnp.arange(N_HEADS * n_tiles)
    slots = jnp.arange(N_HEADS * n_tiles)
    hit = (flat[:, :, None, :] == 1) & (pos[:, :, None, :] == slots[None, None, :, None])
    seq = jnp.sum(jnp.where(hit, codes[None, None, None, :], 0), axis=3)
    seq = jnp.where(slots[None, None, :] == n_items[:, :, None], N_HEADS * n_tiles, seq)
    head = slots[None, None, :] - 2 * n_off[:, :, None]
    seq = jnp.where((head >= 0) & (head < N_HEADS), head * n_tiles + tiles[None, :, None], seq)
    return seq.reshape(-1).astype(jnp.int32), n_off.reshape(-1).astype(jnp.int32)


def _attention_outproj(layer, schedule, qt, qa, k, ka, vt, g_att, yl, ys, x, mod, prep):
    B, S, D = x.shape
    per_layer = lambda a: pl.BlockSpec((1,) + a.shape[1:], lambda b, i, *_: (layer,) + (0,) * (a.ndim - 1))
    row_tile = lambda n: pl.BlockSpec((1, TQ, n), lambda b, i, *_: (b, i, 0))
    grid_spec = pltpu.PrefetchScalarGridSpec(
        num_scalar_prefetch=2,
        grid=(B, S // TQ),
        in_specs=[
            pl.BlockSpec((1, D_ATT, TQ), lambda b, i, *_: (b, 0, i)),
            pl.BlockSpec((1, LANES, TQ), lambda b, i, *_: (b, 0, i)),
            pl.BlockSpec((1, S, D_ATT), lambda b, i, *_: (b, 0, 0)),
            pl.BlockSpec((1, S, LANES), lambda b, i, *_: (b, 0, 0)),
            pl.BlockSpec((1, D_ATT, S), lambda b, i, *_: (b, 0, 0)),
            row_tile(D_ATT), row_tile(D_LRU), row_tile(D_SG), row_tile(D),
            pl.BlockSpec((1, 1, D // TN_ADA, 1, TN_ADA), lambda b, i, *_: (layer, b, 2, 0, 0)),
            per_layer(prep["post_g"]),
            per_layer(prep["w_out"]),
        ],
        out_specs=row_tile(D),
        scratch_shapes=[
            pltpu.VMEM((N_HEADS + 1, 1, TQ), F32),
            pltpu.VMEM((N_HEADS + 1, V_ROWS, TQ), F32),
            pltpu.VMEM((N_HEADS + 1, 2 * LANES, TQ), BF16),
            pltpu.VMEM((2, TQ, TQ), F32),
            pltpu.VMEM((2, TQ, TQ), F32),
            pltpu.VMEM((2, TQ, TQ), F32),
            pltpu.VMEM((2, TQ, TQ), BF16),
            pltpu.VMEM((2, TQ, TQ), BF16),
            pltpu.VMEM((2, 1, TQ), F32),
            pltpu.VMEM((2, 1, TQ), F32),
            pltpu.VMEM((2, 1, TQ), F32),
            pltpu.VMEM((2, 1, TQ), F32),
        ],
    )
    return pl.pallas_call(
        _attn_kernel,
        grid_spec=grid_spec,
        out_shape=jax.ShapeDtypeStruct((B, S, D), F32),
        compiler_params=_params(("arbitrary", "arbitrary")),
        name="fox_attention_outproj",
    )(*schedule, qt, qa, k, ka, vt, g_att, yl, ys, x, mod, prep["post_g"], prep["w_out"])


def _block_diag(w):
    L, G, n, _ = w.shape
    eye = jnp.eye(G, dtype=w.dtype)
    return (w[:, :, :, None, :] * eye[None, :, None, :, None]).reshape(L, G * n, G * n)


def _prepare(pre_g, post_g, w_in, b_f, conv_w, conv_b, lru_wa, lru_ba, lru_wx, lru_bx, lru_lambda,
             sg_ln_g, sg_ln_b, sg_w, sg_b, w_out):
    L, D, _ = w_in.shape
    cuts = [0]
    for n in IN_SIZES:
        cuts.append(cuts[-1] + n)
    w_bf = w_in.astype(BF16)
    wq, wk, wv, wf, wga, wxl, wgl, wsu, wsv, wgs = (w_bf[:, :, cuts[j]:cuts[j + 1]] for j in range(10))
    row = lambda a: a[:, None, :]
    wf3 = jnp.repeat(wf, N_PARTS, axis=2)
    pad = jnp.zeros((L, D, T_ROWS - 2 * D_ATT - N_AUG), BF16)
    return dict(
        pre_g=row(pre_g), post_g=row(post_g),
        we=jnp.concatenate([wxl, wsv], axis=2),
        wg=jnp.concatenate([wga, wgl, wsu, wgs], axis=2),
        wk=wk,
        wt=jnp.concatenate([wq, wv, wf3, pad], axis=2).transpose(0, 2, 1),
        bf=jnp.broadcast_to(jnp.repeat(b_f, N_PARTS, axis=1)[:, :, None], (L, N_AUG, LANES)),
        conv_w=conv_w, conv_b=row(conv_b),
        wa=_block_diag(lru_wa).astype(BF16), ba=row(lru_ba),
        wx=_block_diag(lru_wx).astype(BF16), bx=row(lru_bx), lam=row(lru_lambda),
        ln_g=row(sg_ln_g), ln_b=row(sg_ln_b),
        ws=sg_w.reshape(L, N_SG_GROUPS * SG_CHUNK, SG_CHUNK).astype(BF16),
        sg_bias=jnp.repeat(sg_b.transpose(0, 2, 1), HEAD_DIM, axis=2),
        w_out=w_out.astype(BF16),
        head_sum=(jnp.arange(D_ATT)[:, None] // HEAD_DIM == jnp.arange(LANES)[None, :]).astype(BF16),
    )


def kernel(x, c, ada_w, ada_b, pre_g, post_g, w_in, b_f, conv_w, conv_b, lru_wa, lru_ba, lru_wx,
           lru_bx, lru_lambda, sg_ln_g, sg_ln_b, sg_w, sg_b, w_out):
    mod = _ada(c, ada_w, ada_b)
    prep = _prepare(pre_g, post_g, w_in, b_f, conv_w, conv_b, lru_wa, lru_ba, lru_wx, lru_bx,
                    lru_lambda, sg_ln_g, sg_ln_b, sg_w, sg_b, w_out)
    for layer in range(ada_w.shape[0]):
        qt, vt, k, ka, qa, ga, yl, ys, qs, ks, fs, fe = _inproj(layer, x, mod, prep)
        x = _attention_outproj(layer, _tile_schedule(qs, ks, fs, fe), qt, qa, k, ka, vt, ga, yl, ys, x, mod, prep)
    return x
```

```python
import math

import jax
import jax.numpy as jnp
from jax import lax
from jax.experimental import pallas as pl
from jax.experimental.pallas import tpu as pltpu

HEAD_DIM = 64
D_ATT = 512
D_LRU = 256
D_SG = 256
N_HEADS = D_ATT // HEAD_DIM
N_PAIRS = N_HEADS // 2
N_SG_GROUPS = D_SG // HEAD_DIM
SG_CHUNK = 128
CONV_WIDTH = 4
LRU_C = 8.0
EPS = 1e-6
IN_SIZES = (D_ATT, D_ATT, D_ATT, N_HEADS, D_ATT, D_LRU, D_LRU, D_SG, D_SG, D_SG)

LANES = 128
SUBLANES = 8
BF16_ROWS = 16
VMEM_LIMIT = 56 * 1024 * 1024

TM = 512
TQ = 512
LOOP_STEPS = 2
TN_ADA = 1024
NEG = -1e30
assert TM == TQ

N_PARTS = 3
N_AUG = N_HEADS * N_PARTS
T_ROWS = 2 * D_ATT + 2 * BF16_ROWS
LOG2E = 1.4426950408889634
Q_SCALE = LOG2E / math.sqrt(HEAD_DIM)
V_ROWS = HEAD_DIM + BF16_ROWS
SKIP_LOG2 = 40.0
NORM_SLACK = 1.02
GELU_C0 = math.sqrt(2.0 / math.pi)
GELU_C1 = GELU_C0 * 0.044715

F32 = jnp.float32
BF16 = jnp.bfloat16


def _sigmoid(x):
    return 0.5 * jnp.tanh(0.5 * x) + 0.5


def _silu(x):
    hx = 0.5 * x
    return hx + hx * jnp.tanh(hx)


def _gelu_tanh(x):
    hx = 0.5 * x
    return hx + hx * jnp.tanh(x * (GELU_C0 + GELU_C1 * (x * x)))


def _log_sigmoid(x):
    return jnp.minimum(x, 0.0) - jnp.log1p(jnp.exp(-jnp.abs(x)))


def _split3(x, sel):
    hi = x.astype(BF16).astype(F32)
    r1 = x - hi
    mid = r1.astype(BF16).astype(F32)
    lo = r1 - mid
    return jnp.where(sel == 0, hi, jnp.where(sel == 1, mid, lo))


def _mod_row(ref):
    return jnp.concatenate([ref[0, 0, j] for j in range(ref.shape[2])], axis=1)


def _params(sem):
    return pltpu.CompilerParams(dimension_semantics=sem, vmem_limit_bytes=VMEM_LIMIT)


def _ada_kernel(ct_ref, w_ref, b_ref, o_ref):
    ct = ct_ref[...]
    sc = _silu(ct)
    w = w_ref[0]
    bias = b_ref[0]
    for b in range(ct.shape[1]):
        col = sc[:, b:b + 1]
        o_ref[0, b, 0] = jnp.sum(col * w, axis=0, keepdims=True) + bias


def _ada(c, ada_w, ada_b):
    L, D, N = ada_w.shape
    B = c.shape[0]
    return pl.pallas_call(
        _ada_kernel,
        grid=(L, N // TN_ADA),
        in_specs=[
            pl.BlockSpec((D, B), lambda l, n: (0, 0)),
            pl.BlockSpec((1, D, TN_ADA), lambda l, n: (l, 0, n)),
            pl.BlockSpec((1, 1, TN_ADA), lambda l, n: (l, 0, n)),
        ],
        out_specs=pl.BlockSpec((1, B, 1, 1, TN_ADA), lambda l, n: (l, 0, n, 0, 0)),
        out_shape=jax.ShapeDtypeStruct((L, B, N // TN_ADA, 1, TN_ADA), F32),
        compiler_params=_params(("parallel", "parallel")),
        name="ada_mod",
    )(c.T, ada_w, ada_b.reshape(L, 1, N))


def _lru_mixer(g_lru, r_pre, i_pre, lam, h_ref, xc):
    r = _sigmoid(r_pre)
    ig = _sigmoid(i_pre)
    nlam = -lam
    softplus = jnp.maximum(nlam, 0.0) + jnp.log1p(jnp.exp(-jnp.abs(nlam)))
    log_a = (-LRU_C * r) * softplus
    a = jnp.exp(log_a)
    bt = jnp.sqrt(jnp.tanh(-log_a) * (1.0 + a * a)) * (ig * xc)

    row8 = lax.broadcasted_iota(jnp.int32, (TM, D_LRU), 0) & (SUBLANES - 1)
    d = 1
    while d < SUBLANES:
        valid = row8 >= d
        a_s = jnp.where(valid, pltpu.roll(a, d, axis=0), 1.0)
        b_s = jnp.where(valid, pltpu.roll(bt, d, axis=0), 0.0)
        bt = bt + a * b_s
        a = a * a_s
        d *= 2

    h_prev = h_ref[...]
    sg = _silu(g_lru)
    outs = []
    for g in range(TM // SUBLANES):
        lo = g * SUBLANES
        hg = bt[lo:lo + SUBLANES] + a[lo:lo + SUBLANES] * h_prev
        h_prev = hg[SUBLANES - 1:SUBLANES, :]
        outs.append(hg * sg[lo:lo + SUBLANES])
    h_ref[...] = h_prev
    return jnp.concatenate(outs, axis=0)


def _inproj_kernel(x_ref, shift_ref, scale_ref, g_ref, we_ref, wg_ref, wk_ref, wt_ref, bf_ref,
                   cw_ref, cb_ref, wa_ref, ba_ref, wx_ref, bx_ref, lam_ref,
                   lng_ref, lnb_ref, ws_ref, sb_ref, he_ref,
                   qt_ref, vt_ref, k_ref, ka_ref, qa_ref, ga_ref, yl_ref, ys_ref,
                   qs_ref, ks_ref, fs_ref, fe_ref,
                   fc_ref, xbuf_ref, h_ref):
    i = pl.program_id(1)

    @pl.when(i == 0)
    def _():
        fc_ref[...] = jnp.zeros(fc_ref.shape, F32)
        xbuf_ref[0:SUBLANES, :] = jnp.zeros((SUBLANES, D_LRU), F32)
        h_ref[...] = jnp.zeros(h_ref.shape, F32)

    x = x_ref[0]
    ms = jnp.mean(x * x, axis=-1, keepdims=True)
    gs = g_ref[0] * (1.0 + _mod_row(scale_ref))
    h = (x * lax.rsqrt(ms + EPS)) * gs + _mod_row(shift_ref)
    hb = h.astype(BF16)

    early = jnp.dot(hb, we_ref[0], preferred_element_type=F32)
    x_lru = early[:, :D_LRU]
    sg_v = early[:, D_LRU:]
    tr = lax.dot_general(wt_ref[0], hb, (((1,), (1,)), ((), ())), preferred_element_type=F32)
    tq = tr[:D_ATT] * Q_SCALE
    qt_ref[0] = tq.astype(BF16)
    vt_ref[0] = tr[D_ATT:2 * D_ATT].astype(BF16)
    fl = tr[2 * D_ATT:2 * D_ATT + N_AUG]
    kb = jnp.dot(hb, wk_ref[0], preferred_element_type=F32).astype(BF16)
    k_ref[0] = kb
    qn2 = jnp.sum((tq * tq).reshape(N_HEADS, HEAD_DIM, TM), axis=1)
    qs_ref[0, 0] = jnp.broadcast_to(jnp.sqrt(jnp.max(qn2, axis=1, keepdims=True)), (N_HEADS, LANES))

    v = _gelu_tanh(sg_v)
    mu = jnp.mean(v, axis=-1, keepdims=True)
    var = jnp.mean(jnp.square(v - mu), axis=-1, keepdims=True)
    vnb = (((v - mu) * lax.rsqrt(var + EPS)) * lng_ref[0] + lnb_ref[0]).astype(BF16)
    w_shape = ws_ref.shape[1:]
    t_idx = lax.broadcasted_iota(jnp.int32, w_shape, 0) & (SG_CHUNK - 1)
    s_idx = lax.broadcasted_iota(jnp.int32, w_shape, 1)
    ws = jnp.where(s_idx <= t_idx, ws_ref[0], jnp.zeros(w_shape, BF16))
    grp = lax.broadcasted_iota(jnp.int32, (SG_CHUNK, D_SG), 1) // HEAD_DIM
    zs = []
    for c in range(TM // SG_CHUNK):
        zz = jnp.dot(ws, vnb[c * SG_CHUNK:(c + 1) * SG_CHUNK], preferred_element_type=F32)
        z = zz[0:SG_CHUNK]
        for gi in range(1, N_SG_GROUPS):
            z = jnp.where(grp == gi, zz[gi * SG_CHUNK:(gi + 1) * SG_CHUNK], z)
        zs.append(z + sb_ref[0])
    z_all = jnp.concatenate(zs, axis=0)

    xbuf_ref[SUBLANES:SUBLANES + TM, :] = x_lru
    xc = cb_ref[0]
    for kk in range(CONV_WIDTH):
        off = SUBLANES - (CONV_WIDTH - 1) + kk
        xc = xc + xbuf_ref[off:off + TM, :] * cw_ref[0, kk:kk + 1, :]
    xbuf_ref[0:SUBLANES, :] = x_lru[TM - SUBLANES:, :]
    xcb = xc.astype(BF16)
    r_pre = jnp.dot(xcb, wa_ref[0], preferred_element_type=F32) + ba_ref[0]
    i_pre = jnp.dot(xcb, wx_ref[0], preferred_element_type=F32) + bx_ref[0]

    gates = jnp.dot(hb, wg_ref[0], preferred_element_type=F32)
    kn2 = jnp.dot(kb * kb, he_ref[...], preferred_element_type=F32)
    ks_ref[0, 0] = jnp.sqrt(jnp.max(kn2, axis=0, keepdims=True))

    ga_ref[0] = gates[:, :D_ATT].astype(BF16)
    g_lru = gates[:, D_ATT:D_ATT + D_LRU]
    sg_u = gates[:, D_ATT + D_LRU:D_ATT + D_LRU + D_SG]
    g_sg = gates[:, D_ATT + D_LRU + D_SG:]
    ys_ref[0] = ((_gelu_tanh(sg_u) * z_all) * _silu(g_sg)).astype(BF16)
    yl_ref[0] = _lru_mixer(g_lru, r_pre, i_pre, lam_ref[0], h_ref, xc).astype(BF16)

    reps = TM // LANES
    ls = _log_sigmoid(fl + jnp.concatenate([bf_ref[0]] * reps, axis=1))
    lane = lax.broadcasted_iota(jnp.int32, (N_AUG, TM), 1)
    d = 1
    while d < TM:
        ls = ls + jnp.where(lane >= d, pltpu.roll(ls, d, axis=1), 0.0)
        d *= 2
    f = ls + jnp.concatenate([fc_ref[...]] * reps, axis=1)
    fc_ref[...] = jnp.broadcast_to(f[:, TM - 1:TM], (N_AUG, LANES))
    row = lax.broadcasted_iota(jnp.int32, (N_AUG, TM), 0)
    f2 = f * LOG2E
    fs_ref[0, 0] = jnp.broadcast_to(f2[:, 0:1], (N_AUG, LANES))
    fe_ref[0, 0] = jnp.broadcast_to(f2[:, TM - 1:TM], (N_AUG, LANES))
    parts = _split3(f2, row % N_PARTS)
    ones = jnp.ones((N_AUG, TM), F32)
    zeros = jnp.zeros((LANES - 2 * N_AUG, TM), F32)
    qa_ref[0] = jnp.concatenate([ones, parts, zeros], axis=0).astype(BF16)
    ka_ref[0] = jnp.concatenate([-parts, ones, zeros], axis=0).T.astype(BF16)


def _inproj(layer, x, mod, prep):
    B, S, D = x.shape
    per_layer = lambda a: pl.BlockSpec((1,) + a.shape[1:], lambda b, i: (layer,) + (0,) * (a.ndim - 1))
    mod_chunk = lambda j: pl.BlockSpec((1, 1, D // TN_ADA, 1, TN_ADA), lambda b, i: (layer, b, j, 0, 0))
    row_tile = lambda n: pl.BlockSpec((1, TM, n), lambda b, i: (b, i, 0))
    col_tile = lambda n: pl.BlockSpec((1, n, TM), lambda b, i: (b, 0, i))
    names = ("pre_g", "we", "wg", "wk", "wt", "bf", "conv_w", "conv_b", "wa", "ba", "wx", "bx", "lam",
             "ln_g", "ln_b", "ws", "sg_bias")
    operands = [prep[n] for n in names]
    head_sum = prep["head_sum"]
    stat = lambda rows: pl.BlockSpec((1, 1, rows, LANES), lambda b, i: (b, i, 0, 0))
    stat_shape = lambda rows: jax.ShapeDtypeStruct((B, S // TM, rows, LANES), F32)
    return pl.pallas_call(
        _inproj_kernel,
        grid=(B, S // TM),
        in_specs=[row_tile(D), mod_chunk(0), mod_chunk(1)] + [per_layer(a) for a in operands]
        + [pl.BlockSpec(head_sum.shape, lambda b, i: (0, 0))],
        out_specs=[
            col_tile(D_ATT), col_tile(D_ATT), row_tile(D_ATT), row_tile(LANES), col_tile(LANES),
            row_tile(D_ATT), row_tile(D_LRU), row_tile(D_SG),
            stat(N_HEADS), stat(1), stat(N_AUG), stat(N_AUG),
        ],
        out_shape=[
            jax.ShapeDtypeStruct((B, D_ATT, S), BF16),
            jax.ShapeDtypeStruct((B, D_ATT, S), BF16),
            jax.ShapeDtypeStruct((B, S, D_ATT), BF16),
            jax.ShapeDtypeStruct((B, S, LANES), BF16),
            jax.ShapeDtypeStruct((B, LANES, S), BF16),
            jax.ShapeDtypeStruct((B, S, D_ATT), BF16),
            jax.ShapeDtypeStruct((B, S, D_LRU), BF16),
            jax.ShapeDtypeStruct((B, S, D_SG), BF16),
            stat_shape(N_HEADS),
            stat_shape(1),
            stat_shape(N_AUG),
            stat_shape(N_AUG),
        ],
        scratch_shapes=[
            pltpu.VMEM((N_AUG, LANES), F32),
            pltpu.VMEM((TM + SUBLANES, D_LRU), F32),
            pltpu.VMEM((1, D_LRU), F32),
        ],
        compiler_params=_params(("parallel", "arbitrary")),
        name="inproj",
    )(x, mod, mod, *operands, head_sum)


def _attn_kernel(seq_ref, noff_ref, qt_ref, qa_ref, k_ref, ka_ref, vt_ref, g_ref,
                 yl_ref, ys_ref, x_ref, gate_ref, pg_ref, w_ref, o_ref,
                 m_ref, acc_ref, rhs_ref, bias_ref, s0_ref, s1_ref, p0_ref, p1_ref,
                 mx0_ref, mx1_ref, al0_ref, al1_ref):
    b = pl.program_id(0)
    i = pl.program_id(1)
    n_tiles = pl.num_programs(1)

    @pl.when((b == 0) & (i == 0))
    def _():
        kk = lax.broadcasted_iota(jnp.int32, (TQ, TQ), 0)
        qq = lax.broadcasted_iota(jnp.int32, (TQ, TQ), 1)
        bias_ref[0] = jnp.zeros((TQ, TQ), F32)
        bias_ref[1] = jnp.where(kk <= qq, 0.0, NEG)
        rhs_ref[...] = jnp.zeros(rhs_ref.shape, BF16)
        acc_ref[...] = jnp.zeros(acc_ref.shape, F32)

    qa = qa_ref[0, :2 * N_AUG, :]
    row = lax.broadcasted_iota(jnp.int32, (2 * N_AUG, TQ), 0)
    for h in range(N_HEADS):
        own = (h % 2) * HEAD_DIM
        rhs_ref[h, own:own + HEAD_DIM, :] = qt_ref[0, h * HEAD_DIM:(h + 1) * HEAD_DIM, :]
        lo = h * N_PARTS
        aug = ((row >= lo) & (row < lo + N_PARTS)) | ((row >= N_AUG + lo) & (row < N_AUG + lo + N_PARTS))
        rhs_ref[h, LANES:LANES + 2 * N_AUG, :] = jnp.where(aug, qa, jnp.zeros_like(qa))
    ones_rows = jnp.where(lax.broadcasted_iota(jnp.int32, (BF16_ROWS, TQ), 0) == 0, 1.0, 0.0).astype(BF16)
    m_ref[...] = jnp.full(m_ref.shape, NEG, F32)
    s_bufs = (s0_ref, s1_ref)
    p_bufs = (p0_ref, p1_ref)
    mx_bufs = (mx0_ref, mx1_ref)
    al_bufs = (al0_ref, al1_ref)

    def item(n, hh):
        code = seq_ref[(b * n_tiles + i) * (N_HEADS * n_tiles) + 2 * n + hh]
        return lax.div(code, n_tiles), lax.rem(code, n_tiles)

    HALF = TQ // 2

    def key_lanes(h):
        pair = lax.shift_right_logical(jnp.minimum(h, N_HEADS - 1), 1)
        return pl.ds(pl.multiple_of(pair * LANES, LANES), LANES)

    def scores(n, slot, kind):
        for hh in range(2):
            h, t = item(n, hh)
            if kind == "diag":
                lo_rows = pl.ds(pl.multiple_of(i * TQ, TQ), HALF)
                hi_rows = pl.ds(pl.multiple_of(i * TQ + HALF, HALF), HALF)
                lhs_lo = jnp.concatenate([k_ref[0, lo_rows, key_lanes(h)], ka_ref[0, lo_rows, :]], axis=1)
                lhs_hi = jnp.concatenate([k_ref[0, hi_rows, key_lanes(h)], ka_ref[0, hi_rows, :]], axis=1)
                st_lo = jnp.dot(lhs_lo, rhs_ref[h], preferred_element_type=F32) + bias_ref[1, :HALF, :]
                st_hi = (jnp.dot(lhs_hi, rhs_ref[h, :, HALF:], preferred_element_type=F32)
                         + bias_ref[1, HALF:, HALF:])
                s_bufs[slot][hh, :HALF, :] = st_lo
                s_bufs[slot][hh, HALF:, HALF:] = st_hi
                mx_lo = jnp.max(st_lo, axis=0, keepdims=True)
                mx_hi = jnp.max(st_hi, axis=0, keepdims=True)
                mx_bufs[slot][hh] = jnp.concatenate([mx_lo[:, :HALF], jnp.maximum(mx_lo[:, HALF:], mx_hi)], axis=1)
                continue
            rows = pl.ds(pl.multiple_of(t * TQ, TQ), TQ)
            lhs = jnp.concatenate([k_ref[0, rows, key_lanes(h)], ka_ref[0, rows, :]], axis=1)
            st = jnp.dot(lhs, rhs_ref[h], preferred_element_type=F32)
            if kind == "any":
                st = st + bias_ref[(t == i).astype(jnp.int32)]
            s_bufs[slot][hh] = st
            mx_bufs[slot][hh] = jnp.max(st, axis=0, keepdims=True)

    def softmax(n, slot, kind):
        for hh in range(2):
            h, _ = item(n, hh)
            m_old = m_ref[h]
            m_new = jnp.maximum(m_old, mx_bufs[slot][hh])
            if kind == "diag":
                p_bufs[slot][hh, :HALF, :] = jnp.exp2(s_bufs[slot][hh, :HALF, :] - m_new).astype(BF16)
                p_bufs[slot][hh, HALF:, HALF:] = jnp.exp2(s_bufs[slot][hh, HALF:, HALF:] - m_new[:, HALF:]).astype(BF16)
            else:
                p_bufs[slot][hh] = jnp.exp2(s_bufs[slot][hh] - m_new).astype(BF16)
            al_bufs[slot][hh] = jnp.exp2(m_old - m_new)
            m_ref[h] = m_new

    def values(n, slot, kind):
        for hh in range(2):
            h, t = item(n, hh)
            cols = pl.ds(pl.multiple_of(t * TQ, TQ), TQ)
            head_rows = pl.ds(pl.multiple_of(jnp.minimum(h, N_HEADS - 1) * HEAD_DIM, HEAD_DIM), HEAD_DIM)
            vaug = jnp.concatenate([vt_ref[0, head_rows, cols], ones_rows], axis=0)
            if kind == "diag":
                new = jnp.dot(vaug[:, :HALF], p_bufs[slot][hh, :HALF, :], preferred_element_type=F32)
                new_hi = jnp.dot(vaug[:, HALF:], p_bufs[slot][hh, HALF:, HALF:], preferred_element_type=F32)
                new = jnp.concatenate([new[:, :HALF], new[:, HALF:] + new_hi], axis=1)
            else:
                new = jnp.dot(vaug, p_bufs[slot][hh], preferred_element_type=F32)
            acc_ref[h] = acc_ref[h] * al_bufs[slot][hh] + new

    def full_step(n, parity, kinds):
        softmax(n - 1, 1 - parity, kinds[1])
        scores(n, parity, kinds[0])
        values(n - 2, parity, kinds[2])

    n_off = noff_ref[b * n_tiles + i]
    long_run = n_off >= 2 + LOOP_STEPS
    for kind, peel, wanted in (("off", LOOP_STEPS, long_run), ("off", 0, (n_off >= 2) & ~long_run),
                               ("any", 0, n_off < 2)):
        @pl.when(wanted)
        def _():
            scores(0, 0, kind)
            softmax(0, 0, kind)
            scores(1, 1, kind)
            for j in range(peel):
                full_step(2 + j, j % 2, ("off",) * 3)

    start = jnp.where(long_run, 2 + LOOP_STEPS, 2)
    trips = lax.div(jnp.maximum(n_off - start, 0), LOOP_STEPS)

    def body(u, carry):
        for j in range(LOOP_STEPS):
            full_step(start + LOOP_STEPS * u + j, j % 2, ("off",) * 3)
        return carry

    lax.fori_loop(0, trips, body, 0)
    n0 = start + LOOP_STEPS * trips
    for left in range(2, LOOP_STEPS + N_PAIRS):
        kind = lambda j, left=left: "diag" if j >= max(left - N_PAIRS, 0) else "off"

        @pl.when(n_off + N_PAIRS - n0 == left)
        def _():
            for j in range(left):
                full_step(n0 + j, j % 2, (kind(j), kind(j - 1), kind(j - 2)))
            softmax(n0 + left - 1, (left - 1) % 2, kind(left - 1))
            values(n0 + left - 2, left % 2, kind(left - 2))
            values(n0 + left - 1, (left - 1) % 2, kind(left - 1))

    ya = []
    for p in range(N_PAIRS):
        outs = []
        for hh in range(2):
            a = acc_ref[2 * p + hh]
            outs.append(a[:HEAD_DIM] / a[HEAD_DIM:HEAD_DIM + 1])
        o = jnp.concatenate(outs, axis=0).T
        lanes = slice(p * LANES, (p + 1) * LANES)
        ya.append((o * _silu(g_ref[0, :, lanes].astype(F32))).astype(BF16))

    y = jnp.dot(jnp.concatenate(ya, axis=1), w_ref[0, 0:D_ATT, :], preferred_element_type=F32)
    y = y + jnp.dot(yl_ref[0], w_ref[0, D_ATT:D_ATT + D_LRU, :], preferred_element_type=F32)
    y = y + jnp.dot(ys_ref[0], w_ref[0, D_ATT + D_LRU:, :], preferred_element_type=F32)
    ms = jnp.mean(y * y, axis=-1, keepdims=True)
    yn = (y * lax.rsqrt(ms + EPS)) * pg_ref[0]
    o_ref[0] = x_ref[0] + _mod_row(gate_ref) * yn


def _tile_schedule(qs, ks, fs, fe):
    B, n_tiles = qs.shape[:2]
    qmax = qs[:, :, :, 0]
    kmax = ks[:, :, 0, :N_HEADS]
    f_first = fs[:, :, ::N_PARTS, 0]
    f_last = fe[:, :, ::N_PARTS, 0]
    gap = f_first[:, :, None, :] - f_last[:, None, :, :]
    reach = NORM_SLACK * qmax[:, :, None, :] * (kmax[:, None, :, :] + kmax[:, :, None, :])
    negligible = (gap + reach) <= -SKIP_LOG2
    tiles = jnp.arange(n_tiles)
    skip = negligible & (tiles[None, :] < tiles[:, None])[None, :, :, None]
    first = jnp.min(jnp.where(skip, n_tiles, tiles[None, None, :, None]), axis=2)
    first = first.transpose(0, 2, 1)
    visit = (tiles[None, None, None, :] >= first[:, :, :, None]) & (tiles[None, None, None, :] < tiles[None, None, :, None])
    flat = visit.transpose(0, 2, 1, 3).reshape(B, n_tiles, N_HEADS * n_tiles).astype(jnp.int32)
    pos = jnp.cumsum(flat, axis=2) - flat
    n_items = jnp.sum(flat, axis=2)
    n_off = (n_items + 1) // 2
    codes = jnp.arange(N_HEADS * n_tiles)
    slots = jnp.arange(N_HEADS * n_tiles)
    hit = (flat[:, :, None, :] == 1) & (pos[:, :, None, :] == slots[None, None, :, None])
    seq = jnp.sum(jnp.where(hit, codes[None, None, None, :], 0), axis=3)
    seq = jnp.where(slots[None, None, :] == n_items[:, :, None], N_HEADS * n_tiles, seq)
    head = slots[None, None, :] - 2 * n_off[:, :, None]
    seq = jnp.where((head >= 0) & (head < N_HEADS), head * n_tiles + tiles[None, :, None], seq)
    return seq.reshape(-1).astype(jnp.int32), n_off.reshape(-1).astype(jnp.int32)


def _attention_outproj(layer, schedule, qt, qa, k, ka, vt, g_att, yl, ys, x, mod, prep):
    B, S, D = x.shape
    per_layer = lambda a: pl.BlockSpec((1,) + a.shape[1:], lambda b, i, *_: (layer,) + (0,) * (a.ndim - 1))
    row_tile = lambda n: pl.BlockSpec((1, TQ, n), lambda b, i, *_: (b, i, 0))
    grid_spec = pltpu.PrefetchScalarGridSpec(
        num_scalar_prefetch=2,
        grid=(B, S // TQ),
        in_specs=[
            pl.BlockSpec((1, D_ATT, TQ), lambda b, i, *_: (b, 0, i)),
            pl.BlockSpec((1, LANES, TQ), lambda b, i, *_: (b, 0, i)),
            pl.BlockSpec((1, S, D_ATT), lambda b, i, *_: (b, 0, 0)),
            pl.BlockSpec((1, S, LANES), lambda b, i, *_: (b, 0, 0)),
            pl.BlockSpec((1, D_ATT, S), lambda b, i, *_: (b, 0, 0)),
            row_tile(D_ATT), row_tile(D_LRU), row_tile(D_SG), row_tile(D),
            pl.BlockSpec((1, 1, D // TN_ADA, 1, TN_ADA), lambda b, i, *_: (layer, b, 2, 0, 0)),
            per_layer(prep["post_g"]),
            per_layer(prep["w_out"]),
        ],
        out_specs=row_tile(D),
        scratch_shapes=[
            pltpu.VMEM((N_HEADS + 1, 1, TQ), F32),
            pltpu.VMEM((N_HEADS + 1, V_ROWS, TQ), F32),
            pltpu.VMEM((N_HEADS + 1, 2 * LANES, TQ), BF16),
            pltpu.VMEM((2, TQ, TQ), F32),
            pltpu.VMEM((2, TQ, TQ), F32),
            pltpu.VMEM((2, TQ, TQ), F32),
            pltpu.VMEM((2, TQ, TQ), BF16),
            pltpu.VMEM((2, TQ, TQ), BF16),
            pltpu.VMEM((2, 1, TQ), F32),
            pltpu.VMEM((2, 1, TQ), F32),
            pltpu.VMEM((2, 1, TQ), F32),
            pltpu.VMEM((2, 1, TQ), F32),
        ],
    )
    return pl.pallas_call(
        _attn_kernel,
        grid_spec=grid_spec,
        out_shape=jax.ShapeDtypeStruct((B, S, D), F32),
        compiler_params=_params(("arbitrary", "arbitrary")),
        name="fox_attention_outproj",
    )(*schedule, qt, qa, k, ka, vt, g_att, yl, ys, x, mod, prep["post_g"], prep["w_out"])


def _block_diag(w):
    L, G, n, _ = w.shape
    eye = jnp.eye(G, dtype=w.dtype)
    return (w[:, :, :, None, :] * eye[None, :, None, :, None]).reshape(L, G * n, G * n)


def _prepare(pre_g, post_g, w_in, b_f, conv_w, conv_b, lru_wa, lru_ba, lru_wx, lru_bx, lru_lambda,
             sg_ln_g, sg_ln_b, sg_w, sg_b, w_out):
    L, D, _ = w_in.shape
    cuts = [0]
    for n in IN_SIZES:
        cuts.append(cuts[-1] + n)
    w_bf = w_in.astype(BF16)
    wq, wk, wv, wf, wga, wxl, wgl, wsu, wsv, wgs = (w_bf[:, :, cuts[j]:cuts[j + 1]] for j in range(10))
    row = lambda a: a[:, None, :]
    wf3 = jnp.repeat(wf, N_PARTS, axis=2)
    pad = jnp.zeros((L, D, T_ROWS - 2 * D_ATT - N_AUG), BF16)
    return dict(
        pre_g=row(pre_g), post_g=row(post_g),
        we=jnp.concatenate([wxl, wsv], axis=2),
        wg=jnp.concatenate([wga, wgl, wsu, wgs], axis=2),
        wk=wk,
        wt=jnp.concatenate([wq, wv, wf3, pad], axis=2).transpose(0, 2, 1),
        bf=jnp.broadcast_to(jnp.repeat(b_f, N_PARTS, axis=1)[:, :, None], (L, N_AUG, LANES)),
        conv_w=conv_w, conv_b=row(conv_b),
        wa=_block_diag(lru_wa).astype(BF16), ba=row(lru_ba),
        wx=_block_diag(lru_wx).astype(BF16), bx=row(lru_bx), lam=row(lru_lambda),
        ln_g=row(sg_ln_g), ln_b=row(sg_ln_b),
        ws=sg_w.reshape(L, N_SG_GROUPS * SG_CHUNK, SG_CHUNK).astype(BF16),
        sg_bias=jnp.repeat(sg_b.transpose(0, 2, 1), HEAD_DIM, axis=2),
        w_out=w_out.astype(BF16),
        head_sum=(jnp.arange(D_ATT)[:, None] // HEAD_DIM == jnp.arange(LANES)[None, :]).astype(BF16),
    )


def kernel(x, c, ada_w, ada_b, pre_g, post_g, w_in, b_f, conv_w, conv_b, lru_wa, lru_ba, lru_wx,
           lru_bx, lru_lambda, sg_ln_g, sg_ln_b, sg_w, sg_b, w_out):
    mod = _ada(c, ada_w, ada_b)
    prep = _prepare(pre_g, post_g, w_in, b_f, conv_w, conv_b, lru_wa, lru_ba, lru_wx, lru_bx,
                    lru_lambda, sg_ln_g, sg_ln_b, sg_w, sg_b, w_out)
    for layer in range(ada_w.shape[0]):
        qt, vt, k, ka, qa, ga, yl, ys, qs, ks, fs, fe = _inproj(layer, x, mod, prep)
        x = _attention_outproj(layer, _tile_schedule(qs, ks, fs, fe), qt, qa, k, ka, vt, ga, yl, ys, x, mod, prep)
    return x
```

```python
import math

import jax
import jax.numpy as jnp
from jax import lax
from jax.experimental import pallas as pl
from jax.experimental.pallas import tpu as pltpu

HEAD_DIM = 64
D_ATT = 512
D_LRU = 256
D_SG = 256
N_HEADS = D_ATT // HEAD_DIM
N_PAIRS = N_HEADS // 2
N_SG_GROUPS = D_SG // HEAD_DIM
SG_CHUNK = 128
CONV_WIDTH = 4
LRU_C = 8.0
EPS = 1e-6
IN_SIZES = (D_ATT, D_ATT, D_ATT, N_HEADS, D_ATT, D_LRU, D_LRU, D_SG, D_SG, D_SG)

LANES = 128
SUBLANES = 8
BF16_ROWS = 16
VMEM_LIMIT = 56 * 1024 * 1024

TM = 512
TQ = 512
LOOP_STEPS = 2
TN_ADA = 1024
NEG = -1e30
assert TM == TQ

N_PARTS = 3
N_AUG = N_HEADS * N_PARTS
T_ROWS = 2 * D_ATT + 2 * BF16_ROWS
LOG2E = 1.4426950408889634
Q_SCALE = LOG2E / math.sqrt(HEAD_DIM)
V_ROWS = HEAD_DIM + BF16_ROWS
SKIP_MASS = 2.0 ** -26
NORM_SLACK = 1.02
GELU_C0 = math.sqrt(2.0 / math.pi)
GELU_C1 = GELU_C0 * 0.044715

F32 = jnp.float32
BF16 = jnp.bfloat16


def _sigmoid(x):
    return 0.5 * jnp.tanh(0.5 * x) + 0.5


def _silu(x):
    hx = 0.5 * x
    return hx + hx * jnp.tanh(hx)


def _gelu_tanh(x):
    hx = 0.5 * x
    return hx + hx * jnp.tanh(x * (GELU_C0 + GELU_C1 * (x * x)))


def _log_sigmoid(x):
    return jnp.minimum(x, 0.0) - jnp.log1p(jnp.exp(-jnp.abs(x)))


def _split3(x, sel):
    hi = x.astype(BF16).astype(F32)
    r1 = x - hi
    mid = r1.astype(BF16).astype(F32)
    lo = r1 - mid
    return jnp.where(sel == 0, hi, jnp.where(sel == 1, mid, lo))


def _mod_row(ref):
    return jnp.concatenate([ref[0, 0, j] for j in range(ref.shape[2])], axis=1)


def _params(sem):
    return pltpu.CompilerParams(dimension_semantics=sem, vmem_limit_bytes=VMEM_LIMIT)


def _ada_kernel(ct_ref, w_ref, b_ref, o_ref):
    ct = ct_ref[...]
    sc = _silu(ct)
    w = w_ref[0]
    bias = b_ref[0]
    for b in range(ct.shape[1]):
        col = sc[:, b:b + 1]
        o_ref[0, b, 0] = jnp.sum(col * w, axis=0, keepdims=True) + bias


def _ada(c, ada_w, ada_b):
    L, D, N = ada_w.shape
    B = c.shape[0]
    return pl.pallas_call(
        _ada_kernel,
        grid=(L, N // TN_ADA),
        in_specs=[
            pl.BlockSpec((D, B), lambda l, n: (0, 0)),
            pl.BlockSpec((1, D, TN_ADA), lambda l, n: (l, 0, n)),
            pl.BlockSpec((1, 1, TN_ADA), lambda l, n: (l, 0, n)),
        ],
        out_specs=pl.BlockSpec((1, B, 1, 1, TN_ADA), lambda l, n: (l, 0, n, 0, 0)),
        out_shape=jax.ShapeDtypeStruct((L, B, N // TN_ADA, 1, TN_ADA), F32),
        compiler_params=_params(("parallel", "parallel")),
        name="ada_mod",
    )(c.T, ada_w, ada_b.reshape(L, 1, N))


def _lru_mixer(g_lru, r_pre, i_pre, lam, h_ref, xc):
    r = _sigmoid(r_pre)
    ig = _sigmoid(i_pre)
    nlam = -lam
    softplus = jnp.maximum(nlam, 0.0) + jnp.log1p(jnp.exp(-jnp.abs(nlam)))
    log_a = (-LRU_C * r) * softplus
    a = jnp.exp(log_a)
    bt = jnp.sqrt(jnp.tanh(-log_a) * (1.0 + a * a)) * (ig * xc)

    row8 = lax.broadcasted_iota(jnp.int32, (TM, D_LRU), 0) & (SUBLANES - 1)
    d = 1
    while d < SUBLANES:
        valid = row8 >= d
        a_s = jnp.where(valid, pltpu.roll(a, d, axis=0), 1.0)
        b_s = jnp.where(valid, pltpu.roll(bt, d, axis=0), 0.0)
        bt = bt + a * b_s
        a = a * a_s
        d *= 2

    h_prev = h_ref[...]
    sg = _silu(g_lru)
    outs = []
    for g in range(TM // SUBLANES):
        lo = g * SUBLANES
        hg = bt[lo:lo + SUBLANES] + a[lo:lo + SUBLANES] * h_prev
        h_prev = hg[SUBLANES - 1:SUBLANES, :]
        outs.append(hg * sg[lo:lo + SUBLANES])
    h_ref[...] = h_prev
    return jnp.concatenate(outs, axis=0)


def _inproj_kernel(x_ref, shift_ref, scale_ref, g_ref, we_ref, wg_ref, wk_ref, wt_ref, bf_ref,
                   cw_ref, cb_ref, wa_ref, ba_ref, wx_ref, bx_ref, lam_ref,
                   lng_ref, lnb_ref, ws_ref, sb_ref, he_ref,
                   qt_ref, vt_ref, k_ref, ka_ref, qa_ref, ga_ref, yl_ref, ys_ref,
                   qs_ref, ks_ref, fs_ref, fe_ref,
                   fc_ref, xbuf_ref, h_ref):
    i = pl.program_id(1)

    @pl.when(i == 0)
    def _():
        fc_ref[...] = jnp.zeros(fc_ref.shape, F32)
        xbuf_ref[0:SUBLANES, :] = jnp.zeros((SUBLANES, D_LRU), F32)
        h_ref[...] = jnp.zeros(h_ref.shape, F32)

    x = x_ref[0]
    ms = jnp.mean(x * x, axis=-1, keepdims=True)
    gs = g_ref[0] * (1.0 + _mod_row(scale_ref))
    h = (x * lax.rsqrt(ms + EPS)) * gs + _mod_row(shift_ref)
    hb = h.astype(BF16)

    early = jnp.dot(hb, we_ref[0], preferred_element_type=F32)
    x_lru = early[:, :D_LRU]
    sg_v = early[:, D_LRU:]
    tr = lax.dot_general(wt_ref[0], hb, (((1,), (1,)), ((), ())), preferred_element_type=F32)
    tq = tr[:D_ATT] * Q_SCALE
    qt_ref[0] = tq.astype(BF16)
    vt_ref[0] = tr[D_ATT:2 * D_ATT].astype(BF16)
    fl = tr[2 * D_ATT:2 * D_ATT + N_AUG]
    kb = jnp.dot(hb, wk_ref[0], preferred_element_type=F32).astype(BF16)
    k_ref[0] = kb
    qn2 = jnp.sum((tq * tq).reshape(N_HEADS, HEAD_DIM, TM), axis=1)
    qs_ref[0, 0] = jnp.broadcast_to(jnp.sqrt(jnp.max(qn2, axis=1, keepdims=True)), (N_HEADS, LANES))

    v = _gelu_tanh(sg_v)
    mu = jnp.mean(v, axis=-1, keepdims=True)
    var = jnp.mean(jnp.square(v - mu), axis=-1, keepdims=True)
    vnb = (((v - mu) * lax.rsqrt(var + EPS)) * lng_ref[0] + lnb_ref[0]).astype(BF16)
    w_shape = ws_ref.shape[1:]
    t_idx = lax.broadcasted_iota(jnp.int32, w_shape, 0) & (SG_CHUNK - 1)
    s_idx = lax.broadcasted_iota(jnp.int32, w_shape, 1)
    ws = jnp.where(s_idx <= t_idx, ws_ref[0], jnp.zeros(w_shape, BF16))
    grp = lax.broadcasted_iota(jnp.int32, (SG_CHUNK, D_SG), 1) // HEAD_DIM
    zs = []
    for c in range(TM // SG_CHUNK):
        zz = jnp.dot(ws, vnb[c * SG_CHUNK:(c + 1) * SG_CHUNK], preferred_element_type=F32)
        z = zz[0:SG_CHUNK]
        for gi in range(1, N_SG_GROUPS):
            z = jnp.where(grp == gi, zz[gi * SG_CHUNK:(gi + 1) * SG_CHUNK], z)
        zs.append(z + sb_ref[0])
    z_all = jnp.concatenate(zs, axis=0)

    xbuf_ref[SUBLANES:SUBLANES + TM, :] = x_lru
    xc = cb_ref[0]
    for kk in range(CONV_WIDTH):
        off = SUBLANES - (CONV_WIDTH - 1) + kk
        xc = xc + xbuf_ref[off:off + TM, :] * cw_ref[0, kk:kk + 1, :]
    xbuf_ref[0:SUBLANES, :] = x_lru[TM - SUBLANES:, :]
    xcb = xc.astype(BF16)
    r_pre = jnp.dot(xcb, wa_ref[0], preferred_element_type=F32) + ba_ref[0]
    i_pre = jnp.dot(xcb, wx_ref[0], preferred_element_type=F32) + bx_ref[0]

    gates = jnp.dot(hb, wg_ref[0], preferred_element_type=F32)
    kn2 = jnp.dot(kb * kb, he_ref[...], preferred_element_type=F32)
    ks_ref[0, 0] = jnp.sqrt(jnp.max(kn2, axis=0, keepdims=True))

    ga_ref[0] = gates[:, :D_ATT].astype(BF16)
    g_lru = gates[:, D_ATT:D_ATT + D_LRU]
    sg_u = gates[:, D_ATT + D_LRU:D_ATT + D_LRU + D_SG]
    g_sg = gates[:, D_ATT + D_LRU + D_SG:]
    ys_ref[0] = ((_gelu_tanh(sg_u) * z_all) * _silu(g_sg)).astype(BF16)
    yl_ref[0] = _lru_mixer(g_lru, r_pre, i_pre, lam_ref[0], h_ref, xc).astype(BF16)

    reps = TM // LANES
    ls = _log_sigmoid(fl + jnp.concatenate([bf_ref[0]] * reps, axis=1))
    lane = lax.broadcasted_iota(jnp.int32, (N_AUG, TM), 1)
    d = 1
    while d < TM:
        ls = ls + jnp.where(lane >= d, pltpu.roll(ls, d, axis=1), 0.0)
        d *= 2
    f = ls + jnp.concatenate([fc_ref[...]] * reps, axis=1)
    fc_ref[...] = jnp.broadcast_to(f[:, TM - 1:TM], (N_AUG, LANES))
    row = lax.broadcasted_iota(jnp.int32, (N_AUG, TM), 0)
    f2 = f * LOG2E
    fs_ref[0, 0] = jnp.broadcast_to(f2[:, 0:1], (N_AUG, LANES))
    fe_ref[0, 0] = jnp.broadcast_to(f2[:, TM - 1:TM], (N_AUG, LANES))
    parts = _split3(f2, row % N_PARTS)
    ones = jnp.ones((N_AUG, TM), F32)
    zeros = jnp.zeros((LANES - 2 * N_AUG, TM), F32)
    qa_ref[0] = jnp.concatenate([ones, parts, zeros], axis=0).astype(BF16)
    ka_ref[0] = jnp.concatenate([-parts, ones, zeros], axis=0).T.astype(BF16)


def _inproj(layer, x, mod, prep):
    B, S, D = x.shape
    per_layer = lambda a: pl.BlockSpec((1,) + a.shape[1:], lambda b, i: (layer,) + (0,) * (a.ndim - 1))
    mod_chunk = lambda j: pl.BlockSpec((1, 1, D // TN_ADA, 1, TN_ADA), lambda b, i: (layer, b, j, 0, 0))
    row_tile = lambda n: pl.BlockSpec((1, TM, n), lambda b, i: (b, i, 0))
    col_tile = lambda n: pl.BlockSpec((1, n, TM), lambda b, i: (b, 0, i))
    names = ("pre_g", "we", "wg", "wk", "wt", "bf", "conv_w", "conv_b", "wa", "ba", "wx", "bx", "lam",
             "ln_g", "ln_b", "ws", "sg_bias")
    operands = [prep[n] for n in names]
    head_sum = prep["head_sum"]
    stat = lambda rows: pl.BlockSpec((1, 1, rows, LANES), lambda b, i: (b, i, 0, 0))
    stat_shape = lambda rows: jax.ShapeDtypeStruct((B, S // TM, rows, LANES), F32)
    return pl.pallas_call(
        _inproj_kernel,
        grid=(B, S // TM),
        in_specs=[row_tile(D), mod_chunk(0), mod_chunk(1)] + [per_layer(a) for a in operands]
        + [pl.BlockSpec(head_sum.shape, lambda b, i: (0, 0))],
        out_specs=[
            col_tile(D_ATT), col_tile(D_ATT), row_tile(D_ATT), row_tile(LANES), col_tile(LANES),
            row_tile(D_ATT), row_tile(D_LRU), row_tile(D_SG),
            stat(N_HEADS), stat(1), stat(N_AUG), stat(N_AUG),
        ],
        out_shape=[
            jax.ShapeDtypeStruct((B, D_ATT, S), BF16),
            jax.ShapeDtypeStruct((B, D_ATT, S), BF16),
            jax.ShapeDtypeStruct((B, S, D_ATT), BF16),
            jax.ShapeDtypeStruct((B, S, LANES), BF16),
            jax.ShapeDtypeStruct((B, LANES, S), BF16),
            jax.ShapeDtypeStruct((B, S, D_ATT), BF16),
            jax.ShapeDtypeStruct((B, S, D_LRU), BF16),
            jax.ShapeDtypeStruct((B, S, D_SG), BF16),
            stat_shape(N_HEADS),
            stat_shape(1),
            stat_shape(N_AUG),
            stat_shape(N_AUG),
        ],
        scratch_shapes=[
            pltpu.VMEM((N_AUG, LANES), F32),
            pltpu.VMEM((TM + SUBLANES, D_LRU), F32),
            pltpu.VMEM((1, D_LRU), F32),
        ],
        compiler_params=_params(("parallel", "arbitrary")),
        name="inproj",
    )(x, mod, mod, *operands, head_sum)


def _attn_kernel(seq_ref, noff_ref, qt_ref, qa_ref, k_ref, ka_ref, vt_ref, g_ref,
                 yl_ref, ys_ref, x_ref, gate_ref, pg_ref, w_ref, o_ref,
                 m_ref, acc_ref, rhs_ref, bias_ref, s0_ref, s1_ref, p0_ref, p1_ref,
                 mx0_ref, mx1_ref, al0_ref, al1_ref):
    b = pl.program_id(0)
    i = pl.program_id(1)
    n_tiles = pl.num_programs(1)

    @pl.when((b == 0) & (i == 0))
    def _():
        kk = lax.broadcasted_iota(jnp.int32, (TQ, TQ), 0)
        qq = lax.broadcasted_iota(jnp.int32, (TQ, TQ), 1)
        bias_ref[0] = jnp.zeros((TQ, TQ), F32)
        bias_ref[1] = jnp.where(kk <= qq, 0.0, NEG)
        rhs_ref[...] = jnp.zeros(rhs_ref.shape, BF16)
        acc_ref[...] = jnp.zeros(acc_ref.shape, F32)

    qa = qa_ref[0, :2 * N_AUG, :]
    row = lax.broadcasted_iota(jnp.int32, (2 * N_AUG, TQ), 0)
    for h in range(N_HEADS):
        own = (h % 2) * HEAD_DIM
        rhs_ref[h, own:own + HEAD_DIM, :] = qt_ref[0, h * HEAD_DIM:(h + 1) * HEAD_DIM, :]
        lo = h * N_PARTS
        aug = ((row >= lo) & (row < lo + N_PARTS)) | ((row >= N_AUG + lo) & (row < N_AUG + lo + N_PARTS))
        rhs_ref[h, LANES:LANES + 2 * N_AUG, :] = jnp.where(aug, qa, jnp.zeros_like(qa))
    ones_rows = jnp.where(lax.broadcasted_iota(jnp.int32, (BF16_ROWS, TQ), 0) == 0, 1.0, 0.0).astype(BF16)
    m_ref[...] = jnp.full(m_ref.shape, NEG, F32)
    s_bufs = (s0_ref, s1_ref)
    p_bufs = (p0_ref, p1_ref)
    mx_bufs = (mx0_ref, mx1_ref)
    al_bufs = (al0_ref, al1_ref)

    def item(n, hh):
        code = seq_ref[(b * n_tiles + i) * (N_HEADS * n_tiles) + 2 * n + hh]
        return lax.div(code, n_tiles), lax.rem(code, n_tiles)

    HALF = TQ // 2

    def key_lanes(h):
        pair = lax.shift_right_logical(jnp.minimum(h, N_HEADS - 1), 1)
        return pl.ds(pl.multiple_of(pair * LANES, LANES), LANES)

    def scores(n, slot, kind):
        for hh in range(2):
            h, t = item(n, hh)
            if kind == "diag":
                lo_rows = pl.ds(pl.multiple_of(i * TQ, TQ), HALF)
                hi_rows = pl.ds(pl.multiple_of(i * TQ + HALF, HALF), HALF)
                lhs_lo = jnp.concatenate([k_ref[0, lo_rows, key_lanes(h)], ka_ref[0, lo_rows, :]], axis=1)
                lhs_hi = jnp.concatenate([k_ref[0, hi_rows, key_lanes(h)], ka_ref[0, hi_rows, :]], axis=1)
                st_lo = jnp.dot(lhs_lo, rhs_ref[h], preferred_element_type=F32) + bias_ref[1, :HALF, :]
                st_hi = (jnp.dot(lhs_hi, rhs_ref[h, :, HALF:], preferred_element_type=F32)
                         + bias_ref[1, HALF:, HALF:])
                s_bufs[slot][hh, :HALF, :] = st_lo
                s_bufs[slot][hh, HALF:, HALF:] = st_hi
                mx_lo = jnp.max(st_lo, axis=0, keepdims=True)
                mx_hi = jnp.max(st_hi, axis=0, keepdims=True)
                mx_bufs[slot][hh] = jnp.concatenate([mx_lo[:, :HALF], jnp.maximum(mx_lo[:, HALF:], mx_hi)], axis=1)
                continue
            rows = pl.ds(pl.multiple_of(t * TQ, TQ), TQ)
            lhs = jnp.concatenate([k_ref[0, rows, key_lanes(h)], ka_ref[0, rows, :]], axis=1)
            st = jnp.dot(lhs, rhs_ref[h], preferred_element_type=F32)
            if kind == "any":
                st = st + bias_ref[(t == i).astype(jnp.int32)]
            s_bufs[slot][hh] = st
            mx_bufs[slot][hh] = jnp.max(st, axis=0, keepdims=True)

    def softmax(n, slot, kind):
        for hh in range(2):
            h, _ = item(n, hh)
            m_old = m_ref[h]
            m_new = jnp.maximum(m_old, mx_bufs[slot][hh])
            if kind == "diag":
                p_bufs[slot][hh, :HALF, :] = jnp.exp2(s_bufs[slot][hh, :HALF, :] - m_new).astype(BF16)
                p_bufs[slot][hh, HALF:, HALF:] = jnp.exp2(s_bufs[slot][hh, HALF:, HALF:] - m_new[:, HALF:]).astype(BF16)
            else:
                p_bufs[slot][hh] = jnp.exp2(s_bufs[slot][hh] - m_new).astype(BF16)
            al_bufs[slot][hh] = jnp.exp2(m_old - m_new)
            m_ref[h] = m_new

    def values(n, slot, kind):
        for hh in range(2):
            h, t = item(n, hh)
            cols = pl.ds(pl.multiple_of(t * TQ, TQ), TQ)
            head_rows = pl.ds(pl.multiple_of(jnp.minimum(h, N_HEADS - 1) * HEAD_DIM, HEAD_DIM), HEAD_DIM)
            vaug = jnp.concatenate([vt_ref[0, head_rows, cols], ones_rows], axis=0)
            if kind == "diag":
                new = jnp.dot(vaug[:, :HALF], p_bufs[slot][hh, :HALF, :], preferred_element_type=F32)
                new_hi = jnp.dot(vaug[:, HALF:], p_bufs[slot][hh, HALF:, HALF:], preferred_element_type=F32)
                new = jnp.concatenate([new[:, :HALF], new[:, HALF:] + new_hi], axis=1)
            else:
                new = jnp.dot(vaug, p_bufs[slot][hh], preferred_element_type=F32)
            acc_ref[h] = acc_ref[h] * al_bufs[slot][hh] + new

    def full_step(n, parity, kinds):
        softmax(n - 1, 1 - parity, kinds[1])
        scores(n, parity, kinds[0])
        values(n - 2, parity, kinds[2])

    n_off = noff_ref[b * n_tiles + i]
    long_run = n_off >= 2 + LOOP_STEPS
    for kind, peel, wanted in (("off", LOOP_STEPS, long_run), ("off", 0, (n_off >= 2) & ~long_run),
                               ("any", 0, n_off < 2)):
        @pl.when(wanted)
        def _():
            scores(0, 0, kind)
            softmax(0, 0, kind)
            scores(1, 1, kind)
            for j in range(peel):
                full_step(2 + j, j % 2, ("off",) * 3)

    start = jnp.where(long_run, 2 + LOOP_STEPS, 2)
    trips = lax.div(jnp.maximum(n_off - start, 0), LOOP_STEPS)

    def body(u, carry):
        for j in range(LOOP_STEPS):
            full_step(start + LOOP_STEPS * u + j, j % 2, ("off",) * 3)
        return carry

    lax.fori_loop(0, trips, body, 0)
    n0 = start + LOOP_STEPS * trips
    for left in range(2, LOOP_STEPS + N_PAIRS):
        kind = lambda j, left=left: "diag" if j >= max(left - N_PAIRS, 0) else "off"

        @pl.when(n_off + N_PAIRS - n0 == left)
        def _():
            for j in range(left):
                full_step(n0 + j, j % 2, (kind(j), kind(j - 1), kind(j - 2)))
            softmax(n0 + left - 1, (left - 1) % 2, kind(left - 1))
            values(n0 + left - 2, left % 2, kind(left - 2))
            values(n0 + left - 1, (left - 1) % 2, kind(left - 1))

    ya = []
    for p in range(N_PAIRS):
        outs = []
        for hh in range(2):
            a = acc_ref[2 * p + hh]
            outs.append(a[:HEAD_DIM] / a[HEAD_DIM:HEAD_DIM + 1])
        o = jnp.concatenate(outs, axis=0).T
        lanes = slice(p * LANES, (p + 1) * LANES)
        ya.append((o * _silu(g_ref[0, :, lanes].astype(F32))).astype(BF16))

    y = jnp.dot(jnp.concatenate(ya, axis=1), w_ref[0, 0:D_ATT, :], preferred_element_type=F32)
    y = y + jnp.dot(yl_ref[0], w_ref[0, D_ATT:D_ATT + D_LRU, :], preferred_element_type=F32)
    y = y + jnp.dot(ys_ref[0], w_ref[0, D_ATT + D_LRU:, :], preferred_element_type=F32)
    ms = jnp.mean(y * y, axis=-1, keepdims=True)
    yn = (y * lax.rsqrt(ms + EPS)) * pg_ref[0]
    o_ref[0] = x_ref[0] + _mod_row(gate_ref) * yn


def _tile_schedule(qs, ks, fs, fe):
    B, n_tiles = qs.shape[:2]
    qmax = qs[:, :, :, 0]
    kmax = ks[:, :, 0, :N_HEADS]
    f_first = fs[:, :, ::N_PARTS, 0]
    f_last = fe[:, :, ::N_PARTS, 0]
    gap = f_first[:, :, None, :] - f_last[:, None, :, :]
    reach = NORM_SLACK * qmax[:, :, None, :] * (kmax[:, None, :, :] + kmax[:, :, None, :])
    mass = TM * jnp.exp2(jnp.minimum(gap + reach, 0.0))
    tiles = jnp.arange(n_tiles)
    before = (tiles[None, :] < tiles[:, None])[None, :, :, None]
    skip = jnp.cumsum(jnp.where(before, mass, 1.0), axis=2) <= SKIP_MASS
    first = jnp.sum(skip.astype(jnp.int32), axis=2)
    first = first.transpose(0, 2, 1)
    visit = (tiles[None, None, None, :] >= first[:, :, :, None]) & (tiles[None, None, None, :] < tiles[None, None, :, None])
    flat = visit.transpose(0, 2, 1, 3).reshape(B, n_tiles, N_HEADS * n_tiles).astype(jnp.int32)
    pos = jnp.cumsum(flat, axis=2) - flat
    n_items = jnp.sum(flat, axis=2)
    n_off = (n_items + 1) // 2
    codes = jnp.arange(N_HEADS * n_tiles)
    slots = jnp.arange(N_HEADS * n_tiles)
    hit = (flat[:, :, None, :] == 1) & (pos[:, :, None, :] == slots[None, None, :, None])
    seq = jnp.sum(jnp.where(hit, codes[None, None, None, :], 0), axis=3)
    seq = jnp.where(slots[None, None, :] == n_items[:, :, None], N_HEADS * n_tiles, seq)
    head = slots[None, None, :] - 2 * n_off[:, :, None]
    seq = jnp.where((head >= 0) & (head < N_HEADS), head * n_tiles + tiles[None, :, None], seq)
    return seq.reshape(-1).astype(jnp.int32), n_off.reshape(-1).astype(jnp.int32)


def _attention_outproj(layer, schedule, qt, qa, k, ka, vt, g_att, yl, ys, x, mod, prep):
    B, S, D = x.shape
    per_layer = lambda a: pl.BlockSpec((1,) + a.shape[1:], lambda b, i, *_: (layer,) + (0,) * (a.ndim - 1))
    row_tile = lambda n: pl.BlockSpec((1, TQ, n), lambda b, i, *_: (b, i, 0))
    grid_spec = pltpu.PrefetchScalarGridSpec(
        num_scalar_prefetch=2,
        grid=(B, S // TQ),
        in_specs=[
            pl.BlockSpec((1, D_ATT, TQ), lambda b, i, *_: (b, 0, i)),
            pl.BlockSpec((1, LANES, TQ), lambda b, i, *_: (b, 0, i)),
            pl.BlockSpec((1, S, D_ATT), lambda b, i, *_: (b, 0, 0)),
            pl.BlockSpec((1, S, LANES), lambda b, i, *_: (b, 0, 0)),
            pl.BlockSpec((1, D_ATT, S), lambda b, i, *_: (b, 0, 0)),
            row_tile(D_ATT), row_tile(D_LRU), row_tile(D_SG), row_tile(D),
            pl.BlockSpec((1, 1, D // TN_ADA, 1, TN_ADA), lambda b, i, *_: (layer, b, 2, 0, 0)),
            per_layer(prep["post_g"]),
            per_layer(prep["w_out"]),
        ],
        out_specs=row_tile(D),
        scratch_shapes=[
            pltpu.VMEM((N_HEADS + 1, 1, TQ), F32),
            pltpu.VMEM((N_HEADS + 1, V_ROWS, TQ), F32),
            pltpu.VMEM((N_HEADS + 1, 2 * LANES, TQ), BF16),
            pltpu.VMEM((2, TQ, TQ), F32),
            pltpu.VMEM((2, TQ, TQ), F32),
            pltpu.VMEM((2, TQ, TQ), F32),
            pltpu.VMEM((2, TQ, TQ), BF16),
            pltpu.VMEM((2, TQ, TQ), BF16),
            pltpu.VMEM((2, 1, TQ), F32),
            pltpu.VMEM((2, 1, TQ), F32),
            pltpu.VMEM((2, 1, TQ), F32),
            pltpu.VMEM((2, 1, TQ), F32),
        ],
    )
    return pl.pallas_call(
        _attn_kernel,
        grid_spec=grid_spec,
        out_shape=jax.ShapeDtypeStruct((B, S, D), F32),
        compiler_params=_params(("arbitrary", "arbitrary")),
        name="fox_attention_outproj",
    )(*schedule, qt, qa, k, ka, vt, g_att, yl, ys, x, mod, prep["post_g"], prep["w_out"])


def _block_diag(w):
    L, G, n, _ = w.shape
    eye = jnp.eye(G, dtype=w.dtype)
    return (w[:, :, :, None, :] * eye[None, :, None, :, None]).reshape(L, G * n, G * n)


def _prepare(pre_g, post_g, w_in, b_f, conv_w, conv_b, lru_wa, lru_ba, lru_wx, lru_bx, lru_lambda,
             sg_ln_g, sg_ln_b, sg_w, sg_b, w_out):
    L, D, _ = w_in.shape
    cuts = [0]
    for n in IN_SIZES:
        cuts.append(cuts[-1] + n)
    w_bf = w_in.astype(BF16)
    wq, wk, wv, wf, wga, wxl, wgl, wsu, wsv, wgs = (w_bf[:, :, cuts[j]:cuts[j + 1]] for j in range(10))
    row = lambda a: a[:, None, :]
    wf3 = jnp.repeat(wf, N_PARTS, axis=2)
    pad = jnp.zeros((L, D, T_ROWS - 2 * D_ATT - N_AUG), BF16)
    return dict(
        pre_g=row(pre_g), post_g=row(post_g),
        we=jnp.concatenate([wxl, wsv], axis=2),
        wg=jnp.concatenate([wga, wgl, wsu, wgs], axis=2),
        wk=wk,
        wt=jnp.concatenate([wq, wv, wf3, pad], axis=2).transpose(0, 2, 1),
        bf=jnp.broadcast_to(jnp.repeat(b_f, N_PARTS, axis=1)[:, :, None], (L, N_AUG, LANES)),
        conv_w=conv_w, conv_b=row(conv_b),
        wa=_block_diag(lru_wa).astype(BF16), ba=row(lru_ba),
        wx=_block_diag(lru_wx).astype(BF16), bx=row(lru_bx), lam=row(lru_lambda),
        ln_g=row(sg_ln_g), ln_b=row(sg_ln_b),
        ws=sg_w.reshape(L, N_SG_GROUPS * SG_CHUNK, SG_CHUNK).astype(BF16),
        sg_bias=jnp.repeat(sg_b.transpose(0, 2, 1), HEAD_DIM, axis=2),
        w_out=w_out.astype(BF16),
        head_sum=(jnp.arange(D_ATT)[:, None] // HEAD_DIM == jnp.arange(LANES)[None, :]).astype(BF16),
    )


def kernel(x, c, ada_w, ada_b, pre_g, post_g, w_in, b_f, conv_w, conv_b, lru_wa, lru_ba, lru_wx,
           lru_bx, lru_lambda, sg_ln_g, sg_ln_b, sg_w, sg_b, w_out):
    mod = _ada(c, ada_w, ada_b)
    prep = _prepare(pre_g, post_g, w_in, b_f, conv_w, conv_b, lru_wa, lru_ba, lru_wx, lru_bx,
                    lru_lambda, sg_ln_g, sg_ln_b, sg_w, sg_b, w_out)
    for layer in range(ada_w.shape[0]):
        qt, vt, k, ka, qa, ga, yl, ys, qs, ks, fs, fe = _inproj(layer, x, mod, prep)
        x = _attention_outproj(layer, _tile_schedule(qs, ks, fs, fe), qt, qa, k, ka, vt, ga, yl, ys, x, mod, prep)
    return x
```

```python
import math

import jax
import jax.numpy as jnp
from jax import lax
from jax.experimental import pallas as pl
from jax.experimental.pallas import tpu as pltpu

HEAD_DIM = 64
D_ATT = 512
D_LRU = 256
D_SG = 256
N_HEADS = D_ATT // HEAD_DIM
N_PAIRS = N_HEADS // 2
N_SG_GROUPS = D_SG // HEAD_DIM
SG_CHUNK = 128
CONV_WIDTH = 4
LRU_C = 8.0
EPS = 1e-6
IN_SIZES = (D_ATT, D_ATT, D_ATT, N_HEADS, D_ATT, D_LRU, D_LRU, D_SG, D_SG, D_SG)

LANES = 128
SUBLANES = 8
BF16_ROWS = 16
VMEM_LIMIT = 56 * 1024 * 1024

TM = 512
TQ = 512
LOOP_STEPS = 2
TN_ADA = 1024
NEG = -1e30
assert TM == TQ

N_PARTS = 3
N_AUG = N_HEADS * N_PARTS
T_ROWS = 2 * D_ATT + 2 * BF16_ROWS
LOG2E = 1.4426950408889634
Q_SCALE = LOG2E / math.sqrt(HEAD_DIM)
V_ROWS = HEAD_DIM + BF16_ROWS
SKIP_MASS = 2.0 ** -26
NORM_SLACK = 1.02
GELU_C0 = math.sqrt(2.0 / math.pi)
GELU_C1 = GELU_C0 * 0.044715

F32 = jnp.float32
BF16 = jnp.bfloat16


def _sigmoid(x):
    return 0.5 * jnp.tanh(0.5 * x) + 0.5


def _silu(x):
    hx = 0.5 * x
    return hx + hx * jnp.tanh(hx)


def _gelu_tanh(x):
    hx = 0.5 * x
    return hx + hx * jnp.tanh(x * (GELU_C0 + GELU_C1 * (x * x)))


def _log_sigmoid(x):
    return jnp.minimum(x, 0.0) - jnp.log1p(jnp.exp(-jnp.abs(x)))


def _split3(x, sel):
    hi = x.astype(BF16).astype(F32)
    r1 = x - hi
    mid = r1.astype(BF16).astype(F32)
    lo = r1 - mid
    return jnp.where(sel == 0, hi, jnp.where(sel == 1, mid, lo))


def _mod_row(ref):
    return jnp.concatenate([ref[0, 0, j] for j in range(ref.shape[2])], axis=1)


def _params(sem):
    return pltpu.CompilerParams(dimension_semantics=sem, vmem_limit_bytes=VMEM_LIMIT)


def _ada_kernel(ct_ref, w_ref, b_ref, o_ref):
    ct = ct_ref[...]
    sc = _silu(ct)
    w = w_ref[0]
    bias = b_ref[0]
    for b in range(ct.shape[1]):
        col = sc[:, b:b + 1]
        o_ref[0, b, 0] = jnp.sum(col * w, axis=0, keepdims=True) + bias


def _ada(c, ada_w, ada_b):
    L, D, N = ada_w.shape
    B = c.shape[0]
    return pl.pallas_call(
        _ada_kernel,
        grid=(L, N // TN_ADA),
        in_specs=[
            pl.BlockSpec((D, B), lambda l, n: (0, 0)),
            pl.BlockSpec((1, D, TN_ADA), lambda l, n: (l, 0, n)),
            pl.BlockSpec((1, 1, TN_ADA), lambda l, n: (l, 0, n)),
        ],
        out_specs=pl.BlockSpec((1, B, 1, 1, TN_ADA), lambda l, n: (l, 0, n, 0, 0)),
        out_shape=jax.ShapeDtypeStruct((L, B, N // TN_ADA, 1, TN_ADA), F32),
        compiler_params=_params(("parallel", "parallel")),
        name="ada_mod",
    )(c.T, ada_w, ada_b.reshape(L, 1, N))


def _lru_mixer(g_lru, r_pre, i_pre, lam, h_ref, xc):
    r = _sigmoid(r_pre)
    ig = _sigmoid(i_pre)
    nlam = -lam
    softplus = jnp.maximum(nlam, 0.0) + jnp.log1p(jnp.exp(-jnp.abs(nlam)))
    log_a = (-LRU_C * r) * softplus
    a = jnp.exp(log_a)
    bt = jnp.sqrt(jnp.tanh(-log_a) * (1.0 + a * a)) * (ig * xc)

    row8 = lax.broadcasted_iota(jnp.int32, (TM, D_LRU), 0) & (SUBLANES - 1)
    d = 1
    while d < SUBLANES:
        valid = row8 >= d
        a_s = jnp.where(valid, pltpu.roll(a, d, axis=0), 1.0)
        b_s = jnp.where(valid, pltpu.roll(bt, d, axis=0), 0.0)
        bt = bt + a * b_s
        a = a * a_s
        d *= 2

    h_prev = h_ref[...]
    sg = _silu(g_lru)
    outs = []
    for g in range(TM // SUBLANES):
        lo = g * SUBLANES
        hg = bt[lo:lo + SUBLANES] + a[lo:lo + SUBLANES] * h_prev
        h_prev = hg[SUBLANES - 1:SUBLANES, :]
        outs.append(hg * sg[lo:lo + SUBLANES])
    h_ref[...] = h_prev
    return jnp.concatenate(outs, axis=0)


def _inproj_kernel(x_ref, shift_ref, scale_ref, g_ref, we_ref, wg_ref, wk_ref, wt_ref, bf_ref,
                   cw_ref, cb_ref, wa_ref, ba_ref, wx_ref, bx_ref, lam_ref,
                   lng_ref, lnb_ref, ws_ref, sb_ref, he_ref,
                   qt_ref, vt_ref, k_ref, ka_ref, qa_ref, ga_ref, yl_ref, ys_ref,
                   qs_ref, ks_ref, fs_ref, fe_ref,
                   fc_ref, xbuf_ref, h_ref):
    i = pl.program_id(1)

    @pl.when(i == 0)
    def _():
        fc_ref[...] = jnp.zeros(fc_ref.shape, F32)
        xbuf_ref[0:SUBLANES, :] = jnp.zeros((SUBLANES, D_LRU), F32)
        h_ref[...] = jnp.zeros(h_ref.shape, F32)

    x = x_ref[0]
    ms = jnp.mean(x * x, axis=-1, keepdims=True)
    gs = g_ref[0] * (1.0 + _mod_row(scale_ref))
    h = (x * lax.rsqrt(ms + EPS)) * gs + _mod_row(shift_ref)
    hb = h.astype(BF16)

    early = jnp.dot(hb, we_ref[0], preferred_element_type=F32)
    x_lru = early[:, :D_LRU]
    sg_v = early[:, D_LRU:]
    tr = lax.dot_general(wt_ref[0], hb, (((1,), (1,)), ((), ())), preferred_element_type=F32)
    tq = tr[:D_ATT] * Q_SCALE
    qt_ref[0] = tq.astype(BF16)
    vt_ref[0] = tr[D_ATT:2 * D_ATT].astype(BF16)
    fl = tr[2 * D_ATT:2 * D_ATT + N_AUG]
    kb = jnp.dot(hb, wk_ref[0], preferred_element_type=F32).astype(BF16)
    k_ref[0] = kb
    qn2 = jnp.sum((tq * tq).reshape(N_HEADS, HEAD_DIM, TM), axis=1)
    qs_ref[0, 0] = jnp.broadcast_to(jnp.sqrt(jnp.max(qn2, axis=1, keepdims=True)), (N_HEADS, LANES))

    v = _gelu_tanh(sg_v)
    mu = jnp.mean(v, axis=-1, keepdims=True)
    var = jnp.mean(jnp.square(v - mu), axis=-1, keepdims=True)
    vnb = (((v - mu) * lax.rsqrt(var + EPS)) * lng_ref[0] + lnb_ref[0]).astype(BF16)
    w_shape = ws_ref.shape[1:]
    t_idx = lax.broadcasted_iota(jnp.int32, w_shape, 0) & (SG_CHUNK - 1)
    s_idx = lax.broadcasted_iota(jnp.int32, w_shape, 1)
    ws = jnp.where(s_idx <= t_idx, ws_ref[0], jnp.zeros(w_shape, BF16))
    grp = lax.broadcasted_iota(jnp.int32, (SG_CHUNK, D_SG), 1) // HEAD_DIM
    zs = []
    for c in range(TM // SG_CHUNK):
        zz = jnp.dot(ws, vnb[c * SG_CHUNK:(c + 1) * SG_CHUNK], preferred_element_type=F32)
        z = zz[0:SG_CHUNK]
        for gi in range(1, N_SG_GROUPS):
            z = jnp.where(grp == gi, zz[gi * SG_CHUNK:(gi + 1) * SG_CHUNK], z)
        zs.append(z + sb_ref[0])
    z_all = jnp.concatenate(zs, axis=0)

    xbuf_ref[SUBLANES:SUBLANES + TM, :] = x_lru
    xc = cb_ref[0]
    for kk in range(CONV_WIDTH):
        off = SUBLANES - (CONV_WIDTH - 1) + kk
        xc = xc + xbuf_ref[off:off + TM, :] * cw_ref[0, kk:kk + 1, :]
    xbuf_ref[0:SUBLANES, :] = x_lru[TM - SUBLANES:, :]
    xcb = xc.astype(BF16)
    r_pre = jnp.dot(xcb, wa_ref[0], preferred_element_type=F32) + ba_ref[0]
    i_pre = jnp.dot(xcb, wx_ref[0], preferred_element_type=F32) + bx_ref[0]

    gates = jnp.dot(hb, wg_ref[0], preferred_element_type=F32)
    kn2 = jnp.dot(kb * kb, he_ref[...], preferred_element_type=F32)
    ks_ref[0, 0] = jnp.sqrt(jnp.max(kn2, axis=0, keepdims=True))

    ga_ref[0] = gates[:, :D_ATT].astype(BF16)
    g_lru = gates[:, D_ATT:D_ATT + D_LRU]
    sg_u = gates[:, D_ATT + D_LRU:D_ATT + D_LRU + D_SG]
    g_sg = gates[:, D_ATT + D_LRU + D_SG:]
    ys_ref[0] = ((_gelu_tanh(sg_u) * z_all) * _silu(g_sg)).astype(BF16)
    yl_ref[0] = _lru_mixer(g_lru, r_pre, i_pre, lam_ref[0], h_ref, xc).astype(BF16)

    reps = TM // LANES
    ls = _log_sigmoid(fl + jnp.concatenate([bf_ref[0]] * reps, axis=1))
    lane = lax.broadcasted_iota(jnp.int32, (N_AUG, TM), 1)
    d = 1
    while d < TM:
        ls = ls + jnp.where(lane >= d, pltpu.roll(ls, d, axis=1), 0.0)
        d *= 2
    f = ls + jnp.concatenate([fc_ref[...]] * reps, axis=1)
    fc_ref[...] = jnp.broadcast_to(f[:, TM - 1:TM], (N_AUG, LANES))
    row = lax.broadcasted_iota(jnp.int32, (N_AUG, TM), 0)
    f2 = f * LOG2E
    fs_ref[0, 0] = jnp.broadcast_to(f2[:, 0:1], (N_AUG, LANES))
    fe_ref[0, 0] = jnp.broadcast_to(f2[:, TM - 1:TM], (N_AUG, LANES))
    parts = _split3(f2, row % N_PARTS)
    ones = jnp.ones((N_AUG, TM), F32)
    zeros = jnp.zeros((LANES - 2 * N_AUG, TM), F32)
    qa_ref[0] = jnp.concatenate([ones, parts, zeros], axis=0).astype(BF16)
    ka_ref[0] = jnp.concatenate([-parts, ones, zeros], axis=0).T.astype(BF16)


def _inproj(layer, x, mod, prep):
    B, S, D = x.shape
    per_layer = lambda a: pl.BlockSpec((1,) + a.shape[1:], lambda b, i: (layer,) + (0,) * (a.ndim - 1))
    mod_chunk = lambda j: pl.BlockSpec((1, 1, D // TN_ADA, 1, TN_ADA), lambda b, i: (layer, b, j, 0, 0))
    row_tile = lambda n: pl.BlockSpec((1, TM, n), lambda b, i: (b, i, 0))
    col_tile = lambda n: pl.BlockSpec((1, n, TM), lambda b, i: (b, 0, i))
    names = ("pre_g", "we", "wg", "wk", "wt", "bf", "conv_w", "conv_b", "wa", "ba", "wx", "bx", "lam",
             "ln_g", "ln_b", "ws", "sg_bias")
    operands = [prep[n] for n in names]
    head_sum = prep["head_sum"]
    stat = lambda rows: pl.BlockSpec((1, 1, rows, LANES), lambda b, i: (b, i, 0, 0))
    stat_shape = lambda rows: jax.ShapeDtypeStruct((B, S // TM, rows, LANES), F32)
    return pl.pallas_call(
        _inproj_kernel,
        grid=(B, S // TM),
        in_specs=[row_tile(D), mod_chunk(0), mod_chunk(1)] + [per_layer(a) for a in operands]
        + [pl.BlockSpec(head_sum.shape, lambda b, i: (0, 0))],
        out_specs=[
            col_tile(D_ATT), col_tile(D_ATT), row_tile(D_ATT), row_tile(LANES), col_tile(LANES),
            row_tile(D_ATT), row_tile(D_LRU), row_tile(D_SG),
            stat(N_HEADS), stat(1), stat(N_AUG), stat(N_AUG),
        ],
        out_shape=[
            jax.ShapeDtypeStruct((B, D_ATT, S), BF16),
            jax.ShapeDtypeStruct((B, D_ATT, S), BF16),
            jax.ShapeDtypeStruct((B, S, D_ATT), BF16),
            jax.ShapeDtypeStruct((B, S, LANES), BF16),
            jax.ShapeDtypeStruct((B, LANES, S), BF16),
            jax.ShapeDtypeStruct((B, S, D_ATT), BF16),
            jax.ShapeDtypeStruct((B, S, D_LRU), BF16),
            jax.ShapeDtypeStruct((B, S, D_SG), BF16),
            stat_shape(N_HEADS),
            stat_shape(1),
            stat_shape(N_AUG),
            stat_shape(N_AUG),
        ],
        scratch_shapes=[
            pltpu.VMEM((N_AUG, LANES), F32),
            pltpu.VMEM((TM + SUBLANES, D_LRU), F32),
            pltpu.VMEM((1, D_LRU), F32),
        ],
        compiler_params=_params(("parallel", "arbitrary")),
        name="inproj",
    )(x, mod, mod, *operands, head_sum)


def _attn_kernel(seq_ref, noff_ref, qt_ref, qa_ref, k_ref, ka_ref, vt_ref, g_ref,
                 yl_ref, ys_ref, x_ref, gate_ref, pg_ref, w_ref, o_ref,
                 m_ref, acc_ref, rhs_ref, bias_ref, s0_ref, s1_ref, p0_ref, p1_ref,
                 mx0_ref, mx1_ref, al0_ref, al1_ref):
    b = pl.program_id(0)
    i = pl.program_id(1)
    n_tiles = pl.num_programs(1)

    @pl.when((b == 0) & (i == 0))
    def _():
        kk = lax.broadcasted_iota(jnp.int32, (TQ, TQ), 0)
        qq = lax.broadcasted_iota(jnp.int32, (TQ, TQ), 1)
        bias_ref[0] = jnp.zeros((TQ, TQ), F32)
        bias_ref[1] = jnp.where(kk <= qq, 0.0, NEG)
        rhs_ref[...] = jnp.zeros(rhs_ref.shape, BF16)
        acc_ref[...] = jnp.zeros(acc_ref.shape, F32)

    qa = qa_ref[0, :2 * N_AUG, :]
    row = lax.broadcasted_iota(jnp.int32, (2 * N_AUG, TQ), 0)
    for h in range(N_HEADS):
        own = (h % 2) * HEAD_DIM
        rhs_ref[h, own:own + HEAD_DIM, :] = qt_ref[0, h * HEAD_DIM:(h + 1) * HEAD_DIM, :]
        lo = h * N_PARTS
        aug = ((row >= lo) & (row < lo + N_PARTS)) | ((row >= N_AUG + lo) & (row < N_AUG + lo + N_PARTS))
        rhs_ref[h, LANES:LANES + 2 * N_AUG, :] = jnp.where(aug, qa, jnp.zeros_like(qa))
    ones_rows = jnp.where(lax.broadcasted_iota(jnp.int32, (BF16_ROWS, TQ), 0) == 0, 1.0, 0.0).astype(BF16)
    m_ref[...] = jnp.full(m_ref.shape, NEG, F32)
    s_bufs = (s0_ref, s1_ref)
    p_bufs = (p0_ref, p1_ref)
    mx_bufs = (mx0_ref, mx1_ref)
    al_bufs = (al0_ref, al1_ref)

    def item(n, hh):
        code = seq_ref[(b * n_tiles + i) * (N_HEADS * n_tiles) + 2 * n + hh]
        return lax.div(code, n_tiles), lax.rem(code, n_tiles)

    HALF = TQ // 2

    def key_lanes(h):
        pair = lax.shift_right_logical(jnp.minimum(h, N_HEADS - 1), 1)
        return pl.ds(pl.multiple_of(pair * LANES, LANES), LANES)

    def scores(n, slot, kind):
        for hh in range(2):
            h, t = item(n, hh)
            if kind == "diag":
                lo_rows = pl.ds(pl.multiple_of(i * TQ, TQ), HALF)
                hi_rows = pl.ds(pl.multiple_of(i * TQ + HALF, HALF), HALF)
                lhs_lo = jnp.concatenate([k_ref[0, lo_rows, key_lanes(h)], ka_ref[0, lo_rows, :]], axis=1)
                lhs_hi = jnp.concatenate([k_ref[0, hi_rows, key_lanes(h)], ka_ref[0, hi_rows, :]], axis=1)
                st_lo = jnp.dot(lhs_lo, rhs_ref[h], preferred_element_type=F32) + bias_ref[1, :HALF, :]
                st_hi = (jnp.dot(lhs_hi, rhs_ref[h, :, HALF:], preferred_element_type=F32)
                         + bias_ref[1, HALF:, HALF:])
                s_bufs[slot][hh, :HALF, :] = st_lo
                s_bufs[slot][hh, HALF:, HALF:] = st_hi
                mx_lo = jnp.max(st_lo, axis=0, keepdims=True)
                mx_hi = jnp.max(st_hi, axis=0, keepdims=True)
                mx_bufs[slot][hh] = jnp.concatenate([mx_lo[:, :HALF], jnp.maximum(mx_lo[:, HALF:], mx_hi)], axis=1)
                continue
            rows = pl.ds(pl.multiple_of(t * TQ, TQ), TQ)
            lhs = jnp.concatenate([k_ref[0, rows, key_lanes(h)], ka_ref[0, rows, :]], axis=1)
            st = jnp.dot(lhs, rhs_ref[h], preferred_element_type=F32)
            if kind == "any":
                st = st + bias_ref[(t == i).astype(jnp.int32)]
            s_bufs[slot][hh] = st
            mx_bufs[slot][hh] = jnp.max(st, axis=0, keepdims=True)

    def softmax(n, slot, kind):
        for hh in range(2):
            h, _ = item(n, hh)
            m_old = m_ref[h]
            m_new = jnp.maximum(m_old, mx_bufs[slot][hh])
            if kind == "diag":
                p_bufs[slot][hh, :HALF, :] = jnp.exp2(s_bufs[slot][hh, :HALF, :] - m_new).astype(BF16)
                p_bufs[slot][hh, HALF:, HALF:] = jnp.exp2(s_bufs[slot][hh, HALF:, HALF:] - m_new[:, HALF:]).astype(BF16)
            else:
                p_bufs[slot][hh] = jnp.exp2(s_bufs[slot][hh] - m_new).astype(BF16)
            al_bufs[slot][hh] = jnp.exp2(m_old - m_new)
            m_ref[h] = m_new

    def values(n, slot, kind):
        for hh in range(2):
            h, t = item(n, hh)
            cols = pl.ds(pl.multiple_of(t * TQ, TQ), TQ)
            head_rows = pl.ds(pl.multiple_of(jnp.minimum(h, N_HEADS - 1) * HEAD_DIM, HEAD_DIM), HEAD_DIM)
            vaug = jnp.concatenate([vt_ref[0, head_rows, cols], ones_rows], axis=0)
            if kind == "diag":
                new = jnp.dot(vaug[:, :HALF], p_bufs[slot][hh, :HALF, :], preferred_element_type=F32)
                new_hi = jnp.dot(vaug[:, HALF:], p_bufs[slot][hh, HALF:, HALF:], preferred_element_type=F32)
                new = jnp.concatenate([new[:, :HALF], new[:, HALF:] + new_hi], axis=1)
            else:
                new = jnp.dot(vaug, p_bufs[slot][hh], preferred_element_type=F32)
            acc_ref[h] = acc_ref[h] * al_bufs[slot][hh] + new

    def full_step(n, parity, kinds):
        softmax(n - 1, 1 - parity, kinds[1])
        scores(n, parity, kinds[0])
        values(n - 2, parity, kinds[2])

    n_off = noff_ref[b * n_tiles + i]
    long_run = n_off >= 2 + LOOP_STEPS
    for kind, peel, wanted in (("off", LOOP_STEPS, long_run), ("off", 0, (n_off >= 2) & ~long_run),
                               ("any", 0, n_off < 2)):
        @pl.when(wanted)
        def _():
            scores(0, 0, kind)
            softmax(0, 0, kind)
            scores(1, 1, kind)
            for j in range(peel):
                full_step(2 + j, j % 2, ("off",) * 3)

    start = jnp.where(long_run, 2 + LOOP_STEPS, 2)
    trips = lax.div(jnp.maximum(n_off - start, 0), LOOP_STEPS)

    def body(u, carry):
        for j in range(LOOP_STEPS):
            full_step(start + LOOP_STEPS * u + j, j % 2, ("off",) * 3)
        return carry

    lax.fori_loop(0, trips, body, 0)
    n0 = start + LOOP_STEPS * trips
    for left in range(2, LOOP_STEPS + N_PAIRS):
        kind = lambda j, left=left: "diag" if j >= max(left - N_PAIRS, 0) else "off"

        @pl.when(n_off + N_PAIRS - n0 == left)
        def _():
            for j in range(left):
                full_step(n0 + j, j % 2, (kind(j), kind(j - 1), kind(j - 2)))
            softmax(n0 + left - 1, (left - 1) % 2, kind(left - 1))
            values(n0 + left - 2, left % 2, kind(left - 2))
            values(n0 + left - 1, (left - 1) % 2, kind(left - 1))

    ya = []
    for p in range(N_PAIRS):
        outs = []
        for hh in range(2):
            a = acc_ref[2 * p + hh]
            outs.append(a[:HEAD_DIM] / a[HEAD_DIM:HEAD_DIM + 1])
        o = jnp.concatenate(outs, axis=0).T
        lanes = slice(p * LANES, (p + 1) * LANES)
        ya.append((o * _silu(g_ref[0, :, lanes].astype(F32))).astype(BF16))

    y = jnp.dot(jnp.concatenate(ya, axis=1), w_ref[0, 0:D_ATT, :], preferred_element_type=F32)
    y = y + jnp.dot(yl_ref[0], w_ref[0, D_ATT:D_ATT + D_LRU, :], preferred_element_type=F32)
    y = y + jnp.dot(ys_ref[0], w_ref[0, D_ATT + D_LRU:, :], preferred_element_type=F32)
    ms = jnp.mean(y * y, axis=-1, keepdims=True)
    yn = (y * lax.rsqrt(ms + EPS)) * pg_ref[0]
    o_ref[0] = x_ref[0] + _mod_row(gate_ref) * yn


def _tile_schedule(qs, ks, fs, fe):
    B, n_tiles = qs.shape[:2]
    qmax = qs[:, :, :, 0]
    kmax = ks[:, :, 0, :N_HEADS]
    f_first = fs[:, :, ::N_PARTS, 0]
    f_last = fe[:, :, ::N_PARTS, 0]
    gap = f_first[:, :, None, :] - f_last[:, None, :, :]
    reach = NORM_SLACK * qmax[:, :, None, :] * (kmax[:, None, :, :] + kmax[:, :, None, :])
    mass = TM * jnp.exp2(jnp.minimum(gap + reach, 0.0))
    tiles = jnp.arange(n_tiles)
    before = (tiles[None, :] < tiles[:, None])[None, :, :, None]
    skip = jnp.cumsum(jnp.where(before, mass, 1.0), axis=2) <= SKIP_MASS
    first = jnp.sum(skip.astype(jnp.int32), axis=2)
    first = first.transpose(0, 2, 1)
    visit = (tiles[None, None, None, :] >= first[:, :, :, None]) & (tiles[None, None, None, :] < tiles[None, None, :, None])
    flat = visit.transpose(0, 2, 1, 3).reshape(B, n_tiles, N_HEADS * n_tiles).astype(jnp.int32)
    pos = jnp.cumsum(flat, axis=2) - flat
    n_items = jnp.sum(flat, axis=2)
    n_off = (n_items + 1) // 2
    codes = jnp.arange(N_HEADS * n_tiles)
    slots = jnp.arange(N_HEADS * n_tiles)
    hit = (flat[:, :, None, :] == 1) & (pos[:, :, None, :] == slots[None, None, :, None])
    seq = jnp.sum(jnp.where(hit, codes[None, None, None, :], 0), axis=3)
    seq = jnp.where(slots[None, None, :] == n_items[:, :, None], N_HEADS * n_tiles, seq)
    head = slots[None, None, :] - 2 * n_off[:, :, None]
    seq = jnp.where((head >= 0) & (head < N_HEADS), head * n_tiles + tiles[None, :, None], seq)
    return seq.reshape(-1).astype(jnp.int32), n_off.reshape(-1).astype(jnp.int32)


def _attention_outproj(layer, schedule, qt, qa, k, ka, vt, g_att, yl, ys, x, mod, prep):
    B, S, D = x.shape
    per_layer = lambda a: pl.BlockSpec((1,) + a.shape[1:], lambda b, i, *_: (layer,) + (0,) * (a.ndim - 1))
    row_tile = lambda n: pl.BlockSpec((1, TQ, n), lambda b, i, *_: (b, i, 0))
    grid_spec = pltpu.PrefetchScalarGridSpec(
        num_scalar_prefetch=2,
        grid=(B, S // TQ),
        in_specs=[
            pl.BlockSpec((1, D_ATT, TQ), lambda b, i, *_: (b, 0, i)),
            pl.BlockSpec((1, LANES, TQ), lambda b, i, *_: (b, 0, i)),
            pl.BlockSpec((1, S, D_ATT), lambda b, i, *_: (b, 0, 0)),
            pl.BlockSpec((1, S, LANES), lambda b, i, *_: (b, 0, 0)),
            pl.BlockSpec((1, D_ATT, S), lambda b, i, *_: (b, 0, 0)),
            row_tile(D_ATT), row_tile(D_LRU), row_tile(D_SG), row_tile(D),
            pl.BlockSpec((1, 1, D // TN_ADA, 1, TN_ADA), lambda b, i, *_: (layer, b, 2, 0, 0)),
            per_layer(prep["post_g"]),
            per_layer(prep["w_out"]),
        ],
        out_specs=row_tile(D),
        scratch_shapes=[
            pltpu.VMEM((N_HEADS + 1, 1, TQ), F32),
            pltpu.VMEM((N_HEADS + 1, V_ROWS, TQ), F32),
            pltpu.VMEM((N_HEADS + 1, 2 * LANES, TQ), BF16),
            pltpu.VMEM((2, TQ, TQ), F32),
            pltpu.VMEM((2, TQ, TQ), F32),
            pltpu.VMEM((2, TQ, TQ), F32),
            pltpu.VMEM((2, TQ, TQ), BF16),
            pltpu.VMEM((2, TQ, TQ), BF16),
            pltpu.VMEM((2, 1, TQ), F32),
            pltpu.VMEM((2, 1, TQ), F32),
            pltpu.VMEM((2, 1, TQ), F32),
            pltpu.VMEM((2, 1, TQ), F32),
        ],
    )
    return pl.pallas_call(
        _attn_kernel,
        grid_spec=grid_spec,
        out_shape=jax.ShapeDtypeStruct((B, S, D), F32),
        compiler_params=_params(("arbitrary", "arbitrary")),
        name="fox_attention_outproj",
    )(*schedule, qt, qa, k, ka, vt, g_att, yl, ys, x, mod, prep["post_g"], prep["w_out"])


def _block_diag(w):
    L, G, n, _ = w.shape
    eye = jnp.eye(G, dtype=w.dtype)
    return (w[:, :, :, None, :] * eye[None, :, None, :, None]).reshape(L, G * n, G * n)


def _prepare(pre_g, post_g, w_in, b_f, conv_w, conv_b, lru_wa, lru_ba, lru_wx, lru_bx, lru_lambda,
             sg_ln_g, sg_ln_b, sg_w, sg_b, w_out):
    L, D, _ = w_in.shape
    cuts = [0]
    for n in IN_SIZES:
        cuts.append(cuts[-1] + n)
    wq, wk, wv, wf, wga, wxl, wgl, wsu, wsv, wgs = (w_in[:, :, cuts[j]:cuts[j + 1]] for j in range(10))
    row = lambda a: a[:, None, :]
    wf3 = jnp.repeat(wf, N_PARTS, axis=2)
    pad = jnp.zeros((L, D, T_ROWS - 2 * D_ATT - N_AUG), F32)
    return dict(
        pre_g=row(pre_g), post_g=row(post_g),
        we=jnp.concatenate([wxl, wsv], axis=2).astype(BF16),
        wg=jnp.concatenate([wga, wgl, wsu, wgs], axis=2).astype(BF16),
        wk=wk.astype(BF16),
        wt=jnp.concatenate([wq, wv, wf3, pad], axis=2).transpose(0, 2, 1).astype(BF16),
        bf=jnp.broadcast_to(jnp.repeat(b_f, N_PARTS, axis=1)[:, :, None], (L, N_AUG, LANES)),
        conv_w=conv_w, conv_b=row(conv_b),
        wa=_block_diag(lru_wa).astype(BF16), ba=row(lru_ba),
        wx=_block_diag(lru_wx).astype(BF16), bx=row(lru_bx), lam=row(lru_lambda),
        ln_g=row(sg_ln_g), ln_b=row(sg_ln_b),
        ws=sg_w.reshape(L, N_SG_GROUPS * SG_CHUNK, SG_CHUNK).astype(BF16),
        sg_bias=jnp.repeat(sg_b.transpose(0, 2, 1), HEAD_DIM, axis=2),
        w_out=w_out.astype(BF16),
        head_sum=(jnp.arange(D_ATT)[:, None] // HEAD_DIM == jnp.arange(LANES)[None, :]).astype(BF16),
    )


def kernel(x, c, ada_w, ada_b, pre_g, post_g, w_in, b_f, conv_w, conv_b, lru_wa, lru_ba, lru_wx,
           lru_bx, lru_lambda, sg_ln_g, sg_ln_b, sg_w, sg_b, w_out):
    mod = _ada(c, ada_w, ada_b)
    prep = _prepare(pre_g, post_g, w_in, b_f, conv_w, conv_b, lru_wa, lru_ba, lru_wx, lru_bx,
                    lru_lambda, sg_ln_g, sg_ln_b, sg_w, sg_b, w_out)
    for layer in range(ada_w.shape[0]):
        qt, vt, k, ka, qa, ga, yl, ys, qs, ks, fs, fe = _inproj(layer, x, mod, prep)
        x = _attention_outproj(layer, _tile_schedule(qs, ks, fs, fe), qt, qa, k, ka, vt, ga, yl, ys, x, mod, prep)
    return x
```
